```python
import jax, jax.numpy as jnp
from jax import lax
import numpy as np

D_MODEL = 1024
BATCH = 8
SEQ = 2048
DEPTH = 2
DEC_BATCH = 32
DEC_SEQ = 1
PAST_LEN = 8192
PAGE_SIZE = 128

N_A_LAYERS = DEPTH // 2
N_B_LAYERS = DEPTH - N_A_LAYERS
RMS_EPS = 1e-6
RW_HEAD = 64
RW_HEADS = D_MODEL // RW_HEAD
RW_DECAY_LORA = 64
RW_AAA_LORA = 64
RW_GATE_LORA = 128
RW_GN_EPS = 64e-5
N_HEADS = 16
HEAD_DIM = D_MODEL // N_HEADS
KV_HEADS = 4
GQA_R = N_HEADS // KV_HEADS
ATT_SCALE = HEAD_DIM ** -0.5
ROPE_DIMS = HEAD_DIM // 4
ROPE_THETA = 500000.0
N_KV_STREAMS = 6
CMP_BLOCK = 32
CMP_STRIDE = 16
CMP_HIDDEN = 2 * HEAD_DIM
SEL_BLOCK = 64
N_SEL = 16
N_LOCAL_BLOCKS = 2
FORCE_SCORE = 1.0e4
WINDOW = 512
WIN_Q_BLOCK = 128
SEL_Q_BLOCK = 16
N_GROUPS = 4
EXPERTS_PER_GROUP = 8
TOP_K_IN_GROUP = 2
D_EXPERT = 256

kernel_name = 'rwkv7_yoco_nsa_hmoe_step'

F32 = jnp.float32


def rmsnorm(x, g):
    xf = x.astype(F32)
    y = xf * lax.rsqrt(jnp.mean(xf * xf, axis=-1, keepdims=True) + RMS_EPS)
    return (y * g.astype(F32)).astype(x.dtype)


def rope(x, pos):
    half = ROPE_DIMS // 2
    inv = jnp.power(ROPE_THETA, -jnp.arange(half, dtype=F32) * (2.0 / ROPE_DIMS))
    ang = pos.astype(F32)[:, None] * inv[None, :]
    cos = jnp.cos(ang)[:, None, :]
    sin = jnp.sin(ang)[:, None, :]
    xf = x.astype(F32)
    x1 = xf[..., :half]
    x2 = xf[..., half:ROPE_DIMS]
    out = jnp.concatenate([x1 * cos - x2 * sin, x2 * cos + x1 * sin, xf[..., ROPE_DIMS:]], axis=-1)
    return out.astype(x.dtype)


def masked_softmax(s, mask):
    s = jnp.where(mask, s, -jnp.inf)
    m = jnp.max(s, axis=-1, keepdims=True)
    m = jnp.where(jnp.isfinite(m), m, 0.0)
    e = jnp.where(mask, jnp.exp(s - m), 0.0)
    return e / jnp.maximum(jnp.sum(e, axis=-1, keepdims=True), 1e-30)


def map_query_blocks(fn, blk, q_pos, *arrs):
    T = q_pos.shape[0]
    nb = -(-T // blk)
    pad = nb * blk - T
    if pad:
        q_pos = jnp.pad(q_pos, (0, pad), mode='edge')
        arrs = tuple(jnp.pad(a, [(0, 0), (0, pad)] + [(0, 0)] * (a.ndim - 2)) for a in arrs)
    B = arrs[0].shape[0]
    pos_b = q_pos.reshape(nb, blk)
    arr_b = tuple(jnp.moveaxis(a.reshape(B, nb, blk, *a.shape[2:]), 1, 0) for a in arrs)
    out = lax.map(lambda xs: fn(*xs), (pos_b,) + arr_b)
    out = jnp.moveaxis(out, 0, 1)
    return out.reshape(B, nb * blk, *out.shape[3:])[:, :T]


def rwkv7_mixer(xn, prev, s0, mu, w_r, w_k, w_v, w_o, w0, w1, w2, a0, a1, a2, g1, g2, k_k, k_a, r_k, gn_w, gn_b):
    B, T, D = xn.shape
    xx = prev - xn
    xr, xw, xk, xv, xa, xg = (xn + xx * mu[i] for i in range(6))
    r = xr @ w_r
    k = xk @ w_k
    v = xv @ w_v
    w_log = -jax.nn.softplus(-(w0 + jnp.tanh(xw @ w1) @ w2).astype(F32)) - 0.5
    a = jax.nn.sigmoid((a0 + (xa @ a1) @ a2).astype(F32))
    g = jax.nn.sigmoid(xg @ g1) @ g2
    heads = lambda t: t.astype(F32).reshape(B, T, RW_HEADS, RW_HEAD)
    r, k, v, a, w_log = heads(r), heads(k), heads(v), heads(a), heads(w_log)
    kk = k * k_k.astype(F32).reshape(RW_HEADS, RW_HEAD)
    kk = kk / jnp.maximum(jnp.sqrt(jnp.sum(kk * kk, axis=-1, keepdims=True)), 1e-12)
    k = k * (1.0 + (a - 1.0) * k_a.astype(F32).reshape(RW_HEADS, RW_HEAD))
    decay = jnp.exp(-jnp.exp(w_log))

    def step(S, inp):
        r_t, d_t, k_t, v_t, kk_t, a_t = inp
        sa = jnp.einsum('bhij,bhj->bhi', S, -kk_t)
        S = S * d_t[:, :, None, :] + sa[..., None] * (kk_t * a_t)[:, :, None, :] + v_t[..., None] * k_t[:, :, None, :]
        return S, jnp.einsum('bhij,bhj->bhi', S, r_t)

    tm = lambda t: jnp.swapaxes(t, 0, 1)
    s_T, o = lax.scan(step, s0.astype(F32), (tm(r), tm(decay), tm(k), tm(v), tm(kk), tm(a)))
    o = tm(o)
    mean = jnp.mean(o, axis=-1, keepdims=True)
    var = jnp.mean(jnp.square(o - mean), axis=-1, keepdims=True)
    o = ((o - mean) * lax.rsqrt(var + RW_GN_EPS)).reshape(B, T, D) * gn_w.astype(F32) + gn_b.astype(F32)
    bonus = jnp.sum(r * k * r_k.astype(F32), axis=-1, keepdims=True) * v
    o = (o + bonus.reshape(B, T, D)).astype(xn.dtype)
    return (o * g) @ w_o, s_T


def hmoe(xn, w_rg, b_rg, w_re, b_re, w_gate, w_up, w_down):
    B, T, D = xn.shape
    x = xn.reshape(B * T, D)
    lg = (x @ w_rg + b_rg).astype(F32)
    g_sel = jnp.argmax(lg, axis=-1)
    p_g = jnp.max(jax.nn.softmax(lg, axis=-1), axis=-1)
    le = (x @ w_re + b_re).astype(F32).reshape(-1, N_GROUPS, EXPERTS_PER_GROUP)
    le_g = jnp.take_along_axis(le, g_sel[:, None, None], axis=1)[:, 0]
    top_p, top_i = lax.top_k(jax.nn.softmax(le_g, axis=-1), TOP_K_IN_GROUP)
    w_top = p_g[:, None] * top_p / jnp.sum(top_p, axis=-1, keepdims=True)
    comb = jnp.sum(jax.nn.one_hot(top_i, EXPERTS_PER_GROUP, dtype=F32) * w_top[..., None], axis=1)
    y = jnp.zeros((B * T, D), F32)
    for gi in range(N_GROUPS):
        c = jnp.where((g_sel == gi)[:, None], comb, 0.0).astype(x.dtype)
        hid = jax.nn.silu(jnp.einsum('nd,edf->nef', x, w_gate[gi])) * jnp.einsum('nd,edf->nef', x, w_up[gi])
        y = y + jnp.einsum('nef,ne,efd->nd', hid, c, w_down[gi]).astype(F32)
    return y.astype(xn.dtype).reshape(B, T, D)


def compress(rows, pe, w1, b1, w2):
    B, L, G, hd = rows.shape
    nh = CMP_BLOCK // CMP_STRIDE
    nsub = L // CMP_STRIDE
    n_cmp = nsub - nh + 1
    sub = rows[:, :nsub * CMP_STRIDE].reshape(B, nsub, CMP_STRIDE, G, hd)
    w1b = w1.reshape(nh, CMP_STRIDE, hd, CMP_HIDDEN)
    part = jnp.einsum('bnjgd,hjdc->hbngc', sub, w1b)
    pe_term = jnp.einsum('jd,jdc->c', pe, w1.reshape(CMP_BLOCK, hd, CMP_HIDDEN)) + b1
    pre = sum(part[i, :, i:i + n_cmp] for i in range(nh)) + pe_term
    return jax.nn.gelu(pre.astype(F32)).astype(rows.dtype) @ w2


def shared_kv(h, past_len, past_rows, win_buf, ln_kv, w_kv, cmp_pe, cmp_w1, cmp_b1, cmp_w2):
    B, T, _ = h.shape
    q_pos = past_len + jnp.arange(T)
    kv = (rmsnorm(h, ln_kv) @ w_kv).reshape(B, T, N_KV_STREAMS, KV_HEADS, HEAD_DIM)
    k_sel = rope(kv[:, :, 2], q_pos)
    k_win = rope(kv[:, :, 4], q_pos)
    new_rows = jnp.stack([kv[:, :, 0], kv[:, :, 1], k_sel, kv[:, :, 3]], axis=2)
    new_win = jnp.stack([k_win, kv[:, :, 5]], axis=2)
    if past_rows is None:
        rows, win_all, win_pos0 = new_rows, new_win, past_len
    else:
        rows = jnp.concatenate([past_rows.astype(new_rows.dtype), new_rows], axis=1)
        win_all = jnp.concatenate([win_buf.astype(new_win.dtype), new_win], axis=1)
        win_pos0 = past_len - win_buf.shape[1]
    L = rows.shape[1]
    kc = compress(rows[:, :, 0], cmp_pe[0], cmp_w1[0], cmp_b1[0], cmp_w2[0])
    vc = compress(rows[:, :, 1], cmp_pe[1], cmp_w1[1], cmp_b1[1], cmp_w2[1])
    n_cmp = kc.shape[1]
    cmp_start = jnp.arange(n_cmp) * CMP_STRIDE
    cmp_end = cmp_start + CMP_BLOCK - 1
    n_sblk = -(-L // SEL_BLOCK)
    sel_start = jnp.arange(n_sblk) * SEL_BLOCK
    overlap = ((cmp_start[:, None] < sel_start[None, :] + SEL_BLOCK) & (cmp_end[:, None] >= sel_start[None, :])).astype(F32)
    sel = jnp.pad(rows[:, :, 2:4], ((0, 0), (0, n_sblk * SEL_BLOCK - L), (0, 0), (0, 0), (0, 0)))
    sel = sel.reshape(B, n_sblk, SEL_BLOCK, 2, KV_HEADS, HEAD_DIM).transpose(3, 0, 4, 1, 2, 5)
    shared = (kc, vc, cmp_end, overlap, sel[0], sel[1], win_all[:, :, 0], win_all[:, :, 1], win_pos0)
    new_win_state = win_all[:, -min(WINDOW, win_all.shape[1]):]
    return shared, new_rows, new_win_state


def nsa_mixer(xn, q_pos, shared, w_qg, w_o):
    kc, vc, cmp_end, overlap, kb, vb, kw, vw, win_pos0 = shared
    B, T, _ = xn.shape
    qg = xn @ w_qg
    q = qg[..., :N_HEADS * HEAD_DIM].reshape(B, T, N_HEADS, HEAD_DIM)
    gates = jax.nn.sigmoid(qg[..., N_HEADS * HEAD_DIM:].astype(F32)).reshape(B, T, N_HEADS, 3, 1)
    q_c = q.reshape(B, T, KV_HEADS, GQA_R, HEAD_DIM)
    q_r = rope(q, q_pos).reshape(B, T, KV_HEADS, GQA_R, HEAD_DIM)

    s = jnp.einsum('btgrd,bngd->btgrn', q_c, kc).astype(F32) * ATT_SCALE
    p = masked_softmax(s, (cmp_end[None, :] <= q_pos[:, None])[None, :, None, None, :])
    o_cmp = jnp.einsum('btgrn,bngd->btgrd', p.astype(vc.dtype), vc)

    n_sblk = kb.shape[2]
    imp = jnp.einsum('btgrn,ns->btgs', p, overlap)
    j = jnp.arange(n_sblk)[None, :]
    qblk = (q_pos // SEL_BLOCK)[:, None]
    allowed = j <= qblk
    forced = (j == 0) | ((j > qblk - N_LOCAL_BLOCKS) & allowed)
    score = jnp.where(forced[None, :, None, :], FORCE_SCORE, imp)
    score = jnp.where(allowed[None, :, None, :], score, -jnp.inf)
    top_s, top_i = lax.top_k(score, min(N_SEL, n_sblk))
    valid = jnp.isfinite(top_s)

    bi = jnp.arange(B)[:, None, None, None]
    gi = jnp.arange(KV_HEADS)[None, None, :, None]

    def sel_block(pb, qb, ib, okb):
        kg = kb[bi, gi, ib]
        vg = vb[bi, gi, ib]
        nq, n_k = qb.shape[1], ib.shape[-1]
        sc = jnp.einsum('btgrd,btgkjd->btgrkj', qb, kg).astype(F32) * ATT_SCALE
        tok = ib[..., None] * SEL_BLOCK + jnp.arange(SEL_BLOCK)
        mask = okb[..., None] & (tok <= pb[None, :, None, None, None])
        pr = masked_softmax(sc.reshape(B, nq, KV_HEADS, GQA_R, n_k * SEL_BLOCK), mask.reshape(B, nq, KV_HEADS, 1, n_k * SEL_BLOCK))
        return jnp.einsum('btgrm,btgmd->btgrd', pr.astype(vg.dtype), vg.reshape(B, nq, KV_HEADS, n_k * SEL_BLOCK, HEAD_DIM))

    o_sel = map_query_blocks(sel_block, min(SEL_Q_BLOCK, T), q_pos, q_r, top_i, valid)

    wblk = min(WIN_Q_BLOCK, T)
    span = WINDOW - 1 + wblk
    kp = jnp.pad(kw, ((0, 0), (WINDOW - 1, wblk), (0, 0), (0, 0)))
    vp = jnp.pad(vw, ((0, 0), (WINDOW - 1, wblk), (0, 0), (0, 0)))

    def win_block(pb, qb):
        start = pb[0] - win_pos0
        kblk = lax.dynamic_slice_in_dim(kp, start, span, axis=1)
        vblk = lax.dynamic_slice_in_dim(vp, start, span, axis=1)
        kpos = pb[0] - (WINDOW - 1) + jnp.arange(span)
        mask = (kpos[None, :] <= pb[:, None]) & (kpos[None, :] > pb[:, None] - WINDOW) & (kpos[None, :] >= win_pos0)
        sc = jnp.einsum('btgrd,bsgd->btgrs', qb, kblk).astype(F32) * ATT_SCALE
        pr = masked_softmax(sc, mask[None, :, None, None, :])
        return jnp.einsum('btgrs,bsgd->btgrd', pr.astype(vblk.dtype), vblk)

    o_win = map_query_blocks(win_block, wblk, q_pos, q_r)

    branches = jnp.stack([t.reshape(B, T, N_HEADS, HEAD_DIM) for t in (o_cmp, o_sel, o_win)], axis=3).astype(F32)
    o = jnp.sum(gates * branches, axis=3).astype(xn.dtype).reshape(B, T, N_HEADS * HEAD_DIM)
    return o @ w_o


def run_group(x, past_len, wkv0, shift0, past_rows, win_buf, P):
    B, T, _ = x.shape
    q_pos = past_len + jnp.arange(T)
    h = x
    wkv_new, shift_new = [], []
    shared = new_rows = new_win = None
    for layer in range(DEPTH):
        xn = rmsnorm(h, P['ln_mix'][layer])
        if layer < N_A_LAYERS:
            prev = jnp.concatenate([shift0[layer][:, None].astype(xn.dtype), xn[:, :-1]], axis=1)
            rw = [P[n][layer] for n in ('rw_mu', 'rw_wr', 'rw_wk', 'rw_wv', 'rw_wo', 'rw_w0', 'rw_w1', 'rw_w2',
                                        'rw_a0', 'rw_a1', 'rw_a2', 'rw_g1', 'rw_g2', 'rw_kk', 'rw_ka', 'rw_rk',
                                        'rw_lnw', 'rw_lnb')]
            y, s_T = rwkv7_mixer(xn, prev, wkv0[layer], *rw)
            wkv_new.append(s_T)
            shift_new.append(xn[:, -1])
        else:
            if shared is None:
                shared, new_rows, new_win = shared_kv(h, past_len, past_rows, win_buf, P['ln_kv'], P['w_kv'],
                                                      P['cmp_pe'], P['cmp_w1'], P['cmp_b1'], P['cmp_w2'])
            y = nsa_mixer(xn, q_pos, shared, P['nsa_wqg'][layer - N_A_LAYERS], P['nsa_wo'][layer - N_A_LAYERS])
        h = h + y
        moe = [P[n][layer] for n in ('moe_wrg', 'moe_brg', 'moe_wre', 'moe_bre', 'moe_wgate', 'moe_wup', 'moe_wdown')]
        h = h + hmoe(rmsnorm(h, P['ln_ffn'][layer]), *moe)
    return rmsnorm(h, P['ln_out']), jnp.stack(wkv_new), jnp.stack(shift_new), new_rows, new_win


def setup_inputs(seed: int = 0) -> dict:
    key = jax.random.key(seed)
    ks = iter(jax.random.split(key, 64))
    nrm = lambda shape, scale: jax.random.normal(next(ks), shape, jnp.float32) * scale
    D = D_MODEL
    n_pages = PAST_LEN // PAGE_SIZE
    n_used = DEC_BATCH * n_pages
    n_phys = n_used + max(1, n_used // 4)
    win_buf = min(WINDOW, PAST_LEN)
    page_table = jax.random.permutation(next(ks), n_phys)[:n_used].reshape(DEC_BATCH, n_pages).astype(jnp.int32)
    NA, GE = N_A_LAYERS, N_GROUPS * EXPERTS_PER_GROUP
    return {
        'x_prompt': nrm((BATCH, SEQ, D), 1.0),
        'x_sample': nrm((DEC_BATCH, DEC_SEQ, D), 1.0),
        'state_wkv': nrm((NA, DEC_BATCH, RW_HEADS, RW_HEAD, RW_HEAD), 0.5),
        'state_shift': nrm((NA, DEC_BATCH, D), 1.0),
        'cache_kv': nrm((n_phys, PAGE_SIZE, 4, KV_HEADS, HEAD_DIM), 1.0),
        'cache_win': nrm((DEC_BATCH, win_buf, 2, KV_HEADS, HEAD_DIM), 1.0),
        'page_table': page_table,
        'ln_mix': 1.0 + nrm((DEPTH, D), 0.05),
        'ln_ffn': 1.0 + nrm((DEPTH, D), 0.05),
        'ln_kv': 1.0 + nrm((D,), 0.05),
        'ln_out': 1.0 + nrm((D,), 0.05),
        'rw_mu': jax.random.uniform(next(ks), (NA, 6, D), jnp.float32),
        'rw_wr': nrm((NA, D, D), D ** -0.5),
        'rw_wk': nrm((NA, D, D), D ** -0.5),
        'rw_wv': nrm((NA, D, D), D ** -0.5),
        'rw_wo': nrm((NA, D, D), D ** -0.5),
        'rw_w0': jax.random.uniform(next(ks), (NA, D), jnp.float32, minval=-6.0, maxval=-1.0),
        'rw_w1': nrm((NA, D, RW_DECAY_LORA), D ** -0.5),
        'rw_w2': nrm((NA, RW_DECAY_LORA, D), 0.1 * RW_DECAY_LORA ** -0.5),
        'rw_a0': nrm((NA, D), 0.1),
        'rw_a1': nrm((NA, D, RW_AAA_LORA), D ** -0.5),
        'rw_a2': nrm((NA, RW_AAA_LORA, D), 0.5 * RW_AAA_LORA ** -0.5),
        'rw_g1': nrm((NA, D, RW_GATE_LORA), D ** -0.5),
        'rw_g2': nrm((NA, RW_GATE_LORA, D), RW_GATE_LORA ** -0.5),
        'rw_kk': 0.85 + nrm((NA, D), 0.05),
        'rw_ka': 1.0 + nrm((NA, D), 0.05),
        'rw_rk': nrm((NA, RW_HEADS, RW_HEAD), 0.1),
        'rw_lnw': 1.0 + nrm((NA, D), 0.05),
        'rw_lnb': nrm((NA, D), 0.02),
        'w_kv': nrm((D, N_KV_STREAMS * KV_HEADS * HEAD_DIM), D ** -0.5),
        'cmp_pe': nrm((2, CMP_BLOCK, HEAD_DIM), 0.1),
        'cmp_w1': nrm((2, CMP_BLOCK * HEAD_DIM, CMP_HIDDEN), (CMP_BLOCK * HEAD_DIM) ** -0.5),
        'cmp_b1': nrm((2, CMP_HIDDEN), 0.02),
        'cmp_w2': nrm((2, CMP_HIDDEN, HEAD_DIM), CMP_HIDDEN ** -0.5),
        'nsa_wqg': nrm((N_B_LAYERS, D, N_HEADS * HEAD_DIM + 3 * N_HEADS), D ** -0.5),
        'nsa_wo': nrm((N_B_LAYERS, N_HEADS * HEAD_DIM, D), (N_HEADS * HEAD_DIM) ** -0.5),
        'moe_wrg': nrm((DEPTH, D, N_GROUPS), D ** -0.5),
        'moe_brg': nrm((DEPTH, N_GROUPS), 0.01),
        'moe_wre': nrm((DEPTH, D, GE), D ** -0.5),
        'moe_bre': nrm((DEPTH, GE), 0.01),
        'moe_wgate': nrm((DEPTH, N_GROUPS, EXPERTS_PER_GROUP, D, D_EXPERT), D ** -0.5),
        'moe_wup': nrm((DEPTH, N_GROUPS, EXPERTS_PER_GROUP, D, D_EXPERT), D ** -0.5),
        'moe_wdown': nrm((DEPTH, N_GROUPS, EXPERTS_PER_GROUP, D_EXPERT, D), D_EXPERT ** -0.5),
    }


def reference(x_prompt, x_sample, state_wkv, state_shift, cache_kv, cache_win, page_table,
              ln_mix, ln_ffn, ln_kv, ln_out,
              rw_mu, rw_wr, rw_wk, rw_wv, rw_wo, rw_w0, rw_w1, rw_w2, rw_a0, rw_a1, rw_a2, rw_g1, rw_g2,
              rw_kk, rw_ka, rw_rk, rw_lnw, rw_lnb,
              w_kv, cmp_pe, cmp_w1, cmp_b1, cmp_w2, nsa_wqg, nsa_wo,
              moe_wrg, moe_brg, moe_wre, moe_bre, moe_wgate, moe_wup, moe_wdown):
    P = dict(ln_mix=ln_mix, ln_ffn=ln_ffn, ln_kv=ln_kv, ln_out=ln_out,
             rw_mu=rw_mu, rw_wr=rw_wr, rw_wk=rw_wk, rw_wv=rw_wv, rw_wo=rw_wo, rw_w0=rw_w0, rw_w1=rw_w1,
             rw_w2=rw_w2, rw_a0=rw_a0, rw_a1=rw_a1, rw_a2=rw_a2, rw_g1=rw_g1, rw_g2=rw_g2, rw_kk=rw_kk,
             rw_ka=rw_ka, rw_rk=rw_rk, rw_lnw=rw_lnw, rw_lnb=rw_lnb,
             w_kv=w_kv, cmp_pe=cmp_pe, cmp_w1=cmp_w1, cmp_b1=cmp_b1, cmp_w2=cmp_w2,
             nsa_wqg=nsa_wqg, nsa_wo=nsa_wo,
             moe_wrg=moe_wrg, moe_brg=moe_brg, moe_wre=moe_wre, moe_bre=moe_bre,
             moe_wgate=moe_wgate, moe_wup=moe_wup, moe_wdown=moe_wdown)
    Bp = x_prompt.shape[0]
    wkv0 = jnp.zeros((N_A_LAYERS, Bp, RW_HEADS, RW_HEAD, RW_HEAD), F32)
    shift0 = jnp.zeros((N_A_LAYERS, Bp, D_MODEL), x_prompt.dtype)
    y_prompt, p_wkv, p_shift, p_rows, p_win = run_group(x_prompt, 0, wkv0, shift0, None, None, P)
    Bs = x_sample.shape[0]
    past_len = page_table.shape[1] * cache_kv.shape[1]
    past_rows = cache_kv[page_table].reshape(Bs, past_len, *cache_kv.shape[2:])
    y_sample, s_wkv, s_shift, s_rows, s_win = run_group(x_sample, past_len, state_wkv, state_shift, past_rows, cache_win, P)
    return (y_prompt, y_sample, p_wkv, p_shift, p_rows, p_win, s_wkv, s_shift, s_rows, s_win)
```

```python
import functools

import numpy as np
import jax
import jax.numpy as jnp
from jax import lax
from jax.experimental import pallas as pl
from jax.experimental.pallas import tpu as pltpu

F32 = jnp.float32
BF16 = jnp.bfloat16

D_MODEL = 1024
RMS_EPS = 1e-6
RW_HEAD = 64
RW_HEADS = D_MODEL // RW_HEAD
RW_GN_EPS = 64e-5
N_HEADS = 16
HEAD_DIM = 64
KV_HEADS = 4
GQA_R = N_HEADS // KV_HEADS
ROPE_DIMS = HEAD_DIM // 4
ROPE_THETA = 500000.0
N_KV_STREAMS = 6
CMP_BLOCK = 32
CMP_STRIDE = 16
SEL_BLOCK = 64
N_SEL = 16
N_LOCAL_BLOCKS = 2
FORCE_SCORE = 1.0e4
WINDOW = 512
N_GROUPS = 4
EXPERTS_PER_GROUP = 8
N_EXPERTS = N_GROUPS * EXPERTS_PER_GROUP
D_EXPERT = 256

LANES = 128
VMEM_LIMIT = 48 << 20
NEG_BIG = -1e30
INT_BIG = 2 ** 30


def _round_up(x, m):
    return (x + m - 1) // m * m


def _cparams(sem):
    return pltpu.CompilerParams(dimension_semantics=sem, vmem_limit_bytes=VMEM_LIMIT)


def _mm_kernel(x_ref, w_ref, o_ref):
    o_ref[...] = jnp.dot(x_ref[...].astype(BF16), w_ref[...], preferred_element_type=F32)


def mm(x, w, tm=512):
    m, k = x.shape
    n = w.shape[1]
    tm = min(tm, _round_up(m, 8))
    mp = _round_up(m, tm)
    if mp != m:
        x = jnp.pad(x, ((0, mp - m), (0, 0)))
    out = pl.pallas_call(
        _mm_kernel,
        grid=(mp // tm,),
        in_specs=[pl.BlockSpec((tm, k), lambda i: (i, 0)),
                  pl.BlockSpec((k, n), lambda i: (0, 0))],
        out_specs=pl.BlockSpec((tm, n), lambda i: (i, 0)),
        out_shape=jax.ShapeDtypeStruct((mp, n), F32),
        compiler_params=_cparams(("arbitrary",)),
        name="mm",
    )(x, w.astype(BF16))
    return out[:m]


SCAN_TC = 32


def _scan_kernel(r_ref, d_ref, k_ref, b_ref, n_ref, v_ref, s0_ref, o_ref, st_ref, s_scr, *, tc):
    c = pl.program_id(1)

    @pl.when(c == 0)
    def _():
        s_scr[...] = s0_ref[...]

    def step(t, carry):
        acc = jnp.zeros((RW_HEAD, LANES), F32)
        for j in range(RW_HEAD):
            acc = acc + s_scr[j] * n_ref[t, pl.ds(j, 1), :]
        sa = acc
        vv = v_ref[t]
        out = jnp.zeros((RW_HEAD, LANES), F32)
        for j in range(RW_HEAD):
            sn = (s_scr[j] * d_ref[t, pl.ds(j, 1), :] + sa * b_ref[t, pl.ds(j, 1), :]
                  + vv * k_ref[t, pl.ds(j, 1), :])
            s_scr[j] = sn
            out = out + sn * r_ref[t, pl.ds(j, 1), :]
        o_ref[t] = out
        return carry

    lax.fori_loop(0, tc, step, 0)

    @pl.when(c == pl.num_programs(1) - 1)
    def _():
        st_ref[...] = s_scr[...]


def rwkv_scan(r, d, k, b, nkk, v, s0):
    t, _, l = r.shape
    tc = min(SCAN_TC, t)
    assert t % tc == 0 and l % LANES == 0
    seq = pl.BlockSpec((tc, RW_HEAD, LANES), lambda g, c: (c, 0, g))
    st = pl.BlockSpec((RW_HEAD, RW_HEAD, LANES), lambda g, c: (0, 0, g))
    return pl.pallas_call(
        functools.partial(_scan_kernel, tc=tc),
        grid=(l // LANES, t // tc),
        in_specs=[seq] * 6 + [st],
        out_specs=[seq, st],
        out_shape=[jax.ShapeDtypeStruct((t, RW_HEAD, l), F32),
                   jax.ShapeDtypeStruct((RW_HEAD, RW_HEAD, l), F32)],
        scratch_shapes=[pltpu.VMEM((RW_HEAD, RW_HEAD, LANES), F32)],
        compiler_params=_cparams(("arbitrary", "arbitrary")),
        name="rwkv_scan",
    )(r, d, k, b, nkk, v, s0)


MOE_TM = 256


def _moe_kernel(te_ref, nt_ref, x_ref, c_ref, wgu_ref, wd_ref, o_ref):
    i = pl.program_id(0)

    @pl.when(i < nt_ref[0])
    def _():
        h = jnp.dot(x_ref[...], wgu_ref[0], preferred_element_type=F32)
        g = h[:, :D_EXPERT]
        u = h[:, D_EXPERT:]
        hid = g * jax.nn.sigmoid(g) * u * c_ref[...]
        o_ref[...] = jnp.dot(hid.astype(BF16), wd_ref[0], preferred_element_type=F32)

    @pl.when(i >= nt_ref[0])
    def _():
        o_ref[...] = jnp.zeros_like(o_ref)


def moe_gmm(xs, cs, tile_expert, n_tiles_used, wgu, wd):
    p, dm = xs.shape
    nt = p // MOE_TM

    def row_map(i, te, ntu):
        return (jnp.minimum(i, ntu[0] - 1), 0)

    grid_spec = pltpu.PrefetchScalarGridSpec(
        num_scalar_prefetch=2,
        grid=(nt,),
        in_specs=[pl.BlockSpec((MOE_TM, dm), row_map),
                  pl.BlockSpec((MOE_TM, 1), row_map),
                  pl.BlockSpec((1, dm, 2 * D_EXPERT), lambda i, te, ntu: (te[i], 0, 0)),
                  pl.BlockSpec((1, D_EXPERT, dm), lambda i, te, ntu: (te[i], 0, 0))],
        out_specs=pl.BlockSpec((MOE_TM, dm), lambda i, te, ntu: (i, 0)),
    )
    return pl.pallas_call(
        _moe_kernel,
        grid_spec=grid_spec,
        out_shape=jax.ShapeDtypeStruct((p, dm), F32),
        compiler_params=_cparams(("arbitrary",)),
        name="moe_gmm",
    )(tile_expert, n_tiles_used, xs, cs, wgu, wd)


def hmoe(xn, w_rg, b_rg, w_re, b_re, w_gate, w_up, w_down):
    n, dm = xn.shape
    w_r = jnp.concatenate([w_rg, w_re], axis=1)
    w_r = jnp.pad(w_r, ((0, 0), (0, LANES - w_r.shape[1])))
    logits = mm(xn, w_r)
    lg = logits[:, :N_GROUPS] + b_rg
    le = (logits[:, N_GROUPS:N_GROUPS + N_EXPERTS] + b_re).reshape(n, N_GROUPS, EXPERTS_PER_GROUP)
    g_sel = jnp.argmax(lg, axis=-1)
    p_g = jnp.max(jax.nn.softmax(lg, axis=-1), axis=-1)
    le_g = jnp.take_along_axis(le, g_sel[:, None, None], axis=1)[:, 0]
    top_p, top_i = lax.top_k(jax.nn.softmax(le_g, axis=-1), 2)
    w_top = p_g[:, None] * top_p / jnp.sum(top_p, axis=-1, keepdims=True)
    eid = (g_sel[:, None] * EXPERTS_PER_GROUP + top_i).astype(jnp.int32).reshape(-1)
    cw = w_top.reshape(-1)

    n2 = 2 * n
    onehot = (eid[:, None] == jnp.arange(N_EXPERTS, dtype=jnp.int32)[None, :]).astype(jnp.int32)
    counts = jnp.sum(onehot, axis=0)
    rank = jnp.take_along_axis(jnp.cumsum(onehot, axis=0) - onehot, eid[:, None], axis=1)[:, 0]
    tiles_per = (counts + MOE_TM - 1) // MOE_TM
    tile_end = jnp.cumsum(tiles_per)
    tile_start = tile_end - tiles_per
    pos = tile_start[eid] * MOE_TM + rank
    p_rows = _round_up(n2 + N_EXPERTS * (MOE_TM - 1), MOE_TM)
    nt = p_rows // MOE_TM
    n_used = tile_end[-1].astype(jnp.int32)
    tile_expert = jnp.searchsorted(tile_end, jnp.arange(nt, dtype=jnp.int32), side="right").astype(jnp.int32)
    tile_expert = jnp.minimum(tile_expert, N_EXPERTS - 1)
    last_e = tile_expert[jnp.maximum(n_used - 1, 0)]
    tile_expert = jnp.where(jnp.arange(nt) < n_used, tile_expert, last_e)

    tok = jnp.arange(n2, dtype=jnp.int32) // 2
    row_tok = jnp.zeros((p_rows,), jnp.int32).at[pos].set(tok)
    row_c = jnp.zeros((p_rows,), F32).at[pos].set(cw)
    xs = xn.astype(BF16)[row_tok]
    wgu = jnp.concatenate([w_gate, w_up], axis=-1).reshape(N_EXPERTS, dm, 2 * D_EXPERT).astype(BF16)
    wd = w_down.reshape(N_EXPERTS, D_EXPERT, dm).astype(BF16)
    ys = moe_gmm(xs, row_c[:, None], tile_expert, n_used.reshape(1), wgu, wd)
    pos2 = pos.reshape(n, 2)
    return ys[pos2[:, 0]] + ys[pos2[:, 1]]


def _attn_kernel(lo_ref, hi_ref, q_ref, k_ref, v_ref, qpos_ref, kpos_ref, *rest, window, use_sel):
    if use_sel:
        sel_ref, e_ref, o_ref, m_scr, l_scr, acc_scr = rest
    else:
        o_ref, m_scr, l_scr, acc_scr = rest
    qi = pl.program_id(1)
    kt = pl.program_id(2)

    @pl.when(kt == 0)
    def _():
        m_scr[...] = jnp.full_like(m_scr, NEG_BIG)
        l_scr[...] = jnp.zeros_like(l_scr)
        acc_scr[...] = jnp.zeros_like(acc_scr)

    @pl.when((kt >= lo_ref[qi]) & (kt <= hi_ref[qi]))
    def _():
        q = q_ref[0] * HEAD_DIM ** -0.5
        s = lax.dot_general(q, k_ref[0], (((1,), (1,)), ((), ())), preferred_element_type=F32)
        qp = qpos_ref[...]
        kp = kpos_ref[0]
        mask = kp <= qp
        if window is not None:
            mask = mask & (kp > qp - window)
        if use_sel:
            selm = jnp.dot(sel_ref[0], e_ref[...], preferred_element_type=F32)
            mask = mask & (selm > 0.5)
        s = jnp.where(mask, s, NEG_BIG)
        m_prev = m_scr[...]
        m_new = jnp.maximum(m_prev, jnp.max(s, axis=-1, keepdims=True))
        p = jnp.where(mask, jnp.exp(s - m_new), 0.0)
        alpha = jnp.exp(m_prev - m_new)
        l_scr[...] = alpha * l_scr[...] + jnp.sum(p, axis=-1, keepdims=True)
        acc_scr[...] = alpha * acc_scr[...] + jnp.dot(p.astype(BF16), v_ref[0], preferred_element_type=F32)
        m_scr[...] = m_new

    @pl.when(kt == pl.num_programs(2) - 1)
    def _():
        o_ref[0] = acc_scr[...] / jnp.maximum(l_scr[...], 1e-30)


def attention(q, k, v, qpos, kpos, lo, hi, tq, tk, window=None, sel=None, emat=None, q_rep=1):
    bg, rq, hd = q.shape
    l = k.shape[1]
    nq, nk = rq // tq, l // tk
    nq_inner = nq // q_rep
    kp_b = kpos.shape[0]

    def kmap(b, qi, kt, lo_r, hi_r):
        return (b, jnp.clip(kt, lo_r[qi], hi_r[qi]), 0)

    in_specs = [pl.BlockSpec((1, tq, hd), lambda b, qi, kt, lo_r, hi_r: (b, qi, 0)),
                pl.BlockSpec((1, tk, hd), kmap),
                pl.BlockSpec((1, tk, hd), kmap),
                pl.BlockSpec((tq, 1), lambda b, qi, kt, lo_r, hi_r: (qi % nq_inner, 0)),
                pl.BlockSpec((1, 1, tk), lambda b, qi, kt, lo_r, hi_r:
                             (b if kp_b > 1 else 0, 0, jnp.clip(kt, lo_r[qi], hi_r[qi])))]
    args = [q, k, v, qpos, kpos]
    if sel is not None:
        nb = sel.shape[-1]
        in_specs += [pl.BlockSpec((1, tq, nb), lambda b, qi, kt, lo_r, hi_r: (b, qi % nq_inner, 0)),
                     pl.BlockSpec((nb, tk), lambda b, qi, kt, lo_r, hi_r:
                                  (0, jnp.clip(kt, lo_r[qi], hi_r[qi])))]
        args += [sel, emat]
    grid_spec = pltpu.PrefetchScalarGridSpec(
        num_scalar_prefetch=2,
        grid=(bg, nq, nk),
        in_specs=in_specs,
        out_specs=pl.BlockSpec((1, tq, hd), lambda b, qi, kt, lo_r, hi_r: (b, qi, 0)),
        scratch_shapes=[pltpu.VMEM((tq, 1), F32), pltpu.VMEM((tq, 1), F32), pltpu.VMEM((tq, hd), F32)],
    )
    return pl.pallas_call(
        functools.partial(_attn_kernel, window=window, use_sel=sel is not None),
        grid_spec=grid_spec,
        out_shape=jax.ShapeDtypeStruct((bg, rq, hd), F32),
        compiler_params=_cparams(("arbitrary", "arbitrary", "arbitrary")),
        name="attn_sel" if sel is not None else "attn_win",
    )(lo, hi, *args)


def _cmp_kernel(q_ref, k_ref, v_ref, qpos_ref, cend_ref, ov_ref, o_ref, imp_ref):
    q = q_ref[0] * HEAD_DIM ** -0.5
    s = lax.dot_general(q, k_ref[0], (((1,), (1,)), ((), ())), preferred_element_type=F32)
    mask = cend_ref[...] <= qpos_ref[...]
    s = jnp.where(mask, s, NEG_BIG)
    m = jnp.max(s, axis=-1, keepdims=True)
    e = jnp.where(mask, jnp.exp(s - m), 0.0)
    p = (e / jnp.maximum(jnp.sum(e, axis=-1, keepdims=True), 1e-30)).astype(BF16)
    o_ref[0] = jnp.dot(p, v_ref[0], preferred_element_type=F32)
    imp_ref[0] = jnp.dot(p, ov_ref[...], preferred_element_type=F32)


def cmp_attention(q, kc, vc, qpos, cend, ov, tq, q_rep):
    bg, rq, hd = q.shape
    nc = kc.shape[1]
    nb = ov.shape[1]
    nq = rq // tq
    nq_inner = nq // q_rep
    return pl.pallas_call(
        _cmp_kernel,
        grid=(bg, nq),
        in_specs=[pl.BlockSpec((1, tq, hd), lambda b, qi: (b, qi, 0)),
                  pl.BlockSpec((1, nc, hd), lambda b, qi: (b, 0, 0)),
                  pl.BlockSpec((1, nc, hd), lambda b, qi: (b, 0, 0)),
                  pl.BlockSpec((tq, 1), lambda b, qi: (qi % nq_inner, 0)),
                  pl.BlockSpec((1, nc), lambda b, qi: (0, 0)),
                  pl.BlockSpec((nc, nb), lambda b, qi: (0, 0))],
        out_specs=[pl.BlockSpec((1, tq, hd), lambda b, qi: (b, qi, 0)),
                   pl.BlockSpec((1, tq, nb), lambda b, qi: (b, qi, 0))],
        out_shape=[jax.ShapeDtypeStruct((bg, rq, hd), F32),
                   jax.ShapeDtypeStruct((bg, rq, nb), F32)],
        compiler_params=_cparams(("arbitrary", "arbitrary")),
        name="attn_cmp",
    )(q, kc, vc, qpos, cend, ov)


def _page_kernel(pt_ref, c_ref, o_ref):
    o_ref[0] = c_ref[...]


def gather_pages(cache2, page_table, width):
    n_b, n_pages = page_table.shape
    page = cache2.shape[1]
    grid_spec = pltpu.PrefetchScalarGridSpec(
        num_scalar_prefetch=1,
        grid=(n_b, n_pages),
        in_specs=[pl.BlockSpec((1, page, width), lambda b, p, pt: (pt[b * n_pages + p], 0, 0))],
        out_specs=pl.BlockSpec((1, 1, page, width), lambda b, p, pt: (b, p, 0, 0)),
    )
    return pl.pallas_call(
        _page_kernel,
        grid_spec=grid_spec,
        out_shape=jax.ShapeDtypeStruct((n_b, n_pages, page, width), cache2.dtype),
        compiler_params=_cparams(("arbitrary", "arbitrary")),
        name="gather_pages",
    )(page_table.reshape(-1), cache2)


def rmsnorm(x, g):
    y = x * lax.rsqrt(jnp.mean(x * x, axis=-1, keepdims=True) + RMS_EPS)
    return y * g


def rope(x, pos):
    half = ROPE_DIMS // 2
    inv = jnp.power(ROPE_THETA, -jnp.arange(half, dtype=F32) * (2.0 / ROPE_DIMS))
    ang = pos.astype(F32)[:, None] * inv[None, :]
    cos = jnp.cos(ang)[:, None, :]
    sin = jnp.sin(ang)[:, None, :]
    x1 = x[..., :half]
    x2 = x[..., half:ROPE_DIMS]
    return jnp.concatenate([x1 * cos - x2 * sin, x2 * cos + x1 * sin, x[..., ROPE_DIMS:]], axis=-1)


def compress(sub, pe, w1, b1, w2, n_b, nsub):
    half = CMP_STRIDE * HEAD_DIM
    w1cat = jnp.concatenate([w1[:half], w1[half:]], axis=1)
    hidden = w1.shape[1]
    part = mm(sub, w1cat).reshape(n_b, nsub, KV_HEADS, 2, hidden)
    pe_term = jnp.einsum('jd,jdc->c', pe, w1.reshape(CMP_BLOCK, HEAD_DIM, hidden)) + b1
    pre = part[:, :nsub - 1, :, 0] + part[:, 1:, :, 1] + pe_term
    act = jax.nn.gelu(pre)
    return mm(act.reshape(-1, hidden), w2).reshape(n_b, nsub - 1, KV_HEADS, HEAD_DIM)


def overlap_matrix(n_cmp, n_sblk, nc_pad, nb_pad):
    cs = np.arange(n_cmp) * CMP_STRIDE
    ce = cs + CMP_BLOCK - 1
    ss = np.arange(n_sblk) * SEL_BLOCK
    ov = ((cs[:, None] < ss[None, :] + SEL_BLOCK) & (ce[:, None] >= ss[None, :])).astype(np.float32)
    out = np.zeros((nc_pad, nb_pad), np.float32)
    out[:n_cmp, :n_sblk] = ov
    cend = np.full((1, nc_pad), INT_BIG, np.int32)
    cend[0, :n_cmp] = ce
    return jnp.asarray(out, BF16), jnp.asarray(cend)


def block_scores(imp, q_pos, n_sblk):
    j = jnp.arange(n_sblk)[None, :]
    qblk = (q_pos // SEL_BLOCK)[:, None]
    allowed = j <= qblk
    forced = (j == 0) | ((j > qblk - N_LOCAL_BLOCKS) & allowed)
    score = jnp.where(forced, FORCE_SCORE, imp)
    return jnp.where(allowed, score, -jnp.inf), allowed


def nsa_prompt(qg, kv, n_b, t, cmp_pe, cmp_w1, cmp_b1, cmp_w2):
    pos = jnp.arange(t)
    q = qg[:, :N_HEADS * HEAD_DIM].reshape(n_b, t, N_HEADS, HEAD_DIM)
    gates = jax.nn.sigmoid(qg[:, N_HEADS * HEAD_DIM:]).reshape(n_b, t, N_HEADS, 3)
    kv = kv.reshape(n_b, t, N_KV_STREAMS, KV_HEADS, HEAD_DIM)
    k_sel = rope(kv[:, :, 2], pos)
    k_win = rope(kv[:, :, 4], pos)
    new_rows = jnp.stack([kv[:, :, 0], kv[:, :, 1], k_sel, kv[:, :, 3]], axis=2)
    new_win = jnp.stack([k_win, kv[:, :, 5]], axis=2)

    nsub = t // CMP_STRIDE
    n_cmp = nsub - 1
    n_sblk = t // SEL_BLOCK

    def sub_blocks(rows):
        return rows.reshape(n_b, nsub, CMP_STRIDE, KV_HEADS, HEAD_DIM).transpose(0, 1, 3, 2, 4).reshape(-1, CMP_STRIDE * HEAD_DIM)

    kc = compress(sub_blocks(kv[:, :, 0]), cmp_pe[0], cmp_w1[0], cmp_b1[0], cmp_w2[0], n_b, nsub)
    vc = compress(sub_blocks(kv[:, :, 1]), cmp_pe[1], cmp_w1[1], cmp_b1[1], cmp_w2[1], n_b, nsub)

    bg = n_b * KV_HEADS

    def q_rows(x):
        return x.reshape(n_b, t, KV_HEADS, GQA_R, HEAD_DIM).transpose(0, 2, 3, 1, 4).reshape(bg, GQA_R * t, HEAD_DIM).astype(BF16)

    def kv_rows(x, pad_to=None):
        y = x.transpose(0, 2, 1, 3).reshape(bg, x.shape[1], HEAD_DIM).astype(BF16)
        if pad_to is not None and pad_to != y.shape[1]:
            y = jnp.pad(y, ((0, 0), (0, pad_to - y.shape[1]), (0, 0)))
        return y

    tq = 256
    qpos = pos.astype(jnp.int32)[:, None]
    nc_pad = _round_up(n_cmp, LANES)
    ov, cend = overlap_matrix(n_cmp, n_sblk, nc_pad, n_sblk)
    q_c = q_rows(q)
    o_cmp, imp = cmp_attention(q_c, kv_rows(kc, nc_pad), kv_rows(vc, nc_pad), qpos, cend, ov, tq, GQA_R)
    imp = imp.reshape(bg, GQA_R, t, n_sblk).sum(axis=1)
    score, allowed = block_scores(imp, pos, n_sblk)
    jj = jnp.arange(n_sblk)
    s_a = score[..., :, None]
    s_b = score[..., None, :]
    beats = (s_b > s_a) | ((s_b == s_a) & (jj[None, :] < jj[:, None]))
    rank = jnp.sum(beats, axis=-1)
    sel01 = ((rank < N_SEL) & allowed).astype(BF16)

    q_r = q_rows(rope(q, pos))
    tk = 256
    nq_inner = t // tq
    t0 = (np.arange(GQA_R * nq_inner) % nq_inner) * tq
    hi = (t0 + tq - 1) // tk
    lo_sel = np.zeros_like(hi)
    lo_win = np.maximum(t0 - (WINDOW - 1), 0) // tk
    kpos = pos.astype(jnp.int32)[None, None, :]
    emat = jnp.asarray((np.arange(t)[None, :] // SEL_BLOCK == np.arange(n_sblk)[:, None]).astype(np.float32), BF16)
    o_sel = attention(q_r, kv_rows(k_sel), kv_rows(kv[:, :, 3]), qpos, kpos,
                      jnp.asarray(lo_sel, jnp.int32), jnp.asarray(hi, jnp.int32), tq, tk,
                      sel=sel01, emat=emat, q_rep=GQA_R)
    o_win = attention(q_r, kv_rows(k_win), kv_rows(kv[:, :, 5]), qpos, kpos,
                      jnp.asarray(lo_win, jnp.int32), jnp.asarray(hi, jnp.int32), tq, tk,
                      window=WINDOW, q_rep=GQA_R)

    def heads(o):
        return o.reshape(n_b, KV_HEADS, GQA_R, t, HEAD_DIM).transpose(0, 3, 1, 2, 4).reshape(n_b, t, N_HEADS, HEAD_DIM)

    o = (gates[..., 0:1] * heads(o_cmp) + gates[..., 1:2] * heads(o_sel) + gates[..., 2:3] * heads(o_win))
    return o.reshape(n_b * t, N_HEADS * HEAD_DIM), new_rows, new_win


def nsa_sample(qg, kv, cache_kv, cache_win, page_table, cmp_pe, cmp_w1, cmp_b1, cmp_w2):
    n_b, n_pages = page_table.shape
    n_phys, page = cache_kv.shape[:2]
    past = n_pages * page
    pos = jnp.full((1,), past, jnp.int32)
    q = qg[:, :N_HEADS * HEAD_DIM].reshape(n_b, 1, N_HEADS, HEAD_DIM)
    gates = jax.nn.sigmoid(qg[:, N_HEADS * HEAD_DIM:]).reshape(n_b, N_HEADS, 3)
    kv = kv.reshape(n_b, 1, N_KV_STREAMS, KV_HEADS, HEAD_DIM)
    k_sel = rope(kv[:, :, 2], pos)
    k_win = rope(kv[:, :, 4], pos)
    new_rows = jnp.stack([kv[:, :, 0], kv[:, :, 1], k_sel, kv[:, :, 3]], axis=2)
    new_win = jnp.stack([k_win, kv[:, :, 5]], axis=2)
    win_all = jnp.concatenate([cache_win, new_win], axis=1)
    win_state = win_all[:, -min(WINDOW, win_all.shape[1]):]

    l_tot = past + 1
    nsub = l_tot // CMP_STRIDE
    n_cmp = nsub - 1
    n_sblk = -(-l_tot // SEL_BLOCK)
    row_w = 4 * KV_HEADS * HEAD_DIM
    cmp_w = 2 * KV_HEADS * HEAD_DIM
    pages = gather_pages(cache_kv.reshape(n_phys, page, row_w), page_table, cmp_w)
    sub = pages.reshape(n_b, nsub, CMP_STRIDE, 2, KV_HEADS, HEAD_DIM).transpose(3, 0, 1, 4, 2, 5)
    sub = sub.reshape(2, n_b * nsub * KV_HEADS, CMP_STRIDE * HEAD_DIM)
    kc = compress(sub[0], cmp_pe[0], cmp_w1[0], cmp_b1[0], cmp_w2[0], n_b, nsub)
    vc = compress(sub[1], cmp_pe[1], cmp_w1[1], cmp_b1[1], cmp_w2[1], n_b, nsub)

    bg = n_b * KV_HEADS
    rq = 16

    def q_rows(x):
        y = x.reshape(bg, GQA_R, HEAD_DIM)
        return jnp.pad(y, ((0, 0), (0, rq - GQA_R), (0, 0))).astype(BF16)

    def kv_rows(x, pad_to=None):
        y = x.transpose(0, 2, 1, 3).reshape(bg, x.shape[1], HEAD_DIM).astype(BF16)
        if pad_to is not None and pad_to != y.shape[1]:
            y = jnp.pad(y, ((0, 0), (0, pad_to - y.shape[1]), (0, 0)))
        return y

    qpos = jnp.full((rq, 1), past, jnp.int32)
    nc_pad = _round_up(n_cmp, LANES)
    nb_pad = _round_up(n_sblk, 8)
    ov, cend = overlap_matrix(n_cmp, n_sblk, nc_pad, nb_pad)
    o_cmp, imp = cmp_attention(q_rows(q), kv_rows(kc, nc_pad), kv_rows(vc, nc_pad), qpos, cend, ov, rq, 1)
    imp = imp[:, :GQA_R, :n_sblk].sum(axis=1)
    score, _ = block_scores(imp[:, None, :], pos, n_sblk)
    top_s, top_i = lax.top_k(score[:, 0], min(N_SEL, n_sblk))
    valid = jnp.isfinite(top_s)

    n_past_blk = past // SEL_BLOCK
    blk_per_page = page // SEL_BLOCK
    ib = top_i.reshape(n_b, KV_HEADS, -1)
    ibc = jnp.minimum(ib, n_past_blk - 1)
    phys = jnp.take_along_axis(page_table, (ibc // blk_per_page).reshape(n_b, -1), axis=1).reshape(ib.shape)
    cache6 = cache_kv.reshape(n_phys, blk_per_page, SEL_BLOCK, 4, KV_HEADS, HEAD_DIM)
    gidx = jnp.arange(KV_HEADS)[None, :, None]
    kg = cache6[phys, ibc % blk_per_page, :, 2, gidx, :]
    vg = cache6[phys, ibc % blk_per_page, :, 3, gidx, :]
    is_new = (ib >= n_past_blk)[..., None, None]
    first = (jnp.arange(SEL_BLOCK) == 0)[None, None, None, :, None]
    new_k = jnp.where(first, new_rows[:, 0, 2][:, :, None, None, :], 0.0)
    new_v = jnp.where(first, new_rows[:, 0, 3][:, :, None, None, :], 0.0)
    kg = jnp.where(is_new, new_k, kg)
    vg = jnp.where(is_new, new_v, vg)
    n_keys = ib.shape[-1] * SEL_BLOCK
    tok = ib[..., None] * SEL_BLOCK + jnp.arange(SEL_BLOCK)
    kpos_sel = jnp.where(valid.reshape(ib.shape)[..., None], tok, INT_BIG).reshape(bg, 1, n_keys).astype(jnp.int32)
    zero = jnp.zeros((1,), jnp.int32)
    q_r = q_rows(rope(q, pos))
    o_sel = attention(q_r, kg.reshape(bg, n_keys, HEAD_DIM).astype(BF16), vg.reshape(bg, n_keys, HEAD_DIM).astype(BF16),
                      qpos, kpos_sel, zero, zero, rq, n_keys)

    n_win = win_state.shape[1]
    kpos_win = (past - n_win + 1 + jnp.arange(n_win, dtype=jnp.int32))[None, None, :]
    o_win = attention(q_r, kv_rows(win_state[:, :, 0]), kv_rows(win_state[:, :, 1]),
                      qpos, kpos_win, zero, zero, rq, n_win, window=WINDOW)

    def heads(o):
        return o[:, :GQA_R].reshape(n_b, N_HEADS, HEAD_DIM)

    o = gates[..., 0:1] * heads(o_cmp) + gates[..., 1:2] * heads(o_sel) + gates[..., 2:3] * heads(o_win)
    return o.reshape(n_b, N_HEADS * HEAD_DIM), new_rows, win_state


def rwkv_layer(xn, prev, n_p, t_p, n_s, state_wkv, mu, w_r, w_k, w_v, w_o, w0, w1, w2, a0, a1, a2, g1, g2,
               k_k, k_a, r_k, gn_w, gn_b):
    n_tok = xn.shape[0]
    xx = prev - xn
    xr, xw, xk, xv, xa, xg = (xn + xx * mu[i] for i in range(6))
    r = mm(xr, w_r)
    k = mm(xk, w_k)
    v = mm(xv, w_v)
    w_log = -jax.nn.softplus(-(w0 + mm(jnp.tanh(mm(xw, w1)), w2))) - 0.5
    a = jax.nn.sigmoid(a0 + mm(mm(xa, a1), a2))
    g = mm(jax.nn.sigmoid(mm(xg, g1)), g2)
    hs = lambda x: x.reshape(n_tok, RW_HEADS, RW_HEAD)
    kk = hs(k * k_k)
    kk = (kk / jnp.maximum(jnp.sqrt(jnp.sum(kk * kk, axis=-1, keepdims=True)), 1e-12)).reshape(n_tok, D_MODEL)
    k2 = k * (1.0 + (a - 1.0) * k_a)
    decay = jnp.exp(-jnp.exp(w_log))
    bvec = kk * a

    np_rows = n_p * t_p

    def lanes_p(x):
        return x[:np_rows].reshape(n_p, t_p, RW_HEADS, RW_HEAD).transpose(1, 3, 0, 2).reshape(t_p, RW_HEAD, n_p * RW_HEADS)

    def lanes_s(x):
        return x[np_rows:np_rows + n_s].reshape(n_s, 1, RW_HEADS, RW_HEAD).transpose(1, 3, 0, 2).reshape(1, RW_HEAD, n_s * RW_HEADS)

    seqs = (r, decay, k2, bvec, -kk, v)
    s0_p = jnp.zeros((RW_HEAD, RW_HEAD, n_p * RW_HEADS), F32)
    o_p, st_p = rwkv_scan(*[lanes_p(x) for x in seqs], s0_p)
    s0_s = state_wkv.transpose(3, 2, 0, 1).reshape(RW_HEAD, RW_HEAD, n_s * RW_HEADS)
    o_s, st_s = rwkv_scan(*[lanes_s(x) for x in seqs], s0_s)
    o_p = o_p.reshape(t_p, RW_HEAD, n_p, RW_HEADS).transpose(2, 0, 3, 1).reshape(np_rows, RW_HEADS, RW_HEAD)
    o_s = o_s.reshape(1, RW_HEAD, n_s, RW_HEADS).transpose(2, 0, 3, 1).reshape(n_s, RW_HEADS, RW_HEAD)
    o = jnp.concatenate([o_p, o_s, jnp.zeros((n_tok - np_rows - n_s, RW_HEADS, RW_HEAD), F32)], axis=0)
    wkv_p = st_p.reshape(RW_HEAD, RW_HEAD, n_p, RW_HEADS).transpose(2, 3, 1, 0)
    wkv_s = st_s.reshape(RW_HEAD, RW_HEAD, n_s, RW_HEADS).transpose(2, 3, 1, 0)

    mean = jnp.mean(o, axis=-1, keepdims=True)
    var = jnp.mean(jnp.square(o - mean), axis=-1, keepdims=True)
    o = ((o - mean) * lax.rsqrt(var + RW_GN_EPS)).reshape(n_tok, D_MODEL) * gn_w + gn_b
    bonus = jnp.sum(hs(r) * hs(k2) * r_k, axis=-1, keepdims=True) * hs(v)
    o = o + bonus.reshape(n_tok, D_MODEL)
    return mm(o * g, w_o), wkv_p, wkv_s


def kernel(x_prompt, x_sample, state_wkv, state_shift, cache_kv, cache_win, page_table, ln_mix, ln_ffn, ln_kv, ln_out, rw_mu, rw_wr, rw_wk, rw_wv, rw_wo, rw_w0, rw_w1, rw_w2, rw_a0, rw_a1, rw_a2, rw_g1, rw_g2, rw_kk, rw_ka, rw_rk, rw_lnw, rw_lnb, w_kv, cmp_pe, cmp_w1, cmp_b1, cmp_w2, nsa_wqg, nsa_wo, moe_wrg, moe_brg, moe_wre, moe_bre, moe_wgate, moe_wup, moe_wdown):
    n_p, t_p, dm = x_prompt.shape
    n_s = x_sample.shape[0]
    assert x_sample.shape[1] == 1
    np_rows = n_p * t_p
    n_real = np_rows + n_s
    n_tok = _round_up(n_real, 256)
    h = jnp.concatenate([x_prompt.reshape(np_rows, dm), x_sample.reshape(n_s, dm),
                         jnp.zeros((n_tok - n_real, dm), F32)], axis=0)

    def moe(hh, layer):
        return hmoe(rmsnorm(hh, ln_ffn[layer]), moe_wrg[layer], moe_brg[layer], moe_wre[layer], moe_bre[layer],
                    moe_wgate[layer], moe_wup[layer], moe_wdown[layer])

    xn = rmsnorm(h, ln_mix[0])
    xn_p = xn[:np_rows].reshape(n_p, t_p, dm)
    prev_p = jnp.concatenate([jnp.zeros((n_p, 1, dm), F32), xn_p[:, :-1]], axis=1).reshape(np_rows, dm)
    prev = jnp.concatenate([prev_p, state_shift[0], jnp.zeros((n_tok - n_real, dm), F32)], axis=0)
    y, wkv_p, wkv_s = rwkv_layer(xn, prev, n_p, t_p, n_s, state_wkv[0], rw_mu[0], rw_wr[0], rw_wk[0], rw_wv[0],
                                 rw_wo[0], rw_w0[0], rw_w1[0], rw_w2[0], rw_a0[0], rw_a1[0], rw_a2[0],
                                 rw_g1[0], rw_g2[0], rw_kk[0], rw_ka[0], rw_rk[0], rw_lnw[0], rw_lnb[0])
    p_shift = xn_p[:, -1][None]
    s_shift = xn[np_rows:n_real][None]
    h = h + y
    h = h + moe(h, 0)

    xn = rmsnorm(h, ln_mix[1])
    hk = rmsnorm(h, ln_kv)
    qg = mm(xn, nsa_wqg[0])
    kv = mm(hk, w_kv)
    o_p, p_rows, p_win_all = nsa_prompt(qg[:np_rows], kv[:np_rows], n_p, t_p, cmp_pe, cmp_w1, cmp_b1, cmp_w2)
    o_s, s_rows, s_win = nsa_sample(qg[np_rows:n_real], kv[np_rows:n_real], cache_kv, cache_win, page_table,
                                    cmp_pe, cmp_w1, cmp_b1, cmp_w2)
    o = jnp.concatenate([o_p, o_s, jnp.zeros((n_tok - n_real, dm), F32)], axis=0)
    h = h + mm(o, nsa_wo[0])
    h = h + moe(h, 1)

    y = rmsnorm(h, ln_out)
    y_prompt = y[:np_rows].reshape(n_p, t_p, dm)
    y_sample = y[np_rows:n_real].reshape(n_s, 1, dm)
    p_win = p_win_all[:, -min(WINDOW, t_p):]
    return (y_prompt, y_sample, wkv_p[None], p_shift, p_rows, p_win, wkv_s[None], s_shift, s_rows, s_win)
```

```python
import functools

import numpy as np
import jax
import jax.numpy as jnp
from jax import lax
from jax.experimental import pallas as pl
from jax.experimental.pallas import tpu as pltpu

F32 = jnp.float32
BF16 = jnp.bfloat16

D_MODEL = 1024
RMS_EPS = 1e-6
RW_HEAD = 64
RW_HEADS = D_MODEL // RW_HEAD
RW_GN_EPS = 64e-5
N_HEADS = 16
HEAD_DIM = 64
KV_HEADS = 4
GQA_R = N_HEADS // KV_HEADS
ROPE_DIMS = HEAD_DIM // 4
ROPE_THETA = 500000.0
N_KV_STREAMS = 6
CMP_BLOCK = 32
CMP_STRIDE = 16
SEL_BLOCK = 64
N_SEL = 16
N_LOCAL_BLOCKS = 2
FORCE_SCORE = 1.0e4
WINDOW = 512
N_GROUPS = 4
EXPERTS_PER_GROUP = 8
N_EXPERTS = N_GROUPS * EXPERTS_PER_GROUP
D_EXPERT = 256

LANES = 128
VMEM_LIMIT = 48 << 20
NEG_BIG = -1e30
ATT_SCALE = HEAD_DIM ** -0.5
TOKEN_TILE = 512


def _round_up(x, m):
    return (x + m - 1) // m * m


def _cparams(sem):
    return pltpu.CompilerParams(dimension_semantics=sem, vmem_limit_bytes=VMEM_LIMIT)


_NT = (((1,), (1,)), ((), ()))


def _mm_kernel(x_ref, w_ref, o_ref):
    o_ref[...] = jnp.dot(x_ref[...].astype(BF16), w_ref[...], preferred_element_type=F32)


def mm(x, w, tm=TOKEN_TILE):
    m, k = x.shape
    n = w.shape[1]
    tm = min(tm, _round_up(m, 8))
    mp = _round_up(m, tm)
    if mp != m:
        x = jnp.pad(x, ((0, mp - m), (0, 0)))
    out = pl.pallas_call(
        _mm_kernel,
        grid=(mp // tm,),
        in_specs=[pl.BlockSpec((tm, k), lambda i: (i, 0)),
                  pl.BlockSpec((k, n), lambda i: (0, 0))],
        out_specs=pl.BlockSpec((tm, n), lambda i: (i, 0)),
        out_shape=jax.ShapeDtypeStruct((mp, n), F32),
        compiler_params=_cparams(("arbitrary",)),
        name="mm",
    )(x, w.astype(BF16))
    return out[:m] if mp != m else out


SCAN_TC = 32


def _scan_kernel(r_ref, d_ref, k_ref, b_ref, n_ref, v_ref, s0_ref, o_ref, st_ref, s_scr, *, tc, dot_dtype):
    c = pl.program_id(1)
    rnd = lambda x: x.astype(dot_dtype).astype(F32)

    @pl.when(c == 0)
    def _():
        s_scr[...] = s0_ref[...]

    def step(t, carry):
        acc = jnp.zeros((RW_HEAD, LANES), F32)
        for j in range(RW_HEAD):
            acc = acc + rnd(s_scr[j]) * rnd(n_ref[t, pl.ds(j, 1), :])
        sa = acc
        vv = v_ref[t]
        out = jnp.zeros((RW_HEAD, LANES), F32)
        for j in range(RW_HEAD):
            sn = (s_scr[j] * d_ref[t, pl.ds(j, 1), :] + sa * b_ref[t, pl.ds(j, 1), :]
                  + vv * k_ref[t, pl.ds(j, 1), :])
            s_scr[j] = sn
            out = out + rnd(sn) * rnd(r_ref[t, pl.ds(j, 1), :])
        o_ref[t] = out
        return carry

    lax.fori_loop(0, tc, step, 0)

    @pl.when(c == pl.num_programs(1) - 1)
    def _():
        st_ref[...] = s_scr[...]


def rwkv_scan(r, d, k, b, nkk, v, s0, dot_dtype=F32):
    t, _, l = r.shape
    tc = min(SCAN_TC, t)
    assert t % tc == 0 and l % LANES == 0
    seq = pl.BlockSpec((tc, RW_HEAD, LANES), lambda g, c: (c, 0, g))
    st = pl.BlockSpec((RW_HEAD, RW_HEAD, LANES), lambda g, c: (0, 0, g))
    return pl.pallas_call(
        functools.partial(_scan_kernel, tc=tc, dot_dtype=dot_dtype),
        grid=(l // LANES, t // tc),
        in_specs=[seq] * 6 + [st],
        out_specs=[seq, st],
        out_shape=[jax.ShapeDtypeStruct((t, RW_HEAD, l), F32),
                   jax.ShapeDtypeStruct((RW_HEAD, RW_HEAD, l), F32)],
        scratch_shapes=[pltpu.VMEM((RW_HEAD, RW_HEAD, LANES), F32)],
        compiler_params=_cparams(("arbitrary", "arbitrary")),
        name="rwkv_scan",
    )(r, d, k, b, nkk, v, s0)


MOE_TM = 256


def _moe_kernel(te_ref, nt_ref, x_ref, c_ref, wgu_ref, wd_ref, o_ref):
    i = pl.program_id(0)

    @pl.when(i < nt_ref[0])
    def _():
        h = jnp.dot(x_ref[...], wgu_ref[0], preferred_element_type=F32)
        g = h[:, :D_EXPERT]
        u = h[:, D_EXPERT:]
        hid = g * jax.nn.sigmoid(g) * u * c_ref[...]
        o_ref[...] = jnp.dot(hid.astype(BF16), wd_ref[0], preferred_element_type=F32)

    @pl.when(i >= nt_ref[0])
    def _():
        o_ref[...] = jnp.zeros_like(o_ref)


def moe_gmm(xs, cs, tile_expert, n_tiles_used, wgu, wd):
    p, dm = xs.shape
    nt = p // MOE_TM

    def row_map(i, te, ntu):
        return (jnp.minimum(i, ntu[0] - 1), 0)

    grid_spec = pltpu.PrefetchScalarGridSpec(
        num_scalar_prefetch=2,
        grid=(nt,),
        in_specs=[pl.BlockSpec((MOE_TM, dm), row_map),
                  pl.BlockSpec((MOE_TM, 1), row_map),
                  pl.BlockSpec((1, dm, 2 * D_EXPERT), lambda i, te, ntu: (te[i], 0, 0)),
                  pl.BlockSpec((1, D_EXPERT, dm), lambda i, te, ntu: (te[i], 0, 0))],
        out_specs=pl.BlockSpec((MOE_TM, dm), lambda i, te, ntu: (i, 0)),
    )
    return pl.pallas_call(
        _moe_kernel,
        grid_spec=grid_spec,
        out_shape=jax.ShapeDtypeStruct((p, dm), F32),
        compiler_params=_cparams(("arbitrary",)),
        name="moe_gmm",
    )(tile_expert, n_tiles_used, xs, cs, wgu, wd)


def hmoe(xn, w_rg, b_rg, w_re, b_re, w_gate, w_up, w_down):
    n, dm = xn.shape
    w_r = jnp.concatenate([w_rg, w_re], axis=1)
    w_r = jnp.pad(w_r, ((0, 0), (0, LANES - w_r.shape[1])))
    logits = mm(xn, w_r)
    lg = logits[:, :N_GROUPS] + b_rg
    le = (logits[:, N_GROUPS:N_GROUPS + N_EXPERTS] + b_re).reshape(n, N_GROUPS, EXPERTS_PER_GROUP)
    g_sel = jnp.argmax(lg, axis=-1)
    p_g = jnp.max(jax.nn.softmax(lg, axis=-1), axis=-1)
    le_g = jnp.take_along_axis(le, g_sel[:, None, None], axis=1)[:, 0]
    top_p, top_i = lax.top_k(jax.nn.softmax(le_g, axis=-1), 2)
    w_top = p_g[:, None] * top_p / jnp.sum(top_p, axis=-1, keepdims=True)
    eid = (g_sel[:, None] * EXPERTS_PER_GROUP + top_i).astype(jnp.int32).reshape(-1)
    cw = w_top.reshape(-1)

    n2 = 2 * n
    onehot = (eid[:, None] == jnp.arange(N_EXPERTS, dtype=jnp.int32)[None, :]).astype(jnp.int32)
    counts = jnp.sum(onehot, axis=0)
    rank = jnp.take_along_axis(jnp.cumsum(onehot, axis=0) - onehot, eid[:, None], axis=1)[:, 0]
    tiles_per = (counts + MOE_TM - 1) // MOE_TM
    tile_end = jnp.cumsum(tiles_per)
    tile_start = tile_end - tiles_per
    pos = tile_start[eid] * MOE_TM + rank
    p_rows = _round_up(n2 + N_EXPERTS * (MOE_TM - 1), MOE_TM)
    nt = p_rows // MOE_TM
    n_used = tile_end[-1].astype(jnp.int32)
    tile_ids = jnp.arange(nt, dtype=jnp.int32)
    tile_expert = jnp.sum((tile_end[None, :] <= jnp.minimum(tile_ids, n_used - 1)[:, None]).astype(jnp.int32), axis=1)

    tok = jnp.arange(n2, dtype=jnp.int32) // 2
    row_tok = jnp.zeros((p_rows,), jnp.int32).at[pos].set(tok)
    row_c = jnp.zeros((p_rows,), F32).at[pos].set(cw)
    xs = xn.astype(BF16)[row_tok]
    wgu = jnp.concatenate([w_gate, w_up], axis=-1).reshape(N_EXPERTS, dm, 2 * D_EXPERT).astype(BF16)
    wd = w_down.reshape(N_EXPERTS, D_EXPERT, dm).astype(BF16)
    ys = moe_gmm(xs, row_c[:, None], tile_expert, n_used.reshape(1), wgu, wd)
    pos2 = pos.reshape(n, 2)
    return ys[pos2[:, 0]] + ys[pos2[:, 1]]


ATT_TQ = 256
ATT_TK = 256


def _cmp_sel_kernel(q_ref, kc_ref, vct_ref, ovt_ref, o_ref, sel_ref, *, n_cmp, n_sblk, tq):
    t0 = pl.program_id(2) * tq
    ncp = kc_ref.shape[2]
    n_io = lax.broadcasted_iota(jnp.int32, (ncp, tq), 0)
    t_io = lax.broadcasted_iota(jnp.int32, (ncp, tq), 1) + t0
    mask = (n_io * CMP_STRIDE + (CMP_BLOCK - 1) <= t_io) & (n_io < n_cmp)
    kc = kc_ref[0, 0]
    vct = vct_ref[0, 0]
    ovt = ovt_ref[...]
    imp = jnp.zeros((n_sblk, tq), F32)
    outs = []
    for r in range(GQA_R):
        qr = (q_ref[:, r * HEAD_DIM:(r + 1) * HEAD_DIM] * ATT_SCALE).astype(BF16)
        st = lax.dot_general(kc, qr, _NT, preferred_element_type=F32)
        st = jnp.where(mask, st, NEG_BIG)
        m = jnp.max(st, axis=0, keepdims=True)
        e = jnp.where(mask, jnp.exp(st - m), 0.0)
        p = (e / jnp.maximum(jnp.sum(e, axis=0, keepdims=True), 1e-30)).astype(BF16)
        outs.append(jnp.dot(vct, p, preferred_element_type=F32).T)
        imp = imp + jnp.dot(ovt, p, preferred_element_type=F32)
    o_ref[...] = jnp.concatenate(outs, axis=1)

    j_io = lax.broadcasted_iota(jnp.int32, (n_sblk, tq), 0)
    qblk = jnp.right_shift(lax.broadcasted_iota(jnp.int32, (n_sblk, tq), 1) + t0, SEL_BLOCK.bit_length() - 1)
    allowed = j_io <= qblk
    forced = (j_io == 0) | ((j_io > qblk - N_LOCAL_BLOCKS) & allowed)
    score = jnp.where(forced, FORCE_SCORE, imp)
    score = jnp.where(allowed, score, -jnp.inf)
    rank = jnp.zeros((n_sblk, tq), jnp.int32)
    for jp in range(n_sblk):
        row = score[jp:jp + 1, :]
        beats = (row > score) | ((row == score) & (j_io > jp))
        rank = rank + jnp.where(beats, 1, 0)
    sel = (rank < N_SEL) & allowed
    sel_ref[0, 0] = jnp.where(sel, 1.0, 0.0).astype(BF16)


def cmp_select(q, kc, vct, ovt, n_b, t, n_cmp, n_sblk):
    assert SEL_BLOCK & (SEL_BLOCK - 1) == 0
    tq = ATT_TQ
    nq = t // tq
    ncp = kc.shape[2]
    gw = GQA_R * HEAD_DIM
    return pl.pallas_call(
        functools.partial(_cmp_sel_kernel, n_cmp=n_cmp, n_sblk=n_sblk, tq=tq),
        grid=(n_b, KV_HEADS, nq),
        in_specs=[pl.BlockSpec((tq, gw), lambda b, g, qi: (b * nq + qi, g)),
                  pl.BlockSpec((1, 1, ncp, HEAD_DIM), lambda b, g, qi: (b, g, 0, 0)),
                  pl.BlockSpec((1, 1, HEAD_DIM, ncp), lambda b, g, qi: (b, g, 0, 0)),
                  pl.BlockSpec((n_sblk, ncp), lambda b, g, qi: (0, 0))],
        out_specs=[pl.BlockSpec((tq, gw), lambda b, g, qi: (b * nq + qi, g)),
                   pl.BlockSpec((1, 1, n_sblk, tq), lambda b, g, qi: (b, g, 0, qi))],
        out_shape=[jax.ShapeDtypeStruct((n_b * t, N_HEADS * HEAD_DIM), F32),
                   jax.ShapeDtypeStruct((n_b, KV_HEADS, n_sblk, t), BF16)],
        compiler_params=_cparams(("arbitrary", "arbitrary", "arbitrary")),
        name="attn_cmp_select",
    )(q, kc, vct, ovt)


def _flash_kernel(pq_ref, pk_ref, pfirst_ref, plast_ref, q_ref, k_ref, vt_ref, *rest, window, use_sel, tq, tk):
    if use_sel:
        sel_ref, et_ref, o_ref, m_scr, l_scr, acc_scr = rest
    else:
        o_ref, m_scr, l_scr, acc_scr = rest
    p_id = pl.program_id(2)

    @pl.when(pfirst_ref[p_id] == 1)
    def _():
        m_scr[...] = jnp.full_like(m_scr, NEG_BIG)
        l_scr[...] = jnp.zeros_like(l_scr)
        acc_scr[...] = jnp.zeros_like(acc_scr)

    off = pq_ref[p_id] * tq - pk_ref[p_id] * tk
    diff = lax.broadcasted_iota(jnp.int32, (tk, tq), 0) - lax.broadcasted_iota(jnp.int32, (tk, tq), 1)
    mask = diff <= off
    if window is not None:
        mask = mask & (diff > off - window)
    if use_sel:
        selm = jnp.dot(et_ref[...], sel_ref[0, 0], preferred_element_type=F32)
        mask = mask & (selm > 0.5)
    k = k_ref[0, 0]
    vt = vt_ref[0, 0]
    for r in range(GQA_R):
        qr = (q_ref[:, r * HEAD_DIM:(r + 1) * HEAD_DIM] * ATT_SCALE).astype(BF16)
        st = lax.dot_general(k, qr, _NT, preferred_element_type=F32)
        st = jnp.where(mask, st, NEG_BIG)
        m_prev = m_scr[r]
        m_new = jnp.maximum(m_prev, jnp.max(st, axis=0, keepdims=True))
        p = jnp.where(mask, jnp.exp(st - m_new), 0.0)
        alpha = jnp.exp(m_prev - m_new)
        l_scr[r] = alpha * l_scr[r] + jnp.sum(p, axis=0, keepdims=True)
        acc_scr[r] = alpha * acc_scr[r] + jnp.dot(vt, p.astype(BF16), preferred_element_type=F32)
        m_scr[r] = m_new

    @pl.when(plast_ref[p_id] == 1)
    def _():
        outs = [(acc_scr[r] / jnp.maximum(l_scr[r], 1e-30)).T for r in range(GQA_R)]
        o_ref[...] = jnp.concatenate(outs, axis=1)


def flash_attention(q, k, vt, n_b, t, window=None, sel=None):
    tq, tk = ATT_TQ, ATT_TK
    nq = t // tq
    back = 0 if window is None else -(-(window - 1) // tk)
    pairs = [(qi, kt) for qi in range(nq) for kt in range(0 if window is None else max(0, qi - back), qi * tq // tk + 1)]
    pq = np.array([p[0] for p in pairs], np.int32)
    pk = np.array([p[1] for p in pairs], np.int32)
    pfirst = np.array([1 if i == 0 or pairs[i - 1][0] != p[0] else 0 for i, p in enumerate(pairs)], np.int32)
    plast = np.array([1 if i == len(pairs) - 1 or pairs[i + 1][0] != p[0] else 0 for i, p in enumerate(pairs)], np.int32)
    gw = GQA_R * HEAD_DIM

    in_specs = [pl.BlockSpec((tq, gw), lambda b, g, p, pq_r, pk_r, f_r, l_r: (b * nq + pq_r[p], g)),
                pl.BlockSpec((1, 1, tk, HEAD_DIM), lambda b, g, p, pq_r, pk_r, f_r, l_r: (b, g, pk_r[p], 0)),
                pl.BlockSpec((1, 1, HEAD_DIM, tk), lambda b, g, p, pq_r, pk_r, f_r, l_r: (b, g, 0, pk_r[p]))]
    args = [q, k, vt]
    if sel is not None:
        n_sblk = sel.shape[2]
        emat = (np.arange(t)[:, None] // SEL_BLOCK == np.arange(n_sblk)[None, :]).astype(np.float32)
        in_specs += [pl.BlockSpec((1, 1, n_sblk, tq), lambda b, g, p, pq_r, pk_r, f_r, l_r: (b, g, 0, pq_r[p])),
                     pl.BlockSpec((tk, n_sblk), lambda b, g, p, pq_r, pk_r, f_r, l_r: (pk_r[p], 0))]
        args += [sel, jnp.asarray(emat, BF16)]
    grid_spec = pltpu.PrefetchScalarGridSpec(
        num_scalar_prefetch=4,
        grid=(n_b, KV_HEADS, len(pairs)),
        in_specs=in_specs,
        out_specs=pl.BlockSpec((tq, gw), lambda b, g, p, pq_r, pk_r, f_r, l_r: (b * nq + pq_r[p], g)),
        scratch_shapes=[pltpu.VMEM((GQA_R, 1, tq), F32), pltpu.VMEM((GQA_R, 1, tq), F32),
                        pltpu.VMEM((GQA_R, HEAD_DIM, tq), F32)],
    )
    return pl.pallas_call(
        functools.partial(_flash_kernel, window=window, use_sel=sel is not None, tq=tq, tk=tk),
        grid_spec=grid_spec,
        out_shape=jax.ShapeDtypeStruct((n_b * t, N_HEADS * HEAD_DIM), F32),
        compiler_params=_cparams(("arbitrary", "arbitrary", "arbitrary")),
        name="attn_sel" if sel is not None else "attn_win",
    )(jnp.asarray(pq), jnp.asarray(pk), jnp.asarray(pfirst), jnp.asarray(plast), *args)


def _cmp_kernel(q_ref, k_ref, v_ref, ov_ref, o_ref, imp_ref, *, n_cmp):
    q = q_ref[0] * ATT_SCALE
    s = lax.dot_general(q, k_ref[0], _NT, preferred_element_type=F32)
    mask = lax.broadcasted_iota(jnp.int32, s.shape, 1) < n_cmp
    s = jnp.where(mask, s, NEG_BIG)
    m = jnp.max(s, axis=-1, keepdims=True)
    e = jnp.where(mask, jnp.exp(s - m), 0.0)
    p = (e / jnp.maximum(jnp.sum(e, axis=-1, keepdims=True), 1e-30)).astype(BF16)
    o_ref[0] = jnp.dot(p, v_ref[0], preferred_element_type=F32)
    imp_ref[0] = jnp.dot(p, ov_ref[...], preferred_element_type=F32)


def cmp_attention_sample(q, kc, vc, ov, n_cmp):
    bg, rq, hd = q.shape
    nc = kc.shape[1]
    nb = ov.shape[1]
    return pl.pallas_call(
        functools.partial(_cmp_kernel, n_cmp=n_cmp),
        grid=(bg,),
        in_specs=[pl.BlockSpec((1, rq, hd), lambda b: (b, 0, 0)),
                  pl.BlockSpec((1, nc, hd), lambda b: (b, 0, 0)),
                  pl.BlockSpec((1, nc, hd), lambda b: (b, 0, 0)),
                  pl.BlockSpec((nc, nb), lambda b: (0, 0))],
        out_specs=[pl.BlockSpec((1, rq, hd), lambda b: (b, 0, 0)),
                   pl.BlockSpec((1, rq, nb), lambda b: (b, 0, 0))],
        out_shape=[jax.ShapeDtypeStruct((bg, rq, hd), F32),
                   jax.ShapeDtypeStruct((bg, rq, nb), F32)],
        compiler_params=_cparams(("arbitrary",)),
        name="attn_cmp_sample",
    )(q, kc, vc, ov)


def _page_compress_kernel(pt_ref, c_ref, wbd_ref, pe_ref, w2_ref, o_ref, a_scr, *, n_pages, page):
    p = pl.program_id(1)
    nsub = n_pages * page // CMP_STRIDE
    for s in range(2):
        for gp in range(KV_HEADS // 2):
            x = c_ref[0, s, 2 * gp:2 * gp + 2].reshape(2 * HEAD_DIM, page)
            a_scr[s, gp, pl.ds(pl.multiple_of(p * page, page), page), :] = x.T

    @pl.when(p == n_pages - 1)
    def _():
        hid = pe_ref.shape[-1]
        for s in range(2):
            for gp in range(KV_HEADS // 2):
                acc = jnp.zeros((nsub, 4 * hid), F32)
                for j in range(CMP_STRIDE):
                    lhs = a_scr[s, gp, pl.ds(j, nsub, stride=CMP_STRIDE), :].astype(BF16)
                    acc = acc + jnp.dot(lhs, wbd_ref[s, j], preferred_element_type=F32)
                for g2 in range(2):
                    first = acc[:, 2 * g2 * hid:(2 * g2 + 1) * hid]
                    second = acc[:, (2 * g2 + 1) * hid:(2 * g2 + 2) * hid]
                    pre = first + pltpu.roll(second, nsub - 1, 0) + pe_ref[s]
                    act = jax.nn.gelu(pre).astype(BF16)
                    o_ref[s, 0, 2 * gp + g2] = jnp.dot(act, w2_ref[s], preferred_element_type=F32).astype(BF16)


def page_compress(cache_t, page_table, cmp_pe, cmp_w1, cmp_b1, cmp_w2):
    n_b, n_pages = page_table.shape
    page = cache_t.shape[-1]
    nsub = n_pages * page // CMP_STRIDE
    hid = cmp_w1.shape[-1]
    w1r = cmp_w1.reshape(2, 2, CMP_STRIDE, HEAD_DIM, hid).transpose(0, 2, 3, 1, 4)
    w1r = w1r.reshape(2, CMP_STRIDE, HEAD_DIM, 2 * hid)
    zeros = jnp.zeros_like(w1r)
    wbd = jnp.concatenate([jnp.concatenate([w1r, zeros], axis=-1), jnp.concatenate([zeros, w1r], axis=-1)], axis=2)
    pe_term = (jnp.einsum('sjd,sjdc->sc', cmp_pe, cmp_w1.reshape(2, CMP_BLOCK, HEAD_DIM, hid)) + cmp_b1)[:, None, :]
    grid_spec = pltpu.PrefetchScalarGridSpec(
        num_scalar_prefetch=1,
        grid=(n_b, n_pages),
        in_specs=[pl.BlockSpec((1, 2, KV_HEADS, HEAD_DIM, page), lambda b, p, pt: (pt[b * n_pages + p], 0, 0, 0, 0)),
                  pl.BlockSpec((2, CMP_STRIDE, 2 * HEAD_DIM, 4 * hid), lambda b, p, pt: (0, 0, 0, 0)),
                  pl.BlockSpec((2, 1, hid), lambda b, p, pt: (0, 0, 0)),
                  pl.BlockSpec((2, hid, HEAD_DIM), lambda b, p, pt: (0, 0, 0))],
        out_specs=pl.BlockSpec((2, 1, KV_HEADS, nsub, HEAD_DIM), lambda b, p, pt: (0, b, 0, 0, 0)),
        scratch_shapes=[pltpu.VMEM((2, KV_HEADS // 2, n_pages * page, 2 * HEAD_DIM), F32)],
    )
    return pl.pallas_call(
        functools.partial(_page_compress_kernel, n_pages=n_pages, page=page),
        grid_spec=grid_spec,
        out_shape=jax.ShapeDtypeStruct((2, n_b, KV_HEADS, nsub, HEAD_DIM), BF16),
        compiler_params=_cparams(("arbitrary", "arbitrary")),
        name="page_compress",
    )(page_table.reshape(-1), cache_t, wbd.astype(BF16), pe_term, cmp_w2.astype(BF16))


def _decode_attn_kernel(phys_ref, blk_ref, flag_ref, q_ref, kn_ref, vn_ref, kt_ref, vt_ref, o_ref,
                        m_scr, l_scr, acc_scr, *, mode, past, steps, window, page):
    idx = pl.program_id(0) * KV_HEADS + pl.program_id(1)
    s_id = pl.program_id(2)
    q = q_ref[0, 0] * ATT_SCALE

    @pl.when(s_id == 0)
    def _():
        kn = kn_ref[0, 0].astype(BF16).astype(F32)
        vn = vn_ref[0, 0].astype(BF16).astype(F32)
        s_new = jnp.sum(q.astype(F32) * kn, axis=-1, keepdims=True)
        on = flag_ref[idx] == 1
        m_scr[...] = jnp.where(on, s_new, NEG_BIG)
        l_scr[...] = jnp.where(on, jnp.ones_like(s_new), 0.0)
        acc_scr[...] = jnp.where(on, jnp.broadcast_to(vn, acc_scr.shape), 0.0)

    kt = kt_ref[0, 0, 0].astype(BF16)
    vt = vt_ref[0, 0, 0].astype(BF16)
    tile = kt.shape[-1]
    sc = jnp.dot(q, kt, preferred_element_type=F32)
    lane = lax.broadcasted_iota(jnp.int32, sc.shape, 1)
    if mode == "sel":
        blk = blk_ref[idx * steps + s_id]
        per_page = page // SEL_BLOCK
        pos = (blk // per_page) * page + lane
        mask = (jnp.right_shift(pos, SEL_BLOCK.bit_length() - 1) == blk) & (pos <= past) & (blk >= 0)
    else:
        pos = past - tile + lane
        mask = (pos > past - window) & (pos <= past)
    sc = jnp.where(mask, sc, NEG_BIG)
    m_prev = m_scr[...]
    m_new = jnp.maximum(m_prev, jnp.max(sc, axis=-1, keepdims=True))
    p = jnp.where(mask, jnp.exp(sc - m_new), 0.0)
    alpha = jnp.exp(m_prev - m_new)
    l_scr[...] = alpha * l_scr[...] + jnp.sum(p, axis=-1, keepdims=True)
    acc_scr[...] = alpha * acc_scr[...] + lax.dot_general(p.astype(BF16), vt, _NT, preferred_element_type=F32)
    m_scr[...] = m_new

    @pl.when(s_id == steps - 1)
    def _():
        o_ref[0, 0] = acc_scr[...] / jnp.maximum(l_scr[...], 1e-30)


def decode_attention(q, k_new, v_new, kv_t, streams, phys, blk, flag, mode, past, steps, window=None, page=None):
    n_b, n_g, rows, hd = q.shape
    tile = kv_t.shape[-1]
    ks, vs = streams

    def kmap(st):
        if mode == "sel":
            return lambda b, g, s, ph, bl, fl: (ph[(b * n_g + g) * steps + s], st, g, 0, 0)
        return lambda b, g, s, ph, bl, fl: (b, st, g, 0, 0)

    grid_spec = pltpu.PrefetchScalarGridSpec(
        num_scalar_prefetch=3,
        grid=(n_b, n_g, steps),
        in_specs=[pl.BlockSpec((1, 1, rows, hd), lambda b, g, s, ph, bl, fl: (b, g, 0, 0)),
                  pl.BlockSpec((1, 1, 1, hd), lambda b, g, s, ph, bl, fl: (b, g, 0, 0)),
                  pl.BlockSpec((1, 1, 1, hd), lambda b, g, s, ph, bl, fl: (b, g, 0, 0)),
                  pl.BlockSpec((1, 1, 1, hd, tile), kmap(ks)),
                  pl.BlockSpec((1, 1, 1, hd, tile), kmap(vs))],
        out_specs=pl.BlockSpec((1, 1, rows, hd), lambda b, g, s, ph, bl, fl: (b, g, 0, 0)),
        scratch_shapes=[pltpu.VMEM((rows, 1), F32), pltpu.VMEM((rows, 1), F32), pltpu.VMEM((rows, hd), F32)],
    )
    return pl.pallas_call(
        functools.partial(_decode_attn_kernel, mode=mode, past=past, steps=steps, window=window, page=page),
        grid_spec=grid_spec,
        out_shape=jax.ShapeDtypeStruct((n_b, n_g, rows, hd), F32),
        compiler_params=_cparams(("arbitrary", "arbitrary", "arbitrary")),
        name="attn_decode_" + mode,
    )(phys, blk, flag, q, k_new, v_new, kv_t, kv_t)


def rmsnorm(x, g):
    y = x * lax.rsqrt(jnp.mean(x * x, axis=-1, keepdims=True) + RMS_EPS)
    return y * g


def rope(x, pos):
    half = ROPE_DIMS // 2
    inv = jnp.power(ROPE_THETA, -jnp.arange(half, dtype=F32) * (2.0 / ROPE_DIMS))
    ang = pos.astype(F32)[:, None] * inv[None, :]
    cos = jnp.cos(ang)[:, None, :]
    sin = jnp.sin(ang)[:, None, :]
    x1 = x[..., :half]
    x2 = x[..., half:ROPE_DIMS]
    return jnp.concatenate([x1 * cos - x2 * sin, x2 * cos + x1 * sin, x[..., ROPE_DIMS:]], axis=-1)


def compress(sub, pe, w1, b1, w2, n_b, nsub):
    half = CMP_STRIDE * HEAD_DIM
    w1cat = jnp.concatenate([w1[:half], w1[half:]], axis=1)
    hidden = w1.shape[1]
    part = mm(sub, w1cat).reshape(n_b, nsub, KV_HEADS, 2, hidden)
    pe_term = jnp.einsum('jd,jdc->c', pe, w1.reshape(CMP_BLOCK, HEAD_DIM, hidden)) + b1
    pre = part[:, :nsub - 1, :, 0] + part[:, 1:, :, 1] + pe_term
    act = jax.nn.gelu(pre)
    return mm(act.reshape(-1, hidden), w2).reshape(n_b, nsub - 1, KV_HEADS, HEAD_DIM)


def overlap_matrix(n_cmp, n_sblk, nc_pad, nb_pad):
    cs = np.arange(n_cmp) * CMP_STRIDE
    ce = cs + CMP_BLOCK - 1
    ss = np.arange(n_sblk) * SEL_BLOCK
    ov = ((cs[:, None] < ss[None, :] + SEL_BLOCK) & (ce[:, None] >= ss[None, :])).astype(np.float32)
    out = np.zeros((nc_pad, nb_pad), np.float32)
    out[:n_cmp, :n_sblk] = ov
    return out


def block_scores(imp, q_pos, n_sblk):
    j = jnp.arange(n_sblk)[None, :]
    qblk = (q_pos // SEL_BLOCK)[:, None]
    allowed = j <= qblk
    forced = (j == 0) | ((j > qblk - N_LOCAL_BLOCKS) & allowed)
    score = jnp.where(forced, FORCE_SCORE, imp)
    return jnp.where(allowed, score, -jnp.inf)


def nsa_prompt(qg, kv, n_b, t, cmp_pe, cmp_w1, cmp_b1, cmp_w2):
    n = n_b * t
    pos = jnp.arange(t)
    gates = jax.nn.sigmoid(qg[:n, N_HEADS * HEAD_DIM:]).reshape(n, N_HEADS, 3)
    kv = kv[:n].reshape(n_b, t, N_KV_STREAMS, KV_HEADS, HEAD_DIM)
    k_sel = rope(kv[:, :, 2], pos)
    k_win = rope(kv[:, :, 4], pos)
    new_rows = jnp.stack([kv[:, :, 0], kv[:, :, 1], k_sel, kv[:, :, 3]], axis=2)
    new_win = jnp.stack([k_win, kv[:, :, 5]], axis=2)

    nsub = t // CMP_STRIDE
    n_cmp = nsub - 1
    n_sblk = t // SEL_BLOCK

    def sub_blocks(rows):
        return rows.reshape(n_b, nsub, CMP_STRIDE, KV_HEADS, HEAD_DIM).transpose(0, 1, 3, 2, 4).reshape(-1, CMP_STRIDE * HEAD_DIM)

    kc = compress(sub_blocks(kv[:, :, 0]), cmp_pe[0], cmp_w1[0], cmp_b1[0], cmp_w2[0], n_b, nsub)
    vc = compress(sub_blocks(kv[:, :, 1]), cmp_pe[1], cmp_w1[1], cmp_b1[1], cmp_w2[1], n_b, nsub)
    nc_pad = _round_up(n_cmp, LANES)
    kc = jnp.pad(kc, ((0, 0), (0, nc_pad - n_cmp), (0, 0), (0, 0)))
    vc = jnp.pad(vc, ((0, 0), (0, nc_pad - n_cmp), (0, 0), (0, 0)))
    ovt = jnp.asarray(overlap_matrix(n_cmp, n_sblk, nc_pad, n_sblk).T, BF16)
    o_cmp, sel = cmp_select(qg, kc.transpose(0, 2, 1, 3).astype(BF16), vc.transpose(0, 2, 3, 1).astype(BF16),
                            ovt, n_b, t, n_cmp, n_sblk)

    q_r = rope(qg[:n, :N_HEADS * HEAD_DIM].reshape(n_b, t, N_HEADS, HEAD_DIM), pos).reshape(n, N_HEADS * HEAD_DIM)
    rows_k = lambda x: x.transpose(0, 2, 1, 3).astype(BF16)
    rows_vt = lambda x: x.transpose(0, 2, 3, 1).astype(BF16)
    o_sel = flash_attention(q_r, rows_k(k_sel), rows_vt(kv[:, :, 3]), n_b, t, sel=sel)
    o_win = flash_attention(q_r, rows_k(k_win), rows_vt(kv[:, :, 5]), n_b, t, window=WINDOW)

    heads = lambda o: o.reshape(n, N_HEADS, HEAD_DIM)
    o = gates[..., 0:1] * heads(o_cmp) + gates[..., 1:2] * heads(o_sel) + gates[..., 2:3] * heads(o_win)
    return o.reshape(n, N_HEADS * HEAD_DIM), new_rows, new_win


def nsa_sample(qg, kv, cache_kv, cache_win, page_table, cmp_pe, cmp_w1, cmp_b1, cmp_w2):
    n_b, n_pages = page_table.shape
    n_phys, page = cache_kv.shape[:2]
    past = n_pages * page
    pos = jnp.full((1,), past, jnp.int32)
    q = qg[:, :N_HEADS * HEAD_DIM].reshape(n_b, 1, N_HEADS, HEAD_DIM)
    gates = jax.nn.sigmoid(qg[:, N_HEADS * HEAD_DIM:]).reshape(n_b, N_HEADS, 3)
    kv = kv.reshape(n_b, 1, N_KV_STREAMS, KV_HEADS, HEAD_DIM)
    k_sel = rope(kv[:, :, 2], pos)
    k_win = rope(kv[:, :, 4], pos)
    new_rows = jnp.stack([kv[:, :, 0], kv[:, :, 1], k_sel, kv[:, :, 3]], axis=2)
    new_win = jnp.stack([k_win, kv[:, :, 5]], axis=2)
    n_keep = min(WINDOW, cache_win.shape[1] + 1)
    win_state = jnp.concatenate([cache_win[:, cache_win.shape[1] + 1 - n_keep:], new_win], axis=1)

    cache_t = cache_kv.transpose(0, 2, 3, 4, 1)
    win_t = cache_win.transpose(0, 2, 3, 4, 1)

    l_tot = past + 1
    nsub = l_tot // CMP_STRIDE
    n_cmp = nsub - 1
    n_sblk = -(-l_tot // SEL_BLOCK)
    assert nsub * CMP_STRIDE == past and page % SEL_BLOCK == 0 and cache_win.shape[1] == WINDOW
    kvc = page_compress(cache_t, page_table, cmp_pe, cmp_w1, cmp_b1, cmp_w2)

    bg = n_b * KV_HEADS
    rq = 16
    qpad = lambda x: jnp.pad(x.reshape(n_b, KV_HEADS, GQA_R, HEAD_DIM), ((0, 0), (0, 0), (0, rq - GQA_R), (0, 0))).astype(BF16)

    nb_pad = _round_up(n_sblk, 8)
    ov = jnp.asarray(overlap_matrix(n_cmp, n_sblk, nsub, nb_pad), BF16)
    o_cmp, imp = cmp_attention_sample(qpad(q).reshape(bg, rq, HEAD_DIM), kvc[0].reshape(bg, nsub, HEAD_DIM),
                                      kvc[1].reshape(bg, nsub, HEAD_DIM), ov, n_cmp)
    imp = imp[:, :GQA_R, :n_sblk].sum(axis=1)
    score = block_scores(imp[:, None, :], pos, n_sblk)
    top_s, top_i = lax.top_k(score[:, 0], min(N_SEL, n_sblk))
    valid = jnp.isfinite(top_s)

    n_past_blk = past // SEL_BLOCK
    per_page = page // SEL_BLOCK
    n_k = top_i.shape[-1]
    in_past = valid & (top_i < n_past_blk)
    blk = jnp.where(in_past, top_i, -1).astype(jnp.int32)
    logical = jnp.clip(top_i, 0, n_past_blk - 1) // per_page
    phys = jnp.take_along_axis(page_table, logical.reshape(n_b, KV_HEADS * n_k), axis=1).reshape(bg, n_k)
    phys = jnp.where(in_past, phys, 0).astype(jnp.int32)
    new_flag = jnp.any(valid & (top_i >= n_past_blk), axis=-1).astype(jnp.int32)
    q_r = qpad(rope(q, pos))
    o_sel = decode_attention(q_r, new_rows[:, 0, 2][:, :, None, :], new_rows[:, 0, 3][:, :, None, :], cache_t, (2, 3),
                             phys.reshape(-1), blk.reshape(-1), new_flag, "sel", past, n_k, page=page)
    zeros = jnp.zeros((bg,), jnp.int32)
    o_win = decode_attention(q_r, new_win[:, 0, 0][:, :, None, :], new_win[:, 0, 1][:, :, None, :], win_t, (0, 1),
                             zeros, zeros, jnp.ones((bg,), jnp.int32), "win", past, 1, window=WINDOW)

    heads = lambda o: o.reshape(n_b, KV_HEADS, -1, HEAD_DIM)[:, :, :GQA_R].reshape(n_b, N_HEADS, HEAD_DIM)
    o = gates[..., 0:1] * heads(o_cmp) + gates[..., 1:2] * heads(o_sel) + gates[..., 2:3] * heads(o_win)
    return o.reshape(n_b, N_HEADS * HEAD_DIM), new_rows, win_state


def rwkv_layer(xn, prev, n_p, t_p, n_s, state_wkv, mu, w_r, w_k, w_v, w_o, w0, w1, w2, a0, a1, a2, g1, g2,
               k_k, k_a, r_k, gn_w, gn_b):
    n_tok = xn.shape[0]
    xx = prev - xn
    xr, xw, xk, xv, xa, xg = (xn + xx * mu[i] for i in range(6))
    r = mm(xr, w_r)
    k = mm(xk, w_k)
    v = mm(xv, w_v)
    w_log = -jax.nn.softplus(-(w0 + mm(jnp.tanh(mm(xw, w1)), w2))) - 0.5
    a = jax.nn.sigmoid(a0 + mm(mm(xa, a1), a2))
    g = mm(jax.nn.sigmoid(mm(xg, g1)), g2)
    hs = lambda x: x.reshape(n_tok, RW_HEADS, RW_HEAD)
    kk = hs(k * k_k)
    kk = (kk / jnp.maximum(jnp.sqrt(jnp.sum(kk * kk, axis=-1, keepdims=True)), 1e-12)).reshape(n_tok, D_MODEL)
    k2 = k * (1.0 + (a - 1.0) * k_a)
    decay = jnp.exp(-jnp.exp(w_log))
    bvec = kk * a

    np_rows = n_p * t_p

    def lanes_p(x):
        return x[:np_rows].reshape(n_p, t_p, RW_HEADS, RW_HEAD).transpose(1, 3, 0, 2).reshape(t_p, RW_HEAD, n_p * RW_HEADS)

    def lanes_s(x):
        return x[np_rows:np_rows + n_s].reshape(n_s, 1, RW_HEADS, RW_HEAD).transpose(1, 3, 0, 2).reshape(1, RW_HEAD, n_s * RW_HEADS)

    seqs = (r, decay, k2, bvec, -kk, v)
    s0_p = jnp.zeros((RW_HEAD, RW_HEAD, n_p * RW_HEADS), F32)
    o_p, st_p = rwkv_scan(*[lanes_p(x) for x in seqs], s0_p)
    s0_s = state_wkv.transpose(3, 2, 0, 1).reshape(RW_HEAD, RW_HEAD, n_s * RW_HEADS)
    o_s, st_s = rwkv_scan(*[lanes_s(x) for x in seqs], s0_s, dot_dtype=BF16)
    o_p = o_p.reshape(t_p, RW_HEAD, n_p, RW_HEADS).transpose(2, 0, 3, 1).reshape(np_rows, RW_HEADS, RW_HEAD)
    o_s = o_s.reshape(1, RW_HEAD, n_s, RW_HEADS).transpose(2, 0, 3, 1).reshape(n_s, RW_HEADS, RW_HEAD)
    o = jnp.concatenate([o_p, o_s, jnp.zeros((n_tok - np_rows - n_s, RW_HEADS, RW_HEAD), F32)], axis=0)
    wkv_p = st_p.reshape(RW_HEAD, RW_HEAD, n_p, RW_HEADS).transpose(2, 3, 1, 0)
    wkv_s = st_s.reshape(RW_HEAD, RW_HEAD, n_s, RW_HEADS).transpose(2, 3, 1, 0)

    mean = jnp.mean(o, axis=-1, keepdims=True)
    var = jnp.mean(jnp.square(o - mean), axis=-1, keepdims=True)
    o = ((o - mean) * lax.rsqrt(var + RW_GN_EPS)).reshape(n_tok, D_MODEL) * gn_w + gn_b
    bonus = jnp.sum(hs(r) * hs(k2) * r_k, axis=-1, keepdims=True) * hs(v)
    o = o + bonus.reshape(n_tok, D_MODEL)
    return mm(o * g, w_o), wkv_p, wkv_s


def kernel(x_prompt, x_sample, state_wkv, state_shift, cache_kv, cache_win, page_table, ln_mix, ln_ffn, ln_kv, ln_out, rw_mu, rw_wr, rw_wk, rw_wv, rw_wo, rw_w0, rw_w1, rw_w2, rw_a0, rw_a1, rw_a2, rw_g1, rw_g2, rw_kk, rw_ka, rw_rk, rw_lnw, rw_lnb, w_kv, cmp_pe, cmp_w1, cmp_b1, cmp_w2, nsa_wqg, nsa_wo, moe_wrg, moe_brg, moe_wre, moe_bre, moe_wgate, moe_wup, moe_wdown):
    n_p, t_p, dm = x_prompt.shape
    n_s = x_sample.shape[0]
    assert x_sample.shape[1] == 1
    np_rows = n_p * t_p
    n_real = np_rows + n_s
    n_tok = _round_up(n_real, TOKEN_TILE)
    h = jnp.concatenate([x_prompt.reshape(np_rows, dm), x_sample.reshape(n_s, dm),
                         jnp.zeros((n_tok - n_real, dm), F32)], axis=0)

    def moe(hh, layer):
        return hmoe(rmsnorm(hh, ln_ffn[layer]), moe_wrg[layer], moe_brg[layer], moe_wre[layer], moe_bre[layer],
                    moe_wgate[layer], moe_wup[layer], moe_wdown[layer])

    xn = rmsnorm(h, ln_mix[0])
    xn_p = xn[:np_rows].reshape(n_p, t_p, dm)
    prev_p = jnp.concatenate([jnp.zeros((n_p, 1, dm), F32), xn_p[:, :-1]], axis=1).reshape(np_rows, dm)
    prev = jnp.concatenate([prev_p, state_shift[0], jnp.zeros((n_tok - n_real, dm), F32)], axis=0)
    y, wkv_p, wkv_s = rwkv_layer(xn, prev, n_p, t_p, n_s, state_wkv[0], rw_mu[0], rw_wr[0], rw_wk[0], rw_wv[0],
                                 rw_wo[0], rw_w0[0], rw_w1[0], rw_w2[0], rw_a0[0], rw_a1[0], rw_a2[0],
                                 rw_g1[0], rw_g2[0], rw_kk[0], rw_ka[0], rw_rk[0], rw_lnw[0], rw_lnb[0])
    p_shift = xn_p[:, -1][None]
    s_shift = xn[np_rows:n_real][None]
    h = h + y
    h = h + moe(h, 0)

    xn = rmsnorm(h, ln_mix[1])
    hk = rmsnorm(h, ln_kv)
    qg = mm(xn, nsa_wqg[0])
    kv = mm(hk, w_kv)
    o_p, p_rows, p_win_all = nsa_prompt(qg, kv, n_p, t_p, cmp_pe, cmp_w1, cmp_b1, cmp_w2)
    o_s, s_rows, s_win = nsa_sample(qg[np_rows:n_real], kv[np_rows:n_real], cache_kv, cache_win, page_table,
                                    cmp_pe, cmp_w1, cmp_b1, cmp_w2)
    o = jnp.concatenate([o_p, o_s, jnp.zeros((n_tok - n_real, dm), F32)], axis=0)
    h = h + mm(o, nsa_wo[0])
    h = h + moe(h, 1)

    y = rmsnorm(h, ln_out)
    y_prompt = y[:np_rows].reshape(n_p, t_p, dm)
    y_sample = y[np_rows:n_real].reshape(n_s, 1, dm)
    p_win = p_win_all[:, -min(WINDOW, t_p):]
    return (y_prompt, y_sample, wkv_p[None], p_shift, p_rows, p_win, wkv_s[None], s_shift, s_rows, s_win)
```

```python
import functools

import numpy as np
import jax
import jax.numpy as jnp
from jax import lax
from jax.experimental import pallas as pl
from jax.experimental.pallas import tpu as pltpu

F32 = jnp.float32
BF16 = jnp.bfloat16

D_MODEL = 1024
RMS_EPS = 1e-6
RW_HEAD = 64
RW_HEADS = D_MODEL // RW_HEAD
RW_GN_EPS = 64e-5
N_HEADS = 16
HEAD_DIM = 64
KV_HEADS = 4
GQA_R = N_HEADS // KV_HEADS
ROPE_DIMS = HEAD_DIM // 4
ROPE_THETA = 500000.0
N_KV_STREAMS = 6
CMP_BLOCK = 32
CMP_STRIDE = 16
SEL_BLOCK = 64
N_SEL = 16
N_LOCAL_BLOCKS = 2
FORCE_SCORE = 1.0e4
WINDOW = 512
N_GROUPS = 4
EXPERTS_PER_GROUP = 8
N_EXPERTS = N_GROUPS * EXPERTS_PER_GROUP
D_EXPERT = 256

LANES = 128
VMEM_LIMIT = 48 << 20
NEG_BIG = -1e30
ATT_SCALE = HEAD_DIM ** -0.5
TOKEN_TILE = 512


def _round_up(x, m):
    return (x + m - 1) // m * m


def _cparams(sem):
    return pltpu.CompilerParams(dimension_semantics=sem, vmem_limit_bytes=VMEM_LIMIT)


_NT = (((1,), (1,)), ((), ()))


def _mm_kernel(x_ref, w_ref, o_ref):
    o_ref[...] = jnp.dot(x_ref[...].astype(BF16), w_ref[...], preferred_element_type=F32)


def mm(x, w, tm=TOKEN_TILE):
    m, k = x.shape
    n = w.shape[1]
    tm = min(tm, _round_up(m, 8))
    mp = _round_up(m, tm)
    if mp != m:
        x = jnp.pad(x, ((0, mp - m), (0, 0)))
    out = pl.pallas_call(
        _mm_kernel,
        grid=(mp // tm,),
        in_specs=[pl.BlockSpec((tm, k), lambda i: (i, 0)),
                  pl.BlockSpec((k, n), lambda i: (0, 0))],
        out_specs=pl.BlockSpec((tm, n), lambda i: (i, 0)),
        out_shape=jax.ShapeDtypeStruct((mp, n), F32),
        compiler_params=_cparams(("arbitrary",)),
        name="mm",
    )(x, w.astype(BF16))
    return out[:m] if mp != m else out


SCAN_TC = 32


def _scan_kernel(r_ref, d_ref, k_ref, b_ref, n_ref, v_ref, s0_ref, o_ref, st_ref, s_scr, *, tc, dot_dtype):
    c = pl.program_id(1)
    rnd = lambda x: x.astype(dot_dtype).astype(F32)

    @pl.when(c == 0)
    def _():
        s_scr[...] = s0_ref[...]

    def step(t, carry):
        acc = jnp.zeros((RW_HEAD, LANES), F32)
        for j in range(RW_HEAD):
            acc = acc + rnd(s_scr[j]) * rnd(n_ref[t, pl.ds(j, 1), :])
        sa = acc
        vv = v_ref[t]
        out = jnp.zeros((RW_HEAD, LANES), F32)
        for j in range(RW_HEAD):
            sn = (s_scr[j] * d_ref[t, pl.ds(j, 1), :] + sa * b_ref[t, pl.ds(j, 1), :]
                  + vv * k_ref[t, pl.ds(j, 1), :])
            s_scr[j] = sn
            out = out + rnd(sn) * rnd(r_ref[t, pl.ds(j, 1), :])
        o_ref[t] = out
        return carry

    lax.fori_loop(0, tc, step, 0)

    @pl.when(c == pl.num_programs(1) - 1)
    def _():
        st_ref[...] = s_scr[...]


def rwkv_scan(r, d, k, b, nkk, v, s0, dot_dtype=F32):
    t, _, l = r.shape
    tc = min(SCAN_TC, t)
    assert t % tc == 0 and l % LANES == 0
    seq = pl.BlockSpec((tc, RW_HEAD, LANES), lambda g, c: (c, 0, g))
    st = pl.BlockSpec((RW_HEAD, RW_HEAD, LANES), lambda g, c: (0, 0, g))
    return pl.pallas_call(
        functools.partial(_scan_kernel, tc=tc, dot_dtype=dot_dtype),
        grid=(l // LANES, t // tc),
        in_specs=[seq] * 6 + [st],
        out_specs=[seq, st],
        out_shape=[jax.ShapeDtypeStruct((t, RW_HEAD, l), F32),
                   jax.ShapeDtypeStruct((RW_HEAD, RW_HEAD, l), F32)],
        scratch_shapes=[pltpu.VMEM((RW_HEAD, RW_HEAD, LANES), F32)],
        compiler_params=_cparams(("arbitrary", "arbitrary")),
        name="rwkv_scan",
    )(r, d, k, b, nkk, v, s0)


MOE_TM = 256
MOE_VMEM_LIMIT = 56 << 20


def _moe_kernel(te_ref, nt_ref, tok_ref, x_ref, c_ref, wgu_ref, wd_ref, o_ref, xt_scr):
    i = pl.program_id(0)

    @pl.when(i < nt_ref[0])
    def _():
        base = i * MOE_TM

        def gather_row(r, carry):
            xt_scr[pl.ds(r, 1), :] = x_ref[pl.ds(tok_ref[base + r], 1), :]
            return carry

        lax.fori_loop(0, MOE_TM, gather_row, 0, unroll=8)
        u = xt_scr[...]
        lo = lax.bitcast_convert_type(u << 16, F32).astype(BF16)
        hi = lax.bitcast_convert_type(u & jnp.uint32(0xFFFF0000), F32).astype(BF16)
        x = jnp.concatenate([lo, hi], axis=1)
        h = jnp.dot(x, wgu_ref[0], preferred_element_type=F32)
        g = h[:, :D_EXPERT]
        u2 = h[:, D_EXPERT:]
        hid = g * jax.nn.sigmoid(g) * u2 * c_ref[...]
        o_ref[...] = jnp.dot(hid.astype(BF16), wd_ref[0], preferred_element_type=F32)

    @pl.when(i >= nt_ref[0])
    def _():
        o_ref[...] = jnp.zeros_like(o_ref)


def pack_bf16_pairs(x):
    half = x.shape[1] // 2
    bits = lax.bitcast_convert_type(x.astype(BF16), jnp.uint16).astype(jnp.uint32)
    return bits[:, :half] | (bits[:, half:] << 16)


def moe_gmm(xp, row_tok, cs, tile_expert, n_tiles_used, wgu, wd):
    n, half = xp.shape
    dm = 2 * half
    p = row_tok.shape[0]
    nt = p // MOE_TM

    def row_map(i, te, ntu, tok):
        return (jnp.minimum(i, ntu[0] - 1), 0)

    grid_spec = pltpu.PrefetchScalarGridSpec(
        num_scalar_prefetch=3,
        grid=(nt,),
        in_specs=[pl.BlockSpec((n, half), lambda i, te, ntu, tok: (0, 0), pipeline_mode=pl.Buffered(1)),
                  pl.BlockSpec((MOE_TM, 1), row_map),
                  pl.BlockSpec((1, dm, 2 * D_EXPERT), lambda i, te, ntu, tok: (te[i], 0, 0)),
                  pl.BlockSpec((1, D_EXPERT, dm), lambda i, te, ntu, tok: (te[i], 0, 0))],
        out_specs=pl.BlockSpec((MOE_TM, dm), lambda i, te, ntu, tok: (i, 0)),
        scratch_shapes=[pltpu.VMEM((MOE_TM, half), jnp.uint32)],
    )
    return pl.pallas_call(
        _moe_kernel,
        grid_spec=grid_spec,
        out_shape=jax.ShapeDtypeStruct((p, dm), F32),
        compiler_params=pltpu.CompilerParams(dimension_semantics=("arbitrary",), vmem_limit_bytes=MOE_VMEM_LIMIT),
        name="moe_gmm",
    )(tile_expert, n_tiles_used, row_tok, xp, cs, wgu, wd)


def hmoe(xn, w_rg, b_rg, w_re, b_re, w_gate, w_up, w_down):
    n, dm = xn.shape
    w_r = jnp.concatenate([w_rg, w_re], axis=1)
    w_r = jnp.pad(w_r, ((0, 0), (0, LANES - w_r.shape[1])))
    logits = mm(xn, w_r)
    lg = logits[:, :N_GROUPS] + b_rg
    le = (logits[:, N_GROUPS:N_GROUPS + N_EXPERTS] + b_re).reshape(n, N_GROUPS, EXPERTS_PER_GROUP)
    g_sel = jnp.argmax(lg, axis=-1)
    p_g = jnp.max(jax.nn.softmax(lg, axis=-1), axis=-1)
    le_g = jnp.take_along_axis(le, g_sel[:, None, None], axis=1)[:, 0]
    top_p, top_i = lax.top_k(jax.nn.softmax(le_g, axis=-1), 2)
    w_top = p_g[:, None] * top_p / jnp.sum(top_p, axis=-1, keepdims=True)
    eid = (g_sel[:, None] * EXPERTS_PER_GROUP + top_i).astype(jnp.int32).reshape(-1)
    cw = w_top.reshape(-1)

    n2 = 2 * n
    onehot = (eid[:, None] == jnp.arange(N_EXPERTS, dtype=jnp.int32)[None, :]).astype(jnp.int32)
    counts = jnp.sum(onehot, axis=0)
    rank = jnp.take_along_axis(jnp.cumsum(onehot, axis=0) - onehot, eid[:, None], axis=1)[:, 0]
    tiles_per = (counts + MOE_TM - 1) // MOE_TM
    tile_end = jnp.cumsum(tiles_per)
    tile_start = tile_end - tiles_per
    pos = tile_start[eid] * MOE_TM + rank
    p_rows = _round_up(n2 + N_EXPERTS * (MOE_TM - 1), MOE_TM)
    nt = p_rows // MOE_TM
    n_used = tile_end[-1].astype(jnp.int32)
    tile_ids = jnp.arange(nt, dtype=jnp.int32)
    tile_expert = jnp.sum((tile_end[None, :] <= jnp.minimum(tile_ids, n_used - 1)[:, None]).astype(jnp.int32), axis=1)

    tok = jnp.arange(n2, dtype=jnp.int32) // 2
    row_tok = jnp.zeros((p_rows,), jnp.int32).at[pos].set(tok)
    row_c = jnp.zeros((p_rows,), F32).at[pos].set(cw)
    wgu = jnp.concatenate([w_gate, w_up], axis=-1).reshape(N_EXPERTS, dm, 2 * D_EXPERT).astype(BF16)
    wd = w_down.reshape(N_EXPERTS, D_EXPERT, dm).astype(BF16)
    ys = moe_gmm(pack_bf16_pairs(xn), row_tok, row_c[:, None], tile_expert, n_used.reshape(1), wgu, wd)
    pos2 = pos.reshape(n, 2)
    return ys[pos2[:, 0]] + ys[pos2[:, 1]]


ATT_TQ = 256
ATT_TK = 512


def _cmp_sel_kernel(q_ref, kc_ref, vct_ref, ovt_ref, o_ref, sel_ref, *, n_cmp, n_sblk, tq):
    t0 = pl.program_id(2) * tq
    ncp = kc_ref.shape[2]
    n_io = lax.broadcasted_iota(jnp.int32, (ncp, tq), 0)
    t_io = lax.broadcasted_iota(jnp.int32, (ncp, tq), 1) + t0
    mask = (n_io * CMP_STRIDE + (CMP_BLOCK - 1) <= t_io) & (n_io < n_cmp)
    kc = kc_ref[0, 0]
    vct = vct_ref[0, 0]
    ovt = ovt_ref[...]
    imp = jnp.zeros((n_sblk, tq), F32)
    outs = []
    for r in range(GQA_R):
        qr = (q_ref[:, r * HEAD_DIM:(r + 1) * HEAD_DIM] * ATT_SCALE).astype(BF16)
        st = lax.dot_general(kc, qr, _NT, preferred_element_type=F32)
        st = jnp.where(mask, st, NEG_BIG)
        m = jnp.max(st, axis=0, keepdims=True)
        e = jnp.where(mask, jnp.exp(st - m), 0.0)
        p = (e / jnp.maximum(jnp.sum(e, axis=0, keepdims=True), 1e-30)).astype(BF16)
        outs.append(jnp.dot(vct, p, preferred_element_type=F32).T)
        imp = imp + jnp.dot(ovt, p, preferred_element_type=F32)
    o_ref[...] = jnp.concatenate(outs, axis=1)

    j_io = lax.broadcasted_iota(jnp.int32, (n_sblk, tq), 0)
    qblk = jnp.right_shift(lax.broadcasted_iota(jnp.int32, (n_sblk, tq), 1) + t0, SEL_BLOCK.bit_length() - 1)
    allowed = j_io <= qblk
    forced = (j_io == 0) | ((j_io > qblk - N_LOCAL_BLOCKS) & allowed)
    score = jnp.where(forced, FORCE_SCORE, imp)
    score = jnp.where(allowed, score, -jnp.inf)
    rank = jnp.zeros((n_sblk, tq), jnp.int32)
    for jp in range(n_sblk):
        row = score[jp:jp + 1, :]
        beats = (row > score) | ((row == score) & (j_io > jp))
        rank = rank + jnp.where(beats, 1, 0)
    sel = (rank < N_SEL) & allowed
    sel_ref[0, 0] = jnp.where(sel, 1.0, 0.0).astype(BF16)


def cmp_select(q, kc, vct, ovt, n_b, t, n_cmp, n_sblk):
    assert SEL_BLOCK & (SEL_BLOCK - 1) == 0
    tq = ATT_TQ
    nq = t // tq
    ncp = kc.shape[2]
    gw = GQA_R * HEAD_DIM
    return pl.pallas_call(
        functools.partial(_cmp_sel_kernel, n_cmp=n_cmp, n_sblk=n_sblk, tq=tq),
        grid=(n_b, KV_HEADS, nq),
        in_specs=[pl.BlockSpec((tq, gw), lambda b, g, qi: (b * nq + qi, g)),
                  pl.BlockSpec((1, 1, ncp, HEAD_DIM), lambda b, g, qi: (b, g, 0, 0)),
                  pl.BlockSpec((1, 1, HEAD_DIM, ncp), lambda b, g, qi: (b, g, 0, 0)),
                  pl.BlockSpec((n_sblk, ncp), lambda b, g, qi: (0, 0))],
        out_specs=[pl.BlockSpec((tq, gw), lambda b, g, qi: (b * nq + qi, g)),
                   pl.BlockSpec((1, 1, n_sblk, tq), lambda b, g, qi: (b, g, 0, qi))],
        out_shape=[jax.ShapeDtypeStruct((n_b * t, N_HEADS * HEAD_DIM), F32),
                   jax.ShapeDtypeStruct((n_b, KV_HEADS, n_sblk, t), BF16)],
        compiler_params=_cparams(("arbitrary", "arbitrary", "arbitrary")),
        name="attn_cmp_select",
    )(q, kc, vct, ovt)


def _flash_kernel(pq_ref, pk_ref, pfirst_ref, plast_ref, q_ref, k_ref, vt_ref, *rest, window, use_sel, tq, tk):
    if use_sel:
        sel_ref, et_ref, o_ref, m_scr, l_scr, acc_scr = rest
    else:
        o_ref, m_scr, l_scr, acc_scr = rest
    p_id = pl.program_id(2)

    @pl.when(pfirst_ref[p_id] == 1)
    def _():
        m_scr[...] = jnp.full_like(m_scr, NEG_BIG)
        l_scr[...] = jnp.zeros_like(l_scr)
        acc_scr[...] = jnp.zeros_like(acc_scr)

    off = pq_ref[p_id] * tq - pk_ref[p_id] * tk
    diff = lax.broadcasted_iota(jnp.int32, (tk, tq), 0) - lax.broadcasted_iota(jnp.int32, (tk, tq), 1)
    mask = diff <= off
    if window is not None:
        mask = mask & (diff > off - window)
    if use_sel:
        selm = jnp.dot(et_ref[...], sel_ref[0, 0], preferred_element_type=F32)
        mask = mask & (selm > 0.5)
    k = k_ref[0, 0]
    vt = vt_ref[0, 0]
    heads = range(GQA_R)
    qs = [(q_ref[:, r * HEAD_DIM:(r + 1) * HEAD_DIM] * ATT_SCALE).astype(BF16) for r in heads]
    sts = [jnp.where(mask, lax.dot_general(k, qs[r], _NT, preferred_element_type=F32), NEG_BIG) for r in heads]
    m_prev = [m_scr[r] for r in heads]
    m_new = [jnp.maximum(m_prev[r], jnp.max(sts[r], axis=0, keepdims=True)) for r in heads]
    ps = [jnp.where(mask, jnp.exp(sts[r] - m_new[r]), 0.0) for r in heads]
    alphas = [jnp.exp(m_prev[r] - m_new[r]) for r in heads]
    for r in heads:
        l_scr[r] = alphas[r] * l_scr[r] + jnp.sum(ps[r], axis=0, keepdims=True)
        acc_scr[r] = alphas[r] * acc_scr[r] + jnp.dot(vt, ps[r].astype(BF16), preferred_element_type=F32)
        m_scr[r] = m_new[r]

    @pl.when(plast_ref[p_id] == 1)
    def _():
        outs = [(acc_scr[r] / jnp.maximum(l_scr[r], 1e-30)).T for r in range(GQA_R)]
        o_ref[...] = jnp.concatenate(outs, axis=1)


def flash_attention(q, k, vt, n_b, t, window=None, sel=None):
    tq, tk = ATT_TQ, ATT_TK
    nq = t // tq
    first_tile = lambda qi: 0 if window is None else max(0, qi * tq - (window - 1)) // tk
    pairs = [(qi, kt) for qi in range(nq) for kt in range(first_tile(qi), (qi * tq + tq - 1) // tk + 1)]
    pq = np.array([p[0] for p in pairs], np.int32)
    pk = np.array([p[1] for p in pairs], np.int32)
    pfirst = np.array([1 if i == 0 or pairs[i - 1][0] != p[0] else 0 for i, p in enumerate(pairs)], np.int32)
    plast = np.array([1 if i == len(pairs) - 1 or pairs[i + 1][0] != p[0] else 0 for i, p in enumerate(pairs)], np.int32)
    gw = GQA_R * HEAD_DIM

    in_specs = [pl.BlockSpec((tq, gw), lambda b, g, p, pq_r, pk_r, f_r, l_r: (b * nq + pq_r[p], g)),
                pl.BlockSpec((1, 1, tk, HEAD_DIM), lambda b, g, p, pq_r, pk_r, f_r, l_r: (b, g, pk_r[p], 0)),
                pl.BlockSpec((1, 1, HEAD_DIM, tk), lambda b, g, p, pq_r, pk_r, f_r, l_r: (b, g, 0, pk_r[p]))]
    args = [q, k, vt]
    if sel is not None:
        n_sblk = sel.shape[2]
        emat = (np.arange(t)[:, None] // SEL_BLOCK == np.arange(n_sblk)[None, :]).astype(np.float32)
        in_specs += [pl.BlockSpec((1, 1, n_sblk, tq), lambda b, g, p, pq_r, pk_r, f_r, l_r: (b, g, 0, pq_r[p])),
                     pl.BlockSpec((tk, n_sblk), lambda b, g, p, pq_r, pk_r, f_r, l_r: (pk_r[p], 0))]
        args += [sel, jnp.asarray(emat, BF16)]
    grid_spec = pltpu.PrefetchScalarGridSpec(
        num_scalar_prefetch=4,
        grid=(n_b, KV_HEADS, len(pairs)),
        in_specs=in_specs,
        out_specs=pl.BlockSpec((tq, gw), lambda b, g, p, pq_r, pk_r, f_r, l_r: (b * nq + pq_r[p], g)),
        scratch_shapes=[pltpu.VMEM((GQA_R, 1, tq), F32), pltpu.VMEM((GQA_R, 1, tq), F32),
                        pltpu.VMEM((GQA_R, HEAD_DIM, tq), F32)],
    )
    return pl.pallas_call(
        functools.partial(_flash_kernel, window=window, use_sel=sel is not None, tq=tq, tk=tk),
        grid_spec=grid_spec,
        out_shape=jax.ShapeDtypeStruct((n_b * t, N_HEADS * HEAD_DIM), F32),
        compiler_params=_cparams(("arbitrary", "arbitrary", "arbitrary")),
        name="attn_sel" if sel is not None else "attn_win",
    )(jnp.asarray(pq), jnp.asarray(pk), jnp.asarray(pfirst), jnp.asarray(plast), *args)


def _cmp_kernel(q_ref, k_ref, v_ref, ov_ref, o_ref, imp_ref, *, n_cmp):
    q = q_ref[0] * ATT_SCALE
    s = lax.dot_general(q, k_ref[0], _NT, preferred_element_type=F32)
    mask = lax.broadcasted_iota(jnp.int32, s.shape, 1) < n_cmp
    s = jnp.where(mask, s, NEG_BIG)
    m = jnp.max(s, axis=-1, keepdims=True)
    e = jnp.where(mask, jnp.exp(s - m), 0.0)
    p = (e / jnp.maximum(jnp.sum(e, axis=-1, keepdims=True), 1e-30)).astype(BF16)
    o_ref[0] = jnp.dot(p, v_ref[0], preferred_element_type=F32)
    imp_ref[0] = jnp.dot(p, ov_ref[...], preferred_element_type=F32)


def cmp_attention_sample(q, kc, vc, ov, n_cmp):
    bg, rq, hd = q.shape
    nc = kc.shape[1]
    nb = ov.shape[1]
    return pl.pallas_call(
        functools.partial(_cmp_kernel, n_cmp=n_cmp),
        grid=(bg,),
        in_specs=[pl.BlockSpec((1, rq, hd), lambda b: (b, 0, 0)),
                  pl.BlockSpec((1, nc, hd), lambda b: (b, 0, 0)),
                  pl.BlockSpec((1, nc, hd), lambda b: (b, 0, 0)),
                  pl.BlockSpec((nc, nb), lambda b: (0, 0))],
        out_specs=[pl.BlockSpec((1, rq, hd), lambda b: (b, 0, 0)),
                   pl.BlockSpec((1, rq, nb), lambda b: (b, 0, 0))],
        out_shape=[jax.ShapeDtypeStruct((bg, rq, hd), F32),
                   jax.ShapeDtypeStruct((bg, rq, nb), F32)],
        compiler_params=_cparams(("arbitrary",)),
        name="attn_cmp_sample",
    )(q, kc, vc, ov)


PAGES_PER_STEP = 8


def _page_compress_kernel(pt_ref, *rest, n_pages, page, pb):
    c_refs = rest[:pb]
    wbd_ref, pe_ref, w2_ref, o_ref, a_scr = rest[pb:]
    p = pl.program_id(1)
    nsub = n_pages * page // CMP_STRIDE
    for i in range(pb):
        row0 = pl.multiple_of((p * pb + i) * page, page)
        for s in range(2):
            for gp in range(KV_HEADS // 2):
                x = c_refs[i][0, s, 2 * gp:2 * gp + 2].reshape(2 * HEAD_DIM, page)
                a_scr[s, gp, pl.ds(row0, page), :] = x.T

    @pl.when(p == n_pages // pb - 1)
    def _():
        hid = pe_ref.shape[-1]
        for s in range(2):
            for gp in range(KV_HEADS // 2):
                acc = jnp.zeros((nsub, 4 * hid), F32)
                for j in range(CMP_STRIDE):
                    lhs = a_scr[s, gp, pl.ds(j, nsub, stride=CMP_STRIDE), :].astype(BF16)
                    acc = acc + jnp.dot(lhs, wbd_ref[s, j], preferred_element_type=F32)
                for g2 in range(2):
                    first = acc[:, 2 * g2 * hid:(2 * g2 + 1) * hid]
                    second = acc[:, (2 * g2 + 1) * hid:(2 * g2 + 2) * hid]
                    pre = first + pltpu.roll(second, nsub - 1, 0) + pe_ref[s]
                    act = jax.nn.gelu(pre).astype(BF16)
                    o_ref[s, 0, 2 * gp + g2] = jnp.dot(act, w2_ref[s], preferred_element_type=F32).astype(BF16)


def page_compress(cache_t, page_table, cmp_pe, cmp_w1, cmp_b1, cmp_w2):
    n_b, n_pages = page_table.shape
    page = cache_t.shape[-1]
    nsub = n_pages * page // CMP_STRIDE
    hid = cmp_w1.shape[-1]
    pb = PAGES_PER_STEP
    assert n_pages % pb == 0
    w1r = cmp_w1.reshape(2, 2, CMP_STRIDE, HEAD_DIM, hid).transpose(0, 2, 3, 1, 4)
    w1r = w1r.reshape(2, CMP_STRIDE, HEAD_DIM, 2 * hid)
    zeros = jnp.zeros_like(w1r)
    wbd = jnp.concatenate([jnp.concatenate([w1r, zeros], axis=-1), jnp.concatenate([zeros, w1r], axis=-1)], axis=2)
    pe_term = (jnp.einsum('sjd,sjdc->sc', cmp_pe, cmp_w1.reshape(2, CMP_BLOCK, HEAD_DIM, hid)) + cmp_b1)[:, None, :]
    def page_map(i):
        return lambda b, p, pt: (pt[b * n_pages + p * pb + i], 0, 0, 0, 0)

    grid_spec = pltpu.PrefetchScalarGridSpec(
        num_scalar_prefetch=1,
        grid=(n_b, n_pages // pb),
        in_specs=[pl.BlockSpec((1, 2, KV_HEADS, HEAD_DIM, page), page_map(i)) for i in range(pb)] + [
            pl.BlockSpec((2, CMP_STRIDE, 2 * HEAD_DIM, 4 * hid), lambda b, p, pt: (0, 0, 0, 0)),
            pl.BlockSpec((2, 1, hid), lambda b, p, pt: (0, 0, 0)),
            pl.BlockSpec((2, hid, HEAD_DIM), lambda b, p, pt: (0, 0, 0))],
        out_specs=pl.BlockSpec((2, 1, KV_HEADS, nsub, HEAD_DIM), lambda b, p, pt: (0, b, 0, 0, 0)),
        scratch_shapes=[pltpu.VMEM((2, KV_HEADS // 2, n_pages * page, 2 * HEAD_DIM), F32)],
    )
    return pl.pallas_call(
        functools.partial(_page_compress_kernel, n_pages=n_pages, page=page, pb=pb),
        grid_spec=grid_spec,
        out_shape=jax.ShapeDtypeStruct((2, n_b, KV_HEADS, nsub, HEAD_DIM), BF16),
        compiler_params=_cparams(("arbitrary", "arbitrary")),
        name="page_compress",
    )(page_table.reshape(-1), *([cache_t] * pb), wbd.astype(BF16), pe_term, cmp_w2.astype(BF16))


def _decode_attn_kernel(phys_ref, blk_ref, flag_ref, q_ref, kn_ref, vn_ref, *rest, mode, past, steps, window, page):
    kt_refs = rest[:steps]
    vt_refs = rest[steps:2 * steps]
    o_ref = rest[2 * steps]
    idx = pl.program_id(0) * KV_HEADS + pl.program_id(1)
    rnd = lambda x: x.astype(BF16).astype(F32)
    q = q_ref[0, 0] * ATT_SCALE

    kn = rnd(kn_ref[0, 0])
    vn = rnd(vn_ref[0, 0])
    s_new = jnp.sum(q.astype(F32) * kn, axis=-1, keepdims=True)
    on = flag_ref[idx] == 1
    m = jnp.where(on, s_new, NEG_BIG)
    scs, masks = [], []
    for s in range(steps):
        kt = kt_refs[s][0, 0, 0].astype(BF16)
        tile = kt.shape[-1]
        sc = jnp.dot(q, kt, preferred_element_type=F32)
        lane = lax.broadcasted_iota(jnp.int32, sc.shape, 1)
        if mode == "sel":
            blk = blk_ref[idx * steps + s]
            per_page = page // SEL_BLOCK
            pos = (blk // per_page) * page + lane
            mask = (jnp.right_shift(pos, SEL_BLOCK.bit_length() - 1) == blk) & (pos <= past) & (blk >= 0)
        else:
            pos = past - tile + lane
            mask = (pos > past - window) & (pos <= past)
        sc = jnp.where(mask, sc, NEG_BIG)
        m = jnp.maximum(m, jnp.max(sc, axis=-1, keepdims=True))
        scs.append(sc)
        masks.append(mask)
    p_new = jnp.where(on, jnp.exp(s_new - m), 0.0)
    l = p_new
    acc = rnd(p_new) * vn
    for s in range(steps):
        p = jnp.where(masks[s], jnp.exp(scs[s] - m), 0.0)
        l = l + jnp.sum(p, axis=-1, keepdims=True)
        vt = vt_refs[s][0, 0, 0].astype(BF16)
        acc = acc + lax.dot_general(p.astype(BF16), vt, _NT, preferred_element_type=F32)
    o_ref[0, 0] = acc / jnp.maximum(l, 1e-30)


def decode_attention(q, k_new, v_new, kv_t, streams, phys, blk, flag, mode, past, steps, window=None, page=None):
    n_b, n_g, rows, hd = q.shape
    tile = kv_t.shape[-1]
    ks, vs = streams

    def kmap(st, s):
        if mode == "sel":
            return lambda b, g, ph, bl, fl: (ph[(b * n_g + g) * steps + s], st, g, 0, 0)
        return lambda b, g, ph, bl, fl: (b, st, g, 0, 0)

    tiles = ([pl.BlockSpec((1, 1, 1, hd, tile), kmap(ks, s)) for s in range(steps)]
             + [pl.BlockSpec((1, 1, 1, hd, tile), kmap(vs, s)) for s in range(steps)])
    grid_spec = pltpu.PrefetchScalarGridSpec(
        num_scalar_prefetch=3,
        grid=(n_b, n_g),
        in_specs=[pl.BlockSpec((1, 1, rows, hd), lambda b, g, ph, bl, fl: (b, g, 0, 0)),
                  pl.BlockSpec((1, 1, 1, hd), lambda b, g, ph, bl, fl: (b, g, 0, 0)),
                  pl.BlockSpec((1, 1, 1, hd), lambda b, g, ph, bl, fl: (b, g, 0, 0))] + tiles,
        out_specs=pl.BlockSpec((1, 1, rows, hd), lambda b, g, ph, bl, fl: (b, g, 0, 0)),
    )
    return pl.pallas_call(
        functools.partial(_decode_attn_kernel, mode=mode, past=past, steps=steps, window=window, page=page),
        grid_spec=grid_spec,
        out_shape=jax.ShapeDtypeStruct((n_b, n_g, rows, hd), F32),
        compiler_params=_cparams(("arbitrary", "arbitrary")),
        name="attn_decode_" + mode,
    )(phys, blk, flag, q, k_new, v_new, *([kv_t] * (2 * steps)))


def rmsnorm(x, g):
    y = x * lax.rsqrt(jnp.mean(x * x, axis=-1, keepdims=True) + RMS_EPS)
    return y * g


def rope(x, pos):
    half = ROPE_DIMS // 2
    inv = jnp.power(ROPE_THETA, -jnp.arange(half, dtype=F32) * (2.0 / ROPE_DIMS))
    ang = pos.astype(F32)[:, None] * inv[None, :]
    cos = jnp.cos(ang)[:, None, :]
    sin = jnp.sin(ang)[:, None, :]
    x1 = x[..., :half]
    x2 = x[..., half:ROPE_DIMS]
    return jnp.concatenate([x1 * cos - x2 * sin, x2 * cos + x1 * sin, x[..., ROPE_DIMS:]], axis=-1)


def compress(sub, pe, w1, b1, w2, n_b, nsub):
    half = CMP_STRIDE * HEAD_DIM
    w1cat = jnp.concatenate([w1[:half], w1[half:]], axis=1)
    hidden = w1.shape[1]
    part = mm(sub, w1cat).reshape(n_b, nsub, KV_HEADS, 2, hidden)
    pe_term = jnp.einsum('jd,jdc->c', pe, w1.reshape(CMP_BLOCK, HEAD_DIM, hidden)) + b1
    pre = part[:, :nsub - 1, :, 0] + part[:, 1:, :, 1] + pe_term
    act = jax.nn.gelu(pre)
    return mm(act.reshape(-1, hidden), w2).reshape(n_b, nsub - 1, KV_HEADS, HEAD_DIM)


def overlap_matrix(n_cmp, n_sblk, nc_pad, nb_pad):
    cs = np.arange(n_cmp) * CMP_STRIDE
    ce = cs + CMP_BLOCK - 1
    ss = np.arange(n_sblk) * SEL_BLOCK
    ov = ((cs[:, None] < ss[None, :] + SEL_BLOCK) & (ce[:, None] >= ss[None, :])).astype(np.float32)
    out = np.zeros((nc_pad, nb_pad), np.float32)
    out[:n_cmp, :n_sblk] = ov
    return out


def block_scores(imp, q_pos, n_sblk):
    j = jnp.arange(n_sblk)[None, :]
    qblk = (q_pos // SEL_BLOCK)[:, None]
    allowed = j <= qblk
    forced = (j == 0) | ((j > qblk - N_LOCAL_BLOCKS) & allowed)
    score = jnp.where(forced, FORCE_SCORE, imp)
    return jnp.where(allowed, score, -jnp.inf)


def nsa_prompt(qg, kv, n_b, t, cmp_pe, cmp_w1, cmp_b1, cmp_w2):
    n = n_b * t
    pos = jnp.arange(t)
    gates = jax.nn.sigmoid(qg[:n, N_HEADS * HEAD_DIM:]).reshape(n, N_HEADS, 3)
    kv = kv[:n].reshape(n_b, t, N_KV_STREAMS, KV_HEADS, HEAD_DIM)
    k_sel = rope(kv[:, :, 2], pos)
    k_win = rope(kv[:, :, 4], pos)
    new_rows = jnp.stack([kv[:, :, 0], kv[:, :, 1], k_sel, kv[:, :, 3]], axis=2)
    new_win = jnp.stack([k_win, kv[:, :, 5]], axis=2)

    nsub = t // CMP_STRIDE
    n_cmp = nsub - 1
    n_sblk = t // SEL_BLOCK

    def sub_blocks(rows):
        return rows.reshape(n_b, nsub, CMP_STRIDE, KV_HEADS, HEAD_DIM).transpose(0, 1, 3, 2, 4).reshape(-1, CMP_STRIDE * HEAD_DIM)

    kc = compress(sub_blocks(kv[:, :, 0]), cmp_pe[0], cmp_w1[0], cmp_b1[0], cmp_w2[0], n_b, nsub)
    vc = compress(sub_blocks(kv[:, :, 1]), cmp_pe[1], cmp_w1[1], cmp_b1[1], cmp_w2[1], n_b, nsub)
    nc_pad = _round_up(n_cmp, LANES)
    kc = jnp.pad(kc, ((0, 0), (0, nc_pad - n_cmp), (0, 0), (0, 0)))
    vc = jnp.pad(vc, ((0, 0), (0, nc_pad - n_cmp), (0, 0), (0, 0)))
    ovt = jnp.asarray(overlap_matrix(n_cmp, n_sblk, nc_pad, n_sblk).T, BF16)
    o_cmp, sel = cmp_select(qg, kc.transpose(0, 2, 1, 3).astype(BF16), vc.transpose(0, 2, 3, 1).astype(BF16),
                            ovt, n_b, t, n_cmp, n_sblk)

    q_r = rope(qg[:n, :N_HEADS * HEAD_DIM].reshape(n_b, t, N_HEADS, HEAD_DIM), pos).reshape(n, N_HEADS * HEAD_DIM)
    rows_k = lambda x: x.transpose(0, 2, 1, 3).astype(BF16)
    rows_vt = lambda x: x.transpose(0, 2, 3, 1).astype(BF16)
    o_sel = flash_attention(q_r, rows_k(k_sel), rows_vt(kv[:, :, 3]), n_b, t, sel=sel)
    o_win = flash_attention(q_r, rows_k(k_win), rows_vt(kv[:, :, 5]), n_b, t, window=WINDOW)

    heads = lambda o: o.reshape(n, N_HEADS, HEAD_DIM)
    o = gates[..., 0:1] * heads(o_cmp) + gates[..., 1:2] * heads(o_sel) + gates[..., 2:3] * heads(o_win)
    return o.reshape(n, N_HEADS * HEAD_DIM), new_rows, new_win


def nsa_sample(qg, kv, cache_kv, cache_win, page_table, cmp_pe, cmp_w1, cmp_b1, cmp_w2):
    n_b, n_pages = page_table.shape
    n_phys, page = cache_kv.shape[:2]
    past = n_pages * page
    pos = jnp.full((1,), past, jnp.int32)
    q = qg[:, :N_HEADS * HEAD_DIM].reshape(n_b, 1, N_HEADS, HEAD_DIM)
    gates = jax.nn.sigmoid(qg[:, N_HEADS * HEAD_DIM:]).reshape(n_b, N_HEADS, 3)
    kv = kv.reshape(n_b, 1, N_KV_STREAMS, KV_HEADS, HEAD_DIM)
    k_sel = rope(kv[:, :, 2], pos)
    k_win = rope(kv[:, :, 4], pos)
    new_rows = jnp.stack([kv[:, :, 0], kv[:, :, 1], k_sel, kv[:, :, 3]], axis=2)
    new_win = jnp.stack([k_win, kv[:, :, 5]], axis=2)
    n_keep = min(WINDOW, cache_win.shape[1] + 1)
    win_state = jnp.concatenate([cache_win[:, cache_win.shape[1] + 1 - n_keep:], new_win], axis=1)

    cache_t = cache_kv.transpose(0, 2, 3, 4, 1)
    win_t = cache_win.transpose(0, 2, 3, 4, 1)

    l_tot = past + 1
    nsub = l_tot // CMP_STRIDE
    n_cmp = nsub - 1
    n_sblk = -(-l_tot // SEL_BLOCK)
    assert nsub * CMP_STRIDE == past and page % SEL_BLOCK == 0 and cache_win.shape[1] == WINDOW
    kvc = page_compress(cache_t, page_table, cmp_pe, cmp_w1, cmp_b1, cmp_w2)

    bg = n_b * KV_HEADS
    rq = 16
    qpad = lambda x: jnp.pad(x.reshape(n_b, KV_HEADS, GQA_R, HEAD_DIM), ((0, 0), (0, 0), (0, rq - GQA_R), (0, 0))).astype(BF16)

    nb_pad = _round_up(n_sblk, 8)
    ov = jnp.asarray(overlap_matrix(n_cmp, n_sblk, nsub, nb_pad), BF16)
    o_cmp, imp = cmp_attention_sample(qpad(q).reshape(bg, rq, HEAD_DIM), kvc[0].reshape(bg, nsub, HEAD_DIM),
                                      kvc[1].reshape(bg, nsub, HEAD_DIM), ov, n_cmp)
    imp = imp[:, :GQA_R, :n_sblk].sum(axis=1)
    score = block_scores(imp[:, None, :], pos, n_sblk)
    top_s, top_i = lax.top_k(score[:, 0], min(N_SEL, n_sblk))
    valid = jnp.isfinite(top_s)

    n_past_blk = past // SEL_BLOCK
    per_page = page // SEL_BLOCK
    n_k = top_i.shape[-1]
    in_past = valid & (top_i < n_past_blk)
    blk = jnp.where(in_past, top_i, -1).astype(jnp.int32)
    logical = jnp.clip(top_i, 0, n_past_blk - 1) // per_page
    phys = jnp.take_along_axis(page_table, logical.reshape(n_b, KV_HEADS * n_k), axis=1).reshape(bg, n_k)
    phys = jnp.where(in_past, phys, 0).astype(jnp.int32)
    new_flag = jnp.any(valid & (top_i >= n_past_blk), axis=-1).astype(jnp.int32)
    q_r = qpad(rope(q, pos))
    o_sel = decode_attention(q_r, new_rows[:, 0, 2][:, :, None, :], new_rows[:, 0, 3][:, :, None, :], cache_t, (2, 3),
                             phys.reshape(-1), blk.reshape(-1), new_flag, "sel", past, n_k, page=page)
    zeros = jnp.zeros((bg,), jnp.int32)
    o_win = decode_attention(q_r, new_win[:, 0, 0][:, :, None, :], new_win[:, 0, 1][:, :, None, :], win_t, (0, 1),
                             zeros, zeros, jnp.ones((bg,), jnp.int32), "win", past, 1, window=WINDOW)

    heads = lambda o: o.reshape(n_b, KV_HEADS, -1, HEAD_DIM)[:, :, :GQA_R].reshape(n_b, N_HEADS, HEAD_DIM)
    o = gates[..., 0:1] * heads(o_cmp) + gates[..., 1:2] * heads(o_sel) + gates[..., 2:3] * heads(o_win)
    return o.reshape(n_b, N_HEADS * HEAD_DIM), new_rows, win_state


def rwkv_layer(xn, prev, n_p, t_p, n_s, state_wkv, mu, w_r, w_k, w_v, w_o, w0, w1, w2, a0, a1, a2, g1, g2,
               k_k, k_a, r_k, gn_w, gn_b):
    n_tok = xn.shape[0]
    xx = prev - xn
    xr, xw, xk, xv, xa, xg = (xn + xx * mu[i] for i in range(6))
    r = mm(xr, w_r)
    k = mm(xk, w_k)
    v = mm(xv, w_v)
    w_log = -jax.nn.softplus(-(w0 + mm(jnp.tanh(mm(xw, w1)), w2))) - 0.5
    a = jax.nn.sigmoid(a0 + mm(mm(xa, a1), a2))
    g = mm(jax.nn.sigmoid(mm(xg, g1)), g2)
    hs = lambda x: x.reshape(n_tok, RW_HEADS, RW_HEAD)
    kk = hs(k * k_k)
    kk = (kk / jnp.maximum(jnp.sqrt(jnp.sum(kk * kk, axis=-1, keepdims=True)), 1e-12)).reshape(n_tok, D_MODEL)
    k2 = k * (1.0 + (a - 1.0) * k_a)
    decay = jnp.exp(-jnp.exp(w_log))
    bvec = kk * a

    np_rows = n_p * t_p

    def lanes_p(x):
        return x[:np_rows].reshape(n_p, t_p, RW_HEADS, RW_HEAD).transpose(1, 3, 0, 2).reshape(t_p, RW_HEAD, n_p * RW_HEADS)

    def lanes_s(x):
        return x[np_rows:np_rows + n_s].reshape(n_s, 1, RW_HEADS, RW_HEAD).transpose(1, 3, 0, 2).reshape(1, RW_HEAD, n_s * RW_HEADS)

    seqs = (r, decay, k2, bvec, -kk, v)
    s0_p = jnp.zeros((RW_HEAD, RW_HEAD, n_p * RW_HEADS), F32)
    o_p, st_p = rwkv_scan(*[lanes_p(x) for x in seqs], s0_p)
    s0_s = state_wkv.transpose(3, 2, 0, 1).reshape(RW_HEAD, RW_HEAD, n_s * RW_HEADS)
    o_s, st_s = rwkv_scan(*[lanes_s(x) for x in seqs], s0_s, dot_dtype=BF16)
    o_p = o_p.reshape(t_p, RW_HEAD, n_p, RW_HEADS).transpose(2, 0, 3, 1).reshape(np_rows, RW_HEADS, RW_HEAD)
    o_s = o_s.reshape(1, RW_HEAD, n_s, RW_HEADS).transpose(2, 0, 3, 1).reshape(n_s, RW_HEADS, RW_HEAD)
    o = jnp.concatenate([o_p, o_s, jnp.zeros((n_tok - np_rows - n_s, RW_HEADS, RW_HEAD), F32)], axis=0)
    wkv_p = st_p.reshape(RW_HEAD, RW_HEAD, n_p, RW_HEADS).transpose(2, 3, 1, 0)
    wkv_s = st_s.reshape(RW_HEAD, RW_HEAD, n_s, RW_HEADS).transpose(2, 3, 1, 0)

    mean = jnp.mean(o, axis=-1, keepdims=True)
    var = jnp.mean(jnp.square(o - mean), axis=-1, keepdims=True)
    o = ((o - mean) * lax.rsqrt(var + RW_GN_EPS)).reshape(n_tok, D_MODEL) * gn_w + gn_b
    bonus = jnp.sum(hs(r) * hs(k2) * r_k, axis=-1, keepdims=True) * hs(v)
    o = o + bonus.reshape(n_tok, D_MODEL)
    return mm(o * g, w_o), wkv_p, wkv_s


def kernel(x_prompt, x_sample, state_wkv, state_shift, cache_kv, cache_win, page_table, ln_mix, ln_ffn, ln_kv, ln_out, rw_mu, rw_wr, rw_wk, rw_wv, rw_wo, rw_w0, rw_w1, rw_w2, rw_a0, rw_a1, rw_a2, rw_g1, rw_g2, rw_kk, rw_ka, rw_rk, rw_lnw, rw_lnb, w_kv, cmp_pe, cmp_w1, cmp_b1, cmp_w2, nsa_wqg, nsa_wo, moe_wrg, moe_brg, moe_wre, moe_bre, moe_wgate, moe_wup, moe_wdown):
    n_p, t_p, dm = x_prompt.shape
    n_s = x_sample.shape[0]
    assert x_sample.shape[1] == 1
    np_rows = n_p * t_p
    n_real = np_rows + n_s
    n_tok = _round_up(n_real, TOKEN_TILE)
    h = jnp.concatenate([x_prompt.reshape(np_rows, dm), x_sample.reshape(n_s, dm),
                         jnp.zeros((n_tok - n_real, dm), F32)], axis=0)

    def moe(hh, layer):
        return hmoe(rmsnorm(hh, ln_ffn[layer]), moe_wrg[layer], moe_brg[layer], moe_wre[layer], moe_bre[layer],
                    moe_wgate[layer], moe_wup[layer], moe_wdown[layer])

    xn = rmsnorm(h, ln_mix[0])
    xn_p = xn[:np_rows].reshape(n_p, t_p, dm)
    prev_p = jnp.concatenate([jnp.zeros((n_p, 1, dm), F32), xn_p[:, :-1]], axis=1).reshape(np_rows, dm)
    prev = jnp.concatenate([prev_p, state_shift[0], jnp.zeros((n_tok - n_real, dm), F32)], axis=0)
    y, wkv_p, wkv_s = rwkv_layer(xn, prev, n_p, t_p, n_s, state_wkv[0], rw_mu[0], rw_wr[0], rw_wk[0], rw_wv[0],
                                 rw_wo[0], rw_w0[0], rw_w1[0], rw_w2[0], rw_a0[0], rw_a1[0], rw_a2[0],
                                 rw_g1[0], rw_g2[0], rw_kk[0], rw_ka[0], rw_rk[0], rw_lnw[0], rw_lnb[0])
    p_shift = xn_p[:, -1][None]
    s_shift = xn[np_rows:n_real][None]
    h = h + y
    h = h + moe(h, 0)

    xn = rmsnorm(h, ln_mix[1])
    hk = rmsnorm(h, ln_kv)
    qg = mm(xn, nsa_wqg[0])
    kv = mm(hk, w_kv)
    o_p, p_rows, p_win_all = nsa_prompt(qg, kv, n_p, t_p, cmp_pe, cmp_w1, cmp_b1, cmp_w2)
    o_s, s_rows, s_win = nsa_sample(qg[np_rows:n_real], kv[np_rows:n_real], cache_kv, cache_win, page_table,
                                    cmp_pe, cmp_w1, cmp_b1, cmp_w2)
    o = jnp.concatenate([o_p, o_s, jnp.zeros((n_tok - n_real, dm), F32)], axis=0)
    h = h + mm(o, nsa_wo[0])
    h = h + moe(h, 1)

    y = rmsnorm(h, ln_out)
    y_prompt = y[:np_rows].reshape(n_p, t_p, dm)
    y_sample = y[np_rows:n_real].reshape(n_s, 1, dm)
    p_win = p_win_all[:, -min(WINDOW, t_p):]
    return (y_prompt, y_sample, wkv_p[None], p_shift, p_rows, p_win, wkv_s[None], s_shift, s_rows, s_win)
```

```python
import functools

import numpy as np
import jax
import jax.numpy as jnp
from jax import lax
from jax.experimental import pallas as pl
from jax.experimental.pallas import tpu as pltpu

F32 = jnp.float32
BF16 = jnp.bfloat16

D_MODEL = 1024
RMS_EPS = 1e-6
RW_HEAD = 64
RW_HEADS = D_MODEL // RW_HEAD
RW_GN_EPS = 64e-5
N_HEADS = 16
HEAD_DIM = 64
KV_HEADS = 4
GQA_R = N_HEADS // KV_HEADS
ROPE_DIMS = HEAD_DIM // 4
ROPE_THETA = 500000.0
N_KV_STREAMS = 6
CMP_BLOCK = 32
CMP_STRIDE = 16
SEL_BLOCK = 64
N_SEL = 16
N_LOCAL_BLOCKS = 2
FORCE_SCORE = 1.0e4
WINDOW = 512
N_GROUPS = 4
EXPERTS_PER_GROUP = 8
N_EXPERTS = N_GROUPS * EXPERTS_PER_GROUP
D_EXPERT = 256

LANES = 128
VMEM_LIMIT = 48 << 20
NEG_BIG = -1e30
ATT_SCALE = HEAD_DIM ** -0.5
TOKEN_TILE = 512


def _round_up(x, m):
    return (x + m - 1) // m * m


def _cparams(sem):
    return pltpu.CompilerParams(dimension_semantics=sem, vmem_limit_bytes=VMEM_LIMIT)


_NT = (((1,), (1,)), ((), ()))


def _mm_kernel(x_ref, w_ref, o_ref):
    o_ref[...] = jnp.dot(x_ref[...].astype(BF16), w_ref[...], preferred_element_type=F32)


def mm(x, w, tm=TOKEN_TILE):
    m, k = x.shape
    n = w.shape[1]
    tm = min(tm, _round_up(m, 8))
    mp = _round_up(m, tm)
    if mp != m:
        x = jnp.pad(x, ((0, mp - m), (0, 0)))
    out = pl.pallas_call(
        _mm_kernel,
        grid=(mp // tm,),
        in_specs=[pl.BlockSpec((tm, k), lambda i: (i, 0)),
                  pl.BlockSpec((k, n), lambda i: (0, 0))],
        out_specs=pl.BlockSpec((tm, n), lambda i: (i, 0)),
        out_shape=jax.ShapeDtypeStruct((mp, n), F32),
        compiler_params=_cparams(("arbitrary",)),
        name="mm",
    )(x, w.astype(BF16))
    return out[:m] if mp != m else out


SCAN_TC = 32


def _scan_kernel(r_ref, d_ref, k_ref, b_ref, n_ref, v_ref, rk_ref, gw_ref, gb_ref, s0_ref, o_ref, st_ref, s_scr,
                 *, tc, dot_dtype):
    c = pl.program_id(1)
    rnd = lambda x: x.astype(dot_dtype).astype(F32)
    halves = [(h0, h0 + RW_HEAD // 2) for h0 in (0, RW_HEAD // 2)]

    @pl.when(c == 0)
    def _():
        s_scr[...] = s0_ref[...]

    def step(t, carry):
        vv = v_ref[t]
        outs = []
        for lo, hi in halves:
            sa = jnp.zeros((hi - lo, LANES), F32)
            for j in range(RW_HEAD):
                sa = sa + rnd(s_scr[j, lo:hi]) * rnd(n_ref[t, pl.ds(j, 1), :])
            vh = vv[lo:hi]
            out = jnp.zeros((hi - lo, LANES), F32)
            for j in range(RW_HEAD):
                sn = (s_scr[j, lo:hi] * d_ref[t, pl.ds(j, 1), :] + sa * b_ref[t, pl.ds(j, 1), :]
                      + vh * k_ref[t, pl.ds(j, 1), :])
                s_scr[j, lo:hi] = sn
                out = out + rnd(sn) * rnd(r_ref[t, pl.ds(j, 1), :])
            outs.append(out)
        out = jnp.concatenate(outs, axis=0)
        mean = jnp.mean(out, axis=0, keepdims=True)
        cen = out - mean
        var = jnp.mean(cen * cen, axis=0, keepdims=True)
        bonus = jnp.sum(r_ref[t] * k_ref[t] * rk_ref[...], axis=0, keepdims=True) * vv
        o_ref[t] = cen * lax.rsqrt(var + RW_GN_EPS) * gw_ref[...] + gb_ref[...] + bonus
        return carry

    lax.fori_loop(0, tc, step, 0)

    @pl.when(c == pl.num_programs(1) - 1)
    def _():
        st_ref[...] = s_scr[...]


def rwkv_scan(r, d, k, b, nkk, v, rk, gw, gb, s0, dot_dtype=F32):
    t, _, l = r.shape
    tc = min(SCAN_TC, t)
    assert t % tc == 0 and l % LANES == 0
    seq = pl.BlockSpec((tc, RW_HEAD, LANES), lambda g, c: (c, 0, g))
    vec = pl.BlockSpec((RW_HEAD, LANES), lambda g, c: (0, g))
    st = pl.BlockSpec((RW_HEAD, RW_HEAD, LANES), lambda g, c: (0, 0, g))
    return pl.pallas_call(
        functools.partial(_scan_kernel, tc=tc, dot_dtype=dot_dtype),
        grid=(l // LANES, t // tc),
        in_specs=[seq] * 6 + [vec] * 3 + [st],
        out_specs=[seq, st],
        out_shape=[jax.ShapeDtypeStruct((t, RW_HEAD, l), F32),
                   jax.ShapeDtypeStruct((RW_HEAD, RW_HEAD, l), F32)],
        scratch_shapes=[pltpu.VMEM((RW_HEAD, RW_HEAD, LANES), F32)],
        compiler_params=_cparams(("arbitrary", "arbitrary")),
        name="rwkv_scan",
    )(r, d, k, b, nkk, v, rk, gw, gb, s0)


RW_TM = 256


def _split_dot(x, w):
    hi = x.astype(BF16)
    lo = (x - hi.astype(F32)).astype(BF16)
    return jnp.dot(hi, w, preferred_element_type=F32) + jnp.dot(lo, w, preferred_element_type=F32)


def _rwkv_pre_kernel(x_ref, ps_ref, ln_ref, mu_ref, vec_ref, wr_ref, wk_ref, wv_ref, w1_ref, w2_ref, a1_ref, a2_ref,
                     g1_ref, g2_ref, seg_ref, segt_ref,
                     r_o, d_o, k_o, b_o, n_o, v_o, g_o, xn_o, carry_scr, *, tiles_per_seq, n_seq_tiles):
    i = pl.program_id(0)
    tm = x_ref.shape[0]

    @pl.when(i == 0)
    def _():
        carry_scr[...] = jnp.zeros_like(carry_scr)

    x = x_ref[...]
    xn = x * lax.rsqrt(jnp.mean(x * x, axis=-1, keepdims=True) + RMS_EPS) * ln_ref[...]
    xn_o[...] = xn
    first = jnp.where(i % tiles_per_seq == 0, jnp.zeros_like(carry_scr[...]), carry_scr[...])
    row0 = lax.broadcasted_iota(jnp.int32, xn.shape, 0) == 0
    prev = jnp.where(row0, first, pltpu.roll(xn, 1, 0))
    prev = jnp.where(i >= n_seq_tiles, ps_ref[...], prev)
    carry_scr[...] = xn[tm - 1:tm, :]
    xx = prev - xn
    mix = lambda c: (xn + xx * mu_ref[c:c + 1, :]).astype(BF16)
    dot = lambda a, w_ref: jnp.dot(a, w_ref[...], preferred_element_type=F32)
    w0, a0, k_k, k_a = (vec_ref[c:c + 1, :] for c in range(4))
    r = dot(mix(0), wr_ref)
    k = dot(mix(2), wk_ref)
    v = dot(mix(3), wv_ref)
    w_log = -jax.nn.softplus(-(w0 + dot(jnp.tanh(dot(mix(1), w1_ref)).astype(BF16), w2_ref))) - 0.5
    a = jax.nn.sigmoid(a0 + dot(dot(mix(4), a1_ref).astype(BF16), a2_ref))
    g_o[...] = dot(jax.nn.sigmoid(dot(mix(5), g1_ref)).astype(BF16), g2_ref)
    kk = k * k_k
    ss = _split_dot(_split_dot(kk * kk, seg_ref[...]), segt_ref[...])
    kk = kk / jnp.maximum(jnp.sqrt(ss), 1e-12)
    r_o[...] = r
    d_o[...] = jnp.exp(-jnp.exp(w_log))
    k_o[...] = k * (1.0 + (a - 1.0) * k_a)
    b_o[...] = kk * a
    n_o[...] = -kk
    v_o[...] = v


def rwkv_pre(h, prev_single, n_seq_rows, t, ln, mu, w_r, w_k, w_v, w0, w1, w2, a0, a1, a2, g1, g2, k_k, k_a):
    n, dm = h.shape
    tm = RW_TM
    assert n % tm == 0 and t % tm == 0 and n_seq_rows % tm == 0 and prev_single.shape == (tm, dm)
    seg = (np.arange(dm)[:, None] // RW_HEAD == np.arange(LANES)[None, :]).astype(np.float32)
    row = pl.BlockSpec((tm, dm), lambda i: (i, 0))
    full = lambda a: pl.BlockSpec(a.shape, lambda i: (0,) * a.ndim)
    bf = lambda w: w.astype(BF16)
    consts = [prev_single, ln[None, :], mu, jnp.stack([w0, a0, k_k, k_a]), bf(w_r), bf(w_k), bf(w_v), bf(w1), bf(w2),
              bf(a1), bf(a2), bf(g1), bf(g2), jnp.asarray(seg, BF16), jnp.asarray(seg.T, BF16)]
    return pl.pallas_call(
        functools.partial(_rwkv_pre_kernel, tiles_per_seq=t // tm, n_seq_tiles=n_seq_rows // tm),
        grid=(n // tm,),
        in_specs=[row] + [full(a) for a in consts],
        out_specs=[row] * 8,
        out_shape=[jax.ShapeDtypeStruct((n, dm), F32)] * 8,
        scratch_shapes=[pltpu.VMEM((1, dm), F32)],
        compiler_params=_cparams(("arbitrary",)),
        name="rwkv_pre",
    )(h, *consts)


def _mix_out_kernel(*refs, n_branch, use_gates):
    o_refs = refs[:n_branch]
    gate_ref, h_ref, wo_ref, ln_ref, wr_ref = refs[n_branch:n_branch + 5]
    e_refs = refs[n_branch + 5:n_branch + 5 + (n_branch if use_gates else 0)]
    h_o, xp_o, lg_o = refs[-3:]
    if use_gates:
        sig = jax.nn.sigmoid(gate_ref[...])
        mixed = sum(_split_dot(sig, e_refs[b][...]) * o_refs[b][...] for b in range(n_branch))
    else:
        mixed = o_refs[0][...] * gate_ref[...]
    h = h_ref[...] + jnp.dot(mixed.astype(BF16), wo_ref[...], preferred_element_type=F32)
    h_o[...] = h
    xn = h * lax.rsqrt(jnp.mean(h * h, axis=-1, keepdims=True) + RMS_EPS) * ln_ref[...]
    xb = xn.astype(BF16)
    lg_o[...] = jnp.dot(xb, wr_ref[...], preferred_element_type=F32)
    u = lax.bitcast_convert_type(xb.astype(F32), jnp.uint32)
    half = u.shape[1] // 2
    xp_o[...] = (u[:, :half] >> 16) | (u[:, half:] & jnp.uint32(0xFFFF0000))


def mix_out(branches, gate, h, w_o, ln, w_router, gate_expand=None, gate_col=0):
    n, dm = h.shape
    tm = RW_TM
    nb = len(branches)
    row = lambda a: pl.BlockSpec((tm, a.shape[1]), lambda i: (i, 0))
    full = lambda a: pl.BlockSpec(a.shape, lambda i: (0,) * a.ndim)
    consts = [w_o.astype(BF16), ln[None, :], w_router.astype(BF16)] + list(gate_expand or [])
    gate_spec = row(gate) if gate_expand is None else pl.BlockSpec((tm, LANES), lambda i: (i, gate_col))
    return pl.pallas_call(
        functools.partial(_mix_out_kernel, n_branch=nb, use_gates=gate_expand is not None),
        grid=(n // tm,),
        in_specs=[row(a) for a in branches] + [gate_spec, row(h)] + [full(a) for a in consts],
        out_specs=[pl.BlockSpec((tm, dm), lambda i: (i, 0)), pl.BlockSpec((tm, dm // 2), lambda i: (i, 0)),
                   pl.BlockSpec((tm, LANES), lambda i: (i, 0))],
        out_shape=[jax.ShapeDtypeStruct((n, dm), F32), jax.ShapeDtypeStruct((n, dm // 2), jnp.uint32),
                   jax.ShapeDtypeStruct((n, LANES), F32)],
        compiler_params=_cparams(("arbitrary",)),
        name="mix_out",
    )(*branches, gate, h, *consts)


MOE_TM = 256
MOE_VMEM_LIMIT = 56 << 20


def _moe_kernel(te_ref, nt_ref, tok_ref, x_ref, c_ref, wgu_ref, wd_ref, o_ref, xt_scr):
    i = pl.program_id(0)

    @pl.when(i < nt_ref[0])
    def _():
        base = i * MOE_TM

        def gather_row(r, carry):
            xt_scr[pl.ds(r, 1), :] = x_ref[pl.ds(tok_ref[base + r], 1), :]
            return carry

        lax.fori_loop(0, MOE_TM, gather_row, 0, unroll=8)
        u = xt_scr[...]
        lo = lax.bitcast_convert_type(u << 16, F32).astype(BF16)
        hi = lax.bitcast_convert_type(u & jnp.uint32(0xFFFF0000), F32).astype(BF16)
        x = jnp.concatenate([lo, hi], axis=1)
        h = jnp.dot(x, wgu_ref[0], preferred_element_type=F32)
        g = h[:, :D_EXPERT]
        u2 = h[:, D_EXPERT:]
        hid = g * jax.nn.sigmoid(g) * u2 * c_ref[...]
        o_ref[...] = jnp.dot(hid.astype(BF16), wd_ref[0], preferred_element_type=F32)

    @pl.when(i >= nt_ref[0])
    def _():
        o_ref[...] = jnp.zeros_like(o_ref)


def moe_gmm(xp, row_tok, cs, tile_expert, n_tiles_used, wgu, wd):
    n, half = xp.shape
    dm = 2 * half
    p = row_tok.shape[0]
    nt = p // MOE_TM

    def row_map(i, te, ntu, tok):
        return (jnp.minimum(i, ntu[0] - 1), 0)

    grid_spec = pltpu.PrefetchScalarGridSpec(
        num_scalar_prefetch=3,
        grid=(nt,),
        in_specs=[pl.BlockSpec((n, half), lambda i, te, ntu, tok: (0, 0), pipeline_mode=pl.Buffered(1)),
                  pl.BlockSpec((MOE_TM, 1), row_map),
                  pl.BlockSpec((1, dm, 2 * D_EXPERT), lambda i, te, ntu, tok: (te[i], 0, 0)),
                  pl.BlockSpec((1, D_EXPERT, dm), lambda i, te, ntu, tok: (te[i], 0, 0))],
        out_specs=pl.BlockSpec((MOE_TM, dm), lambda i, te, ntu, tok: (i, 0)),
        scratch_shapes=[pltpu.VMEM((MOE_TM, half), jnp.uint32)],
    )
    return pl.pallas_call(
        _moe_kernel,
        grid_spec=grid_spec,
        out_shape=jax.ShapeDtypeStruct((p, dm), F32),
        compiler_params=pltpu.CompilerParams(dimension_semantics=("arbitrary",), vmem_limit_bytes=MOE_VMEM_LIMIT),
        name="moe_gmm",
    )(tile_expert, n_tiles_used, row_tok, xp, cs, wgu, wd)


def router_weight(w_rg, w_re):
    w_r = jnp.concatenate([w_rg, w_re], axis=1)
    return jnp.pad(w_r, ((0, 0), (0, LANES - w_r.shape[1])))


def hmoe(xp, logits, b_rg, b_re, w_gate, w_up, w_down):
    n = xp.shape[0]
    dm = 2 * xp.shape[1]
    lg = logits[:, :N_GROUPS] + b_rg
    le = (logits[:, N_GROUPS:N_GROUPS + N_EXPERTS] + b_re).reshape(n, N_GROUPS, EXPERTS_PER_GROUP)
    g_sel = jnp.argmax(lg, axis=-1)
    p_g = jnp.max(jax.nn.softmax(lg, axis=-1), axis=-1)
    le_g = jnp.take_along_axis(le, g_sel[:, None, None], axis=1)[:, 0]
    top_p, top_i = lax.top_k(jax.nn.softmax(le_g, axis=-1), 2)
    w_top = p_g[:, None] * top_p / jnp.sum(top_p, axis=-1, keepdims=True)
    eid = (g_sel[:, None] * EXPERTS_PER_GROUP + top_i).astype(jnp.int32).reshape(-1)
    cw = w_top.reshape(-1)

    n2 = 2 * n
    onehot = (eid[:, None] == jnp.arange(N_EXPERTS, dtype=jnp.int32)[None, :]).astype(jnp.int32)
    counts = jnp.sum(onehot, axis=0)
    rank = jnp.take_along_axis(jnp.cumsum(onehot, axis=0) - onehot, eid[:, None], axis=1)[:, 0]
    tiles_per = (counts + MOE_TM - 1) // MOE_TM
    tile_end = jnp.cumsum(tiles_per)
    tile_start = tile_end - tiles_per
    pos = tile_start[eid] * MOE_TM + rank
    p_rows = _round_up(n2 + N_EXPERTS * (MOE_TM - 1), MOE_TM)
    nt = p_rows // MOE_TM
    n_used = tile_end[-1].astype(jnp.int32)
    tile_ids = jnp.arange(nt, dtype=jnp.int32)
    tile_expert = jnp.sum((tile_end[None, :] <= jnp.minimum(tile_ids, n_used - 1)[:, None]).astype(jnp.int32), axis=1)

    tok = jnp.arange(n2, dtype=jnp.int32) // 2
    row_tok = jnp.zeros((p_rows,), jnp.int32).at[pos].set(tok)
    row_c = jnp.zeros((p_rows,), F32).at[pos].set(cw)
    wgu = jnp.concatenate([w_gate, w_up], axis=-1).reshape(N_EXPERTS, dm, 2 * D_EXPERT).astype(BF16)
    wd = w_down.reshape(N_EXPERTS, D_EXPERT, dm).astype(BF16)
    ys = moe_gmm(xp, row_tok, row_c[:, None], tile_expert, n_used.reshape(1), wgu, wd)
    pos2 = pos.reshape(n, 2)
    return ys[pos2[:, 0]] + ys[pos2[:, 1]]


ATT_TQ = 256
ATT_TK = 512


def _cmp_sel_kernel(q_ref, kc_ref, vct_ref, ovt_ref, o_ref, sel_ref, *, n_cmp, n_sblk, tq):
    t0 = pl.program_id(2) * tq
    ncp = kc_ref.shape[2]
    n_io = lax.broadcasted_iota(jnp.int32, (ncp, tq), 0)
    t_io = lax.broadcasted_iota(jnp.int32, (ncp, tq), 1) + t0
    mask = (n_io * CMP_STRIDE + (CMP_BLOCK - 1) <= t_io) & (n_io < n_cmp)
    kc = kc_ref[0, 0]
    vct = vct_ref[0, 0]
    ovt = ovt_ref[...]
    imp = jnp.zeros((n_sblk, tq), F32)
    outs = []
    for r in range(GQA_R):
        qr = (q_ref[:, r * HEAD_DIM:(r + 1) * HEAD_DIM] * ATT_SCALE).astype(BF16)
        st = lax.dot_general(kc, qr, _NT, preferred_element_type=F32)
        st = jnp.where(mask, st, NEG_BIG)
        m = jnp.max(st, axis=0, keepdims=True)
        e = jnp.where(mask, jnp.exp(st - m), 0.0)
        p = (e / jnp.maximum(jnp.sum(e, axis=0, keepdims=True), 1e-30)).astype(BF16)
        outs.append(jnp.dot(vct, p, preferred_element_type=F32).T)
        imp = imp + jnp.dot(ovt, p, preferred_element_type=F32)
    o_ref[...] = jnp.concatenate(outs, axis=1)

    j_io = lax.broadcasted_iota(jnp.int32, (n_sblk, tq), 0)
    qblk = jnp.right_shift(lax.broadcasted_iota(jnp.int32, (n_sblk, tq), 1) + t0, SEL_BLOCK.bit_length() - 1)
    allowed = j_io <= qblk
    forced = (j_io == 0) | ((j_io > qblk - N_LOCAL_BLOCKS) & allowed)
    score = jnp.where(forced, FORCE_SCORE, imp)
    score = jnp.where(allowed, score, -jnp.inf)
    rank = jnp.zeros((n_sblk, tq), jnp.int32)
    for jp in range(n_sblk):
        row = score[jp:jp + 1, :]
        beats = (row > score) | ((row == score) & (j_io > jp))
        rank = rank + jnp.where(beats, 1, 0)
    sel = (rank < N_SEL) & allowed
    sel_ref[0, 0] = jnp.where(sel, 1.0, 0.0).astype(BF16)


def cmp_select(q, kc, vct, ovt, n_b, t, n_cmp, n_sblk, n_out):
    assert SEL_BLOCK & (SEL_BLOCK - 1) == 0
    tq = ATT_TQ
    nq = t // tq
    ncp = kc.shape[2]
    gw = GQA_R * HEAD_DIM
    return pl.pallas_call(
        functools.partial(_cmp_sel_kernel, n_cmp=n_cmp, n_sblk=n_sblk, tq=tq),
        grid=(n_b, KV_HEADS, nq),
        in_specs=[pl.BlockSpec((tq, gw), lambda b, g, qi: (b * nq + qi, g)),
                  pl.BlockSpec((1, 1, ncp, HEAD_DIM), lambda b, g, qi: (b, g, 0, 0)),
                  pl.BlockSpec((1, 1, HEAD_DIM, ncp), lambda b, g, qi: (b, g, 0, 0)),
                  pl.BlockSpec((n_sblk, ncp), lambda b, g, qi: (0, 0))],
        out_specs=[pl.BlockSpec((tq, gw), lambda b, g, qi: (b * nq + qi, g)),
                   pl.BlockSpec((1, 1, n_sblk, tq), lambda b, g, qi: (b, g, 0, qi))],
        out_shape=[jax.ShapeDtypeStruct((n_out, N_HEADS * HEAD_DIM), F32),
                   jax.ShapeDtypeStruct((n_b, KV_HEADS, n_sblk, t), BF16)],
        compiler_params=_cparams(("arbitrary", "arbitrary", "arbitrary")),
        name="attn_cmp_select",
    )(q, kc, vct, ovt)


def _flash_kernel(pq_ref, pk_ref, pfirst_ref, plast_ref, q_ref, k_ref, vt_ref, *rest, window, use_sel, tq, tk):
    if use_sel:
        sel_ref, et_ref, o_ref, m_scr, l_scr, acc_scr = rest
    else:
        o_ref, m_scr, l_scr, acc_scr = rest
    p_id = pl.program_id(2)

    @pl.when(pfirst_ref[p_id] == 1)
    def _():
        m_scr[...] = jnp.full_like(m_scr, NEG_BIG)
        l_scr[...] = jnp.zeros_like(l_scr)
        acc_scr[...] = jnp.zeros_like(acc_scr)

    off = pq_ref[p_id] * tq - pk_ref[p_id] * tk
    diff = lax.broadcasted_iota(jnp.int32, (tk, tq), 0) - lax.broadcasted_iota(jnp.int32, (tk, tq), 1)
    mask = diff <= off
    if window is not None:
        mask = mask & (diff > off - window)
    if use_sel:
        selm = jnp.dot(et_ref[...], sel_ref[0, 0], preferred_element_type=F32)
        mask = mask & (selm > 0.5)
    k = k_ref[0, 0]
    vt = vt_ref[0, 0]
    heads = range(GQA_R)
    qs = [(q_ref[:, r * HEAD_DIM:(r + 1) * HEAD_DIM] * ATT_SCALE).astype(BF16) for r in heads]
    sts = [jnp.where(mask, lax.dot_general(k, qs[r], _NT, preferred_element_type=F32), NEG_BIG) for r in heads]
    m_prev = [m_scr[r] for r in heads]
    m_new = [jnp.maximum(m_prev[r], jnp.max(sts[r], axis=0, keepdims=True)) for r in heads]
    ps = [jnp.where(mask, jnp.exp(sts[r] - m_new[r]), 0.0) for r in heads]
    alphas = [jnp.exp(m_prev[r] - m_new[r]) for r in heads]
    for r in heads:
        l_scr[r] = alphas[r] * l_scr[r] + jnp.sum(ps[r], axis=0, keepdims=True)
        acc_scr[r] = alphas[r] * acc_scr[r] + jnp.dot(vt, ps[r].astype(BF16), preferred_element_type=F32)
        m_scr[r] = m_new[r]

    @pl.when(plast_ref[p_id] == 1)
    def _():
        outs = [(acc_scr[r] / jnp.maximum(l_scr[r], 1e-30)).T for r in range(GQA_R)]
        o_ref[...] = jnp.concatenate(outs, axis=1)


def flash_attention(q, k, vt, n_b, t, n_out, window=None, sel=None):
    tq, tk = ATT_TQ, ATT_TK
    nq = t // tq
    first_tile = lambda qi: 0 if window is None else max(0, qi * tq - (window - 1)) // tk
    pairs = [(qi, kt) for qi in range(nq) for kt in range(first_tile(qi), (qi * tq + tq - 1) // tk + 1)]
    pq = np.array([p[0] for p in pairs], np.int32)
    pk = np.array([p[1] for p in pairs], np.int32)
    pfirst = np.array([1 if i == 0 or pairs[i - 1][0] != p[0] else 0 for i, p in enumerate(pairs)], np.int32)
    plast = np.array([1 if i == len(pairs) - 1 or pairs[i + 1][0] != p[0] else 0 for i, p in enumerate(pairs)], np.int32)
    gw = GQA_R * HEAD_DIM

    in_specs = [pl.BlockSpec((tq, gw), lambda b, g, p, pq_r, pk_r, f_r, l_r: (b * nq + pq_r[p], g)),
                pl.BlockSpec((1, 1, tk, HEAD_DIM), lambda b, g, p, pq_r, pk_r, f_r, l_r: (b, g, pk_r[p], 0)),
                pl.BlockSpec((1, 1, HEAD_DIM, tk), lambda b, g, p, pq_r, pk_r, f_r, l_r: (b, g, 0, pk_r[p]))]
    args = [q, k, vt]
    if sel is not None:
        n_sblk = sel.shape[2]
        emat = (np.arange(t)[:, None] // SEL_BLOCK == np.arange(n_sblk)[None, :]).astype(np.float32)
        in_specs += [pl.BlockSpec((1, 1, n_sblk, tq), lambda b, g, p, pq_r, pk_r, f_r, l_r: (b, g, 0, pq_r[p])),
                     pl.BlockSpec((tk, n_sblk), lambda b, g, p, pq_r, pk_r, f_r, l_r: (pk_r[p], 0))]
        args += [sel, jnp.asarray(emat, BF16)]
    grid_spec = pltpu.PrefetchScalarGridSpec(
        num_scalar_prefetch=4,
        grid=(n_b, KV_HEADS, len(pairs)),
        in_specs=in_specs,
        out_specs=pl.BlockSpec((tq, gw), lambda b, g, p, pq_r, pk_r, f_r, l_r: (b * nq + pq_r[p], g)),
        scratch_shapes=[pltpu.VMEM((GQA_R, 1, tq), F32), pltpu.VMEM((GQA_R, 1, tq), F32),
                        pltpu.VMEM((GQA_R, HEAD_DIM, tq), F32)],
    )
    return pl.pallas_call(
        functools.partial(_flash_kernel, window=window, use_sel=sel is not None, tq=tq, tk=tk),
        grid_spec=grid_spec,
        out_shape=jax.ShapeDtypeStruct((n_out, N_HEADS * HEAD_DIM), F32),
        compiler_params=_cparams(("arbitrary", "arbitrary", "arbitrary")),
        name="attn_sel" if sel is not None else "attn_win",
    )(jnp.asarray(pq), jnp.asarray(pk), jnp.asarray(pfirst), jnp.asarray(plast), *args)


def _cmp_kernel(q_ref, k_ref, v_ref, ov_ref, o_ref, imp_ref, *, n_cmp):
    q = q_ref[0] * ATT_SCALE
    s = lax.dot_general(q, k_ref[0], _NT, preferred_element_type=F32)
    mask = lax.broadcasted_iota(jnp.int32, s.shape, 1) < n_cmp
    s = jnp.where(mask, s, NEG_BIG)
    m = jnp.max(s, axis=-1, keepdims=True)
    e = jnp.where(mask, jnp.exp(s - m), 0.0)
    p = (e / jnp.maximum(jnp.sum(e, axis=-1, keepdims=True), 1e-30)).astype(BF16)
    o_ref[0] = jnp.dot(p, v_ref[0], preferred_element_type=F32)
    imp_ref[0] = jnp.dot(p, ov_ref[...], preferred_element_type=F32)


def cmp_attention_sample(q, kc, vc, ov, n_cmp):
    bg, rq, hd = q.shape
    nc = kc.shape[1]
    nb = ov.shape[1]
    return pl.pallas_call(
        functools.partial(_cmp_kernel, n_cmp=n_cmp),
        grid=(bg,),
        in_specs=[pl.BlockSpec((1, rq, hd), lambda b: (b, 0, 0)),
                  pl.BlockSpec((1, nc, hd), lambda b: (b, 0, 0)),
                  pl.BlockSpec((1, nc, hd), lambda b: (b, 0, 0)),
                  pl.BlockSpec((nc, nb), lambda b: (0, 0))],
        out_specs=[pl.BlockSpec((1, rq, hd), lambda b: (b, 0, 0)),
                   pl.BlockSpec((1, rq, nb), lambda b: (b, 0, 0))],
        out_shape=[jax.ShapeDtypeStruct((bg, rq, hd), F32),
                   jax.ShapeDtypeStruct((bg, rq, nb), F32)],
        compiler_params=_cparams(("arbitrary",)),
        name="attn_cmp_sample",
    )(q, kc, vc, ov)


PAGES_PER_STEP = 8


def _page_compress_kernel(pt_ref, *rest, n_pages, page, pb):
    c_refs = rest[:pb]
    wbd_ref, pe_ref, w2_ref, o_ref, a_scr = rest[pb:]
    p = pl.program_id(1)
    nsub = n_pages * page // CMP_STRIDE
    for i in range(pb):
        row0 = pl.multiple_of((p * pb + i) * page, page)
        for s in range(2):
            for gp in range(KV_HEADS // 2):
                x = c_refs[i][0, s, 2 * gp:2 * gp + 2].reshape(2 * HEAD_DIM, page)
                a_scr[s, gp, pl.ds(row0, page), :] = x.T

    @pl.when(p == n_pages // pb - 1)
    def _():
        hid = pe_ref.shape[-1]
        for s in range(2):
            for gp in range(KV_HEADS // 2):
                acc = jnp.zeros((nsub, 4 * hid), F32)
                for j in range(CMP_STRIDE):
                    lhs = a_scr[s, gp, pl.ds(j, nsub, stride=CMP_STRIDE), :].astype(BF16)
                    acc = acc + jnp.dot(lhs, wbd_ref[s, j], preferred_element_type=F32)
                for g2 in range(2):
                    first = acc[:, 2 * g2 * hid:(2 * g2 + 1) * hid]
                    second = acc[:, (2 * g2 + 1) * hid:(2 * g2 + 2) * hid]
                    pre = first + pltpu.roll(second, nsub - 1, 0) + pe_ref[s]
                    act = jax.nn.gelu(pre).astype(BF16)
                    o_ref[s, 0, 2 * gp + g2] = jnp.dot(act, w2_ref[s], preferred_element_type=F32).astype(BF16)


def page_compress(cache_t, page_table, cmp_pe, cmp_w1, cmp_b1, cmp_w2):
    n_b, n_pages = page_table.shape
    page = cache_t.shape[-1]
    nsub = n_pages * page // CMP_STRIDE
    hid = cmp_w1.shape[-1]
    pb = PAGES_PER_STEP
    assert n_pages % pb == 0
    w1r = cmp_w1.reshape(2, 2, CMP_STRIDE, HEAD_DIM, hid).transpose(0, 2, 3, 1, 4)
    w1r = w1r.reshape(2, CMP_STRIDE, HEAD_DIM, 2 * hid)
    zeros = jnp.zeros_like(w1r)
    wbd = jnp.concatenate([jnp.concatenate([w1r, zeros], axis=-1), jnp.concatenate([zeros, w1r], axis=-1)], axis=2)
    pe_term = (jnp.einsum('sjd,sjdc->sc', cmp_pe, cmp_w1.reshape(2, CMP_BLOCK, HEAD_DIM, hid)) + cmp_b1)[:, None, :]
    def page_map(i):
        return lambda b, p, pt: (pt[b * n_pages + p * pb + i], 0, 0, 0, 0)

    grid_spec = pltpu.PrefetchScalarGridSpec(
        num_scalar_prefetch=1,
        grid=(n_b, n_pages // pb),
        in_specs=[pl.BlockSpec((1, 2, KV_HEADS, HEAD_DIM, page), page_map(i)) for i in range(pb)] + [
            pl.BlockSpec((2, CMP_STRIDE, 2 * HEAD_DIM, 4 * hid), lambda b, p, pt: (0, 0, 0, 0)),
            pl.BlockSpec((2, 1, hid), lambda b, p, pt: (0, 0, 0)),
            pl.BlockSpec((2, hid, HEAD_DIM), lambda b, p, pt: (0, 0, 0))],
        out_specs=pl.BlockSpec((2, 1, KV_HEADS, nsub, HEAD_DIM), lambda b, p, pt: (0, b, 0, 0, 0)),
        scratch_shapes=[pltpu.VMEM((2, KV_HEADS // 2, n_pages * page, 2 * HEAD_DIM), F32)],
    )
    return pl.pallas_call(
        functools.partial(_page_compress_kernel, n_pages=n_pages, page=page, pb=pb),
        grid_spec=grid_spec,
        out_shape=jax.ShapeDtypeStruct((2, n_b, KV_HEADS, nsub, HEAD_DIM), BF16),
        compiler_params=_cparams(("arbitrary", "arbitrary")),
        name="page_compress",
    )(page_table.reshape(-1), *([cache_t] * pb), wbd.astype(BF16), pe_term, cmp_w2.astype(BF16))


def _decode_attn_kernel(phys_ref, blk_ref, flag_ref, q_ref, kn_ref, vn_ref, *rest, mode, past, steps, window, page):
    kt_refs = rest[:steps]
    vt_refs = rest[steps:2 * steps]
    o_ref = rest[2 * steps]
    idx = pl.program_id(0) * KV_HEADS + pl.program_id(1)
    rnd = lambda x: x.astype(BF16).astype(F32)
    q = q_ref[0, 0] * ATT_SCALE

    kn = rnd(kn_ref[0, 0])
    vn = rnd(vn_ref[0, 0])
    s_new = jnp.sum(q.astype(F32) * kn, axis=-1, keepdims=True)
    on = flag_ref[idx] == 1
    m = jnp.where(on, s_new, NEG_BIG)
    scs, masks = [], []
    for s in range(steps):
        kt = kt_refs[s][0, 0, 0].astype(BF16)
        tile = kt.shape[-1]
        sc = jnp.dot(q, kt, preferred_element_type=F32)
        lane = lax.broadcasted_iota(jnp.int32, sc.shape, 1)
        if mode == "sel":
            blk = blk_ref[idx * steps + s]
            per_page = page // SEL_BLOCK
            pos = (blk // per_page) * page + lane
            mask = (jnp.right_shift(pos, SEL_BLOCK.bit_length() - 1) == blk) & (pos <= past) & (blk >= 0)
        else:
            pos = past - tile + lane
            mask = (pos > past - window) & (pos <= past)
        sc = jnp.where(mask, sc, NEG_BIG)
        m = jnp.maximum(m, jnp.max(sc, axis=-1, keepdims=True))
        scs.append(sc)
        masks.append(mask)
    p_new = jnp.where(on, jnp.exp(s_new - m), 0.0)
    l = p_new
    acc = rnd(p_new) * vn
    for s in range(steps):
        p = jnp.where(masks[s], jnp.exp(scs[s] - m), 0.0)
        l = l + jnp.sum(p, axis=-1, keepdims=True)
        vt = vt_refs[s][0, 0, 0].astype(BF16)
        acc = acc + lax.dot_general(p.astype(BF16), vt, _NT, preferred_element_type=F32)
    o_ref[0, 0] = acc / jnp.maximum(l, 1e-30)


def decode_attention(q, k_new, v_new, kv_t, streams, phys, blk, flag, mode, past, steps, window=None, page=None):
    n_b, n_g, rows, hd = q.shape
    tile = kv_t.shape[-1]
    ks, vs = streams

    def kmap(st, s):
        if mode == "sel":
            return lambda b, g, ph, bl, fl: (ph[(b * n_g + g) * steps + s], st, g, 0, 0)
        return lambda b, g, ph, bl, fl: (b, st, g, 0, 0)

    tiles = ([pl.BlockSpec((1, 1, 1, hd, tile), kmap(ks, s)) for s in range(steps)]
             + [pl.BlockSpec((1, 1, 1, hd, tile), kmap(vs, s)) for s in range(steps)])
    grid_spec = pltpu.PrefetchScalarGridSpec(
        num_scalar_prefetch=3,
        grid=(n_b, n_g),
        in_specs=[pl.BlockSpec((1, 1, rows, hd), lambda b, g, ph, bl, fl: (b, g, 0, 0)),
                  pl.BlockSpec((1, 1, 1, hd), lambda b, g, ph, bl, fl: (b, g, 0, 0)),
                  pl.BlockSpec((1, 1, 1, hd), lambda b, g, ph, bl, fl: (b, g, 0, 0))] + tiles,
        out_specs=pl.BlockSpec((1, 1, rows, hd), lambda b, g, ph, bl, fl: (b, g, 0, 0)),
    )
    return pl.pallas_call(
        functools.partial(_decode_attn_kernel, mode=mode, past=past, steps=steps, window=window, page=page),
        grid_spec=grid_spec,
        out_shape=jax.ShapeDtypeStruct((n_b, n_g, rows, hd), F32),
        compiler_params=_cparams(("arbitrary", "arbitrary")),
        name="attn_decode_" + mode,
    )(phys, blk, flag, q, k_new, v_new, *([kv_t] * (2 * steps)))


def rmsnorm(x, g):
    y = x * lax.rsqrt(jnp.mean(x * x, axis=-1, keepdims=True) + RMS_EPS)
    return y * g


def rope(x, pos):
    half = ROPE_DIMS // 2
    inv = jnp.power(ROPE_THETA, -jnp.arange(half, dtype=F32) * (2.0 / ROPE_DIMS))
    ang = pos.astype(F32)[:, None] * inv[None, :]
    cos = jnp.cos(ang)[:, None, :]
    sin = jnp.sin(ang)[:, None, :]
    x1 = x[..., :half]
    x2 = x[..., half:ROPE_DIMS]
    return jnp.concatenate([x1 * cos - x2 * sin, x2 * cos + x1 * sin, x[..., ROPE_DIMS:]], axis=-1)


def compress(sub, pe, w1, b1, w2, n_b, nsub):
    half = CMP_STRIDE * HEAD_DIM
    w1cat = jnp.concatenate([w1[:half], w1[half:]], axis=1)
    hidden = w1.shape[1]
    part = mm(sub, w1cat).reshape(n_b, nsub, KV_HEADS, 2, hidden)
    pe_term = jnp.einsum('jd,jdc->c', pe, w1.reshape(CMP_BLOCK, HEAD_DIM, hidden)) + b1
    pre = part[:, :nsub - 1, :, 0] + part[:, 1:, :, 1] + pe_term
    act = jax.nn.gelu(pre)
    return mm(act.reshape(-1, hidden), w2).reshape(n_b, nsub - 1, KV_HEADS, HEAD_DIM)


def overlap_matrix(n_cmp, n_sblk, nc_pad, nb_pad):
    cs = np.arange(n_cmp) * CMP_STRIDE
    ce = cs + CMP_BLOCK - 1
    ss = np.arange(n_sblk) * SEL_BLOCK
    ov = ((cs[:, None] < ss[None, :] + SEL_BLOCK) & (ce[:, None] >= ss[None, :])).astype(np.float32)
    out = np.zeros((nc_pad, nb_pad), np.float32)
    out[:n_cmp, :n_sblk] = ov
    return out


def block_scores(imp, q_pos, n_sblk):
    j = jnp.arange(n_sblk)[None, :]
    qblk = (q_pos // SEL_BLOCK)[:, None]
    allowed = j <= qblk
    forced = (j == 0) | ((j > qblk - N_LOCAL_BLOCKS) & allowed)
    score = jnp.where(forced, FORCE_SCORE, imp)
    return jnp.where(allowed, score, -jnp.inf)


def nsa_prompt(qg, kv, n_b, t, cmp_pe, cmp_w1, cmp_b1, cmp_w2):
    n = n_b * t
    pos = jnp.arange(t)
    kv = kv[:n].reshape(n_b, t, N_KV_STREAMS, KV_HEADS, HEAD_DIM)
    k_sel = rope(kv[:, :, 2], pos)
    k_win = rope(kv[:, :, 4], pos)
    new_rows = jnp.stack([kv[:, :, 0], kv[:, :, 1], k_sel, kv[:, :, 3]], axis=2)
    new_win = jnp.stack([k_win, kv[:, :, 5]], axis=2)

    nsub = t // CMP_STRIDE
    n_cmp = nsub - 1
    n_sblk = t // SEL_BLOCK

    def sub_blocks(rows):
        return rows.reshape(n_b, nsub, CMP_STRIDE, KV_HEADS, HEAD_DIM).transpose(0, 1, 3, 2, 4).reshape(-1, CMP_STRIDE * HEAD_DIM)

    kc = compress(sub_blocks(kv[:, :, 0]), cmp_pe[0], cmp_w1[0], cmp_b1[0], cmp_w2[0], n_b, nsub)
    vc = compress(sub_blocks(kv[:, :, 1]), cmp_pe[1], cmp_w1[1], cmp_b1[1], cmp_w2[1], n_b, nsub)
    nc_pad = _round_up(n_cmp, LANES)
    kc = jnp.pad(kc, ((0, 0), (0, nc_pad - n_cmp), (0, 0), (0, 0)))
    vc = jnp.pad(vc, ((0, 0), (0, nc_pad - n_cmp), (0, 0), (0, 0)))
    ovt = jnp.asarray(overlap_matrix(n_cmp, n_sblk, nc_pad, n_sblk).T, BF16)
    o_cmp, sel = cmp_select(qg, kc.transpose(0, 2, 1, 3).astype(BF16), vc.transpose(0, 2, 3, 1).astype(BF16),
                            ovt, n_b, t, n_cmp, n_sblk, qg.shape[0])

    q_r = rope(qg[:n, :N_HEADS * HEAD_DIM].reshape(n_b, t, N_HEADS, HEAD_DIM), pos).reshape(n, N_HEADS * HEAD_DIM)
    rows_k = lambda x: x.transpose(0, 2, 1, 3).astype(BF16)
    rows_vt = lambda x: x.transpose(0, 2, 3, 1).astype(BF16)
    o_sel = flash_attention(q_r, rows_k(k_sel), rows_vt(kv[:, :, 3]), n_b, t, qg.shape[0], sel=sel)
    o_win = flash_attention(q_r, rows_k(k_win), rows_vt(kv[:, :, 5]), n_b, t, qg.shape[0], window=WINDOW)
    return (o_cmp, o_sel, o_win), new_rows, new_win


def nsa_sample(qg, kv, cache_kv, cache_win, page_table, cmp_pe, cmp_w1, cmp_b1, cmp_w2):
    n_b, n_pages = page_table.shape
    n_phys, page = cache_kv.shape[:2]
    past = n_pages * page
    pos = jnp.full((1,), past, jnp.int32)
    q = qg[:, :N_HEADS * HEAD_DIM].reshape(n_b, 1, N_HEADS, HEAD_DIM)
    kv = kv.reshape(n_b, 1, N_KV_STREAMS, KV_HEADS, HEAD_DIM)
    k_sel = rope(kv[:, :, 2], pos)
    k_win = rope(kv[:, :, 4], pos)
    new_rows = jnp.stack([kv[:, :, 0], kv[:, :, 1], k_sel, kv[:, :, 3]], axis=2)
    new_win = jnp.stack([k_win, kv[:, :, 5]], axis=2)
    n_keep = min(WINDOW, cache_win.shape[1] + 1)
    win_state = jnp.concatenate([cache_win[:, cache_win.shape[1] + 1 - n_keep:], new_win], axis=1)

    cache_t = cache_kv.transpose(0, 2, 3, 4, 1)
    win_t = cache_win.transpose(0, 2, 3, 4, 1)

    l_tot = past + 1
    nsub = l_tot // CMP_STRIDE
    n_cmp = nsub - 1
    n_sblk = -(-l_tot // SEL_BLOCK)
    assert nsub * CMP_STRIDE == past and page % SEL_BLOCK == 0 and cache_win.shape[1] == WINDOW
    kvc = page_compress(cache_t, page_table, cmp_pe, cmp_w1, cmp_b1, cmp_w2)

    bg = n_b * KV_HEADS
    rq = 16
    qpad = lambda x: jnp.pad(x.reshape(n_b, KV_HEADS, GQA_R, HEAD_DIM), ((0, 0), (0, 0), (0, rq - GQA_R), (0, 0))).astype(BF16)

    nb_pad = _round_up(n_sblk, 8)
    ov = jnp.asarray(overlap_matrix(n_cmp, n_sblk, nsub, nb_pad), BF16)
    o_cmp, imp = cmp_attention_sample(qpad(q).reshape(bg, rq, HEAD_DIM), kvc[0].reshape(bg, nsub, HEAD_DIM),
                                      kvc[1].reshape(bg, nsub, HEAD_DIM), ov, n_cmp)
    imp = imp[:, :GQA_R, :n_sblk].sum(axis=1)
    score = block_scores(imp[:, None, :], pos, n_sblk)
    top_s, top_i = lax.top_k(score[:, 0], min(N_SEL, n_sblk))
    valid = jnp.isfinite(top_s)

    n_past_blk = past // SEL_BLOCK
    per_page = page // SEL_BLOCK
    n_k = top_i.shape[-1]
    in_past = valid & (top_i < n_past_blk)
    blk = jnp.where(in_past, top_i, -1).astype(jnp.int32)
    logical = jnp.clip(top_i, 0, n_past_blk - 1) // per_page
    phys = jnp.take_along_axis(page_table, logical.reshape(n_b, KV_HEADS * n_k), axis=1).reshape(bg, n_k)
    phys = jnp.where(in_past, phys, 0).astype(jnp.int32)
    new_flag = jnp.any(valid & (top_i >= n_past_blk), axis=-1).astype(jnp.int32)
    q_r = qpad(rope(q, pos))
    o_sel = decode_attention(q_r, new_rows[:, 0, 2][:, :, None, :], new_rows[:, 0, 3][:, :, None, :], cache_t, (2, 3),
                             phys.reshape(-1), blk.reshape(-1), new_flag, "sel", past, n_k, page=page)
    zeros = jnp.zeros((bg,), jnp.int32)
    o_win = decode_attention(q_r, new_win[:, 0, 0][:, :, None, :], new_win[:, 0, 1][:, :, None, :], win_t, (0, 1),
                             zeros, zeros, jnp.ones((bg,), jnp.int32), "win", past, 1, window=WINDOW)

    heads = lambda o: o.reshape(n_b, KV_HEADS, -1, HEAD_DIM)[:, :, :GQA_R].reshape(n_b, N_HEADS * HEAD_DIM)
    return (heads(o_cmp), heads(o_sel), heads(o_win)), new_rows, win_state


def rwkv_layer(h, n_p, t_p, n_s, state_wkv, state_shift, ln, mu, w_r, w_k, w_v, w0, w1, w2, a0, a1, a2, g1, g2,
               k_k, k_a, r_k, gn_w, gn_b):
    n_tok = h.shape[0]
    np_rows = n_p * t_p
    prev_single = jnp.pad(state_shift, ((0, RW_TM - n_s), (0, 0)))
    r, decay, k2, bvec, nkk, v, g, xn = rwkv_pre(h, prev_single, np_rows, t_p, ln, mu, w_r, w_k, w_v, w0, w1, w2,
                                                  a0, a1, a2, g1, g2, k_k, k_a)

    def lanes_p(x):
        return x[:np_rows].reshape(n_p, t_p, RW_HEADS, RW_HEAD).transpose(1, 3, 0, 2).reshape(t_p, RW_HEAD, n_p * RW_HEADS)

    def lanes_s(x):
        return x[np_rows:np_rows + n_s].reshape(n_s, 1, RW_HEADS, RW_HEAD).transpose(1, 3, 0, 2).reshape(1, RW_HEAD, n_s * RW_HEADS)

    head_vec = lambda w, nb: jnp.tile(w.reshape(RW_HEADS, RW_HEAD).T, (1, nb))
    seqs = (r, decay, k2, bvec, nkk, v)
    s0_p = jnp.zeros((RW_HEAD, RW_HEAD, n_p * RW_HEADS), F32)
    o_p, st_p = rwkv_scan(*[lanes_p(x) for x in seqs], head_vec(r_k, n_p), head_vec(gn_w, n_p), head_vec(gn_b, n_p), s0_p)
    s0_s = state_wkv.transpose(3, 2, 0, 1).reshape(RW_HEAD, RW_HEAD, n_s * RW_HEADS)
    o_s, st_s = rwkv_scan(*[lanes_s(x) for x in seqs], head_vec(r_k, n_s), head_vec(gn_w, n_s), head_vec(gn_b, n_s),
                          s0_s, dot_dtype=BF16)
    o_p = o_p.reshape(t_p, RW_HEAD, n_p, RW_HEADS).transpose(2, 0, 3, 1).reshape(np_rows, D_MODEL)
    o_s = o_s.reshape(1, RW_HEAD, n_s, RW_HEADS).transpose(2, 0, 3, 1).reshape(n_s, D_MODEL)
    o = jnp.concatenate([o_p, o_s, jnp.zeros((n_tok - np_rows - n_s, D_MODEL), F32)], axis=0)
    wkv_p = st_p.reshape(RW_HEAD, RW_HEAD, n_p, RW_HEADS).transpose(2, 3, 1, 0)
    wkv_s = st_s.reshape(RW_HEAD, RW_HEAD, n_s, RW_HEADS).transpose(2, 3, 1, 0)
    return o, g, xn, wkv_p, wkv_s


def kernel(x_prompt, x_sample, state_wkv, state_shift, cache_kv, cache_win, page_table, ln_mix, ln_ffn, ln_kv, ln_out, rw_mu, rw_wr, rw_wk, rw_wv, rw_wo, rw_w0, rw_w1, rw_w2, rw_a0, rw_a1, rw_a2, rw_g1, rw_g2, rw_kk, rw_ka, rw_rk, rw_lnw, rw_lnb, w_kv, cmp_pe, cmp_w1, cmp_b1, cmp_w2, nsa_wqg, nsa_wo, moe_wrg, moe_brg, moe_wre, moe_bre, moe_wgate, moe_wup, moe_wdown):
    n_p, t_p, dm = x_prompt.shape
    n_s = x_sample.shape[0]
    assert x_sample.shape[1] == 1
    np_rows = n_p * t_p
    n_real = np_rows + n_s
    n_tok = _round_up(n_real, TOKEN_TILE)
    h = jnp.concatenate([x_prompt.reshape(np_rows, dm), x_sample.reshape(n_s, dm),
                         jnp.zeros((n_tok - n_real, dm), F32)], axis=0)

    def moe(xp, logits, layer):
        return hmoe(xp, logits, moe_brg[layer], moe_bre[layer], moe_wgate[layer], moe_wup[layer], moe_wdown[layer])

    o, g, xn, wkv_p, wkv_s = rwkv_layer(h, n_p, t_p, n_s, state_wkv[0], state_shift[0], ln_mix[0], rw_mu[0], rw_wr[0],
                                        rw_wk[0], rw_wv[0], rw_w0[0], rw_w1[0], rw_w2[0], rw_a0[0], rw_a1[0], rw_a2[0],
                                        rw_g1[0], rw_g2[0], rw_kk[0], rw_ka[0], rw_rk[0], rw_lnw[0], rw_lnb[0])
    p_shift = xn[:np_rows].reshape(n_p, t_p, dm)[:, -1][None]
    s_shift = xn[np_rows:n_real][None]
    h, xp, logits = mix_out([o], g, h, rw_wo[0], ln_ffn[0], router_weight(moe_wrg[0], moe_wre[0]))
    h = h + moe(xp, logits, 0)

    xn = rmsnorm(h, ln_mix[1])
    hk = rmsnorm(h, ln_kv)
    n_q = N_HEADS * HEAD_DIM
    gate_cols = _round_up(nsa_wqg.shape[-1] - n_q, LANES)
    qg = mm(xn, jnp.pad(nsa_wqg[0], ((0, 0), (0, n_q + gate_cols - nsa_wqg.shape[-1]))))
    kv = mm(hk, w_kv)
    o_p, p_rows, p_win_all = nsa_prompt(qg, kv, n_p, t_p, cmp_pe, cmp_w1, cmp_b1, cmp_w2)
    o_s, s_rows, s_win = nsa_sample(qg[np_rows:n_real, :nsa_wqg.shape[-1]], kv[np_rows:n_real], cache_kv, cache_win,
                                    page_table, cmp_pe, cmp_w1, cmp_b1, cmp_w2)
    tail = lambda x: jnp.concatenate([x, jnp.zeros((n_tok - n_real, dm), F32)], axis=0)
    branches = [bp.at[np_rows:].set(tail(bs)) for bp, bs in zip(o_p, o_s)]
    expand = [jnp.asarray((np.arange(gate_cols)[:, None] == 3 * (np.arange(n_q)[None, :] // HEAD_DIM) + br)
                          .astype(np.float32), BF16) for br in range(3)]
    h, xp, logits = mix_out(branches, qg, h, nsa_wo[0], ln_ffn[1], router_weight(moe_wrg[1], moe_wre[1]),
                            gate_expand=expand, gate_col=n_q // LANES)
    h = h + moe(xp, logits, 1)

    y = rmsnorm(h, ln_out)
    y_prompt = y[:np_rows].reshape(n_p, t_p, dm)
    y_sample = y[np_rows:n_real].reshape(n_s, 1, dm)
    p_win = p_win_all[:, -min(WINDOW, t_p):]
    return (y_prompt, y_sample, wkv_p[None], p_shift, p_rows, p_win, wkv_s[None], s_shift, s_rows, s_win)
```

```python
import functools

import numpy as np
import jax
import jax.numpy as jnp
from jax import lax
from jax.experimental import pallas as pl
from jax.experimental.pallas import tpu as pltpu

F32 = jnp.float32
BF16 = jnp.bfloat16

D_MODEL = 1024
RMS_EPS = 1e-6
RW_HEAD = 64
RW_HEADS = D_MODEL // RW_HEAD
RW_GN_EPS = 64e-5
N_HEADS = 16
HEAD_DIM = 64
KV_HEADS = 4
GQA_R = N_HEADS // KV_HEADS
ROPE_DIMS = HEAD_DIM // 4
ROPE_THETA = 500000.0
N_KV_STREAMS = 6
CMP_BLOCK = 32
CMP_STRIDE = 16
SEL_BLOCK = 64
N_SEL = 16
N_LOCAL_BLOCKS = 2
FORCE_SCORE = 1.0e4
WINDOW = 512
N_GROUPS = 4
EXPERTS_PER_GROUP = 8
N_EXPERTS = N_GROUPS * EXPERTS_PER_GROUP
D_EXPERT = 256

LANES = 128
VMEM_LIMIT = 48 << 20
NEG_BIG = -1e30
ATT_SCALE = HEAD_DIM ** -0.5
TOKEN_TILE = 512


def _round_up(x, m):
    return (x + m - 1) // m * m


def _cparams(sem):
    return pltpu.CompilerParams(dimension_semantics=sem, vmem_limit_bytes=VMEM_LIMIT)


_NT = (((1,), (1,)), ((), ()))


def _mm_kernel(x_ref, w_ref, o_ref):
    o_ref[...] = jnp.dot(x_ref[...].astype(BF16), w_ref[...], preferred_element_type=F32)


def mm(x, w, tm=TOKEN_TILE):
    m, k = x.shape
    n = w.shape[1]
    tm = min(tm, _round_up(m, 8))
    mp = _round_up(m, tm)
    if mp != m:
        x = jnp.pad(x, ((0, mp - m), (0, 0)))
    out = pl.pallas_call(
        _mm_kernel,
        grid=(mp // tm,),
        in_specs=[pl.BlockSpec((tm, k), lambda i: (i, 0)),
                  pl.BlockSpec((k, n), lambda i: (0, 0))],
        out_specs=pl.BlockSpec((tm, n), lambda i: (i, 0)),
        out_shape=jax.ShapeDtypeStruct((mp, n), F32),
        compiler_params=_cparams(("arbitrary",)),
        name="mm",
    )(x, w.astype(BF16))
    return out[:m] if mp != m else out


SCAN_TC = 32


def _scan_kernel(r_ref, d_ref, k_ref, b_ref, n_ref, v_ref, rk_ref, gw_ref, gb_ref, s0_ref, o_ref, st_ref, s_scr,
                 *, tc, dot_dtype):
    c = pl.program_id(1)
    rnd = lambda x: x.astype(dot_dtype).astype(F32)
    halves = [(h0, h0 + RW_HEAD // 2) for h0 in (0, RW_HEAD // 2)]

    @pl.when(c == 0)
    def _():
        s_scr[...] = s0_ref[...]

    def step(t, carry):
        vv = v_ref[t]
        outs = []
        for lo, hi in halves:
            parts = [jnp.zeros((hi - lo, LANES), F32)] * 2
            for j in range(RW_HEAD):
                parts[j % 2] = parts[j % 2] + rnd(s_scr[j, lo:hi]) * rnd(n_ref[t, pl.ds(j, 1), :])
            sa = parts[0] + parts[1]
            vh = vv[lo:hi]
            out = jnp.zeros((hi - lo, LANES), F32)
            for j in range(RW_HEAD):
                sn = (s_scr[j, lo:hi] * d_ref[t, pl.ds(j, 1), :] + sa * b_ref[t, pl.ds(j, 1), :]
                      + vh * k_ref[t, pl.ds(j, 1), :])
                s_scr[j, lo:hi] = sn
                out = out + rnd(sn) * rnd(r_ref[t, pl.ds(j, 1), :])
            outs.append(out)
        out = jnp.concatenate(outs, axis=0)
        mean = jnp.mean(out, axis=0, keepdims=True)
        cen = out - mean
        var = jnp.mean(cen * cen, axis=0, keepdims=True)
        bonus = jnp.sum(r_ref[t] * k_ref[t] * rk_ref[...], axis=0, keepdims=True) * vv
        o_ref[t] = cen * lax.rsqrt(var + RW_GN_EPS) * gw_ref[...] + gb_ref[...] + bonus
        return carry

    lax.fori_loop(0, tc, step, 0)

    @pl.when(c == pl.num_programs(1) - 1)
    def _():
        st_ref[...] = s_scr[...]


def rwkv_scan(r, d, k, b, nkk, v, rk, gw, gb, s0, dot_dtype=F32):
    t, _, l = r.shape
    tc = min(SCAN_TC, t)
    assert t % tc == 0 and l % LANES == 0
    seq = pl.BlockSpec((tc, RW_HEAD, LANES), lambda g, c: (c, 0, g))
    vec = pl.BlockSpec((RW_HEAD, LANES), lambda g, c: (0, g))
    st = pl.BlockSpec((RW_HEAD, RW_HEAD, LANES), lambda g, c: (0, 0, g))
    return pl.pallas_call(
        functools.partial(_scan_kernel, tc=tc, dot_dtype=dot_dtype),
        grid=(l // LANES, t // tc),
        in_specs=[seq] * 6 + [vec] * 3 + [st],
        out_specs=[seq, st],
        out_shape=[jax.ShapeDtypeStruct((t, RW_HEAD, l), F32),
                   jax.ShapeDtypeStruct((RW_HEAD, RW_HEAD, l), F32)],
        scratch_shapes=[pltpu.VMEM((RW_HEAD, RW_HEAD, LANES), F32)],
        compiler_params=_cparams(("arbitrary", "arbitrary")),
        name="rwkv_scan",
    )(r, d, k, b, nkk, v, rk, gw, gb, s0)


RW_TM = 256


def _split_dot(x, w):
    hi = x.astype(BF16)
    lo = (x - hi.astype(F32)).astype(BF16)
    return jnp.dot(hi, w, preferred_element_type=F32) + jnp.dot(lo, w, preferred_element_type=F32)


def _rwkv_pre_kernel(x_ref, ps_ref, ln_ref, mu_ref, vec_ref, wr_ref, wk_ref, wv_ref, w1_ref, w2_ref, a1_ref, a2_ref,
                     g1_ref, g2_ref, seg_ref, segt_ref,
                     r_o, d_o, k_o, b_o, n_o, v_o, g_o, xn_o, carry_scr, *, tiles_per_seq, n_seq_tiles):
    i = pl.program_id(0)
    tm = x_ref.shape[0]

    @pl.when(i == 0)
    def _():
        carry_scr[...] = jnp.zeros_like(carry_scr)

    x = x_ref[...]
    xn = x * lax.rsqrt(jnp.mean(x * x, axis=-1, keepdims=True) + RMS_EPS) * ln_ref[...]
    xn_o[...] = xn
    first = jnp.where(i % tiles_per_seq == 0, jnp.zeros_like(carry_scr[...]), carry_scr[...])
    row0 = lax.broadcasted_iota(jnp.int32, xn.shape, 0) == 0
    prev = jnp.where(row0, first, pltpu.roll(xn, 1, 0))
    prev = jnp.where(i >= n_seq_tiles, ps_ref[...], prev)
    carry_scr[...] = xn[tm - 1:tm, :]
    xx = prev - xn
    mix = lambda c: (xn + xx * mu_ref[c:c + 1, :]).astype(BF16)
    dot = lambda a, w_ref: jnp.dot(a, w_ref[...], preferred_element_type=F32)
    w0, a0, k_k, k_a = (vec_ref[c:c + 1, :] for c in range(4))
    r = dot(mix(0), wr_ref)
    k = dot(mix(2), wk_ref)
    v = dot(mix(3), wv_ref)
    w_log = -jax.nn.softplus(-(w0 + dot(jnp.tanh(dot(mix(1), w1_ref)).astype(BF16), w2_ref))) - 0.5
    a = jax.nn.sigmoid(a0 + dot(dot(mix(4), a1_ref).astype(BF16), a2_ref))
    g_o[...] = dot(jax.nn.sigmoid(dot(mix(5), g1_ref)).astype(BF16), g2_ref)
    kk = k * k_k
    ss = _split_dot(_split_dot(kk * kk, seg_ref[...]), segt_ref[...])
    kk = kk / jnp.maximum(jnp.sqrt(ss), 1e-12)
    r_o[...] = r
    d_o[...] = jnp.exp(-jnp.exp(w_log))
    k_o[...] = k * (1.0 + (a - 1.0) * k_a)
    b_o[...] = kk * a
    n_o[...] = -kk
    v_o[...] = v


def rwkv_pre(h, prev_single, n_seq_rows, t, ln, mu, w_r, w_k, w_v, w0, w1, w2, a0, a1, a2, g1, g2, k_k, k_a):
    n, dm = h.shape
    tm = RW_TM
    assert n % tm == 0 and t % tm == 0 and n_seq_rows % tm == 0 and prev_single.shape == (tm, dm)
    seg = (np.arange(dm)[:, None] // RW_HEAD == np.arange(LANES)[None, :]).astype(np.float32)
    row = pl.BlockSpec((tm, dm), lambda i: (i, 0))
    full = lambda a: pl.BlockSpec(a.shape, lambda i: (0,) * a.ndim)
    bf = lambda w: w.astype(BF16)
    consts = [prev_single, ln[None, :], mu, jnp.stack([w0, a0, k_k, k_a]), bf(w_r), bf(w_k), bf(w_v), bf(w1), bf(w2),
              bf(a1), bf(a2), bf(g1), bf(g2), jnp.asarray(seg, BF16), jnp.asarray(seg.T, BF16)]
    return pl.pallas_call(
        functools.partial(_rwkv_pre_kernel, tiles_per_seq=t // tm, n_seq_tiles=n_seq_rows // tm),
        grid=(n // tm,),
        in_specs=[row] + [full(a) for a in consts],
        out_specs=[row] * 8,
        out_shape=[jax.ShapeDtypeStruct((n, dm), F32)] * 8,
        scratch_shapes=[pltpu.VMEM((1, dm), F32)],
        compiler_params=_cparams(("arbitrary",)),
        name="rwkv_pre",
    )(h, *consts)


def _mix_out_kernel(*refs, n_branch, use_gates):
    o_refs = refs[:n_branch]
    gate_ref, h_ref, wo_ref, ln_ref, wr_ref = refs[n_branch:n_branch + 5]
    e_refs = refs[n_branch + 5:n_branch + 5 + (n_branch if use_gates else 0)]
    h_o, xp_o, lg_o = refs[-3:]
    if use_gates:
        sig = jax.nn.sigmoid(gate_ref[...])
        mixed = sum(_split_dot(sig, e_refs[b][...]) * o_refs[b][...] for b in range(n_branch))
    else:
        mixed = o_refs[0][...] * gate_ref[...]
    h = h_ref[...] + jnp.dot(mixed.astype(BF16), wo_ref[...], preferred_element_type=F32)
    h_o[...] = h
    xn = h * lax.rsqrt(jnp.mean(h * h, axis=-1, keepdims=True) + RMS_EPS) * ln_ref[...]
    xb = xn.astype(BF16)
    lg_o[...] = jnp.dot(xb, wr_ref[...], preferred_element_type=F32)
    u = lax.bitcast_convert_type(xb.astype(F32), jnp.uint32)
    half = u.shape[1] // 2
    xp_o[...] = (u[:, :half] >> 16) | (u[:, half:] & jnp.uint32(0xFFFF0000))


def mix_out(branches, gate, h, w_o, ln, w_router, gate_expand=None, gate_col=0):
    n, dm = h.shape
    tm = RW_TM
    nb = len(branches)
    row = lambda a: pl.BlockSpec((tm, a.shape[1]), lambda i: (i, 0))
    full = lambda a: pl.BlockSpec(a.shape, lambda i: (0,) * a.ndim)
    consts = [w_o.astype(BF16), ln[None, :], w_router.astype(BF16)] + list(gate_expand or [])
    gate_spec = row(gate) if gate_expand is None else pl.BlockSpec((tm, LANES), lambda i: (i, gate_col))
    return pl.pallas_call(
        functools.partial(_mix_out_kernel, n_branch=nb, use_gates=gate_expand is not None),
        grid=(n // tm,),
        in_specs=[row(a) for a in branches] + [gate_spec, row(h)] + [full(a) for a in consts],
        out_specs=[pl.BlockSpec((tm, dm), lambda i: (i, 0)), pl.BlockSpec((tm, dm // 2), lambda i: (i, 0)),
                   pl.BlockSpec((tm, LANES), lambda i: (i, 0))],
        out_shape=[jax.ShapeDtypeStruct((n, dm), F32), jax.ShapeDtypeStruct((n, dm // 2), jnp.uint32),
                   jax.ShapeDtypeStruct((n, LANES), F32)],
        compiler_params=_cparams(("arbitrary",)),
        name="mix_out",
    )(*branches, gate, h, *consts)


MOE_TM = 256
MOE_VMEM_LIMIT = 56 << 20


def _moe_kernel(te_ref, nt_ref, tok_ref, x_ref, c_ref, wgu_ref, wd_ref, o_ref, xt_scr):
    i = pl.program_id(0)

    @pl.when(i < nt_ref[0])
    def _():
        base = i * MOE_TM

        def gather_row(r, carry):
            xt_scr[pl.ds(r, 1), :] = x_ref[pl.ds(tok_ref[base + r], 1), :]
            return carry

        lax.fori_loop(0, MOE_TM, gather_row, 0, unroll=8)
        u = xt_scr[...]
        lo = lax.bitcast_convert_type(u << 16, F32).astype(BF16)
        hi = lax.bitcast_convert_type(u & jnp.uint32(0xFFFF0000), F32).astype(BF16)
        x = jnp.concatenate([lo, hi], axis=1)
        h = jnp.dot(x, wgu_ref[0], preferred_element_type=F32)
        g = h[:, :D_EXPERT]
        u2 = h[:, D_EXPERT:]
        hid = g * jax.nn.sigmoid(g) * u2 * c_ref[...]
        o_ref[...] = jnp.dot(hid.astype(BF16), wd_ref[0], preferred_element_type=F32)

    @pl.when(i >= nt_ref[0])
    def _():
        o_ref[...] = jnp.zeros_like(o_ref)


def moe_gmm(xp, row_tok, cs, tile_expert, n_tiles_used, wgu, wd):
    n, half = xp.shape
    dm = 2 * half
    p = row_tok.shape[0]
    nt = p // MOE_TM

    def row_map(i, te, ntu, tok):
        return (jnp.minimum(i, ntu[0] - 1), 0)

    grid_spec = pltpu.PrefetchScalarGridSpec(
        num_scalar_prefetch=3,
        grid=(nt,),
        in_specs=[pl.BlockSpec((n, half), lambda i, te, ntu, tok: (0, 0), pipeline_mode=pl.Buffered(1)),
                  pl.BlockSpec((MOE_TM, 1), row_map),
                  pl.BlockSpec((1, dm, 2 * D_EXPERT), lambda i, te, ntu, tok: (te[i], 0, 0)),
                  pl.BlockSpec((1, D_EXPERT, dm), lambda i, te, ntu, tok: (te[i], 0, 0))],
        out_specs=pl.BlockSpec((MOE_TM, dm), lambda i, te, ntu, tok: (i, 0)),
        scratch_shapes=[pltpu.VMEM((MOE_TM, half), jnp.uint32)],
    )
    return pl.pallas_call(
        _moe_kernel,
        grid_spec=grid_spec,
        out_shape=jax.ShapeDtypeStruct((p, dm), F32),
        compiler_params=pltpu.CompilerParams(dimension_semantics=("arbitrary",), vmem_limit_bytes=MOE_VMEM_LIMIT),
        name="moe_gmm",
    )(tile_expert, n_tiles_used, row_tok, xp, cs, wgu, wd)


def router_weight(w_rg, w_re):
    w_r = jnp.concatenate([w_rg, w_re], axis=1)
    return jnp.pad(w_r, ((0, 0), (0, LANES - w_r.shape[1])))


def hmoe(xp, logits, set_sizes, b_rg, b_re, w_gate, w_up, w_down):
    n = xp.shape[0]
    dm = 2 * xp.shape[1]
    lg = logits[:, :N_GROUPS] + b_rg
    le = (logits[:, N_GROUPS:N_GROUPS + N_EXPERTS] + b_re).reshape(n, N_GROUPS, EXPERTS_PER_GROUP)
    g_sel = jnp.argmax(lg, axis=-1)
    p_g = jnp.max(jax.nn.softmax(lg, axis=-1), axis=-1)
    le_g = jnp.take_along_axis(le, g_sel[:, None, None], axis=1)[:, 0]
    top_p, top_i = lax.top_k(jax.nn.softmax(le_g, axis=-1), 2)
    w_top = p_g[:, None] * top_p / jnp.sum(top_p, axis=-1, keepdims=True)
    eid = (g_sel[:, None] * EXPERTS_PER_GROUP + top_i).astype(jnp.int32).reshape(-1)
    cw = w_top.reshape(-1)

    n2 = 2 * n
    onehot = (eid[:, None] == jnp.arange(N_EXPERTS, dtype=jnp.int32)[None, :]).astype(jnp.int32)
    counts = jnp.sum(onehot, axis=0)
    rank = jnp.take_along_axis(jnp.cumsum(onehot, axis=0) - onehot, eid[:, None], axis=1)[:, 0]
    tiles_per = (counts + MOE_TM - 1) // MOE_TM
    tile_end = jnp.cumsum(tiles_per)
    tile_start = tile_end - tiles_per
    pos = tile_start[eid] * MOE_TM + rank
    p_rows = _round_up(n2 + N_EXPERTS * (MOE_TM - 1), MOE_TM)
    nt = p_rows // MOE_TM
    n_used = tile_end[-1].astype(jnp.int32)
    tile_ids = jnp.arange(nt, dtype=jnp.int32)
    tile_expert = jnp.sum((tile_end[None, :] <= jnp.minimum(tile_ids, n_used - 1)[:, None]).astype(jnp.int32), axis=1)

    tok = jnp.arange(n2, dtype=jnp.int32) // 2
    row_tok = jnp.zeros((p_rows,), jnp.int32).at[pos].set(tok)
    row_c = jnp.zeros((p_rows,), F32).at[pos].set(cw)
    wgu = jnp.concatenate([w_gate, w_up], axis=-1).reshape(N_EXPERTS, dm, 2 * D_EXPERT).astype(BF16)
    wd = w_down.reshape(N_EXPERTS, D_EXPERT, dm).astype(BF16)
    ys = moe_gmm(xp, row_tok, row_c[:, None], tile_expert, n_used.reshape(1), wgu, wd)
    pos2 = pos.reshape(n, 2)
    bounds = np.cumsum([0] + list(set_sizes))
    return [ys[pos2[a:b, 0]] + ys[pos2[a:b, 1]] for a, b in zip(bounds[:-1], bounds[1:])]


ATT_TQ = 256
ATT_TK = 512


def _cmp_sel_kernel(q_ref, kc_ref, vct_ref, ovt_ref, o_ref, sel_ref, *, n_cmp, n_sblk, tq):
    t0 = pl.program_id(2) * tq
    ncp = kc_ref.shape[2]
    n_io = lax.broadcasted_iota(jnp.int32, (ncp, tq), 0)
    t_io = lax.broadcasted_iota(jnp.int32, (ncp, tq), 1) + t0
    mask = (n_io * CMP_STRIDE + (CMP_BLOCK - 1) <= t_io) & (n_io < n_cmp)
    kc = kc_ref[0, 0]
    vct = vct_ref[0, 0]
    ovt = ovt_ref[...]
    imp = jnp.zeros((n_sblk, tq), F32)
    outs = []
    for r in range(GQA_R):
        qr = (q_ref[:, r * HEAD_DIM:(r + 1) * HEAD_DIM] * ATT_SCALE).astype(BF16)
        st = lax.dot_general(kc, qr, _NT, preferred_element_type=F32)
        st = jnp.where(mask, st, NEG_BIG)
        m = jnp.max(st, axis=0, keepdims=True)
        e = jnp.where(mask, jnp.exp(st - m), 0.0)
        p = (e / jnp.maximum(jnp.sum(e, axis=0, keepdims=True), 1e-30)).astype(BF16)
        outs.append(jnp.dot(vct, p, preferred_element_type=F32).T)
        imp = imp + jnp.dot(ovt, p, preferred_element_type=F32)
    o_ref[...] = jnp.concatenate(outs, axis=1)

    j_io = lax.broadcasted_iota(jnp.int32, (n_sblk, tq), 0)
    qblk = jnp.right_shift(lax.broadcasted_iota(jnp.int32, (n_sblk, tq), 1) + t0, SEL_BLOCK.bit_length() - 1)
    allowed = j_io <= qblk
    forced = (j_io == 0) | ((j_io > qblk - N_LOCAL_BLOCKS) & allowed)
    score = jnp.where(forced, FORCE_SCORE, imp)
    score = jnp.where(allowed, score, -jnp.inf)
    rank = jnp.zeros((n_sblk, tq), jnp.int32)
    for jp in range(n_sblk):
        row = score[jp:jp + 1, :]
        beats = (row > score) | ((row == score) & (j_io > jp))
        rank = rank + jnp.where(beats, 1, 0)
    sel = (rank < N_SEL) & allowed
    sel_ref[0, 0] = jnp.where(sel, 1.0, 0.0).astype(BF16)


def cmp_select(q, kc, vct, ovt, n_b, t, n_cmp, n_sblk, n_out):
    assert SEL_BLOCK & (SEL_BLOCK - 1) == 0
    tq = ATT_TQ
    nq = t // tq
    ncp = kc.shape[2]
    gw = GQA_R * HEAD_DIM
    return pl.pallas_call(
        functools.partial(_cmp_sel_kernel, n_cmp=n_cmp, n_sblk=n_sblk, tq=tq),
        grid=(n_b, KV_HEADS, nq),
        in_specs=[pl.BlockSpec((tq, gw), lambda b, g, qi: (b * nq + qi, g)),
                  pl.BlockSpec((1, 1, ncp, HEAD_DIM), lambda b, g, qi: (b, g, 0, 0)),
                  pl.BlockSpec((1, 1, HEAD_DIM, ncp), lambda b, g, qi: (b, g, 0, 0)),
                  pl.BlockSpec((n_sblk, ncp), lambda b, g, qi: (0, 0))],
        out_specs=[pl.BlockSpec((tq, gw), lambda b, g, qi: (b * nq + qi, g)),
                   pl.BlockSpec((1, 1, n_sblk, tq), lambda b, g, qi: (b, g, 0, qi))],
        out_shape=[jax.ShapeDtypeStruct((n_out, N_HEADS * HEAD_DIM), F32),
                   jax.ShapeDtypeStruct((n_b, KV_HEADS, n_sblk, t), BF16)],
        compiler_params=_cparams(("arbitrary", "arbitrary", "arbitrary")),
        name="attn_cmp_select",
    )(q, kc, vct, ovt)


def _flash_kernel(pq_ref, pk_ref, pfirst_ref, plast_ref, q_ref, k_ref, vt_ref, *rest, window, use_sel, tq, tk):
    if use_sel:
        sel_ref, et_ref, o_ref, m_scr, l_scr, acc_scr = rest
    else:
        o_ref, m_scr, l_scr, acc_scr = rest
    p_id = pl.program_id(2)

    @pl.when(pfirst_ref[p_id] == 1)
    def _():
        m_scr[...] = jnp.full_like(m_scr, NEG_BIG)
        l_scr[...] = jnp.zeros_like(l_scr)
        acc_scr[...] = jnp.zeros_like(acc_scr)

    off = pq_ref[p_id] * tq - pk_ref[p_id] * tk
    diff = lax.broadcasted_iota(jnp.int32, (tk, tq), 0) - lax.broadcasted_iota(jnp.int32, (tk, tq), 1)
    mask = diff <= off
    if window is not None:
        mask = mask & (diff > off - window)
    if use_sel:
        selm = jnp.dot(et_ref[...], sel_ref[0, 0], preferred_element_type=F32)
        mask = mask & (selm > 0.5)
    k = k_ref[0, 0]
    vt = vt_ref[0, 0]
    heads = range(GQA_R)
    qs = [(q_ref[:, r * HEAD_DIM:(r + 1) * HEAD_DIM] * ATT_SCALE).astype(BF16) for r in heads]
    sts = [jnp.where(mask, lax.dot_general(k, qs[r], _NT, preferred_element_type=F32), NEG_BIG) for r in heads]
    m_prev = [m_scr[r] for r in heads]
    m_new = [jnp.maximum(m_prev[r], jnp.max(sts[r], axis=0, keepdims=True)) for r in heads]
    ps = [jnp.where(mask, jnp.exp(sts[r] - m_new[r]), 0.0) for r in heads]
    alphas = [jnp.exp(m_prev[r] - m_new[r]) for r in heads]
    for r in heads:
        l_scr[r] = alphas[r] * l_scr[r] + jnp.sum(ps[r], axis=0, keepdims=True)
        acc_scr[r] = alphas[r] * acc_scr[r] + jnp.dot(vt, ps[r].astype(BF16), preferred_element_type=F32)
        m_scr[r] = m_new[r]

    @pl.when(plast_ref[p_id] == 1)
    def _():
        outs = [(acc_scr[r] / jnp.maximum(l_scr[r], 1e-30)).T for r in range(GQA_R)]
        o_ref[...] = jnp.concatenate(outs, axis=1)


def flash_attention(q, k, vt, n_b, t, n_out, window=None, sel=None):
    tq, tk = ATT_TQ, ATT_TK
    nq = t // tq
    first_tile = lambda qi: 0 if window is None else max(0, qi * tq - (window - 1)) // tk
    pairs = [(qi, kt) for qi in range(nq) for kt in range(first_tile(qi), (qi * tq + tq - 1) // tk + 1)]
    pq = np.array([p[0] for p in pairs], np.int32)
    pk = np.array([p[1] for p in pairs], np.int32)
    pfirst = np.array([1 if i == 0 or pairs[i - 1][0] != p[0] else 0 for i, p in enumerate(pairs)], np.int32)
    plast = np.array([1 if i == len(pairs) - 1 or pairs[i + 1][0] != p[0] else 0 for i, p in enumerate(pairs)], np.int32)
    gw = GQA_R * HEAD_DIM

    in_specs = [pl.BlockSpec((tq, gw), lambda b, g, p, pq_r, pk_r, f_r, l_r: (b * nq + pq_r[p], g)),
                pl.BlockSpec((1, 1, tk, HEAD_DIM), lambda b, g, p, pq_r, pk_r, f_r, l_r: (b, g, pk_r[p], 0)),
                pl.BlockSpec((1, 1, HEAD_DIM, tk), lambda b, g, p, pq_r, pk_r, f_r, l_r: (b, g, 0, pk_r[p]))]
    args = [q, k, vt]
    if sel is not None:
        n_sblk = sel.shape[2]
        emat = (np.arange(t)[:, None] // SEL_BLOCK == np.arange(n_sblk)[None, :]).astype(np.float32)
        in_specs += [pl.BlockSpec((1, 1, n_sblk, tq), lambda b, g, p, pq_r, pk_r, f_r, l_r: (b, g, 0, pq_r[p])),
                     pl.BlockSpec((tk, n_sblk), lambda b, g, p, pq_r, pk_r, f_r, l_r: (pk_r[p], 0))]
        args += [sel, jnp.asarray(emat, BF16)]
    grid_spec = pltpu.PrefetchScalarGridSpec(
        num_scalar_prefetch=4,
        grid=(n_b, KV_HEADS, len(pairs)),
        in_specs=in_specs,
        out_specs=pl.BlockSpec((tq, gw), lambda b, g, p, pq_r, pk_r, f_r, l_r: (b * nq + pq_r[p], g)),
        scratch_shapes=[pltpu.VMEM((GQA_R, 1, tq), F32), pltpu.VMEM((GQA_R, 1, tq), F32),
                        pltpu.VMEM((GQA_R, HEAD_DIM, tq), F32)],
    )
    return pl.pallas_call(
        functools.partial(_flash_kernel, window=window, use_sel=sel is not None, tq=tq, tk=tk),
        grid_spec=grid_spec,
        out_shape=jax.ShapeDtypeStruct((n_out, N_HEADS * HEAD_DIM), F32),
        compiler_params=_cparams(("arbitrary", "arbitrary", "arbitrary")),
        name="attn_sel" if sel is not None else "attn_win",
    )(jnp.asarray(pq), jnp.asarray(pk), jnp.asarray(pfirst), jnp.asarray(plast), *args)


def _cmp_kernel(q_ref, k_ref, v_ref, ov_ref, o_ref, imp_ref, *, n_cmp):
    q = q_ref[0] * ATT_SCALE
    s = lax.dot_general(q, k_ref[0], _NT, preferred_element_type=F32)
    mask = lax.broadcasted_iota(jnp.int32, s.shape, 1) < n_cmp
    s = jnp.where(mask, s, NEG_BIG)
    m = jnp.max(s, axis=-1, keepdims=True)
    e = jnp.where(mask, jnp.exp(s - m), 0.0)
    p = (e / jnp.maximum(jnp.sum(e, axis=-1, keepdims=True), 1e-30)).astype(BF16)
    o_ref[0] = jnp.dot(p, v_ref[0], preferred_element_type=F32)
    imp_ref[0] = jnp.dot(p, ov_ref[...], preferred_element_type=F32)


def cmp_attention_sample(q, kc, vc, ov, n_cmp):
    bg, rq, hd = q.shape
    nc = kc.shape[1]
    nb = ov.shape[1]
    return pl.pallas_call(
        functools.partial(_cmp_kernel, n_cmp=n_cmp),
        grid=(bg,),
        in_specs=[pl.BlockSpec((1, rq, hd), lambda b: (b, 0, 0)),
                  pl.BlockSpec((1, nc, hd), lambda b: (b, 0, 0)),
                  pl.BlockSpec((1, nc, hd), lambda b: (b, 0, 0)),
                  pl.BlockSpec((nc, nb), lambda b: (0, 0))],
        out_specs=[pl.BlockSpec((1, rq, hd), lambda b: (b, 0, 0)),
                   pl.BlockSpec((1, rq, nb), lambda b: (b, 0, 0))],
        out_shape=[jax.ShapeDtypeStruct((bg, rq, hd), F32),
                   jax.ShapeDtypeStruct((bg, rq, nb), F32)],
        compiler_params=_cparams(("arbitrary",)),
        name="attn_cmp_sample",
    )(q, kc, vc, ov)


PAGES_PER_STEP = 8


def _page_compress_kernel(pt_ref, *rest, n_pages, page, pb):
    c_refs = rest[:pb]
    wbd_ref, pe_ref, w2_ref, o_ref, a_scr = rest[pb:]
    p = pl.program_id(1)
    nsub = n_pages * page // CMP_STRIDE
    for i in range(pb):
        row0 = pl.multiple_of((p * pb + i) * page, page)
        for s in range(2):
            for gp in range(KV_HEADS // 2):
                x = c_refs[i][0, s, 2 * gp:2 * gp + 2].reshape(2 * HEAD_DIM, page)
                a_scr[s, gp, pl.ds(row0, page), :] = x.T

    @pl.when(p == n_pages // pb - 1)
    def _():
        hid = pe_ref.shape[-1]
        for s in range(2):
            for gp in range(KV_HEADS // 2):
                acc = jnp.zeros((nsub, 4 * hid), F32)
                for j in range(CMP_STRIDE):
                    lhs = a_scr[s, gp, pl.ds(j, nsub, stride=CMP_STRIDE), :].astype(BF16)
                    acc = acc + jnp.dot(lhs, wbd_ref[s, j], preferred_element_type=F32)
                for g2 in range(2):
                    first = acc[:, 2 * g2 * hid:(2 * g2 + 1) * hid]
                    second = acc[:, (2 * g2 + 1) * hid:(2 * g2 + 2) * hid]
                    pre = first + pltpu.roll(second, nsub - 1, 0) + pe_ref[s]
                    act = jax.nn.gelu(pre).astype(BF16)
                    o_ref[s, 0, 2 * gp + g2] = jnp.dot(act, w2_ref[s], preferred_element_type=F32).astype(BF16)


def page_compress(cache_t, page_table, cmp_pe, cmp_w1, cmp_b1, cmp_w2):
    n_b, n_pages = page_table.shape
    page = cache_t.shape[-1]
    nsub = n_pages * page // CMP_STRIDE
    hid = cmp_w1.shape[-1]
    pb = PAGES_PER_STEP
    assert n_pages % pb == 0
    w1r = cmp_w1.reshape(2, 2, CMP_STRIDE, HEAD_DIM, hid).transpose(0, 2, 3, 1, 4)
    w1r = w1r.reshape(2, CMP_STRIDE, HEAD_DIM, 2 * hid)
    zeros = jnp.zeros_like(w1r)
    wbd = jnp.concatenate([jnp.concatenate([w1r, zeros], axis=-1), jnp.concatenate([zeros, w1r], axis=-1)], axis=2)
    pe_term = (jnp.einsum('sjd,sjdc->sc', cmp_pe, cmp_w1.reshape(2, CMP_BLOCK, HEAD_DIM, hid)) + cmp_b1)[:, None, :]
    def page_map(i):
        return lambda b, p, pt: (pt[b * n_pages + p * pb + i], 0, 0, 0, 0)

    grid_spec = pltpu.PrefetchScalarGridSpec(
        num_scalar_prefetch=1,
        grid=(n_b, n_pages // pb),
        in_specs=[pl.BlockSpec((1, 2, KV_HEADS, HEAD_DIM, page), page_map(i)) for i in range(pb)] + [
            pl.BlockSpec((2, CMP_STRIDE, 2 * HEAD_DIM, 4 * hid), lambda b, p, pt: (0, 0, 0, 0)),
            pl.BlockSpec((2, 1, hid), lambda b, p, pt: (0, 0, 0)),
            pl.BlockSpec((2, hid, HEAD_DIM), lambda b, p, pt: (0, 0, 0))],
        out_specs=pl.BlockSpec((2, 1, KV_HEADS, nsub, HEAD_DIM), lambda b, p, pt: (0, b, 0, 0, 0)),
        scratch_shapes=[pltpu.VMEM((2, KV_HEADS // 2, n_pages * page, 2 * HEAD_DIM), F32)],
    )
    return pl.pallas_call(
        functools.partial(_page_compress_kernel, n_pages=n_pages, page=page, pb=pb),
        grid_spec=grid_spec,
        out_shape=jax.ShapeDtypeStruct((2, n_b, KV_HEADS, nsub, HEAD_DIM), BF16),
        compiler_params=_cparams(("arbitrary", "arbitrary")),
        name="page_compress",
    )(page_table.reshape(-1), *([cache_t] * pb), wbd.astype(BF16), pe_term, cmp_w2.astype(BF16))


def _decode_attn_kernel(phys_ref, blk_ref, flag_ref, q_ref, kn_ref, vn_ref, *rest, mode, past, steps, window, page):
    kt_refs = rest[:steps]
    vt_refs = rest[steps:2 * steps]
    o_ref = rest[2 * steps]
    idx = pl.program_id(0) * KV_HEADS + pl.program_id(1)
    rnd = lambda x: x.astype(BF16).astype(F32)
    q = q_ref[0, 0] * ATT_SCALE

    kn = rnd(kn_ref[0, 0])
    vn = rnd(vn_ref[0, 0])
    s_new = jnp.sum(q.astype(F32) * kn, axis=-1, keepdims=True)
    on = flag_ref[idx] == 1
    m = jnp.where(on, s_new, NEG_BIG)
    scs, masks = [], []
    for s in range(steps):
        kt = kt_refs[s][0, 0, 0].astype(BF16)
        tile = kt.shape[-1]
        sc = jnp.dot(q, kt, preferred_element_type=F32)
        lane = lax.broadcasted_iota(jnp.int32, sc.shape, 1)
        if mode == "sel":
            blk = blk_ref[idx * steps + s]
            per_page = page // SEL_BLOCK
            pos = (blk // per_page) * page + lane
            mask = (jnp.right_shift(pos, SEL_BLOCK.bit_length() - 1) == blk) & (pos <= past) & (blk >= 0)
        else:
            pos = past - tile + lane
            mask = (pos > past - window) & (pos <= past)
        sc = jnp.where(mask, sc, NEG_BIG)
        m = jnp.maximum(m, jnp.max(sc, axis=-1, keepdims=True))
        scs.append(sc)
        masks.append(mask)
    p_new = jnp.where(on, jnp.exp(s_new - m), 0.0)
    l = p_new
    acc = rnd(p_new) * vn
    for s in range(steps):
        p = jnp.where(masks[s], jnp.exp(scs[s] - m), 0.0)
        l = l + jnp.sum(p, axis=-1, keepdims=True)
        vt = vt_refs[s][0, 0, 0].astype(BF16)
        acc = acc + lax.dot_general(p.astype(BF16), vt, _NT, preferred_element_type=F32)
    o_ref[0, 0] = acc / jnp.maximum(l, 1e-30)


def decode_attention(q, k_new, v_new, kv_t, streams, phys, blk, flag, mode, past, steps, window=None, page=None):
    n_b, n_g, rows, hd = q.shape
    tile = kv_t.shape[-1]
    ks, vs = streams

    def kmap(st, s):
        if mode == "sel":
            return lambda b, g, ph, bl, fl: (ph[(b * n_g + g) * steps + s], st, g, 0, 0)
        return lambda b, g, ph, bl, fl: (b, st, g, 0, 0)

    tiles = ([pl.BlockSpec((1, 1, 1, hd, tile), kmap(ks, s)) for s in range(steps)]
             + [pl.BlockSpec((1, 1, 1, hd, tile), kmap(vs, s)) for s in range(steps)])
    grid_spec = pltpu.PrefetchScalarGridSpec(
        num_scalar_prefetch=3,
        grid=(n_b, n_g),
        in_specs=[pl.BlockSpec((1, 1, rows, hd), lambda b, g, ph, bl, fl: (b, g, 0, 0)),
                  pl.BlockSpec((1, 1, 1, hd), lambda b, g, ph, bl, fl: (b, g, 0, 0)),
                  pl.BlockSpec((1, 1, 1, hd), lambda b, g, ph, bl, fl: (b, g, 0, 0))] + tiles,
        out_specs=pl.BlockSpec((1, 1, rows, hd), lambda b, g, ph, bl, fl: (b, g, 0, 0)),
    )
    return pl.pallas_call(
        functools.partial(_decode_attn_kernel, mode=mode, past=past, steps=steps, window=window, page=page),
        grid_spec=grid_spec,
        out_shape=jax.ShapeDtypeStruct((n_b, n_g, rows, hd), F32),
        compiler_params=_cparams(("arbitrary", "arbitrary")),
        name="attn_decode_" + mode,
    )(phys, blk, flag, q, k_new, v_new, *([kv_t] * (2 * steps)))


def rmsnorm(x, g):
    y = x * lax.rsqrt(jnp.mean(x * x, axis=-1, keepdims=True) + RMS_EPS)
    return y * g


def rope(x, pos):
    half = ROPE_DIMS // 2
    inv = jnp.power(ROPE_THETA, -jnp.arange(half, dtype=F32) * (2.0 / ROPE_DIMS))
    ang = pos.astype(F32)[:, None] * inv[None, :]
    cos = jnp.cos(ang)[:, None, :]
    sin = jnp.sin(ang)[:, None, :]
    x1 = x[..., :half]
    x2 = x[..., half:ROPE_DIMS]
    return jnp.concatenate([x1 * cos - x2 * sin, x2 * cos + x1 * sin, x[..., ROPE_DIMS:]], axis=-1)


def compress(sub, pe, w1, b1, w2, n_b, nsub):
    half = CMP_STRIDE * HEAD_DIM
    w1cat = jnp.concatenate([w1[:half], w1[half:]], axis=1)
    hidden = w1.shape[1]
    part = mm(sub, w1cat).reshape(n_b, nsub, KV_HEADS, 2, hidden)
    pe_term = jnp.einsum('jd,jdc->c', pe, w1.reshape(CMP_BLOCK, HEAD_DIM, hidden)) + b1
    pre = part[:, :nsub - 1, :, 0] + part[:, 1:, :, 1] + pe_term
    act = jax.nn.gelu(pre)
    return mm(act.reshape(-1, hidden), w2).reshape(n_b, nsub - 1, KV_HEADS, HEAD_DIM)


def overlap_matrix(n_cmp, n_sblk, nc_pad, nb_pad):
    cs = np.arange(n_cmp) * CMP_STRIDE
    ce = cs + CMP_BLOCK - 1
    ss = np.arange(n_sblk) * SEL_BLOCK
    ov = ((cs[:, None] < ss[None, :] + SEL_BLOCK) & (ce[:, None] >= ss[None, :])).astype(np.float32)
    out = np.zeros((nc_pad, nb_pad), np.float32)
    out[:n_cmp, :n_sblk] = ov
    return out


def block_scores(imp, q_pos, n_sblk):
    j = jnp.arange(n_sblk)[None, :]
    qblk = (q_pos // SEL_BLOCK)[:, None]
    allowed = j <= qblk
    forced = (j == 0) | ((j > qblk - N_LOCAL_BLOCKS) & allowed)
    score = jnp.where(forced, FORCE_SCORE, imp)
    return jnp.where(allowed, score, -jnp.inf)


def nsa_prompt(qg, kv, n_b, t, cmp_pe, cmp_w1, cmp_b1, cmp_w2):
    n = n_b * t
    pos = jnp.arange(t)
    kv = kv[:n].reshape(n_b, t, N_KV_STREAMS, KV_HEADS, HEAD_DIM)
    k_sel = rope(kv[:, :, 2], pos)
    k_win = rope(kv[:, :, 4], pos)
    new_rows = jnp.stack([kv[:, :, 0], kv[:, :, 1], k_sel, kv[:, :, 3]], axis=2)
    new_win = jnp.stack([k_win, kv[:, :, 5]], axis=2)

    nsub = t // CMP_STRIDE
    n_cmp = nsub - 1
    n_sblk = t // SEL_BLOCK

    def sub_blocks(rows):
        return rows.reshape(n_b, nsub, CMP_STRIDE, KV_HEADS, HEAD_DIM).transpose(0, 1, 3, 2, 4).reshape(-1, CMP_STRIDE * HEAD_DIM)

    kc = compress(sub_blocks(kv[:, :, 0]), cmp_pe[0], cmp_w1[0], cmp_b1[0], cmp_w2[0], n_b, nsub)
    vc = compress(sub_blocks(kv[:, :, 1]), cmp_pe[1], cmp_w1[1], cmp_b1[1], cmp_w2[1], n_b, nsub)
    nc_pad = _round_up(n_cmp, LANES)
    kc = jnp.pad(kc, ((0, 0), (0, nc_pad - n_cmp), (0, 0), (0, 0)))
    vc = jnp.pad(vc, ((0, 0), (0, nc_pad - n_cmp), (0, 0), (0, 0)))
    ovt = jnp.asarray(overlap_matrix(n_cmp, n_sblk, nc_pad, n_sblk).T, BF16)
    o_cmp, sel = cmp_select(qg, kc.transpose(0, 2, 1, 3).astype(BF16), vc.transpose(0, 2, 3, 1).astype(BF16),
                            ovt, n_b, t, n_cmp, n_sblk, qg.shape[0])

    q_r = rope(qg[:n, :N_HEADS * HEAD_DIM].reshape(n_b, t, N_HEADS, HEAD_DIM), pos).reshape(n, N_HEADS * HEAD_DIM)
    rows_k = lambda x: x.transpose(0, 2, 1, 3).astype(BF16)
    rows_vt = lambda x: x.transpose(0, 2, 3, 1).astype(BF16)
    o_sel = flash_attention(q_r, rows_k(k_sel), rows_vt(kv[:, :, 3]), n_b, t, qg.shape[0], sel=sel)
    o_win = flash_attention(q_r, rows_k(k_win), rows_vt(kv[:, :, 5]), n_b, t, qg.shape[0], window=WINDOW)
    return (o_cmp, o_sel, o_win), new_rows, new_win


def nsa_sample(qg, kv, cache_kv, cache_win, page_table, cmp_pe, cmp_w1, cmp_b1, cmp_w2):
    n_b, n_pages = page_table.shape
    n_phys, page = cache_kv.shape[:2]
    past = n_pages * page
    pos = jnp.full((1,), past, jnp.int32)
    q = qg[:, :N_HEADS * HEAD_DIM].reshape(n_b, 1, N_HEADS, HEAD_DIM)
    kv = kv.reshape(n_b, 1, N_KV_STREAMS, KV_HEADS, HEAD_DIM)
    k_sel = rope(kv[:, :, 2], pos)
    k_win = rope(kv[:, :, 4], pos)
    new_rows = jnp.stack([kv[:, :, 0], kv[:, :, 1], k_sel, kv[:, :, 3]], axis=2)
    new_win = jnp.stack([k_win, kv[:, :, 5]], axis=2)
    n_keep = min(WINDOW, cache_win.shape[1] + 1)
    win_state = jnp.concatenate([cache_win[:, cache_win.shape[1] + 1 - n_keep:], new_win], axis=1)

    cache_t = cache_kv.transpose(0, 2, 3, 4, 1)
    win_t = cache_win.transpose(0, 2, 3, 4, 1)

    l_tot = past + 1
    nsub = l_tot // CMP_STRIDE
    n_cmp = nsub - 1
    n_sblk = -(-l_tot // SEL_BLOCK)
    assert nsub * CMP_STRIDE == past and page % SEL_BLOCK == 0 and cache_win.shape[1] == WINDOW
    kvc = page_compress(cache_t, page_table, cmp_pe, cmp_w1, cmp_b1, cmp_w2)

    bg = n_b * KV_HEADS
    rq = 16
    qpad = lambda x: jnp.pad(x.reshape(n_b, KV_HEADS, GQA_R, HEAD_DIM), ((0, 0), (0, 0), (0, rq - GQA_R), (0, 0))).astype(BF16)

    nb_pad = _round_up(n_sblk, 8)
    ov = jnp.asarray(overlap_matrix(n_cmp, n_sblk, nsub, nb_pad), BF16)
    o_cmp, imp = cmp_attention_sample(qpad(q).reshape(bg, rq, HEAD_DIM), kvc[0].reshape(bg, nsub, HEAD_DIM),
                                      kvc[1].reshape(bg, nsub, HEAD_DIM), ov, n_cmp)
    imp = imp[:, :GQA_R, :n_sblk].sum(axis=1)
    score = block_scores(imp[:, None, :], pos, n_sblk)
    top_s, top_i = lax.top_k(score[:, 0], min(N_SEL, n_sblk))
    valid = jnp.isfinite(top_s)

    n_past_blk = past // SEL_BLOCK
    per_page = page // SEL_BLOCK
    n_k = top_i.shape[-1]
    in_past = valid & (top_i < n_past_blk)
    blk = jnp.where(in_past, top_i, -1).astype(jnp.int32)
    logical = jnp.clip(top_i, 0, n_past_blk - 1) // per_page
    phys = jnp.take_along_axis(page_table, logical.reshape(n_b, KV_HEADS * n_k), axis=1).reshape(bg, n_k)
    phys = jnp.where(in_past, phys, 0).astype(jnp.int32)
    new_flag = jnp.any(valid & (top_i >= n_past_blk), axis=-1).astype(jnp.int32)
    q_r = qpad(rope(q, pos))
    o_sel = decode_attention(q_r, new_rows[:, 0, 2][:, :, None, :], new_rows[:, 0, 3][:, :, None, :], cache_t, (2, 3),
                             phys.reshape(-1), blk.reshape(-1), new_flag, "sel", past, n_k, page=page)
    zeros = jnp.zeros((bg,), jnp.int32)
    o_win = decode_attention(q_r, new_win[:, 0, 0][:, :, None, :], new_win[:, 0, 1][:, :, None, :], win_t, (0, 1),
                             zeros, zeros, jnp.ones((bg,), jnp.int32), "win", past, 1, window=WINDOW)

    heads = lambda o: o.reshape(n_b, KV_HEADS, -1, HEAD_DIM)[:, :, :GQA_R].reshape(n_b, N_HEADS * HEAD_DIM)
    return (heads(o_cmp), heads(o_sel), heads(o_win)), new_rows, win_state


def rwkv_layer(h_p, h_t, n_p, t_p, n_s, state_wkv, state_shift, ln, mu, w_r, w_k, w_v, w0, w1, w2, a0, a1, a2, g1, g2,
               k_k, k_a, r_k, gn_w, gn_b):
    np_rows = n_p * t_p
    weights = (ln, mu, w_r, w_k, w_v, w0, w1, w2, a0, a1, a2, g1, g2, k_k, k_a)
    pre_p = rwkv_pre(h_p, jnp.zeros((RW_TM, D_MODEL), F32), np_rows, t_p, *weights)
    pre_t = rwkv_pre(h_t, jnp.pad(state_shift, ((0, RW_TM - n_s), (0, 0))), 0, RW_TM, *weights)

    def lanes(x, nb, t):
        return x.reshape(nb, t, RW_HEADS, RW_HEAD).transpose(1, 3, 0, 2).reshape(t, RW_HEAD, nb * RW_HEADS)

    def rows(o, nb, t):
        return o.reshape(t, RW_HEAD, nb, RW_HEADS).transpose(2, 0, 3, 1).reshape(nb * t, D_MODEL)

    head_vec = lambda w, nb: jnp.tile(w.reshape(RW_HEADS, RW_HEAD).T, (1, nb))
    s0_p = jnp.zeros((RW_HEAD, RW_HEAD, n_p * RW_HEADS), F32)
    o_p, st_p = rwkv_scan(*[lanes(x, n_p, t_p) for x in pre_p[:6]],
                          head_vec(r_k, n_p), head_vec(gn_w, n_p), head_vec(gn_b, n_p), s0_p)
    s0_s = state_wkv.transpose(3, 2, 0, 1).reshape(RW_HEAD, RW_HEAD, n_s * RW_HEADS)
    o_s, st_s = rwkv_scan(*[lanes(x[:n_s], n_s, 1) for x in pre_t[:6]],
                          head_vec(r_k, n_s), head_vec(gn_w, n_s), head_vec(gn_b, n_s), s0_s, dot_dtype=BF16)
    o_t = jnp.pad(rows(o_s, n_s, 1), ((0, RW_TM - n_s), (0, 0)))
    wkv_p = st_p.reshape(RW_HEAD, RW_HEAD, n_p, RW_HEADS).transpose(2, 3, 1, 0)
    wkv_s = st_s.reshape(RW_HEAD, RW_HEAD, n_s, RW_HEADS).transpose(2, 3, 1, 0)
    return (rows(o_p, n_p, t_p), pre_p[6], pre_p[7]), (o_t, pre_t[6], pre_t[7]), wkv_p, wkv_s


def kernel(x_prompt, x_sample, state_wkv, state_shift, cache_kv, cache_win, page_table, ln_mix, ln_ffn, ln_kv, ln_out, rw_mu, rw_wr, rw_wk, rw_wv, rw_wo, rw_w0, rw_w1, rw_w2, rw_a0, rw_a1, rw_a2, rw_g1, rw_g2, rw_kk, rw_ka, rw_rk, rw_lnw, rw_lnb, w_kv, cmp_pe, cmp_w1, cmp_b1, cmp_w2, nsa_wqg, nsa_wo, moe_wrg, moe_brg, moe_wre, moe_bre, moe_wgate, moe_wup, moe_wdown):
    n_p, t_p, dm = x_prompt.shape
    n_s = x_sample.shape[0]
    assert x_sample.shape[1] == 1
    np_rows = n_p * t_p
    h_p = x_prompt.reshape(np_rows, dm)
    h_t = jnp.pad(x_sample.reshape(n_s, dm), ((0, RW_TM - n_s), (0, 0)))
    sizes = (np_rows, RW_TM)

    def moe(fronts, layer):
        ys = hmoe(jnp.concatenate([f[1] for f in fronts]), jnp.concatenate([f[2] for f in fronts]), sizes,
                  moe_brg[layer], moe_bre[layer], moe_wgate[layer], moe_wup[layer], moe_wdown[layer])
        return [f[0] + y for f, y in zip(fronts, ys)]

    mix_p, mix_t, wkv_p, wkv_s = rwkv_layer(h_p, h_t, n_p, t_p, n_s, state_wkv[0], state_shift[0], ln_mix[0], rw_mu[0],
                                            rw_wr[0], rw_wk[0], rw_wv[0], rw_w0[0], rw_w1[0], rw_w2[0], rw_a0[0], rw_a1[0],
                                            rw_a2[0], rw_g1[0], rw_g2[0], rw_kk[0], rw_ka[0], rw_rk[0], rw_lnw[0], rw_lnb[0])
    p_shift = mix_p[2].reshape(n_p, t_p, dm)[:, -1][None]
    s_shift = mix_t[2][:n_s][None]
    w_router = router_weight(moe_wrg[0], moe_wre[0])
    h_p, h_t = moe([mix_out([o], g, hh, rw_wo[0], ln_ffn[0], w_router)
                    for (o, g, _), hh in ((mix_p, h_p), (mix_t, h_t))], 0)

    n_q = N_HEADS * HEAD_DIM
    gate_cols = _round_up(nsa_wqg.shape[-1] - n_q, LANES)
    w_qg = jnp.pad(nsa_wqg[0], ((0, 0), (0, n_q + gate_cols - nsa_wqg.shape[-1])))
    qg_p, qg_t = (mm(rmsnorm(hh, ln_mix[1]), w_qg) for hh in (h_p, h_t))
    kv_p, kv_t = (mm(rmsnorm(hh, ln_kv), w_kv) for hh in (h_p, h_t))
    o_p, p_rows, p_win_all = nsa_prompt(qg_p, kv_p, n_p, t_p, cmp_pe, cmp_w1, cmp_b1, cmp_w2)
    o_s, s_rows, s_win = nsa_sample(qg_t[:n_s, :nsa_wqg.shape[-1]], kv_t[:n_s], cache_kv, cache_win,
                                    page_table, cmp_pe, cmp_w1, cmp_b1, cmp_w2)
    o_t = [jnp.pad(x, ((0, RW_TM - n_s), (0, 0))) for x in o_s]
    expand = [jnp.asarray((np.arange(gate_cols)[:, None] == 3 * (np.arange(n_q)[None, :] // HEAD_DIM) + br)
                          .astype(np.float32), BF16) for br in range(3)]
    w_router = router_weight(moe_wrg[1], moe_wre[1])
    h_p, h_t = moe([mix_out(list(br), qg, hh, nsa_wo[0], ln_ffn[1], w_router, gate_expand=expand, gate_col=n_q // LANES)
                    for br, qg, hh in ((o_p, qg_p, h_p), (o_t, qg_t, h_t))], 1)

    y_prompt = rmsnorm(h_p, ln_out).reshape(n_p, t_p, dm)
    y_sample = rmsnorm(h_t[:n_s], ln_out).reshape(n_s, 1, dm)
    p_win = p_win_all[:, -min(WINDOW, t_p):]
    return (y_prompt, y_sample, wkv_p[None], p_shift, p_rows, p_win, wkv_s[None], s_shift, s_rows, s_win)
```

```python
import functools

import numpy as np
import jax
import jax.numpy as jnp
from jax import lax
from jax.experimental import pallas as pl
from jax.experimental.pallas import tpu as pltpu

F32 = jnp.float32
BF16 = jnp.bfloat16

D_MODEL = 1024
RMS_EPS = 1e-6
RW_HEAD = 64
RW_HEADS = D_MODEL // RW_HEAD
RW_GN_EPS = 64e-5
N_HEADS = 16
HEAD_DIM = 64
KV_HEADS = 4
GQA_R = N_HEADS // KV_HEADS
ROPE_DIMS = HEAD_DIM // 4
ROPE_THETA = 500000.0
N_KV_STREAMS = 6
CMP_BLOCK = 32
CMP_STRIDE = 16
SEL_BLOCK = 64
N_SEL = 16
N_LOCAL_BLOCKS = 2
FORCE_SCORE = 1.0e4
WINDOW = 512
N_GROUPS = 4
EXPERTS_PER_GROUP = 8
N_EXPERTS = N_GROUPS * EXPERTS_PER_GROUP
D_EXPERT = 256

LANES = 128
VMEM_LIMIT = 48 << 20
NEG_BIG = -1e30
ATT_SCALE = HEAD_DIM ** -0.5
TOKEN_TILE = 512


def _round_up(x, m):
    return (x + m - 1) // m * m


def _cparams(sem):
    return pltpu.CompilerParams(dimension_semantics=sem, vmem_limit_bytes=VMEM_LIMIT)


_NT = (((1,), (1,)), ((), ()))


def _mm_kernel(x_ref, w_ref, o_ref):
    o_ref[...] = jnp.dot(x_ref[...].astype(BF16), w_ref[...], preferred_element_type=F32)


def mm(x, w, tm=TOKEN_TILE):
    m, k = x.shape
    n = w.shape[1]
    tm = min(tm, _round_up(m, 8))
    mp = _round_up(m, tm)
    if mp != m:
        x = jnp.pad(x, ((0, mp - m), (0, 0)))
    out = pl.pallas_call(
        _mm_kernel,
        grid=(mp // tm,),
        in_specs=[pl.BlockSpec((tm, k), lambda i: (i, 0)),
                  pl.BlockSpec((k, n), lambda i: (0, 0))],
        out_specs=pl.BlockSpec((tm, n), lambda i: (i, 0)),
        out_shape=jax.ShapeDtypeStruct((mp, n), F32),
        compiler_params=_cparams(("arbitrary",)),
        name="mm",
    )(x, w.astype(BF16))
    return out[:m] if mp != m else out


SCAN_TC = 32


def _scan_kernel(r_ref, d_ref, k_ref, b_ref, n_ref, v_ref, rk_ref, gw_ref, gb_ref, s0_ref, o_ref, st_ref, s_scr,
                 *, tc, dot_dtype):
    c = pl.program_id(1)
    rnd = lambda x: x.astype(dot_dtype).astype(F32)
    halves = [(h0, h0 + RW_HEAD // 2) for h0 in (0, RW_HEAD // 2)]

    @pl.when(c == 0)
    def _():
        s_scr[...] = s0_ref[...]

    def step(t, carry):
        vv = v_ref[t]
        outs = []
        for lo, hi in halves:
            parts = [jnp.zeros((hi - lo, LANES), F32)] * 2
            for j in range(RW_HEAD):
                parts[j % 2] = parts[j % 2] + rnd(s_scr[j, lo:hi]) * rnd(n_ref[t, pl.ds(j, 1), :])
            sa = parts[0] + parts[1]
            vh = vv[lo:hi]
            out = jnp.zeros((hi - lo, LANES), F32)
            for j in range(RW_HEAD):
                sn = (s_scr[j, lo:hi] * d_ref[t, pl.ds(j, 1), :] + sa * b_ref[t, pl.ds(j, 1), :]
                      + vh * k_ref[t, pl.ds(j, 1), :])
                s_scr[j, lo:hi] = sn
                out = out + rnd(sn) * rnd(r_ref[t, pl.ds(j, 1), :])
            outs.append(out)
        out = jnp.concatenate(outs, axis=0)
        mean = jnp.mean(out, axis=0, keepdims=True)
        cen = out - mean
        var = jnp.mean(cen * cen, axis=0, keepdims=True)
        bonus = jnp.sum(r_ref[t] * k_ref[t] * rk_ref[...], axis=0, keepdims=True) * vv
        o_ref[t] = cen * lax.rsqrt(var + RW_GN_EPS) * gw_ref[...] + gb_ref[...] + bonus
        return carry

    lax.fori_loop(0, tc, step, 0)

    @pl.when(c == pl.num_programs(1) - 1)
    def _():
        st_ref[...] = s_scr[...]


def rwkv_scan(r, d, k, b, nkk, v, rk, gw, gb, s0, dot_dtype=F32):
    t, _, l = r.shape
    tc = min(SCAN_TC, t)
    assert t % tc == 0 and l % LANES == 0
    seq = pl.BlockSpec((tc, RW_HEAD, LANES), lambda g, c: (c, 0, g))
    vec = pl.BlockSpec((RW_HEAD, LANES), lambda g, c: (0, g))
    st = pl.BlockSpec((RW_HEAD, RW_HEAD, LANES), lambda g, c: (0, 0, g))
    return pl.pallas_call(
        functools.partial(_scan_kernel, tc=tc, dot_dtype=dot_dtype),
        grid=(l // LANES, t // tc),
        in_specs=[seq] * 6 + [vec] * 3 + [st],
        out_specs=[seq, st],
        out_shape=[jax.ShapeDtypeStruct((t, RW_HEAD, l), F32),
                   jax.ShapeDtypeStruct((RW_HEAD, RW_HEAD, l), F32)],
        scratch_shapes=[pltpu.VMEM((RW_HEAD, RW_HEAD, LANES), F32)],
        compiler_params=_cparams(("arbitrary", "arbitrary")),
        name="rwkv_scan",
    )(r, d, k, b, nkk, v, rk, gw, gb, s0)


RW_TM = 256


def _split_dot(x, w):
    hi = x.astype(BF16)
    lo = (x - hi.astype(F32)).astype(BF16)
    return jnp.dot(hi, w, preferred_element_type=F32) + jnp.dot(lo, w, preferred_element_type=F32)


def _rwkv_pre_kernel(x_ref, ps_ref, ln_ref, mu_ref, vec_ref, wr_ref, wk_ref, wv_ref, w1_ref, w2_ref, a1_ref, a2_ref,
                     g1_ref, g2_ref, seg_ref, segt_ref,
                     r_o, d_o, k_o, b_o, n_o, v_o, g_o, xn_o, carry_scr, *, tiles_per_seq, n_seq_tiles):
    i = pl.program_id(0)
    tm = x_ref.shape[0]

    @pl.when(i == 0)
    def _():
        carry_scr[...] = jnp.zeros_like(carry_scr)

    x = x_ref[...]
    xn = x * lax.rsqrt(jnp.mean(x * x, axis=-1, keepdims=True) + RMS_EPS) * ln_ref[...]
    xn_o[...] = xn
    first = jnp.where(i % tiles_per_seq == 0, jnp.zeros_like(carry_scr[...]), carry_scr[...])
    row0 = lax.broadcasted_iota(jnp.int32, xn.shape, 0) == 0
    prev = jnp.where(row0, first, pltpu.roll(xn, 1, 0))
    prev = jnp.where(i >= n_seq_tiles, ps_ref[...], prev)
    carry_scr[...] = xn[tm - 1:tm, :]
    xx = prev - xn
    mix = lambda c: (xn + xx * mu_ref[c:c + 1, :]).astype(BF16)
    dot = lambda a, w_ref: jnp.dot(a, w_ref[...], preferred_element_type=F32)
    w0, a0, k_k, k_a = (vec_ref[c:c + 1, :] for c in range(4))
    r = dot(mix(0), wr_ref)
    k = dot(mix(2), wk_ref)
    v = dot(mix(3), wv_ref)
    w_log = -jax.nn.softplus(-(w0 + dot(jnp.tanh(dot(mix(1), w1_ref)).astype(BF16), w2_ref))) - 0.5
    a = jax.nn.sigmoid(a0 + dot(dot(mix(4), a1_ref).astype(BF16), a2_ref))
    g_o[...] = dot(jax.nn.sigmoid(dot(mix(5), g1_ref)).astype(BF16), g2_ref)
    kk = k * k_k
    ss = _split_dot(_split_dot(kk * kk, seg_ref[...]), segt_ref[...])
    kk = kk / jnp.maximum(jnp.sqrt(ss), 1e-12)
    r_o[...] = r
    d_o[...] = jnp.exp(-jnp.exp(w_log))
    k_o[...] = k * (1.0 + (a - 1.0) * k_a)
    b_o[...] = kk * a
    n_o[...] = -kk
    v_o[...] = v


def rwkv_pre(h, prev_single, n_seq_rows, t, ln, mu, w_r, w_k, w_v, w0, w1, w2, a0, a1, a2, g1, g2, k_k, k_a):
    n, dm = h.shape
    tm = RW_TM
    assert n % tm == 0 and t % tm == 0 and n_seq_rows % tm == 0 and prev_single.shape == (tm, dm)
    seg = (np.arange(dm)[:, None] // RW_HEAD == np.arange(LANES)[None, :]).astype(np.float32)
    row = pl.BlockSpec((tm, dm), lambda i: (i, 0))
    full = lambda a: pl.BlockSpec(a.shape, lambda i: (0,) * a.ndim)
    bf = lambda w: w.astype(BF16)
    consts = [prev_single, ln[None, :], mu, jnp.stack([w0, a0, k_k, k_a]), bf(w_r), bf(w_k), bf(w_v), bf(w1), bf(w2),
              bf(a1), bf(a2), bf(g1), bf(g2), jnp.asarray(seg, BF16), jnp.asarray(seg.T, BF16)]
    return pl.pallas_call(
        functools.partial(_rwkv_pre_kernel, tiles_per_seq=t // tm, n_seq_tiles=n_seq_rows // tm),
        grid=(n // tm,),
        in_specs=[row] + [full(a) for a in consts],
        out_specs=[row] * 8,
        out_shape=[jax.ShapeDtypeStruct((n, dm), F32)] * 8,
        scratch_shapes=[pltpu.VMEM((1, dm), F32)],
        compiler_params=_cparams(("arbitrary",)),
        name="rwkv_pre",
    )(h, *consts)


def _mix_out_kernel(*refs, n_branch, use_gates):
    o_refs = refs[:n_branch]
    gate_ref, h_ref, wo_ref, ln_ref, wr_ref = refs[n_branch:n_branch + 5]
    e_refs = refs[n_branch + 5:n_branch + 5 + (n_branch if use_gates else 0)]
    h_o, xp_o, lg_o = refs[-3:]
    if use_gates:
        sig = jax.nn.sigmoid(gate_ref[...])
        mixed = sum(_split_dot(sig, e_refs[b][...]) * o_refs[b][...] for b in range(n_branch))
    else:
        mixed = o_refs[0][...] * gate_ref[...]
    h = h_ref[...] + jnp.dot(mixed.astype(BF16), wo_ref[...], preferred_element_type=F32)
    h_o[...] = h
    xn = h * lax.rsqrt(jnp.mean(h * h, axis=-1, keepdims=True) + RMS_EPS) * ln_ref[...]
    xb = xn.astype(BF16)
    lg_o[...] = jnp.dot(xb, wr_ref[...], preferred_element_type=F32)
    u = lax.bitcast_convert_type(xb.astype(F32), jnp.uint32)
    half = u.shape[1] // 2
    xp_o[...] = (u[:, :half] >> 16) | (u[:, half:] & jnp.uint32(0xFFFF0000))


def mix_out(branches, gate, h, w_o, ln, w_router, gate_expand=None, gate_col=0):
    n, dm = h.shape
    tm = RW_TM
    nb = len(branches)
    row = lambda a: pl.BlockSpec((tm, a.shape[1]), lambda i: (i, 0))
    full = lambda a: pl.BlockSpec(a.shape, lambda i: (0,) * a.ndim)
    consts = [w_o.astype(BF16), ln[None, :], w_router.astype(BF16)] + list(gate_expand or [])
    gate_spec = row(gate) if gate_expand is None else pl.BlockSpec((tm, LANES), lambda i: (i, gate_col))
    return pl.pallas_call(
        functools.partial(_mix_out_kernel, n_branch=nb, use_gates=gate_expand is not None),
        grid=(n // tm,),
        in_specs=[row(a) for a in branches] + [gate_spec, row(h)] + [full(a) for a in consts],
        out_specs=[pl.BlockSpec((tm, dm), lambda i: (i, 0)), pl.BlockSpec((tm, dm // 2), lambda i: (i, 0)),
                   pl.BlockSpec((tm, LANES), lambda i: (i, 0))],
        out_shape=[jax.ShapeDtypeStruct((n, dm), F32), jax.ShapeDtypeStruct((n, dm // 2), jnp.uint32),
                   jax.ShapeDtypeStruct((n, LANES), F32)],
        compiler_params=_cparams(("arbitrary",)),
        name="mix_out",
    )(*branches, gate, h, *consts)


MOE_TM = 256
MOE_VMEM_LIMIT = 56 << 20


def _moe_kernel(te_ref, nt_ref, tok_ref, x_ref, c_ref, wgu_ref, wd_ref, o_ref, xt_scr):
    i = pl.program_id(0)

    @pl.when(i < nt_ref[0])
    def _():
        base = i * MOE_TM

        def gather_row(r, carry):
            xt_scr[pl.ds(r, 1), :] = x_ref[pl.ds(tok_ref[base + r], 1), :]
            return carry

        lax.fori_loop(0, MOE_TM, gather_row, 0, unroll=8)
        u = xt_scr[...]
        lo = lax.bitcast_convert_type(u << 16, F32).astype(BF16)
        hi = lax.bitcast_convert_type(u & jnp.uint32(0xFFFF0000), F32).astype(BF16)
        x = jnp.concatenate([lo, hi], axis=1)
        h = jnp.dot(x, wgu_ref[0], preferred_element_type=F32)
        g = h[:, :D_EXPERT]
        u2 = h[:, D_EXPERT:]
        hid = g * jax.nn.sigmoid(g) * u2 * c_ref[...]
        o_ref[...] = jnp.dot(hid.astype(BF16), wd_ref[0], preferred_element_type=F32)

    @pl.when(i >= nt_ref[0])
    def _():
        o_ref[...] = jnp.zeros_like(o_ref)


def moe_gmm(xp, row_tok, cs, tile_expert, n_tiles_used, wgu, wd):
    n, half = xp.shape
    dm = 2 * half
    p = row_tok.shape[0]
    nt = p // MOE_TM

    def row_map(i, te, ntu, tok):
        return (jnp.minimum(i, ntu[0] - 1), 0)

    grid_spec = pltpu.PrefetchScalarGridSpec(
        num_scalar_prefetch=3,
        grid=(nt,),
        in_specs=[pl.BlockSpec((n, half), lambda i, te, ntu, tok: (0, 0), pipeline_mode=pl.Buffered(1)),
                  pl.BlockSpec((MOE_TM, 1), row_map),
                  pl.BlockSpec((1, dm, 2 * D_EXPERT), lambda i, te, ntu, tok: (te[i], 0, 0)),
                  pl.BlockSpec((1, D_EXPERT, dm), lambda i, te, ntu, tok: (te[i], 0, 0))],
        out_specs=pl.BlockSpec((MOE_TM, dm), lambda i, te, ntu, tok: (i, 0)),
        scratch_shapes=[pltpu.VMEM((MOE_TM, half), jnp.uint32)],
    )
    return pl.pallas_call(
        _moe_kernel,
        grid_spec=grid_spec,
        out_shape=jax.ShapeDtypeStruct((p, dm), F32),
        compiler_params=pltpu.CompilerParams(dimension_semantics=("arbitrary",), vmem_limit_bytes=MOE_VMEM_LIMIT),
        name="moe_gmm",
    )(tile_expert, n_tiles_used, row_tok, xp, cs, wgu, wd)


def router_weight(w_rg, w_re):
    w_r = jnp.concatenate([w_rg, w_re], axis=1)
    return jnp.pad(w_r, ((0, 0), (0, LANES - w_r.shape[1])))


def hmoe(xp, logits, set_sizes, b_rg, b_re, w_gate, w_up, w_down):
    n = xp.shape[0]
    dm = 2 * xp.shape[1]
    lg = logits[:, :N_GROUPS] + b_rg
    le = (logits[:, N_GROUPS:N_GROUPS + N_EXPERTS] + b_re).reshape(n, N_GROUPS, EXPERTS_PER_GROUP)
    g_sel = jnp.argmax(lg, axis=-1)
    p_g = jnp.max(jax.nn.softmax(lg, axis=-1), axis=-1)
    le_g = jnp.take_along_axis(le, g_sel[:, None, None], axis=1)[:, 0]
    top_p, top_i = lax.top_k(jax.nn.softmax(le_g, axis=-1), 2)
    w_top = p_g[:, None] * top_p / jnp.sum(top_p, axis=-1, keepdims=True)
    eid = (g_sel[:, None] * EXPERTS_PER_GROUP + top_i).astype(jnp.int32).reshape(-1)
    cw = w_top.reshape(-1)

    n2 = 2 * n
    onehot = (eid[:, None] == jnp.arange(N_EXPERTS, dtype=jnp.int32)[None, :]).astype(jnp.int32)
    counts = jnp.sum(onehot, axis=0)
    rank = jnp.take_along_axis(jnp.cumsum(onehot, axis=0) - onehot, eid[:, None], axis=1)[:, 0]
    tiles_per = (counts + MOE_TM - 1) // MOE_TM
    tile_end = jnp.cumsum(tiles_per)
    tile_start = tile_end - tiles_per
    pos = tile_start[eid] * MOE_TM + rank
    p_rows = _round_up(n2 + N_EXPERTS * (MOE_TM - 1), MOE_TM)
    nt = p_rows // MOE_TM
    n_used = tile_end[-1].astype(jnp.int32)
    tile_ids = jnp.arange(nt, dtype=jnp.int32)
    tile_expert = jnp.sum((tile_end[None, :] <= jnp.minimum(tile_ids, n_used - 1)[:, None]).astype(jnp.int32), axis=1)

    tok = jnp.arange(n2, dtype=jnp.int32) // 2
    row_tok = jnp.zeros((p_rows,), jnp.int32).at[pos].set(tok)
    row_c = jnp.zeros((p_rows,), F32).at[pos].set(cw)
    wgu = jnp.concatenate([w_gate, w_up], axis=-1).reshape(N_EXPERTS, dm, 2 * D_EXPERT).astype(BF16)
    wd = w_down.reshape(N_EXPERTS, D_EXPERT, dm).astype(BF16)
    ys = moe_gmm(xp, row_tok, row_c[:, None], tile_expert, n_used.reshape(1), wgu, wd)
    pos2 = pos.reshape(n, 2)
    bounds = np.cumsum([0] + list(set_sizes))
    return [ys[pos2[a:b, 0]] + ys[pos2[a:b, 1]] for a, b in zip(bounds[:-1], bounds[1:])]


ATT_TQ = 256
ATT_TK = 512


def _cmp_sel_kernel(q_ref, kc_ref, vct_ref, ovt_ref, o_ref, sel_ref, *, n_cmp, n_sblk, tq):
    t0 = pl.program_id(2) * tq
    ncp = kc_ref.shape[2]
    n_io = lax.broadcasted_iota(jnp.int32, (ncp, tq), 0)
    t_io = lax.broadcasted_iota(jnp.int32, (ncp, tq), 1) + t0
    mask = (n_io * CMP_STRIDE + (CMP_BLOCK - 1) <= t_io) & (n_io < n_cmp)
    kc = kc_ref[0, 0]
    vct = vct_ref[0, 0]
    ovt = ovt_ref[...]
    imp = jnp.zeros((n_sblk, tq), F32)
    outs = []
    for r in range(GQA_R):
        qr = (q_ref[:, r * HEAD_DIM:(r + 1) * HEAD_DIM] * ATT_SCALE).astype(BF16)
        st = lax.dot_general(kc, qr, _NT, preferred_element_type=F32)
        st = jnp.where(mask, st, NEG_BIG)
        m = jnp.max(st, axis=0, keepdims=True)
        e = jnp.where(mask, jnp.exp(st - m), 0.0)
        p = (e / jnp.maximum(jnp.sum(e, axis=0, keepdims=True), 1e-30)).astype(BF16)
        outs.append(jnp.dot(vct, p, preferred_element_type=F32).T)
        imp = imp + jnp.dot(ovt, p, preferred_element_type=F32)
    o_ref[...] = jnp.concatenate(outs, axis=1)

    j_io = lax.broadcasted_iota(jnp.int32, (n_sblk, tq), 0)
    qblk = jnp.right_shift(lax.broadcasted_iota(jnp.int32, (n_sblk, tq), 1) + t0, SEL_BLOCK.bit_length() - 1)
    allowed = j_io <= qblk
    forced = (j_io == 0) | ((j_io > qblk - N_LOCAL_BLOCKS) & allowed)
    score = jnp.where(forced, FORCE_SCORE, imp)
    score = jnp.where(allowed, score, -jnp.inf)
    rank = jnp.zeros((n_sblk, tq), jnp.int32)
    for jp in range(n_sblk):
        row = score[jp:jp + 1, :]
        beats = (row > score) | ((row == score) & (j_io > jp))
        rank = rank + jnp.where(beats, 1, 0)
    sel = (rank < N_SEL) & allowed
    sel_ref[0, 0] = jnp.where(sel, 1.0, 0.0).astype(BF16)


def cmp_select(q, kc, vct, ovt, n_b, t, n_cmp, n_sblk, n_out):
    assert SEL_BLOCK & (SEL_BLOCK - 1) == 0
    tq = ATT_TQ
    nq = t // tq
    ncp = kc.shape[2]
    gw = GQA_R * HEAD_DIM
    return pl.pallas_call(
        functools.partial(_cmp_sel_kernel, n_cmp=n_cmp, n_sblk=n_sblk, tq=tq),
        grid=(n_b, KV_HEADS, nq),
        in_specs=[pl.BlockSpec((tq, gw), lambda b, g, qi: (b * nq + qi, g)),
                  pl.BlockSpec((1, 1, ncp, HEAD_DIM), lambda b, g, qi: (b, g, 0, 0)),
                  pl.BlockSpec((1, 1, HEAD_DIM, ncp), lambda b, g, qi: (b, g, 0, 0)),
                  pl.BlockSpec((n_sblk, ncp), lambda b, g, qi: (0, 0))],
        out_specs=[pl.BlockSpec((tq, gw), lambda b, g, qi: (b * nq + qi, g)),
                   pl.BlockSpec((1, 1, n_sblk, tq), lambda b, g, qi: (b, g, 0, qi))],
        out_shape=[jax.ShapeDtypeStruct((n_out, N_HEADS * HEAD_DIM), F32),
                   jax.ShapeDtypeStruct((n_b, KV_HEADS, n_sblk, t), BF16)],
        compiler_params=_cparams(("arbitrary", "arbitrary", "arbitrary")),
        name="attn_cmp_select",
    )(q, kc, vct, ovt)


def _flash_kernel(pq_ref, pk_ref, pfirst_ref, plast_ref, q_ref, kt_ref, vt_ref, *rest, window, use_sel, tq, tk):
    if use_sel:
        sel_ref, et_ref, o_ref, m_scr, l_scr, acc_scr = rest
    else:
        o_ref, m_scr, l_scr, acc_scr = rest
    p_id = pl.program_id(2)

    @pl.when(pfirst_ref[p_id] == 1)
    def _():
        m_scr[...] = jnp.full_like(m_scr, NEG_BIG)
        l_scr[...] = jnp.zeros_like(l_scr)
        acc_scr[...] = jnp.zeros_like(acc_scr)

    off = pq_ref[p_id] * tq - pk_ref[p_id] * tk
    k = kt_ref[0, 0, 0].astype(BF16).T
    vt = vt_ref[0, 0, 0].astype(BF16)
    for c in range(tq // LANES):
        cols = slice(c * LANES, (c + 1) * LANES)
        diff = (lax.broadcasted_iota(jnp.int32, (tk, LANES), 0) - lax.broadcasted_iota(jnp.int32, (tk, LANES), 1)
                - c * LANES)
        mask = diff <= off
        if window is not None:
            mask = mask & (diff > off - window)
        if use_sel:
            selm = jnp.dot(et_ref[...], sel_ref[0, 0, :, cols], preferred_element_type=F32)
            mask = mask & (selm > 0.5)
        for r in range(GQA_R):
            qr = (q_ref[cols, r * HEAD_DIM:(r + 1) * HEAD_DIM] * ATT_SCALE).astype(BF16)
            st = jnp.where(mask, lax.dot_general(k, qr, _NT, preferred_element_type=F32), NEG_BIG)
            m_prev = m_scr[r, :, cols]
            m_new = jnp.maximum(m_prev, jnp.max(st, axis=0, keepdims=True))
            p = jnp.where(mask, jnp.exp(st - m_new), 0.0)
            alpha = jnp.exp(m_prev - m_new)
            l_scr[r, :, cols] = alpha * l_scr[r, :, cols] + jnp.sum(p, axis=0, keepdims=True)
            acc_scr[r, :, cols] = (alpha * acc_scr[r, :, cols]
                                   + jnp.dot(vt, p.astype(BF16), preferred_element_type=F32))
            m_scr[r, :, cols] = m_new

    @pl.when(plast_ref[p_id] == 1)
    def _():
        outs = [(acc_scr[r] / jnp.maximum(l_scr[r], 1e-30)).T for r in range(GQA_R)]
        o_ref[...] = jnp.concatenate(outs, axis=1)


def flash_attention(q, kv_t, k_stream, v_stream, n_b, t, window=None, sel=None):
    tq, tk = ATT_TQ, ATT_TK
    nq = t // tq
    first_tile = lambda qi: 0 if window is None else max(0, qi * tq - (window - 1)) // tk
    pairs = [(qi, kt) for qi in range(nq) for kt in range(first_tile(qi), (qi * tq + tq - 1) // tk + 1)]
    pq = np.array([p[0] for p in pairs], np.int32)
    pk = np.array([p[1] for p in pairs], np.int32)
    pfirst = np.array([1 if i == 0 or pairs[i - 1][0] != p[0] else 0 for i, p in enumerate(pairs)], np.int32)
    plast = np.array([1 if i == len(pairs) - 1 or pairs[i + 1][0] != p[0] else 0 for i, p in enumerate(pairs)], np.int32)
    gw = GQA_R * HEAD_DIM

    def tile_map(stream):
        return lambda b, g, p, pq_r, pk_r, f_r, l_r: (b, stream, g, 0, pk_r[p])

    in_specs = [pl.BlockSpec((tq, gw), lambda b, g, p, pq_r, pk_r, f_r, l_r: (b * nq + pq_r[p], g)),
                pl.BlockSpec((1, 1, 1, HEAD_DIM, tk), tile_map(k_stream)),
                pl.BlockSpec((1, 1, 1, HEAD_DIM, tk), tile_map(v_stream))]
    args = [q, kv_t, kv_t]
    if sel is not None:
        n_sblk = sel.shape[2]
        emat = (np.arange(t)[:, None] // SEL_BLOCK == np.arange(n_sblk)[None, :]).astype(np.float32)
        in_specs += [pl.BlockSpec((1, 1, n_sblk, tq), lambda b, g, p, pq_r, pk_r, f_r, l_r: (b, g, 0, pq_r[p])),
                     pl.BlockSpec((tk, n_sblk), lambda b, g, p, pq_r, pk_r, f_r, l_r: (pk_r[p], 0))]
        args += [sel, jnp.asarray(emat, BF16)]
    grid_spec = pltpu.PrefetchScalarGridSpec(
        num_scalar_prefetch=4,
        grid=(n_b, KV_HEADS, len(pairs)),
        in_specs=in_specs,
        out_specs=pl.BlockSpec((tq, gw), lambda b, g, p, pq_r, pk_r, f_r, l_r: (b * nq + pq_r[p], g)),
        scratch_shapes=[pltpu.VMEM((GQA_R, 1, tq), F32), pltpu.VMEM((GQA_R, 1, tq), F32),
                        pltpu.VMEM((GQA_R, HEAD_DIM, tq), F32)],
    )
    return pl.pallas_call(
        functools.partial(_flash_kernel, window=window, use_sel=sel is not None, tq=tq, tk=tk),
        grid_spec=grid_spec,
        out_shape=jax.ShapeDtypeStruct((n_b * t, N_HEADS * HEAD_DIM), F32),
        compiler_params=_cparams(("arbitrary", "arbitrary", "arbitrary")),
        name="attn_sel" if sel is not None else "attn_win",
    )(jnp.asarray(pq), jnp.asarray(pk), jnp.asarray(pfirst), jnp.asarray(plast), *args)


def _attn_pre_kernel(h_ref, ln_ref, wq_ref, wkv_ref, rc_ref, rm_ref, rp_ref, qg_o, qr_o, kvt_o):
    x = h_ref[...]
    xs = x * lax.rsqrt(jnp.mean(x * x, axis=-1, keepdims=True) + RMS_EPS)
    qg = jnp.dot((xs * ln_ref[0:1, :]).astype(BF16), wq_ref[...], preferred_element_type=F32)
    kv = jnp.dot((xs * ln_ref[1:2, :]).astype(BF16), wkv_ref[...], preferred_element_type=F32)
    qg_o[...] = qg

    def rope(v):
        reps = v.shape[1] // LANES
        tile = lambda ref: jnp.concatenate([ref[...]] * reps, axis=1)
        n = v.shape[1]
        return v * tile(rc_ref) + pltpu.roll(v, n - ROPE_DIMS // 2, 1) * tile(rm_ref) + pltpu.roll(v, ROPE_DIMS // 2, 1) * tile(rp_ref)

    n_q = N_HEADS * HEAD_DIM
    sw = KV_HEADS * HEAD_DIM
    qr_o[...] = rope(qg[:, :n_q])
    kv_r = jnp.concatenate([kv[:, :2 * sw], rope(kv[:, 2 * sw:3 * sw]), kv[:, 3 * sw:4 * sw],
                            rope(kv[:, 4 * sw:5 * sw]), kv[:, 5 * sw:]], axis=1)
    kvt_o[0] = kv_r.T


def attn_pre(h, n_b, t, ln_q, ln_kv, w_qg, w_kv):
    n, dm = h.shape
    tm = RW_TM
    tiles = t // tm
    half = ROPE_DIMS // 2
    inv = jnp.power(ROPE_THETA, -jnp.arange(half, dtype=F32) * (2.0 / ROPE_DIMS))
    ang = jnp.arange(t, dtype=F32)[:, None] * inv[None, :]
    cos, sin = jnp.cos(ang), jnp.sin(ang)
    pad = jnp.zeros((t, HEAD_DIM - ROPE_DIMS), F32)
    zero = jnp.zeros((t, half), F32)
    head = lambda parts: jnp.concatenate(parts, axis=1)
    two = lambda x: jnp.concatenate([x, x], axis=1)
    rc = two(head([cos, cos, pad + 1.0]))
    rm = two(head([-sin, zero, pad]))
    rp = two(head([zero, sin, pad]))
    nq_cols = w_qg.shape[1]
    nkv = w_kv.shape[1]
    tab = pl.BlockSpec((tm, LANES), lambda i: (i % tiles, 0))
    full = lambda a: pl.BlockSpec(a.shape, lambda i: (0,) * a.ndim)
    consts = [jnp.stack([ln_q, ln_kv]), w_qg.astype(BF16), w_kv.astype(BF16)]
    return pl.pallas_call(
        _attn_pre_kernel,
        grid=(n // tm,),
        in_specs=[pl.BlockSpec((tm, dm), lambda i: (i, 0))] + [full(a) for a in consts] + [tab, tab, tab],
        out_specs=[pl.BlockSpec((tm, nq_cols), lambda i: (i, 0)),
                   pl.BlockSpec((tm, N_HEADS * HEAD_DIM), lambda i: (i, 0)),
                   pl.BlockSpec((1, nkv, tm), lambda i: (i // tiles, 0, i % tiles))],
        out_shape=[jax.ShapeDtypeStruct((n, nq_cols), F32), jax.ShapeDtypeStruct((n, N_HEADS * HEAD_DIM), F32),
                   jax.ShapeDtypeStruct((n_b, nkv, t), F32)],
        compiler_params=_cparams(("arbitrary",)),
        name="attn_pre",
    )(h, *consts, rc, rm, rp)


def _cmp_kernel(q_ref, k_ref, v_ref, ov_ref, o_ref, imp_ref, *, n_cmp):
    q = q_ref[0] * ATT_SCALE
    s = lax.dot_general(q, k_ref[0], _NT, preferred_element_type=F32)
    mask = lax.broadcasted_iota(jnp.int32, s.shape, 1) < n_cmp
    s = jnp.where(mask, s, NEG_BIG)
    m = jnp.max(s, axis=-1, keepdims=True)
    e = jnp.where(mask, jnp.exp(s - m), 0.0)
    p = (e / jnp.maximum(jnp.sum(e, axis=-1, keepdims=True), 1e-30)).astype(BF16)
    o_ref[0] = jnp.dot(p, v_ref[0], preferred_element_type=F32)
    imp_ref[0] = jnp.dot(p, ov_ref[...], preferred_element_type=F32)


def cmp_attention_sample(q, kc, vc, ov, n_cmp):
    bg, rq, hd = q.shape
    nc = kc.shape[1]
    nb = ov.shape[1]
    return pl.pallas_call(
        functools.partial(_cmp_kernel, n_cmp=n_cmp),
        grid=(bg,),
        in_specs=[pl.BlockSpec((1, rq, hd), lambda b: (b, 0, 0)),
                  pl.BlockSpec((1, nc, hd), lambda b: (b, 0, 0)),
                  pl.BlockSpec((1, nc, hd), lambda b: (b, 0, 0)),
                  pl.BlockSpec((nc, nb), lambda b: (0, 0))],
        out_specs=[pl.BlockSpec((1, rq, hd), lambda b: (b, 0, 0)),
                   pl.BlockSpec((1, rq, nb), lambda b: (b, 0, 0))],
        out_shape=[jax.ShapeDtypeStruct((bg, rq, hd), F32),
                   jax.ShapeDtypeStruct((bg, rq, nb), F32)],
        compiler_params=_cparams(("arbitrary",)),
        name="attn_cmp_sample",
    )(q, kc, vc, ov)


PAGES_PER_STEP = 8


def _page_compress_kernel(pt_ref, *rest, n_pages, page, pb):
    c_refs = rest[:pb]
    wbd_ref, pe_ref, w2_ref, o_ref, a_scr = rest[pb:]
    p = pl.program_id(1)
    nsub = n_pages * page // CMP_STRIDE
    for i in range(pb):
        row0 = pl.multiple_of((p * pb + i) * page, page)
        for s in range(2):
            for gp in range(KV_HEADS // 2):
                x = c_refs[i][0, s, 2 * gp:2 * gp + 2].reshape(2 * HEAD_DIM, page)
                a_scr[s, gp, pl.ds(row0, page), :] = x.T

    @pl.when(p == n_pages // pb - 1)
    def _():
        hid = pe_ref.shape[-1]
        for s in range(2):
            for gp in range(KV_HEADS // 2):
                acc = jnp.zeros((nsub, 4 * hid), F32)
                for j in range(CMP_STRIDE):
                    lhs = a_scr[s, gp, pl.ds(j, nsub, stride=CMP_STRIDE), :].astype(BF16)
                    acc = acc + jnp.dot(lhs, wbd_ref[s, j], preferred_element_type=F32)
                for g2 in range(2):
                    first = acc[:, 2 * g2 * hid:(2 * g2 + 1) * hid]
                    second = acc[:, (2 * g2 + 1) * hid:(2 * g2 + 2) * hid]
                    pre = first + pltpu.roll(second, nsub - 1, 0) + pe_ref[s]
                    act = jax.nn.gelu(pre).astype(BF16)
                    o_ref[s, 0, 2 * gp + g2] = jnp.dot(act, w2_ref[s], preferred_element_type=F32).astype(BF16)


def page_compress(cache_t, page_table, cmp_pe, cmp_w1, cmp_b1, cmp_w2):
    n_b, n_pages = page_table.shape
    page = cache_t.shape[-1]
    nsub = n_pages * page // CMP_STRIDE
    hid = cmp_w1.shape[-1]
    pb = PAGES_PER_STEP
    assert n_pages % pb == 0
    w1r = cmp_w1.reshape(2, 2, CMP_STRIDE, HEAD_DIM, hid).transpose(0, 2, 3, 1, 4)
    w1r = w1r.reshape(2, CMP_STRIDE, HEAD_DIM, 2 * hid)
    zeros = jnp.zeros_like(w1r)
    wbd = jnp.concatenate([jnp.concatenate([w1r, zeros], axis=-1), jnp.concatenate([zeros, w1r], axis=-1)], axis=2)
    pe_term = (jnp.einsum('sjd,sjdc->sc', cmp_pe, cmp_w1.reshape(2, CMP_BLOCK, HEAD_DIM, hid)) + cmp_b1)[:, None, :]
    def page_map(i):
        return lambda b, p, pt: (pt[b * n_pages + p * pb + i], 0, 0, 0, 0)

    grid_spec = pltpu.PrefetchScalarGridSpec(
        num_scalar_prefetch=1,
        grid=(n_b, n_pages // pb),
        in_specs=[pl.BlockSpec((1, 2, KV_HEADS, HEAD_DIM, page), page_map(i)) for i in range(pb)] + [
            pl.BlockSpec((2, CMP_STRIDE, 2 * HEAD_DIM, 4 * hid), lambda b, p, pt: (0, 0, 0, 0)),
            pl.BlockSpec((2, 1, hid), lambda b, p, pt: (0, 0, 0)),
            pl.BlockSpec((2, hid, HEAD_DIM), lambda b, p, pt: (0, 0, 0))],
        out_specs=pl.BlockSpec((2, 1, KV_HEADS, nsub, HEAD_DIM), lambda b, p, pt: (0, b, 0, 0, 0)),
        scratch_shapes=[pltpu.VMEM((2, KV_HEADS // 2, n_pages * page, 2 * HEAD_DIM), F32)],
    )
    return pl.pallas_call(
        functools.partial(_page_compress_kernel, n_pages=n_pages, page=page, pb=pb),
        grid_spec=grid_spec,
        out_shape=jax.ShapeDtypeStruct((2, n_b, KV_HEADS, nsub, HEAD_DIM), BF16),
        compiler_params=_cparams(("arbitrary", "arbitrary")),
        name="page_compress",
    )(page_table.reshape(-1), *([cache_t] * pb), wbd.astype(BF16), pe_term, cmp_w2.astype(BF16))


def _decode_attn_kernel(phys_ref, blk_ref, flag_ref, q_ref, kn_ref, vn_ref, *rest, mode, past, steps, window, page):
    kt_refs = rest[:steps]
    vt_refs = rest[steps:2 * steps]
    o_ref = rest[2 * steps]
    idx = pl.program_id(0) * KV_HEADS + pl.program_id(1)
    rnd = lambda x: x.astype(BF16).astype(F32)
    q = q_ref[0, 0] * ATT_SCALE

    kn = rnd(kn_ref[0, 0])
    vn = rnd(vn_ref[0, 0])
    s_new = jnp.sum(q.astype(F32) * kn, axis=-1, keepdims=True)
    on = flag_ref[idx] == 1
    m = jnp.where(on, s_new, NEG_BIG)
    scs, masks = [], []
    for s in range(steps):
        kt = kt_refs[s][0, 0, 0].astype(BF16)
        tile = kt.shape[-1]
        sc = jnp.dot(q, kt, preferred_element_type=F32)
        lane = lax.broadcasted_iota(jnp.int32, sc.shape, 1)
        if mode == "sel":
            blk = blk_ref[idx * steps + s]
            per_page = page // SEL_BLOCK
            pos = (blk // per_page) * page + lane
            mask = (jnp.right_shift(pos, SEL_BLOCK.bit_length() - 1) == blk) & (pos <= past) & (blk >= 0)
        else:
            pos = past - tile + lane
            mask = (pos > past - window) & (pos <= past)
        sc = jnp.where(mask, sc, NEG_BIG)
        m = jnp.maximum(m, jnp.max(sc, axis=-1, keepdims=True))
        scs.append(sc)
        masks.append(mask)
    p_new = jnp.where(on, jnp.exp(s_new - m), 0.0)
    l = p_new
    acc = rnd(p_new) * vn
    for s in range(steps):
        p = jnp.where(masks[s], jnp.exp(scs[s] - m), 0.0)
        l = l + jnp.sum(p, axis=-1, keepdims=True)
        vt = vt_refs[s][0, 0, 0].astype(BF16)
        acc = acc + lax.dot_general(p.astype(BF16), vt, _NT, preferred_element_type=F32)
    o_ref[0, 0] = acc / jnp.maximum(l, 1e-30)


def decode_attention(q, k_new, v_new, kv_t, streams, phys, blk, flag, mode, past, steps, window=None, page=None):
    n_b, n_g, rows, hd = q.shape
    tile = kv_t.shape[-1]
    ks, vs = streams

    def kmap(st, s):
        if mode == "sel":
            return lambda b, g, ph, bl, fl: (ph[(b * n_g + g) * steps + s], st, g, 0, 0)
        return lambda b, g, ph, bl, fl: (b, st, g, 0, 0)

    tiles = ([pl.BlockSpec((1, 1, 1, hd, tile), kmap(ks, s)) for s in range(steps)]
             + [pl.BlockSpec((1, 1, 1, hd, tile), kmap(vs, s)) for s in range(steps)])
    grid_spec = pltpu.PrefetchScalarGridSpec(
        num_scalar_prefetch=3,
        grid=(n_b, n_g),
        in_specs=[pl.BlockSpec((1, 1, rows, hd), lambda b, g, ph, bl, fl: (b, g, 0, 0)),
                  pl.BlockSpec((1, 1, 1, hd), lambda b, g, ph, bl, fl: (b, g, 0, 0)),
                  pl.BlockSpec((1, 1, 1, hd), lambda b, g, ph, bl, fl: (b, g, 0, 0))] + tiles,
        out_specs=pl.BlockSpec((1, 1, rows, hd), lambda b, g, ph, bl, fl: (b, g, 0, 0)),
    )
    return pl.pallas_call(
        functools.partial(_decode_attn_kernel, mode=mode, past=past, steps=steps, window=window, page=page),
        grid_spec=grid_spec,
        out_shape=jax.ShapeDtypeStruct((n_b, n_g, rows, hd), F32),
        compiler_params=_cparams(("arbitrary", "arbitrary")),
        name="attn_decode_" + mode,
    )(phys, blk, flag, q, k_new, v_new, *([kv_t] * (2 * steps)))


def rmsnorm(x, g):
    y = x * lax.rsqrt(jnp.mean(x * x, axis=-1, keepdims=True) + RMS_EPS)
    return y * g


def rope(x, pos):
    half = ROPE_DIMS // 2
    inv = jnp.power(ROPE_THETA, -jnp.arange(half, dtype=F32) * (2.0 / ROPE_DIMS))
    ang = pos.astype(F32)[:, None] * inv[None, :]
    cos = jnp.cos(ang)[:, None, :]
    sin = jnp.sin(ang)[:, None, :]
    x1 = x[..., :half]
    x2 = x[..., half:ROPE_DIMS]
    return jnp.concatenate([x1 * cos - x2 * sin, x2 * cos + x1 * sin, x[..., ROPE_DIMS:]], axis=-1)


def compress(sub, pe, w1, b1, w2, n_b, nsub):
    half = CMP_STRIDE * HEAD_DIM
    w1cat = jnp.concatenate([w1[:half], w1[half:]], axis=1)
    hidden = w1.shape[1]
    part = mm(sub, w1cat).reshape(n_b, nsub, KV_HEADS, 2, hidden)
    pe_term = jnp.einsum('jd,jdc->c', pe, w1.reshape(CMP_BLOCK, HEAD_DIM, hidden)) + b1
    pre = part[:, :nsub - 1, :, 0] + part[:, 1:, :, 1] + pe_term
    act = jax.nn.gelu(pre)
    return mm(act.reshape(-1, hidden), w2).reshape(n_b, nsub - 1, KV_HEADS, HEAD_DIM)


def overlap_matrix(n_cmp, n_sblk, nc_pad, nb_pad):
    cs = np.arange(n_cmp) * CMP_STRIDE
    ce = cs + CMP_BLOCK - 1
    ss = np.arange(n_sblk) * SEL_BLOCK
    ov = ((cs[:, None] < ss[None, :] + SEL_BLOCK) & (ce[:, None] >= ss[None, :])).astype(np.float32)
    out = np.zeros((nc_pad, nb_pad), np.float32)
    out[:n_cmp, :n_sblk] = ov
    return out


def block_scores(imp, q_pos, n_sblk):
    j = jnp.arange(n_sblk)[None, :]
    qblk = (q_pos // SEL_BLOCK)[:, None]
    allowed = j <= qblk
    forced = (j == 0) | ((j > qblk - N_LOCAL_BLOCKS) & allowed)
    score = jnp.where(forced, FORCE_SCORE, imp)
    return jnp.where(allowed, score, -jnp.inf)


def nsa_prompt(qg, q_r, kv_t, n_b, t, cmp_pe, cmp_w1, cmp_b1, cmp_w2):
    nsub = t // CMP_STRIDE
    n_cmp = nsub - 1
    n_sblk = t // SEL_BLOCK

    def sub_blocks(s):
        x = kv_t[:, s].reshape(n_b, KV_HEADS, HEAD_DIM, nsub, CMP_STRIDE)
        return x.transpose(0, 3, 1, 4, 2).reshape(-1, CMP_STRIDE * HEAD_DIM)

    kc = compress(sub_blocks(0), cmp_pe[0], cmp_w1[0], cmp_b1[0], cmp_w2[0], n_b, nsub)
    vc = compress(sub_blocks(1), cmp_pe[1], cmp_w1[1], cmp_b1[1], cmp_w2[1], n_b, nsub)
    nc_pad = _round_up(n_cmp, LANES)
    kc = jnp.pad(kc, ((0, 0), (0, nc_pad - n_cmp), (0, 0), (0, 0)))
    vc = jnp.pad(vc, ((0, 0), (0, nc_pad - n_cmp), (0, 0), (0, 0)))
    ovt = jnp.asarray(overlap_matrix(n_cmp, n_sblk, nc_pad, n_sblk).T, BF16)
    o_cmp, sel = cmp_select(qg, kc.transpose(0, 2, 1, 3).astype(BF16), vc.transpose(0, 2, 3, 1).astype(BF16),
                            ovt, n_b, t, n_cmp, n_sblk, qg.shape[0])
    o_sel = flash_attention(q_r, kv_t, 2, 3, n_b, t, sel=sel)
    o_win = flash_attention(q_r, kv_t, 4, 5, n_b, t, window=WINDOW)
    return o_cmp, o_sel, o_win


def nsa_sample(qg, kv, cache_kv, cache_win, page_table, cmp_pe, cmp_w1, cmp_b1, cmp_w2):
    n_b, n_pages = page_table.shape
    n_phys, page = cache_kv.shape[:2]
    past = n_pages * page
    pos = jnp.full((1,), past, jnp.int32)
    q = qg[:, :N_HEADS * HEAD_DIM].reshape(n_b, 1, N_HEADS, HEAD_DIM)
    kv = kv.reshape(n_b, 1, N_KV_STREAMS, KV_HEADS, HEAD_DIM)
    k_sel = rope(kv[:, :, 2], pos)
    k_win = rope(kv[:, :, 4], pos)
    new_rows = jnp.stack([kv[:, :, 0], kv[:, :, 1], k_sel, kv[:, :, 3]], axis=2)
    new_win = jnp.stack([k_win, kv[:, :, 5]], axis=2)
    n_keep = min(WINDOW, cache_win.shape[1] + 1)
    win_state = jnp.concatenate([cache_win[:, cache_win.shape[1] + 1 - n_keep:], new_win], axis=1)

    cache_t = cache_kv.transpose(0, 2, 3, 4, 1)
    win_t = cache_win.transpose(0, 2, 3, 4, 1)

    l_tot = past + 1
    nsub = l_tot // CMP_STRIDE
    n_cmp = nsub - 1
    n_sblk = -(-l_tot // SEL_BLOCK)
    assert nsub * CMP_STRIDE == past and page % SEL_BLOCK == 0 and cache_win.shape[1] == WINDOW
    kvc = page_compress(cache_t, page_table, cmp_pe, cmp_w1, cmp_b1, cmp_w2)

    bg = n_b * KV_HEADS
    rq = 16
    qpad = lambda x: jnp.pad(x.reshape(n_b, KV_HEADS, GQA_R, HEAD_DIM), ((0, 0), (0, 0), (0, rq - GQA_R), (0, 0))).astype(BF16)

    nb_pad = _round_up(n_sblk, 8)
    ov = jnp.asarray(overlap_matrix(n_cmp, n_sblk, nsub, nb_pad), BF16)
    o_cmp, imp = cmp_attention_sample(qpad(q).reshape(bg, rq, HEAD_DIM), kvc[0].reshape(bg, nsub, HEAD_DIM),
                                      kvc[1].reshape(bg, nsub, HEAD_DIM), ov, n_cmp)
    imp = imp[:, :GQA_R, :n_sblk].sum(axis=1)
    score = block_scores(imp[:, None, :], pos, n_sblk)
    top_s, top_i = lax.top_k(score[:, 0], min(N_SEL, n_sblk))
    valid = jnp.isfinite(top_s)

    n_past_blk = past // SEL_BLOCK
    per_page = page // SEL_BLOCK
    n_k = top_i.shape[-1]
    in_past = valid & (top_i < n_past_blk)
    blk = jnp.where(in_past, top_i, -1).astype(jnp.int32)
    logical = jnp.clip(top_i, 0, n_past_blk - 1) // per_page
    phys = jnp.take_along_axis(page_table, logical.reshape(n_b, KV_HEADS * n_k), axis=1).reshape(bg, n_k)
    phys = jnp.where(in_past, phys, 0).astype(jnp.int32)
    new_flag = jnp.any(valid & (top_i >= n_past_blk), axis=-1).astype(jnp.int32)
    q_r = qpad(rope(q, pos))
    o_sel = decode_attention(q_r, new_rows[:, 0, 2][:, :, None, :], new_rows[:, 0, 3][:, :, None, :], cache_t, (2, 3),
                             phys.reshape(-1), blk.reshape(-1), new_flag, "sel", past, n_k, page=page)
    zeros = jnp.zeros((bg,), jnp.int32)
    o_win = decode_attention(q_r, new_win[:, 0, 0][:, :, None, :], new_win[:, 0, 1][:, :, None, :], win_t, (0, 1),
                             zeros, zeros, jnp.ones((bg,), jnp.int32), "win", past, 1, window=WINDOW)

    heads = lambda o: o.reshape(n_b, KV_HEADS, -1, HEAD_DIM)[:, :, :GQA_R].reshape(n_b, N_HEADS * HEAD_DIM)
    return (heads(o_cmp), heads(o_sel), heads(o_win)), new_rows, win_state


def rwkv_layer(h_p, h_t, n_p, t_p, n_s, state_wkv, state_shift, ln, mu, w_r, w_k, w_v, w0, w1, w2, a0, a1, a2, g1, g2,
               k_k, k_a, r_k, gn_w, gn_b):
    np_rows = n_p * t_p
    weights = (ln, mu, w_r, w_k, w_v, w0, w1, w2, a0, a1, a2, g1, g2, k_k, k_a)
    pre_p = rwkv_pre(h_p, jnp.zeros((RW_TM, D_MODEL), F32), np_rows, t_p, *weights)
    pre_t = rwkv_pre(h_t, jnp.pad(state_shift, ((0, RW_TM - n_s), (0, 0))), 0, RW_TM, *weights)

    def lanes(x, nb, t):
        return x.reshape(nb, t, RW_HEADS, RW_HEAD).transpose(1, 3, 0, 2).reshape(t, RW_HEAD, nb * RW_HEADS)

    def rows(o, nb, t):
        return o.reshape(t, RW_HEAD, nb, RW_HEADS).transpose(2, 0, 3, 1).reshape(nb * t, D_MODEL)

    head_vec = lambda w, nb: jnp.tile(w.reshape(RW_HEADS, RW_HEAD).T, (1, nb))
    s0_p = jnp.zeros((RW_HEAD, RW_HEAD, n_p * RW_HEADS), F32)
    o_p, st_p = rwkv_scan(*[lanes(x, n_p, t_p) for x in pre_p[:6]],
                          head_vec(r_k, n_p), head_vec(gn_w, n_p), head_vec(gn_b, n_p), s0_p)
    s0_s = state_wkv.transpose(3, 2, 0, 1).reshape(RW_HEAD, RW_HEAD, n_s * RW_HEADS)
    o_s, st_s = rwkv_scan(*[lanes(x[:n_s], n_s, 1) for x in pre_t[:6]],
                          head_vec(r_k, n_s), head_vec(gn_w, n_s), head_vec(gn_b, n_s), s0_s, dot_dtype=BF16)
    o_t = jnp.pad(rows(o_s, n_s, 1), ((0, RW_TM - n_s), (0, 0)))
    wkv_p = st_p.reshape(RW_HEAD, RW_HEAD, n_p, RW_HEADS).transpose(2, 3, 1, 0)
    wkv_s = st_s.reshape(RW_HEAD, RW_HEAD, n_s, RW_HEADS).transpose(2, 3, 1, 0)
    return (rows(o_p, n_p, t_p), pre_p[6], pre_p[7]), (o_t, pre_t[6], pre_t[7]), wkv_p, wkv_s


def kernel(x_prompt, x_sample, state_wkv, state_shift, cache_kv, cache_win, page_table, ln_mix, ln_ffn, ln_kv, ln_out, rw_mu, rw_wr, rw_wk, rw_wv, rw_wo, rw_w0, rw_w1, rw_w2, rw_a0, rw_a1, rw_a2, rw_g1, rw_g2, rw_kk, rw_ka, rw_rk, rw_lnw, rw_lnb, w_kv, cmp_pe, cmp_w1, cmp_b1, cmp_w2, nsa_wqg, nsa_wo, moe_wrg, moe_brg, moe_wre, moe_bre, moe_wgate, moe_wup, moe_wdown):
    n_p, t_p, dm = x_prompt.shape
    n_s = x_sample.shape[0]
    assert x_sample.shape[1] == 1
    np_rows = n_p * t_p
    h_p = x_prompt.reshape(np_rows, dm)
    h_t = jnp.pad(x_sample.reshape(n_s, dm), ((0, RW_TM - n_s), (0, 0)))
    sizes = (np_rows, RW_TM)

    def moe(fronts, layer):
        ys = hmoe(jnp.concatenate([f[1] for f in fronts]), jnp.concatenate([f[2] for f in fronts]), sizes,
                  moe_brg[layer], moe_bre[layer], moe_wgate[layer], moe_wup[layer], moe_wdown[layer])
        return [f[0] + y for f, y in zip(fronts, ys)]

    mix_p, mix_t, wkv_p, wkv_s = rwkv_layer(h_p, h_t, n_p, t_p, n_s, state_wkv[0], state_shift[0], ln_mix[0], rw_mu[0],
                                            rw_wr[0], rw_wk[0], rw_wv[0], rw_w0[0], rw_w1[0], rw_w2[0], rw_a0[0], rw_a1[0],
                                            rw_a2[0], rw_g1[0], rw_g2[0], rw_kk[0], rw_ka[0], rw_rk[0], rw_lnw[0], rw_lnb[0])
    p_shift = mix_p[2].reshape(n_p, t_p, dm)[:, -1][None]
    s_shift = mix_t[2][:n_s][None]
    w_router = router_weight(moe_wrg[0], moe_wre[0])
    h_p, h_t = moe([mix_out([o], g, hh, rw_wo[0], ln_ffn[0], w_router)
                    for (o, g, _), hh in ((mix_p, h_p), (mix_t, h_t))], 0)

    n_q = N_HEADS * HEAD_DIM
    gate_cols = _round_up(nsa_wqg.shape[-1] - n_q, LANES)
    w_qg = jnp.pad(nsa_wqg[0], ((0, 0), (0, n_q + gate_cols - nsa_wqg.shape[-1])))
    qg_p, qr_p, kvt = attn_pre(h_p, n_p, t_p, ln_mix[1], ln_kv, w_qg, w_kv)
    kvt = kvt.reshape(n_p, N_KV_STREAMS, KV_HEADS, HEAD_DIM, t_p)
    o_p = nsa_prompt(qg_p, qr_p, kvt, n_p, t_p, cmp_pe, cmp_w1, cmp_b1, cmp_w2)
    p_rows = kvt[:, :4].transpose(0, 4, 1, 2, 3)
    p_win = kvt[:, 4:, :, :, t_p - min(WINDOW, t_p):].transpose(0, 4, 1, 2, 3)
    qg_t = mm(rmsnorm(h_t, ln_mix[1]), w_qg)
    kv_t = mm(rmsnorm(h_t, ln_kv), w_kv)
    o_s, s_rows, s_win = nsa_sample(qg_t[:n_s, :nsa_wqg.shape[-1]], kv_t[:n_s], cache_kv, cache_win,
                                    page_table, cmp_pe, cmp_w1, cmp_b1, cmp_w2)
    o_t = [jnp.pad(x, ((0, RW_TM - n_s), (0, 0))) for x in o_s]
    expand = [jnp.asarray((np.arange(gate_cols)[:, None] == 3 * (np.arange(n_q)[None, :] // HEAD_DIM) + br)
                          .astype(np.float32), BF16) for br in range(3)]
    w_router = router_weight(moe_wrg[1], moe_wre[1])
    h_p, h_t = moe([mix_out(list(br), qg, hh, nsa_wo[0], ln_ffn[1], w_router, gate_expand=expand, gate_col=n_q // LANES)
                    for br, qg, hh in ((o_p, qg_p, h_p), (o_t, qg_t, h_t))], 1)

    y_prompt = rmsnorm(h_p, ln_out).reshape(n_p, t_p, dm)
    y_sample = rmsnorm(h_t[:n_s], ln_out).reshape(n_s, 1, dm)
    return (y_prompt, y_sample, wkv_p[None], p_shift, p_rows, p_win, wkv_s[None], s_shift, s_rows, s_win)
```

```python
import functools

import numpy as np
import jax
import jax.numpy as jnp
from jax import lax
from jax.experimental import pallas as pl
from jax.experimental.pallas import tpu as pltpu

F32 = jnp.float32
BF16 = jnp.bfloat16

D_MODEL = 1024
RMS_EPS = 1e-6
RW_HEAD = 64
RW_HEADS = D_MODEL // RW_HEAD
RW_GN_EPS = 64e-5
N_HEADS = 16
HEAD_DIM = 64
KV_HEADS = 4
GQA_R = N_HEADS // KV_HEADS
ROPE_DIMS = HEAD_DIM // 4
ROPE_THETA = 500000.0
N_KV_STREAMS = 6
CMP_BLOCK = 32
CMP_STRIDE = 16
SEL_BLOCK = 64
N_SEL = 16
N_LOCAL_BLOCKS = 2
FORCE_SCORE = 1.0e4
WINDOW = 512
N_GROUPS = 4
EXPERTS_PER_GROUP = 8
N_EXPERTS = N_GROUPS * EXPERTS_PER_GROUP
D_EXPERT = 256

LANES = 128
VMEM_LIMIT = 48 << 20
NEG_BIG = -1e30
ATT_SCALE = HEAD_DIM ** -0.5
TOKEN_TILE = 512


def _round_up(x, m):
    return (x + m - 1) // m * m


def _cparams(sem):
    return pltpu.CompilerParams(dimension_semantics=sem, vmem_limit_bytes=VMEM_LIMIT)


_NT = (((1,), (1,)), ((), ()))


def _mm_kernel(x_ref, w_ref, o_ref):
    o_ref[...] = jnp.dot(x_ref[...].astype(BF16), w_ref[...], preferred_element_type=F32)


def mm(x, w, tm=TOKEN_TILE):
    m, k = x.shape
    n = w.shape[1]
    tm = min(tm, _round_up(m, 8))
    mp = _round_up(m, tm)
    if mp != m:
        x = jnp.pad(x, ((0, mp - m), (0, 0)))
    out = pl.pallas_call(
        _mm_kernel,
        grid=(mp // tm,),
        in_specs=[pl.BlockSpec((tm, k), lambda i: (i, 0)),
                  pl.BlockSpec((k, n), lambda i: (0, 0))],
        out_specs=pl.BlockSpec((tm, n), lambda i: (i, 0)),
        out_shape=jax.ShapeDtypeStruct((mp, n), F32),
        compiler_params=_cparams(("arbitrary",)),
        name="mm",
    )(x, w.astype(BF16))
    return out[:m] if mp != m else out


SCAN_TC = 32


def _scan_kernel(r_ref, d_ref, k_ref, b_ref, n_ref, v_ref, rk_ref, gw_ref, gb_ref, s0_ref, o_ref, st_ref, s_scr,
                 *, tc, dot_dtype):
    c = pl.program_id(1)
    rnd = lambda x: x.astype(dot_dtype).astype(F32)
    halves = [(h0, h0 + RW_HEAD // 2) for h0 in (0, RW_HEAD // 2)]

    @pl.when(c == 0)
    def _():
        s_scr[...] = s0_ref[...]

    def step(t, carry):
        vv = v_ref[t]
        outs = []
        for lo, hi in halves:
            parts = [jnp.zeros((hi - lo, LANES), F32)] * 2
            for j in range(RW_HEAD):
                parts[j % 2] = parts[j % 2] + rnd(s_scr[j, lo:hi]) * rnd(n_ref[t, pl.ds(j, 1), :])
            sa = parts[0] + parts[1]
            vh = vv[lo:hi]
            out = jnp.zeros((hi - lo, LANES), F32)
            for j in range(RW_HEAD):
                sn = (s_scr[j, lo:hi] * d_ref[t, pl.ds(j, 1), :] + sa * b_ref[t, pl.ds(j, 1), :]
                      + vh * k_ref[t, pl.ds(j, 1), :])
                s_scr[j, lo:hi] = sn
                out = out + rnd(sn) * rnd(r_ref[t, pl.ds(j, 1), :])
            outs.append(out)
        out = jnp.concatenate(outs, axis=0)
        mean = jnp.mean(out, axis=0, keepdims=True)
        cen = out - mean
        var = jnp.mean(cen * cen, axis=0, keepdims=True)
        bonus = jnp.sum(r_ref[t] * k_ref[t] * rk_ref[...], axis=0, keepdims=True) * vv
        o_ref[t] = cen * lax.rsqrt(var + RW_GN_EPS) * gw_ref[...] + gb_ref[...] + bonus
        return carry

    lax.fori_loop(0, tc, step, 0)

    @pl.when(c == pl.num_programs(1) - 1)
    def _():
        st_ref[...] = s_scr[...]


def rwkv_scan(r, d, k, b, nkk, v, rk, gw, gb, s0, dot_dtype=F32):
    t, _, l = r.shape
    tc = min(SCAN_TC, t)
    assert t % tc == 0 and l % LANES == 0
    seq = pl.BlockSpec((tc, RW_HEAD, LANES), lambda g, c: (c, 0, g))
    vec = pl.BlockSpec((RW_HEAD, LANES), lambda g, c: (0, g))
    st = pl.BlockSpec((RW_HEAD, RW_HEAD, LANES), lambda g, c: (0, 0, g))
    return pl.pallas_call(
        functools.partial(_scan_kernel, tc=tc, dot_dtype=dot_dtype),
        grid=(l // LANES, t // tc),
        in_specs=[seq] * 6 + [vec] * 3 + [st],
        out_specs=[seq, st],
        out_shape=[jax.ShapeDtypeStruct((t, RW_HEAD, l), F32),
                   jax.ShapeDtypeStruct((RW_HEAD, RW_HEAD, l), F32)],
        scratch_shapes=[pltpu.VMEM((RW_HEAD, RW_HEAD, LANES), F32)],
        compiler_params=_cparams(("arbitrary", "arbitrary")),
        name="rwkv_scan",
    )(r, d, k, b, nkk, v, rk, gw, gb, s0)


RW_TM = 256


def _split_dot(x, w):
    hi = x.astype(BF16)
    lo = (x - hi.astype(F32)).astype(BF16)
    return jnp.dot(hi, w, preferred_element_type=F32) + jnp.dot(lo, w, preferred_element_type=F32)


def _rwkv_pre_kernel(x_ref, ps_ref, ln_ref, mu_ref, vec_ref, wr_ref, wk_ref, wv_ref, w1_ref, w2_ref, a1_ref, a2_ref,
                     g1_ref, g2_ref, seg_ref, segt_ref,
                     r_o, d_o, k_o, b_o, n_o, v_o, g_o, xn_o, carry_scr, *, tiles_per_seq, n_seq_tiles):
    i = pl.program_id(0)
    tm = x_ref.shape[0]

    @pl.when(i == 0)
    def _():
        carry_scr[...] = jnp.zeros_like(carry_scr)

    x = x_ref[...]
    xn = x * lax.rsqrt(jnp.mean(x * x, axis=-1, keepdims=True) + RMS_EPS) * ln_ref[...]
    xn_o[...] = xn
    first = jnp.where(i % tiles_per_seq == 0, jnp.zeros_like(carry_scr[...]), carry_scr[...])
    row0 = lax.broadcasted_iota(jnp.int32, xn.shape, 0) == 0
    prev = jnp.where(row0, first, pltpu.roll(xn, 1, 0))
    prev = jnp.where(i >= n_seq_tiles, ps_ref[...], prev)
    carry_scr[...] = xn[tm - 1:tm, :]
    xx = prev - xn
    mix = lambda c: (xn + xx * mu_ref[c:c + 1, :]).astype(BF16)
    dot = lambda a, w_ref: jnp.dot(a, w_ref[...], preferred_element_type=F32)
    w0, a0, k_k, k_a = (vec_ref[c:c + 1, :] for c in range(4))
    r = dot(mix(0), wr_ref)
    k = dot(mix(2), wk_ref)
    v = dot(mix(3), wv_ref)
    w_log = -jax.nn.softplus(-(w0 + dot(jnp.tanh(dot(mix(1), w1_ref)).astype(BF16), w2_ref))) - 0.5
    a = jax.nn.sigmoid(a0 + dot(dot(mix(4), a1_ref).astype(BF16), a2_ref))
    g_o[...] = dot(jax.nn.sigmoid(dot(mix(5), g1_ref)).astype(BF16), g2_ref)
    kk = k * k_k
    ss = _split_dot(_split_dot(kk * kk, seg_ref[...]), segt_ref[...])
    kk = kk / jnp.maximum(jnp.sqrt(ss), 1e-12)
    r_o[...] = r
    d_o[...] = jnp.exp(-jnp.exp(w_log))
    k_o[...] = k * (1.0 + (a - 1.0) * k_a)
    b_o[...] = kk * a
    n_o[...] = -kk
    v_o[...] = v


def rwkv_pre(h, prev_single, n_seq_rows, t, ln, mu, w_r, w_k, w_v, w0, w1, w2, a0, a1, a2, g1, g2, k_k, k_a):
    n, dm = h.shape
    tm = RW_TM
    assert n % tm == 0 and t % tm == 0 and n_seq_rows % tm == 0 and prev_single.shape == (tm, dm)
    seg = (np.arange(dm)[:, None] // RW_HEAD == np.arange(LANES)[None, :]).astype(np.float32)
    row = pl.BlockSpec((tm, dm), lambda i: (i, 0))
    full = lambda a: pl.BlockSpec(a.shape, lambda i: (0,) * a.ndim)
    bf = lambda w: w.astype(BF16)
    consts = [prev_single, ln[None, :], mu, jnp.stack([w0, a0, k_k, k_a]), bf(w_r), bf(w_k), bf(w_v), bf(w1), bf(w2),
              bf(a1), bf(a2), bf(g1), bf(g2), jnp.asarray(seg, BF16), jnp.asarray(seg.T, BF16)]
    return pl.pallas_call(
        functools.partial(_rwkv_pre_kernel, tiles_per_seq=t // tm, n_seq_tiles=n_seq_rows // tm),
        grid=(n // tm,),
        in_specs=[row] + [full(a) for a in consts],
        out_specs=[row] * 8,
        out_shape=[jax.ShapeDtypeStruct((n, dm), F32)] * 8,
        scratch_shapes=[pltpu.VMEM((1, dm), F32)],
        compiler_params=_cparams(("arbitrary",)),
        name="rwkv_pre",
    )(h, *consts)


def _mix_out_kernel(*refs, n_branch, use_gates):
    o_refs = refs[:n_branch]
    gate_ref, h_ref, wo_ref, ln_ref, wr_ref = refs[n_branch:n_branch + 5]
    e_refs = refs[n_branch + 5:n_branch + 5 + (n_branch if use_gates else 0)]
    h_o, xp_o, lg_o = refs[-3:]
    if use_gates:
        sig = jax.nn.sigmoid(gate_ref[...])
        mixed = sum(_split_dot(sig, e_refs[b][...]) * o_refs[b][...] for b in range(n_branch))
    else:
        mixed = o_refs[0][...] * gate_ref[...]
    h = h_ref[...] + jnp.dot(mixed.astype(BF16), wo_ref[...], preferred_element_type=F32)
    h_o[...] = h
    xn = h * lax.rsqrt(jnp.mean(h * h, axis=-1, keepdims=True) + RMS_EPS) * ln_ref[...]
    xb = xn.astype(BF16)
    lg_o[...] = jnp.dot(xb, wr_ref[...], preferred_element_type=F32)
    u = lax.bitcast_convert_type(xb.astype(F32), jnp.uint32)
    half = u.shape[1] // 2
    xp_o[...] = (u[:, :half] >> 16) | (u[:, half:] & jnp.uint32(0xFFFF0000))


def mix_out(branches, gate, h, w_o, ln, w_router, gate_expand=None, gate_col=0):
    n, dm = h.shape
    tm = RW_TM
    nb = len(branches)
    row = lambda a: pl.BlockSpec((tm, a.shape[1]), lambda i: (i, 0))
    full = lambda a: pl.BlockSpec(a.shape, lambda i: (0,) * a.ndim)
    consts = [w_o.astype(BF16), ln[None, :], w_router.astype(BF16)] + list(gate_expand or [])
    gate_spec = row(gate) if gate_expand is None else pl.BlockSpec((tm, LANES), lambda i: (i, gate_col))
    return pl.pallas_call(
        functools.partial(_mix_out_kernel, n_branch=nb, use_gates=gate_expand is not None),
        grid=(n // tm,),
        in_specs=[row(a) for a in branches] + [gate_spec, row(h)] + [full(a) for a in consts],
        out_specs=[pl.BlockSpec((tm, dm), lambda i: (i, 0)), pl.BlockSpec((tm, dm // 2), lambda i: (i, 0)),
                   pl.BlockSpec((tm, LANES), lambda i: (i, 0))],
        out_shape=[jax.ShapeDtypeStruct((n, dm), F32), jax.ShapeDtypeStruct((n, dm // 2), jnp.uint32),
                   jax.ShapeDtypeStruct((n, LANES), F32)],
        compiler_params=_cparams(("arbitrary",)),
        name="mix_out",
    )(*branches, gate, h, *consts)


MOE_TM = 256
MOE_VMEM_LIMIT = 56 << 20


def _moe_kernel(te_ref, nt_ref, tok_ref, x_ref, c_ref, wgu_ref, wd_ref, o_ref, xt_scr):
    i = pl.program_id(0)

    @pl.when(i < nt_ref[0])
    def _():
        base = i * MOE_TM

        def gather_row(r, carry):
            xt_scr[pl.ds(r, 1), :] = x_ref[pl.ds(tok_ref[base + r], 1), :]
            return carry

        lax.fori_loop(0, MOE_TM, gather_row, 0, unroll=8)
        u = xt_scr[...]
        lo = lax.bitcast_convert_type(u << 16, F32).astype(BF16)
        hi = lax.bitcast_convert_type(u & jnp.uint32(0xFFFF0000), F32).astype(BF16)
        x = jnp.concatenate([lo, hi], axis=1)
        h = jnp.dot(x, wgu_ref[0], preferred_element_type=F32)
        g = h[:, :D_EXPERT]
        u2 = h[:, D_EXPERT:]
        hid = g * jax.nn.sigmoid(g) * u2 * c_ref[...]
        o_ref[...] = jnp.dot(hid.astype(BF16), wd_ref[0], preferred_element_type=F32)

    @pl.when(i >= nt_ref[0])
    def _():
        o_ref[...] = jnp.zeros_like(o_ref)


def moe_gmm(xp, row_tok, cs, tile_expert, n_tiles_used, wgu, wd):
    n, half = xp.shape
    dm = 2 * half
    p = row_tok.shape[0]
    nt = p // MOE_TM

    def row_map(i, te, ntu, tok):
        return (jnp.minimum(i, ntu[0] - 1), 0)

    grid_spec = pltpu.PrefetchScalarGridSpec(
        num_scalar_prefetch=3,
        grid=(nt,),
        in_specs=[pl.BlockSpec((n, half), lambda i, te, ntu, tok: (0, 0), pipeline_mode=pl.Buffered(1)),
                  pl.BlockSpec((MOE_TM, 1), row_map),
                  pl.BlockSpec((1, dm, 2 * D_EXPERT), lambda i, te, ntu, tok: (te[i], 0, 0)),
                  pl.BlockSpec((1, D_EXPERT, dm), lambda i, te, ntu, tok: (te[i], 0, 0))],
        out_specs=pl.BlockSpec((MOE_TM, dm), lambda i, te, ntu, tok: (i, 0)),
        scratch_shapes=[pltpu.VMEM((MOE_TM, half), jnp.uint32)],
    )
    return pl.pallas_call(
        _moe_kernel,
        grid_spec=grid_spec,
        out_shape=jax.ShapeDtypeStruct((p, dm), F32),
        compiler_params=pltpu.CompilerParams(dimension_semantics=("arbitrary",), vmem_limit_bytes=MOE_VMEM_LIMIT),
        name="moe_gmm",
    )(tile_expert, n_tiles_used, row_tok, xp, cs, wgu, wd)


def router_weight(w_rg, w_re):
    w_r = jnp.concatenate([w_rg, w_re], axis=1)
    return jnp.pad(w_r, ((0, 0), (0, LANES - w_r.shape[1])))


def hmoe(xp, logits, set_sizes, b_rg, b_re, w_gate, w_up, w_down):
    n = xp.shape[0]
    dm = 2 * xp.shape[1]
    lg = logits[:, :N_GROUPS] + b_rg
    le = (logits[:, N_GROUPS:N_GROUPS + N_EXPERTS] + b_re).reshape(n, N_GROUPS, EXPERTS_PER_GROUP)
    g_sel = jnp.argmax(lg, axis=-1)
    p_g = jnp.max(jax.nn.softmax(lg, axis=-1), axis=-1)
    le_g = jnp.take_along_axis(le, g_sel[:, None, None], axis=1)[:, 0]
    top_p, top_i = lax.top_k(jax.nn.softmax(le_g, axis=-1), 2)
    w_top = p_g[:, None] * top_p / jnp.sum(top_p, axis=-1, keepdims=True)
    eid = (g_sel[:, None] * EXPERTS_PER_GROUP + top_i).astype(jnp.int32).reshape(-1)
    cw = w_top.reshape(-1)

    n2 = 2 * n
    onehot = (eid[:, None] == jnp.arange(N_EXPERTS, dtype=jnp.int32)[None, :]).astype(jnp.int32)
    counts = jnp.sum(onehot, axis=0)
    rank = jnp.take_along_axis(jnp.cumsum(onehot, axis=0) - onehot, eid[:, None], axis=1)[:, 0]
    tiles_per = (counts + MOE_TM - 1) // MOE_TM
    tile_end = jnp.cumsum(tiles_per)
    tile_start = tile_end - tiles_per
    pos = tile_start[eid] * MOE_TM + rank
    p_rows = _round_up(n2 + N_EXPERTS * (MOE_TM - 1), MOE_TM)
    nt = p_rows // MOE_TM
    n_used = tile_end[-1].astype(jnp.int32)
    tile_ids = jnp.arange(nt, dtype=jnp.int32)
    tile_expert = jnp.sum((tile_end[None, :] <= jnp.minimum(tile_ids, n_used - 1)[:, None]).astype(jnp.int32), axis=1)

    tok = jnp.arange(n2, dtype=jnp.int32) // 2
    row_tok = jnp.zeros((p_rows,), jnp.int32).at[pos].set(tok)
    row_c = jnp.zeros((p_rows,), F32).at[pos].set(cw)
    wgu = jnp.concatenate([w_gate, w_up], axis=-1).reshape(N_EXPERTS, dm, 2 * D_EXPERT).astype(BF16)
    wd = w_down.reshape(N_EXPERTS, D_EXPERT, dm).astype(BF16)
    ys = moe_gmm(xp, row_tok, row_c[:, None], tile_expert, n_used.reshape(1), wgu, wd)
    pos2 = pos.reshape(n, 2)
    bounds = np.cumsum([0] + list(set_sizes))
    return [ys[pos2[a:b, 0]] + ys[pos2[a:b, 1]] for a, b in zip(bounds[:-1], bounds[1:])]


ATT_TQ = 256
ATT_TK = 512


def _cmp_sel_kernel(q_ref, kc_ref, vct_ref, ovt_ref, o_ref, sel_ref, *, n_cmp, n_sblk, tq):
    t0 = pl.program_id(2) * tq
    ncp = kc_ref.shape[2]
    n_io = lax.broadcasted_iota(jnp.int32, (ncp, tq), 0)
    t_io = lax.broadcasted_iota(jnp.int32, (ncp, tq), 1) + t0
    mask = (n_io * CMP_STRIDE + (CMP_BLOCK - 1) <= t_io) & (n_io < n_cmp)
    kc = kc_ref[0, 0]
    vct = vct_ref[0, 0]
    ovt = ovt_ref[...]
    imp = jnp.zeros((n_sblk, tq), F32)
    outs = []
    for r in range(GQA_R):
        qr = (q_ref[:, r * HEAD_DIM:(r + 1) * HEAD_DIM] * ATT_SCALE).astype(BF16)
        st = lax.dot_general(kc, qr, _NT, preferred_element_type=F32)
        st = jnp.where(mask, st, NEG_BIG)
        m = jnp.max(st, axis=0, keepdims=True)
        e = jnp.where(mask, jnp.exp(st - m), 0.0)
        p = (e / jnp.maximum(jnp.sum(e, axis=0, keepdims=True), 1e-30)).astype(BF16)
        outs.append(jnp.dot(vct, p, preferred_element_type=F32).T)
        imp = imp + jnp.dot(ovt, p, preferred_element_type=F32)
    o_ref[...] = jnp.concatenate(outs, axis=1)

    j_io = lax.broadcasted_iota(jnp.int32, (n_sblk, tq), 0)
    qblk = jnp.right_shift(lax.broadcasted_iota(jnp.int32, (n_sblk, tq), 1) + t0, SEL_BLOCK.bit_length() - 1)
    allowed = j_io <= qblk
    forced = (j_io == 0) | ((j_io > qblk - N_LOCAL_BLOCKS) & allowed)
    score = jnp.where(forced, FORCE_SCORE, imp)
    score = jnp.where(allowed, score, -jnp.inf)
    rank = jnp.zeros((n_sblk, tq), jnp.int32)
    for jp in range(n_sblk):
        row = score[jp:jp + 1, :]
        beats = (row > score) | ((row == score) & (j_io > jp))
        rank = rank + jnp.where(beats, 1, 0)
    sel = (rank < N_SEL) & allowed
    sel_ref[0, 0] = jnp.where(sel, 1.0, 0.0).astype(BF16)


def cmp_select(q, kc, vct, ovt, n_b, t, n_cmp, n_sblk, n_out):
    assert SEL_BLOCK & (SEL_BLOCK - 1) == 0
    tq = ATT_TQ
    nq = t // tq
    ncp = kc.shape[2]
    gw = GQA_R * HEAD_DIM
    return pl.pallas_call(
        functools.partial(_cmp_sel_kernel, n_cmp=n_cmp, n_sblk=n_sblk, tq=tq),
        grid=(n_b, KV_HEADS, nq),
        in_specs=[pl.BlockSpec((tq, gw), lambda b, g, qi: (b * nq + qi, g)),
                  pl.BlockSpec((1, 1, ncp, HEAD_DIM), lambda b, g, qi: (b, g, 0, 0)),
                  pl.BlockSpec((1, 1, HEAD_DIM, ncp), lambda b, g, qi: (b, g, 0, 0)),
                  pl.BlockSpec((n_sblk, ncp), lambda b, g, qi: (0, 0))],
        out_specs=[pl.BlockSpec((tq, gw), lambda b, g, qi: (b * nq + qi, g)),
                   pl.BlockSpec((1, 1, n_sblk, tq), lambda b, g, qi: (b, g, 0, qi))],
        out_shape=[jax.ShapeDtypeStruct((n_out, N_HEADS * HEAD_DIM), F32),
                   jax.ShapeDtypeStruct((n_b, KV_HEADS, n_sblk, t), BF16)],
        compiler_params=_cparams(("arbitrary", "arbitrary", "arbitrary")),
        name="attn_cmp_select",
    )(q, kc, vct, ovt)


def _flash_kernel(pq_ref, pk_ref, pfirst_ref, plast_ref, q_ref, kt_ref, vt_ref, *rest, window, use_sel, tq, tk):
    if use_sel:
        sel_ref, et_ref, o_ref, m_scr, l_scr, acc_scr = rest
    else:
        o_ref, m_scr, l_scr, acc_scr = rest
    p_id = pl.program_id(2)

    @pl.when(pfirst_ref[p_id] == 1)
    def _():
        m_scr[...] = jnp.full_like(m_scr, NEG_BIG)
        l_scr[...] = jnp.zeros_like(l_scr)
        acc_scr[...] = jnp.zeros_like(acc_scr)

    off = pq_ref[p_id] * tq - pk_ref[p_id] * tk
    diff = lax.broadcasted_iota(jnp.int32, (tk, tq), 0) - lax.broadcasted_iota(jnp.int32, (tk, tq), 1)
    mask = diff <= off
    if window is not None:
        mask = mask & (diff > off - window)
    if use_sel:
        selm = jnp.dot(et_ref[...], sel_ref[0, 0], preferred_element_type=F32)
        mask = mask & (selm > 0.5)
    k = kt_ref[0, 0, 0].astype(BF16).T
    vt = vt_ref[0, 0, 0].astype(BF16)
    heads = range(GQA_R)
    qs = [(q_ref[:, r * HEAD_DIM:(r + 1) * HEAD_DIM] * ATT_SCALE).astype(BF16) for r in heads]
    sts = [jnp.where(mask, lax.dot_general(k, qs[r], _NT, preferred_element_type=F32), NEG_BIG) for r in heads]
    m_prev = [m_scr[r] for r in heads]
    m_new = [jnp.maximum(m_prev[r], jnp.max(sts[r], axis=0, keepdims=True)) for r in heads]
    ps = [jnp.where(mask, jnp.exp(sts[r] - m_new[r]), 0.0) for r in heads]
    alphas = [jnp.exp(m_prev[r] - m_new[r]) for r in heads]
    for r in heads:
        l_scr[r] = alphas[r] * l_scr[r] + jnp.sum(ps[r], axis=0, keepdims=True)
        acc_scr[r] = alphas[r] * acc_scr[r] + jnp.dot(vt, ps[r].astype(BF16), preferred_element_type=F32)
        m_scr[r] = m_new[r]

    @pl.when(plast_ref[p_id] == 1)
    def _():
        outs = [(acc_scr[r] / jnp.maximum(l_scr[r], 1e-30)).T for r in range(GQA_R)]
        o_ref[...] = jnp.concatenate(outs, axis=1)


def flash_attention(q, kv_t, k_stream, v_stream, n_b, t, window=None, sel=None):
    tq, tk = ATT_TQ, ATT_TK
    nq = t // tq
    first_tile = lambda qi: 0 if window is None else max(0, qi * tq - (window - 1)) // tk
    pairs = [(qi, kt) for qi in range(nq) for kt in range(first_tile(qi), (qi * tq + tq - 1) // tk + 1)]
    pq = np.array([p[0] for p in pairs], np.int32)
    pk = np.array([p[1] for p in pairs], np.int32)
    pfirst = np.array([1 if i == 0 or pairs[i - 1][0] != p[0] else 0 for i, p in enumerate(pairs)], np.int32)
    plast = np.array([1 if i == len(pairs) - 1 or pairs[i + 1][0] != p[0] else 0 for i, p in enumerate(pairs)], np.int32)
    gw = GQA_R * HEAD_DIM

    def tile_map(stream):
        return lambda b, g, p, pq_r, pk_r, f_r, l_r: (b, stream, g, 0, pk_r[p])

    in_specs = [pl.BlockSpec((tq, gw), lambda b, g, p, pq_r, pk_r, f_r, l_r: (b * nq + pq_r[p], g)),
                pl.BlockSpec((1, 1, 1, HEAD_DIM, tk), tile_map(k_stream)),
                pl.BlockSpec((1, 1, 1, HEAD_DIM, tk), tile_map(v_stream))]
    args = [q, kv_t, kv_t]
    if sel is not None:
        n_sblk = sel.shape[2]
        emat = (np.arange(t)[:, None] // SEL_BLOCK == np.arange(n_sblk)[None, :]).astype(np.float32)
        in_specs += [pl.BlockSpec((1, 1, n_sblk, tq), lambda b, g, p, pq_r, pk_r, f_r, l_r: (b, g, 0, pq_r[p])),
                     pl.BlockSpec((tk, n_sblk), lambda b, g, p, pq_r, pk_r, f_r, l_r: (pk_r[p], 0))]
        args += [sel, jnp.asarray(emat, BF16)]
    grid_spec = pltpu.PrefetchScalarGridSpec(
        num_scalar_prefetch=4,
        grid=(n_b, KV_HEADS, len(pairs)),
        in_specs=in_specs,
        out_specs=pl.BlockSpec((tq, gw), lambda b, g, p, pq_r, pk_r, f_r, l_r: (b * nq + pq_r[p], g)),
        scratch_shapes=[pltpu.VMEM((GQA_R, 1, tq), F32), pltpu.VMEM((GQA_R, 1, tq), F32),
                        pltpu.VMEM((GQA_R, HEAD_DIM, tq), F32)],
    )
    return pl.pallas_call(
        functools.partial(_flash_kernel, window=window, use_sel=sel is not None, tq=tq, tk=tk),
        grid_spec=grid_spec,
        out_shape=jax.ShapeDtypeStruct((n_b * t, N_HEADS * HEAD_DIM), F32),
        compiler_params=_cparams(("arbitrary", "arbitrary", "arbitrary")),
        name="attn_sel" if sel is not None else "attn_win",
    )(jnp.asarray(pq), jnp.asarray(pk), jnp.asarray(pfirst), jnp.asarray(plast), *args)


def _attn_pre_kernel(h_ref, ln_ref, wq_ref, wkv_ref, rc_ref, rm_ref, rp_ref, qg_o, qr_o, kvt_o):
    x = h_ref[...]
    xs = x * lax.rsqrt(jnp.mean(x * x, axis=-1, keepdims=True) + RMS_EPS)
    qg = jnp.dot((xs * ln_ref[0:1, :]).astype(BF16), wq_ref[...], preferred_element_type=F32)
    kv = jnp.dot((xs * ln_ref[1:2, :]).astype(BF16), wkv_ref[...], preferred_element_type=F32)
    qg_o[...] = qg

    def rope(v):
        reps = v.shape[1] // LANES
        tile = lambda ref: jnp.concatenate([ref[...]] * reps, axis=1)
        n = v.shape[1]
        return v * tile(rc_ref) + pltpu.roll(v, n - ROPE_DIMS // 2, 1) * tile(rm_ref) + pltpu.roll(v, ROPE_DIMS // 2, 1) * tile(rp_ref)

    n_q = N_HEADS * HEAD_DIM
    sw = KV_HEADS * HEAD_DIM
    qr_o[...] = rope(qg[:, :n_q])
    kv_r = jnp.concatenate([kv[:, :2 * sw], rope(kv[:, 2 * sw:3 * sw]), kv[:, 3 * sw:4 * sw],
                            rope(kv[:, 4 * sw:5 * sw]), kv[:, 5 * sw:]], axis=1)
    kvt_o[0] = kv_r.T


def attn_pre(h, n_b, t, ln_q, ln_kv, w_qg, w_kv):
    n, dm = h.shape
    tm = RW_TM
    tiles = t // tm
    half = ROPE_DIMS // 2
    inv = jnp.power(ROPE_THETA, -jnp.arange(half, dtype=F32) * (2.0 / ROPE_DIMS))
    ang = jnp.arange(t, dtype=F32)[:, None] * inv[None, :]
    cos, sin = jnp.cos(ang), jnp.sin(ang)
    pad = jnp.zeros((t, HEAD_DIM - ROPE_DIMS), F32)
    zero = jnp.zeros((t, half), F32)
    head = lambda parts: jnp.concatenate(parts, axis=1)
    two = lambda x: jnp.concatenate([x, x], axis=1)
    rc = two(head([cos, cos, pad + 1.0]))
    rm = two(head([-sin, zero, pad]))
    rp = two(head([zero, sin, pad]))
    nq_cols = w_qg.shape[1]
    nkv = w_kv.shape[1]
    tab = pl.BlockSpec((tm, LANES), lambda i: (i % tiles, 0))
    full = lambda a: pl.BlockSpec(a.shape, lambda i: (0,) * a.ndim)
    consts = [jnp.stack([ln_q, ln_kv]), w_qg.astype(BF16), w_kv.astype(BF16)]
    return pl.pallas_call(
        _attn_pre_kernel,
        grid=(n // tm,),
        in_specs=[pl.BlockSpec((tm, dm), lambda i: (i, 0))] + [full(a) for a in consts] + [tab, tab, tab],
        out_specs=[pl.BlockSpec((tm, nq_cols), lambda i: (i, 0)),
                   pl.BlockSpec((tm, N_HEADS * HEAD_DIM), lambda i: (i, 0)),
                   pl.BlockSpec((1, nkv, tm), lambda i: (i // tiles, 0, i % tiles))],
        out_shape=[jax.ShapeDtypeStruct((n, nq_cols), F32), jax.ShapeDtypeStruct((n, N_HEADS * HEAD_DIM), F32),
                   jax.ShapeDtypeStruct((n_b, nkv, t), F32)],
        compiler_params=_cparams(("arbitrary",)),
        name="attn_pre",
    )(h, *consts, rc, rm, rp)


def _cmp_kernel(q_ref, k_ref, v_ref, ov_ref, o_ref, imp_ref, *, n_cmp):
    q = q_ref[0] * ATT_SCALE
    s = lax.dot_general(q, k_ref[0], _NT, preferred_element_type=F32)
    mask = lax.broadcasted_iota(jnp.int32, s.shape, 1) < n_cmp
    s = jnp.where(mask, s, NEG_BIG)
    m = jnp.max(s, axis=-1, keepdims=True)
    e = jnp.where(mask, jnp.exp(s - m), 0.0)
    p = (e / jnp.maximum(jnp.sum(e, axis=-1, keepdims=True), 1e-30)).astype(BF16)
    o_ref[0] = jnp.dot(p, v_ref[0], preferred_element_type=F32)
    imp_ref[0] = jnp.dot(p, ov_ref[...], preferred_element_type=F32)


def cmp_attention_sample(q, kc, vc, ov, n_cmp):
    bg, rq, hd = q.shape
    nc = kc.shape[1]
    nb = ov.shape[1]
    return pl.pallas_call(
        functools.partial(_cmp_kernel, n_cmp=n_cmp),
        grid=(bg,),
        in_specs=[pl.BlockSpec((1, rq, hd), lambda b: (b, 0, 0)),
                  pl.BlockSpec((1, nc, hd), lambda b: (b, 0, 0)),
                  pl.BlockSpec((1, nc, hd), lambda b: (b, 0, 0)),
                  pl.BlockSpec((nc, nb), lambda b: (0, 0))],
        out_specs=[pl.BlockSpec((1, rq, hd), lambda b: (b, 0, 0)),
                   pl.BlockSpec((1, rq, nb), lambda b: (b, 0, 0))],
        out_shape=[jax.ShapeDtypeStruct((bg, rq, hd), F32),
                   jax.ShapeDtypeStruct((bg, rq, nb), F32)],
        compiler_params=_cparams(("arbitrary",)),
        name="attn_cmp_sample",
    )(q, kc, vc, ov)


PAGES_PER_STEP = 8


def _page_compress_kernel(pt_ref, *rest, n_pages, page, pb):
    c_refs = rest[:pb]
    wbd_ref, pe_ref, w2_ref, o_ref, a_scr = rest[pb:]
    p = pl.program_id(1)
    nsub = n_pages * page // CMP_STRIDE
    for i in range(pb):
        row0 = pl.multiple_of((p * pb + i) * page, page)
        for s in range(2):
            for gp in range(KV_HEADS // 2):
                x = c_refs[i][0, s, 2 * gp:2 * gp + 2].reshape(2 * HEAD_DIM, page)
                a_scr[s, gp, pl.ds(row0, page), :] = x.T

    @pl.when(p == n_pages // pb - 1)
    def _():
        hid = pe_ref.shape[-1]
        for s in range(2):
            for gp in range(KV_HEADS // 2):
                acc = jnp.zeros((nsub, 4 * hid), F32)
                for j in range(CMP_STRIDE):
                    lhs = a_scr[s, gp, pl.ds(j, nsub, stride=CMP_STRIDE), :].astype(BF16)
                    acc = acc + jnp.dot(lhs, wbd_ref[s, j], preferred_element_type=F32)
                for g2 in range(2):
                    first = acc[:, 2 * g2 * hid:(2 * g2 + 1) * hid]
                    second = acc[:, (2 * g2 + 1) * hid:(2 * g2 + 2) * hid]
                    pre = first + pltpu.roll(second, nsub - 1, 0) + pe_ref[s]
                    act = jax.nn.gelu(pre).astype(BF16)
                    o_ref[s, 0, 2 * gp + g2] = jnp.dot(act, w2_ref[s], preferred_element_type=F32).astype(BF16)


def page_compress(cache_t, page_table, cmp_pe, cmp_w1, cmp_b1, cmp_w2):
    n_b, n_pages = page_table.shape
    page = cache_t.shape[-1]
    nsub = n_pages * page // CMP_STRIDE
    hid = cmp_w1.shape[-1]
    pb = PAGES_PER_STEP
    assert n_pages % pb == 0
    w1r = cmp_w1.reshape(2, 2, CMP_STRIDE, HEAD_DIM, hid).transpose(0, 2, 3, 1, 4)
    w1r = w1r.reshape(2, CMP_STRIDE, HEAD_DIM, 2 * hid)
    zeros = jnp.zeros_like(w1r)
    wbd = jnp.concatenate([jnp.concatenate([w1r, zeros], axis=-1), jnp.concatenate([zeros, w1r], axis=-1)], axis=2)
    pe_term = (jnp.einsum('sjd,sjdc->sc', cmp_pe, cmp_w1.reshape(2, CMP_BLOCK, HEAD_DIM, hid)) + cmp_b1)[:, None, :]
    def page_map(i):
        return lambda b, p, pt: (pt[b * n_pages + p * pb + i], 0, 0, 0, 0)

    grid_spec = pltpu.PrefetchScalarGridSpec(
        num_scalar_prefetch=1,
        grid=(n_b, n_pages // pb),
        in_specs=[pl.BlockSpec((1, 2, KV_HEADS, HEAD_DIM, page), page_map(i)) for i in range(pb)] + [
            pl.BlockSpec((2, CMP_STRIDE, 2 * HEAD_DIM, 4 * hid), lambda b, p, pt: (0, 0, 0, 0)),
            pl.BlockSpec((2, 1, hid), lambda b, p, pt: (0, 0, 0)),
            pl.BlockSpec((2, hid, HEAD_DIM), lambda b, p, pt: (0, 0, 0))],
        out_specs=pl.BlockSpec((2, 1, KV_HEADS, nsub, HEAD_DIM), lambda b, p, pt: (0, b, 0, 0, 0)),
        scratch_shapes=[pltpu.VMEM((2, KV_HEADS // 2, n_pages * page, 2 * HEAD_DIM), F32)],
    )
    return pl.pallas_call(
        functools.partial(_page_compress_kernel, n_pages=n_pages, page=page, pb=pb),
        grid_spec=grid_spec,
        out_shape=jax.ShapeDtypeStruct((2, n_b, KV_HEADS, nsub, HEAD_DIM), BF16),
        compiler_params=_cparams(("arbitrary", "arbitrary")),
        name="page_compress",
    )(page_table.reshape(-1), *([cache_t] * pb), wbd.astype(BF16), pe_term, cmp_w2.astype(BF16))


def _decode_attn_kernel(phys_ref, blk_ref, flag_ref, q_ref, kn_ref, vn_ref, *rest, mode, past, steps, window, page):
    kt_refs = rest[:steps]
    vt_refs = rest[steps:2 * steps]
    o_ref = rest[2 * steps]
    idx = pl.program_id(0) * KV_HEADS + pl.program_id(1)
    rnd = lambda x: x.astype(BF16).astype(F32)
    q = q_ref[0, 0] * ATT_SCALE

    kn = rnd(kn_ref[0, 0])
    vn = rnd(vn_ref[0, 0])
    s_new = jnp.sum(q.astype(F32) * kn, axis=-1, keepdims=True)
    on = flag_ref[idx] == 1
    m = jnp.where(on, s_new, NEG_BIG)
    scs, masks = [], []
    for s in range(steps):
        kt = kt_refs[s][0, 0, 0].astype(BF16)
        tile = kt.shape[-1]
        sc = jnp.dot(q, kt, preferred_element_type=F32)
        lane = lax.broadcasted_iota(jnp.int32, sc.shape, 1)
        if mode == "sel":
            blk = blk_ref[idx * steps + s]
            per_page = page // SEL_BLOCK
            pos = (blk // per_page) * page + lane
            mask = (jnp.right_shift(pos, SEL_BLOCK.bit_length() - 1) == blk) & (pos <= past) & (blk >= 0)
        else:
            pos = past - tile + lane
            mask = (pos > past - window) & (pos <= past)
        sc = jnp.where(mask, sc, NEG_BIG)
        m = jnp.maximum(m, jnp.max(sc, axis=-1, keepdims=True))
        scs.append(sc)
        masks.append(mask)
    p_new = jnp.where(on, jnp.exp(s_new - m), 0.0)
    l = p_new
    acc = rnd(p_new) * vn
    for s in range(steps):
        p = jnp.where(masks[s], jnp.exp(scs[s] - m), 0.0)
        l = l + jnp.sum(p, axis=-1, keepdims=True)
        vt = vt_refs[s][0, 0, 0].astype(BF16)
        acc = acc + lax.dot_general(p.astype(BF16), vt, _NT, preferred_element_type=F32)
    o_ref[0, 0] = acc / jnp.maximum(l, 1e-30)


def decode_attention(q, k_new, v_new, kv_t, streams, phys, blk, flag, mode, past, steps, window=None, page=None):
    n_b, n_g, rows, hd = q.shape
    tile = kv_t.shape[-1]
    ks, vs = streams

    def kmap(st, s):
        if mode == "sel":
            return lambda b, g, ph, bl, fl: (ph[(b * n_g + g) * steps + s], st, g, 0, 0)
        return lambda b, g, ph, bl, fl: (b, st, g, 0, 0)

    tiles = ([pl.BlockSpec((1, 1, 1, hd, tile), kmap(ks, s)) for s in range(steps)]
             + [pl.BlockSpec((1, 1, 1, hd, tile), kmap(vs, s)) for s in range(steps)])
    grid_spec = pltpu.PrefetchScalarGridSpec(
        num_scalar_prefetch=3,
        grid=(n_b, n_g),
        in_specs=[pl.BlockSpec((1, 1, rows, hd), lambda b, g, ph, bl, fl: (b, g, 0, 0)),
                  pl.BlockSpec((1, 1, 1, hd), lambda b, g, ph, bl, fl: (b, g, 0, 0)),
                  pl.BlockSpec((1, 1, 1, hd), lambda b, g, ph, bl, fl: (b, g, 0, 0))] + tiles,
        out_specs=pl.BlockSpec((1, 1, rows, hd), lambda b, g, ph, bl, fl: (b, g, 0, 0)),
    )
    return pl.pallas_call(
        functools.partial(_decode_attn_kernel, mode=mode, past=past, steps=steps, window=window, page=page),
        grid_spec=grid_spec,
        out_shape=jax.ShapeDtypeStruct((n_b, n_g, rows, hd), F32),
        compiler_params=_cparams(("arbitrary", "arbitrary")),
        name="attn_decode_" + mode,
    )(phys, blk, flag, q, k_new, v_new, *([kv_t] * (2 * steps)))


def rmsnorm(x, g):
    y = x * lax.rsqrt(jnp.mean(x * x, axis=-1, keepdims=True) + RMS_EPS)
    return y * g


def rope(x, pos):
    half = ROPE_DIMS // 2
    inv = jnp.power(ROPE_THETA, -jnp.arange(half, dtype=F32) * (2.0 / ROPE_DIMS))
    ang = pos.astype(F32)[:, None] * inv[None, :]
    cos = jnp.cos(ang)[:, None, :]
    sin = jnp.sin(ang)[:, None, :]
    x1 = x[..., :half]
    x2 = x[..., half:ROPE_DIMS]
    return jnp.concatenate([x1 * cos - x2 * sin, x2 * cos + x1 * sin, x[..., ROPE_DIMS:]], axis=-1)


def compress(sub, pe, w1, b1, w2, n_b, nsub):
    half = CMP_STRIDE * HEAD_DIM
    w1cat = jnp.concatenate([w1[:half], w1[half:]], axis=1)
    hidden = w1.shape[1]
    part = mm(sub, w1cat).reshape(n_b, nsub, KV_HEADS, 2, hidden)
    pe_term = jnp.einsum('jd,jdc->c', pe, w1.reshape(CMP_BLOCK, HEAD_DIM, hidden)) + b1
    pre = part[:, :nsub - 1, :, 0] + part[:, 1:, :, 1] + pe_term
    act = jax.nn.gelu(pre)
    return mm(act.reshape(-1, hidden), w2).reshape(n_b, nsub - 1, KV_HEADS, HEAD_DIM)


def overlap_matrix(n_cmp, n_sblk, nc_pad, nb_pad):
    cs = np.arange(n_cmp) * CMP_STRIDE
    ce = cs + CMP_BLOCK - 1
    ss = np.arange(n_sblk) * SEL_BLOCK
    ov = ((cs[:, None] < ss[None, :] + SEL_BLOCK) & (ce[:, None] >= ss[None, :])).astype(np.float32)
    out = np.zeros((nc_pad, nb_pad), np.float32)
    out[:n_cmp, :n_sblk] = ov
    return out


def block_scores(imp, q_pos, n_sblk):
    j = jnp.arange(n_sblk)[None, :]
    qblk = (q_pos // SEL_BLOCK)[:, None]
    allowed = j <= qblk
    forced = (j == 0) | ((j > qblk - N_LOCAL_BLOCKS) & allowed)
    score = jnp.where(forced, FORCE_SCORE, imp)
    return jnp.where(allowed, score, -jnp.inf)


def nsa_prompt(qg, q_r, kv_t, n_b, t, cmp_pe, cmp_w1, cmp_b1, cmp_w2):
    nsub = t // CMP_STRIDE
    n_cmp = nsub - 1
    n_sblk = t // SEL_BLOCK

    def sub_blocks(s):
        x = kv_t[:, s].reshape(n_b, KV_HEADS, HEAD_DIM, nsub, CMP_STRIDE)
        return x.transpose(0, 3, 1, 4, 2).reshape(-1, CMP_STRIDE * HEAD_DIM)

    kc = compress(sub_blocks(0), cmp_pe[0], cmp_w1[0], cmp_b1[0], cmp_w2[0], n_b, nsub)
    vc = compress(sub_blocks(1), cmp_pe[1], cmp_w1[1], cmp_b1[1], cmp_w2[1], n_b, nsub)
    nc_pad = _round_up(n_cmp, LANES)
    kc = jnp.pad(kc, ((0, 0), (0, nc_pad - n_cmp), (0, 0), (0, 0)))
    vc = jnp.pad(vc, ((0, 0), (0, nc_pad - n_cmp), (0, 0), (0, 0)))
    ovt = jnp.asarray(overlap_matrix(n_cmp, n_sblk, nc_pad, n_sblk).T, BF16)
    o_cmp, sel = cmp_select(qg, kc.transpose(0, 2, 1, 3).astype(BF16), vc.transpose(0, 2, 3, 1).astype(BF16),
                            ovt, n_b, t, n_cmp, n_sblk, qg.shape[0])
    o_sel = flash_attention(q_r, kv_t, 2, 3, n_b, t, sel=sel)
    o_win = flash_attention(q_r, kv_t, 4, 5, n_b, t, window=WINDOW)
    return o_cmp, o_sel, o_win


def nsa_sample(qg, kv, cache_kv, cache_win, page_table, cmp_pe, cmp_w1, cmp_b1, cmp_w2):
    n_b, n_pages = page_table.shape
    n_phys, page = cache_kv.shape[:2]
    past = n_pages * page
    pos = jnp.full((1,), past, jnp.int32)
    q = qg[:, :N_HEADS * HEAD_DIM].reshape(n_b, 1, N_HEADS, HEAD_DIM)
    kv = kv.reshape(n_b, 1, N_KV_STREAMS, KV_HEADS, HEAD_DIM)
    k_sel = rope(kv[:, :, 2], pos)
    k_win = rope(kv[:, :, 4], pos)
    new_rows = jnp.stack([kv[:, :, 0], kv[:, :, 1], k_sel, kv[:, :, 3]], axis=2)
    new_win = jnp.stack([k_win, kv[:, :, 5]], axis=2)
    n_keep = min(WINDOW, cache_win.shape[1] + 1)
    win_state = jnp.concatenate([cache_win[:, cache_win.shape[1] + 1 - n_keep:], new_win], axis=1)

    cache_t = cache_kv.transpose(0, 2, 3, 4, 1)
    win_t = cache_win.transpose(0, 2, 3, 4, 1)

    l_tot = past + 1
    nsub = l_tot // CMP_STRIDE
    n_cmp = nsub - 1
    n_sblk = -(-l_tot // SEL_BLOCK)
    assert nsub * CMP_STRIDE == past and page % SEL_BLOCK == 0 and cache_win.shape[1] == WINDOW
    kvc = page_compress(cache_t, page_table, cmp_pe, cmp_w1, cmp_b1, cmp_w2)

    bg = n_b * KV_HEADS
    rq = 16
    qpad = lambda x: jnp.pad(x.reshape(n_b, KV_HEADS, GQA_R, HEAD_DIM), ((0, 0), (0, 0), (0, rq - GQA_R), (0, 0))).astype(BF16)

    nb_pad = _round_up(n_sblk, 8)
    ov = jnp.asarray(overlap_matrix(n_cmp, n_sblk, nsub, nb_pad), BF16)
    o_cmp, imp = cmp_attention_sample(qpad(q).reshape(bg, rq, HEAD_DIM), kvc[0].reshape(bg, nsub, HEAD_DIM),
                                      kvc[1].reshape(bg, nsub, HEAD_DIM), ov, n_cmp)
    imp = imp[:, :GQA_R, :n_sblk].sum(axis=1)
    score = block_scores(imp[:, None, :], pos, n_sblk)
    top_s, top_i = lax.top_k(score[:, 0], min(N_SEL, n_sblk))
    valid = jnp.isfinite(top_s)

    n_past_blk = past // SEL_BLOCK
    per_page = page // SEL_BLOCK
    n_k = top_i.shape[-1]
    in_past = valid & (top_i < n_past_blk)
    blk = jnp.where(in_past, top_i, -1).astype(jnp.int32)
    logical = jnp.clip(top_i, 0, n_past_blk - 1) // per_page
    phys = jnp.take_along_axis(page_table, logical.reshape(n_b, KV_HEADS * n_k), axis=1).reshape(bg, n_k)
    phys = jnp.where(in_past, phys, 0).astype(jnp.int32)
    new_flag = jnp.any(valid & (top_i >= n_past_blk), axis=-1).astype(jnp.int32)
    q_r = qpad(rope(q, pos))
    o_sel = decode_attention(q_r, new_rows[:, 0, 2][:, :, None, :], new_rows[:, 0, 3][:, :, None, :], cache_t, (2, 3),
                             phys.reshape(-1), blk.reshape(-1), new_flag, "sel", past, n_k, page=page)
    zeros = jnp.zeros((bg,), jnp.int32)
    o_win = decode_attention(q_r, new_win[:, 0, 0][:, :, None, :], new_win[:, 0, 1][:, :, None, :], win_t, (0, 1),
                             zeros, zeros, jnp.ones((bg,), jnp.int32), "win", past, 1, window=WINDOW)

    heads = lambda o: o.reshape(n_b, KV_HEADS, -1, HEAD_DIM)[:, :, :GQA_R].reshape(n_b, N_HEADS * HEAD_DIM)
    return (heads(o_cmp), heads(o_sel), heads(o_win)), new_rows, win_state


def rwkv_layer(h_p, h_t, n_p, t_p, n_s, state_wkv, state_shift, ln, mu, w_r, w_k, w_v, w0, w1, w2, a0, a1, a2, g1, g2,
               k_k, k_a, r_k, gn_w, gn_b):
    np_rows = n_p * t_p
    weights = (ln, mu, w_r, w_k, w_v, w0, w1, w2, a0, a1, a2, g1, g2, k_k, k_a)
    pre_p = rwkv_pre(h_p, jnp.zeros((RW_TM, D_MODEL), F32), np_rows, t_p, *weights)
    pre_t = rwkv_pre(h_t, jnp.pad(state_shift, ((0, RW_TM - n_s), (0, 0))), 0, RW_TM, *weights)

    def lanes(x, nb, t):
        return x.reshape(nb, t, RW_HEADS, RW_HEAD).transpose(1, 3, 0, 2).reshape(t, RW_HEAD, nb * RW_HEADS)

    def rows(o, nb, t):
        return o.reshape(t, RW_HEAD, nb, RW_HEADS).transpose(2, 0, 3, 1).reshape(nb * t, D_MODEL)

    head_vec = lambda w, nb: jnp.tile(w.reshape(RW_HEADS, RW_HEAD).T, (1, nb))
    s0_p = jnp.zeros((RW_HEAD, RW_HEAD, n_p * RW_HEADS), F32)
    o_p, st_p = rwkv_scan(*[lanes(x, n_p, t_p) for x in pre_p[:6]],
                          head_vec(r_k, n_p), head_vec(gn_w, n_p), head_vec(gn_b, n_p), s0_p)
    s0_s = state_wkv.transpose(3, 2, 0, 1).reshape(RW_HEAD, RW_HEAD, n_s * RW_HEADS)
    o_s, st_s = rwkv_scan(*[lanes(x[:n_s], n_s, 1) for x in pre_t[:6]],
                          head_vec(r_k, n_s), head_vec(gn_w, n_s), head_vec(gn_b, n_s), s0_s, dot_dtype=BF16)
    o_t = jnp.pad(rows(o_s, n_s, 1), ((0, RW_TM - n_s), (0, 0)))
    wkv_p = st_p.reshape(RW_HEAD, RW_HEAD, n_p, RW_HEADS).transpose(2, 3, 1, 0)
    wkv_s = st_s.reshape(RW_HEAD, RW_HEAD, n_s, RW_HEADS).transpose(2, 3, 1, 0)
    return (rows(o_p, n_p, t_p), pre_p[6], pre_p[7]), (o_t, pre_t[6], pre_t[7]), wkv_p, wkv_s


def kernel(x_prompt, x_sample, state_wkv, state_shift, cache_kv, cache_win, page_table, ln_mix, ln_ffn, ln_kv, ln_out, rw_mu, rw_wr, rw_wk, rw_wv, rw_wo, rw_w0, rw_w1, rw_w2, rw_a0, rw_a1, rw_a2, rw_g1, rw_g2, rw_kk, rw_ka, rw_rk, rw_lnw, rw_lnb, w_kv, cmp_pe, cmp_w1, cmp_b1, cmp_w2, nsa_wqg, nsa_wo, moe_wrg, moe_brg, moe_wre, moe_bre, moe_wgate, moe_wup, moe_wdown):
    n_p, t_p, dm = x_prompt.shape
    n_s = x_sample.shape[0]
    assert x_sample.shape[1] == 1
    np_rows = n_p * t_p
    h_p = x_prompt.reshape(np_rows, dm)
    h_t = jnp.pad(x_sample.reshape(n_s, dm), ((0, RW_TM - n_s), (0, 0)))
    sizes = (np_rows, RW_TM)

    def moe(fronts, layer):
        ys = hmoe(jnp.concatenate([f[1] for f in fronts]), jnp.concatenate([f[2] for f in fronts]), sizes,
                  moe_brg[layer], moe_bre[layer], moe_wgate[layer], moe_wup[layer], moe_wdown[layer])
        return [f[0] + y for f, y in zip(fronts, ys)]

    mix_p, mix_t, wkv_p, wkv_s = rwkv_layer(h_p, h_t, n_p, t_p, n_s, state_wkv[0], state_shift[0], ln_mix[0], rw_mu[0],
                                            rw_wr[0], rw_wk[0], rw_wv[0], rw_w0[0], rw_w1[0], rw_w2[0], rw_a0[0], rw_a1[0],
                                            rw_a2[0], rw_g1[0], rw_g2[0], rw_kk[0], rw_ka[0], rw_rk[0], rw_lnw[0], rw_lnb[0])
    p_shift = mix_p[2].reshape(n_p, t_p, dm)[:, -1][None]
    s_shift = mix_t[2][:n_s][None]
    w_router = router_weight(moe_wrg[0], moe_wre[0])
    h_p, h_t = moe([mix_out([o], g, hh, rw_wo[0], ln_ffn[0], w_router)
                    for (o, g, _), hh in ((mix_p, h_p), (mix_t, h_t))], 0)

    n_q = N_HEADS * HEAD_DIM
    gate_cols = _round_up(nsa_wqg.shape[-1] - n_q, LANES)
    w_qg = jnp.pad(nsa_wqg[0], ((0, 0), (0, n_q + gate_cols - nsa_wqg.shape[-1])))
    qg_p, qr_p, kvt = attn_pre(h_p, n_p, t_p, ln_mix[1], ln_kv, w_qg, w_kv)
    kvt = kvt.reshape(n_p, N_KV_STREAMS, KV_HEADS, HEAD_DIM, t_p)
    o_p = nsa_prompt(qg_p, qr_p, kvt, n_p, t_p, cmp_pe, cmp_w1, cmp_b1, cmp_w2)
    p_rows = kvt[:, :4].transpose(0, 4, 1, 2, 3)
    p_win = kvt[:, 4:, :, :, t_p - min(WINDOW, t_p):].transpose(0, 4, 1, 2, 3)
    qg_t = mm(rmsnorm(h_t, ln_mix[1]), w_qg)
    kv_t = mm(rmsnorm(h_t, ln_kv), w_kv)
    o_s, s_rows, s_win = nsa_sample(qg_t[:n_s, :nsa_wqg.shape[-1]], kv_t[:n_s], cache_kv, cache_win,
                                    page_table, cmp_pe, cmp_w1, cmp_b1, cmp_w2)
    o_t = [jnp.pad(x, ((0, RW_TM - n_s), (0, 0))) for x in o_s]
    expand = [jnp.asarray((np.arange(gate_cols)[:, None] == 3 * (np.arange(n_q)[None, :] // HEAD_DIM) + br)
                          .astype(np.float32), BF16) for br in range(3)]
    w_router = router_weight(moe_wrg[1], moe_wre[1])
    h_p, h_t = moe([mix_out(list(br), qg, hh, nsa_wo[0], ln_ffn[1], w_router, gate_expand=expand, gate_col=n_q // LANES)
                    for br, qg, hh in ((o_p, qg_p, h_p), (o_t, qg_t, h_t))], 1)

    y_prompt = rmsnorm(h_p, ln_out).reshape(n_p, t_p, dm)
    y_sample = rmsnorm(h_t[:n_s], ln_out).reshape(n_s, 1, dm)
    return (y_prompt, y_sample, wkv_p[None], p_shift, p_rows, p_win, wkv_s[None], s_shift, s_rows, s_win)
```

```python
import functools

import numpy as np
import jax
import jax.numpy as jnp
from jax import lax
from jax.experimental import pallas as pl
from jax.experimental.pallas import tpu as pltpu

F32 = jnp.float32
BF16 = jnp.bfloat16

D_MODEL = 1024
RMS_EPS = 1e-6
RW_HEAD = 64
RW_HEADS = D_MODEL // RW_HEAD
RW_GN_EPS = 64e-5
N_HEADS = 16
HEAD_DIM = 64
KV_HEADS = 4
GQA_R = N_HEADS // KV_HEADS
ROPE_DIMS = HEAD_DIM // 4
ROPE_THETA = 500000.0
N_KV_STREAMS = 6
CMP_BLOCK = 32
CMP_STRIDE = 16
SEL_BLOCK = 64
N_SEL = 16
N_LOCAL_BLOCKS = 2
FORCE_SCORE = 1.0e4
WINDOW = 512
N_GROUPS = 4
EXPERTS_PER_GROUP = 8
N_EXPERTS = N_GROUPS * EXPERTS_PER_GROUP
D_EXPERT = 256

LANES = 128
VMEM_LIMIT = 48 << 20
NEG_BIG = -1e30
M_FLOOR = -1e29
ATT_SCALE = HEAD_DIM ** -0.5
TOKEN_TILE = 512


def _round_up(x, m):
    return (x + m - 1) // m * m


def _cparams(sem):
    return pltpu.CompilerParams(dimension_semantics=sem, vmem_limit_bytes=VMEM_LIMIT)


_NT = (((1,), (1,)), ((), ()))


def _mm_kernel(x_ref, w_ref, o_ref):
    o_ref[...] = jnp.dot(x_ref[...].astype(BF16), w_ref[...], preferred_element_type=F32)


def mm(x, w, tm=TOKEN_TILE):
    m, k = x.shape
    n = w.shape[1]
    tm = min(tm, _round_up(m, 8))
    mp = _round_up(m, tm)
    if mp != m:
        x = jnp.pad(x, ((0, mp - m), (0, 0)))
    out = pl.pallas_call(
        _mm_kernel,
        grid=(mp // tm,),
        in_specs=[pl.BlockSpec((tm, k), lambda i: (i, 0)),
                  pl.BlockSpec((k, n), lambda i: (0, 0))],
        out_specs=pl.BlockSpec((tm, n), lambda i: (i, 0)),
        out_shape=jax.ShapeDtypeStruct((mp, n), F32),
        compiler_params=_cparams(("arbitrary",)),
        name="mm",
    )(x, w.astype(BF16))
    return out[:m] if mp != m else out


SCAN_TC = 32


def _scan_kernel(r_ref, d_ref, kin_ref, a_ref, v_ref, kkp_ref, kap_ref, rk_ref, gw_ref, gb_ref, s0_ref, o_ref, st_ref,
                 s_scr, n_ref, k_ref, b_ref, *, tc, dot_dtype):
    c = pl.program_id(1)
    rnd = lambda x: x.astype(dot_dtype).astype(F32)
    halves = [(h0, h0 + RW_HEAD // 2) for h0 in (0, RW_HEAD // 2)]

    @pl.when(c == 0)
    def _():
        s_scr[...] = s0_ref[...]

    def step(t, carry):
        vv = v_ref[t]
        k_t = kin_ref[t]
        a_t = a_ref[t]
        kk = k_t * kkp_ref[...]
        kk = kk / jnp.maximum(jnp.sqrt(jnp.sum(kk * kk, axis=0, keepdims=True)), 1e-12)
        k2 = k_t * (1.0 + (a_t - 1.0) * kap_ref[...])
        n_ref[...] = kk
        k_ref[...] = k2
        b_ref[...] = kk * a_t
        outs = []
        for lo, hi in halves:
            parts = [jnp.zeros((hi - lo, LANES), F32)] * 2
            for j in range(RW_HEAD):
                parts[j % 2] = parts[j % 2] + rnd(s_scr[j, lo:hi]) * rnd(n_ref[pl.ds(j, 1), :])
            sa = -(parts[0] + parts[1])
            vh = vv[lo:hi]
            out = jnp.zeros((hi - lo, LANES), F32)
            for j in range(RW_HEAD):
                sn = (s_scr[j, lo:hi] * d_ref[t, pl.ds(j, 1), :] + sa * b_ref[pl.ds(j, 1), :]
                      + vh * k_ref[pl.ds(j, 1), :])
                s_scr[j, lo:hi] = sn
                out = out + rnd(sn) * rnd(r_ref[t, pl.ds(j, 1), :])
            outs.append(out)
        out = jnp.concatenate(outs, axis=0)
        mean = jnp.mean(out, axis=0, keepdims=True)
        cen = out - mean
        var = jnp.mean(cen * cen, axis=0, keepdims=True)
        bonus = jnp.sum(r_ref[t] * k2 * rk_ref[...], axis=0, keepdims=True) * vv
        o_ref[t] = cen * lax.rsqrt(var + RW_GN_EPS) * gw_ref[...] + gb_ref[...] + bonus
        return carry

    lax.fori_loop(0, tc, step, 0)

    @pl.when(c == pl.num_programs(1) - 1)
    def _():
        st_ref[...] = s_scr[...]


def rwkv_scan(r, d, k, a, v, kkp, kap, rk, gw, gb, s0, dot_dtype=F32):
    t, _, l = r.shape
    tc = min(SCAN_TC, t)
    assert t % tc == 0 and l % LANES == 0
    seq = pl.BlockSpec((tc, RW_HEAD, LANES), lambda g, c: (c, 0, g))
    vec = pl.BlockSpec((RW_HEAD, LANES), lambda g, c: (0, g))
    st = pl.BlockSpec((RW_HEAD, RW_HEAD, LANES), lambda g, c: (0, 0, g))
    return pl.pallas_call(
        functools.partial(_scan_kernel, tc=tc, dot_dtype=dot_dtype),
        grid=(l // LANES, t // tc),
        in_specs=[seq] * 5 + [vec] * 5 + [st],
        out_specs=[seq, st],
        out_shape=[jax.ShapeDtypeStruct((t, RW_HEAD, l), F32),
                   jax.ShapeDtypeStruct((RW_HEAD, RW_HEAD, l), F32)],
        scratch_shapes=[pltpu.VMEM((RW_HEAD, RW_HEAD, LANES), F32)] + [pltpu.VMEM((RW_HEAD, LANES), F32)] * 3,
        compiler_params=_cparams(("arbitrary", "arbitrary")),
        name="rwkv_scan",
    )(r, d, k, a, v, kkp, kap, rk, gw, gb, s0)


RW_TM = 256


def _split_dot(x, w):
    hi = x.astype(BF16)
    lo = (x - hi.astype(F32)).astype(BF16)
    return jnp.dot(hi, w, preferred_element_type=F32) + jnp.dot(lo, w, preferred_element_type=F32)


def _rwkv_pre_kernel(x_ref, ps_ref, ln_ref, mu_ref, vec_ref, wr_ref, wk_ref, wv_ref, w1_ref, w2_ref, a1_ref, a2_ref,
                     g1_ref, g2_ref,
                     r_o, d_o, k_o, a_o, v_o, g_o, xn_o, carry_scr, *, tiles_per_seq, n_seq_tiles):
    i = pl.program_id(0)
    tm = x_ref.shape[0]

    @pl.when(i == 0)
    def _():
        carry_scr[...] = jnp.zeros_like(carry_scr)

    x = x_ref[...]
    xn = x * lax.rsqrt(jnp.mean(x * x, axis=-1, keepdims=True) + RMS_EPS) * ln_ref[...]
    xn_o[...] = xn
    first = jnp.where(i % tiles_per_seq == 0, jnp.zeros_like(carry_scr[...]), carry_scr[...])
    row0 = lax.broadcasted_iota(jnp.int32, xn.shape, 0) == 0
    prev = jnp.where(row0, first, pltpu.roll(xn, 1, 0))
    prev = jnp.where(i >= n_seq_tiles, ps_ref[...], prev)
    carry_scr[...] = xn[tm - 1:tm, :]
    xx = prev - xn
    mix = lambda c: (xn + xx * mu_ref[c:c + 1, :]).astype(BF16)
    dot = lambda a, w_ref: jnp.dot(a, w_ref[...], preferred_element_type=F32)
    w0, a0 = (vec_ref[c:c + 1, :] for c in range(2))
    r_o[...] = dot(mix(0), wr_ref)
    k_o[...] = dot(mix(2), wk_ref)
    v_o[...] = dot(mix(3), wv_ref)
    w_log = -jax.nn.softplus(-(w0 + dot(jnp.tanh(dot(mix(1), w1_ref)).astype(BF16), w2_ref))) - 0.5
    d_o[...] = jnp.exp(-jnp.exp(w_log))
    a_o[...] = jax.nn.sigmoid(a0 + dot(dot(mix(4), a1_ref).astype(BF16), a2_ref))
    g_o[...] = dot(jax.nn.sigmoid(dot(mix(5), g1_ref)).astype(BF16), g2_ref)


def rwkv_pre(h, prev_single, n_seq_rows, t, ln, mu, w_r, w_k, w_v, w0, w1, w2, a0, a1, a2, g1, g2):
    n, dm = h.shape
    tm = RW_TM
    assert n % tm == 0 and t % tm == 0 and n_seq_rows % tm == 0 and prev_single.shape == (tm, dm)
    row = pl.BlockSpec((tm, dm), lambda i: (i, 0))
    full = lambda a: pl.BlockSpec(a.shape, lambda i: (0,) * a.ndim)
    bf = lambda w: w.astype(BF16)
    consts = [prev_single, ln[None, :], mu, jnp.stack([w0, a0]), bf(w_r), bf(w_k), bf(w_v), bf(w1), bf(w2),
              bf(a1), bf(a2), bf(g1), bf(g2)]
    return pl.pallas_call(
        functools.partial(_rwkv_pre_kernel, tiles_per_seq=t // tm, n_seq_tiles=n_seq_rows // tm),
        grid=(n // tm,),
        in_specs=[row] + [full(a) for a in consts],
        out_specs=[row] * 7,
        out_shape=[jax.ShapeDtypeStruct((n, dm), F32)] * 7,
        scratch_shapes=[pltpu.VMEM((1, dm), F32)],
        compiler_params=_cparams(("arbitrary",)),
        name="rwkv_pre",
    )(h, *consts)


def _mix_out_kernel(*refs, n_branch, use_gates):
    o_refs = refs[:n_branch]
    gate_ref, h_ref, wo_ref, ln_ref, wr_ref = refs[n_branch:n_branch + 5]
    e_refs = refs[n_branch + 5:n_branch + 5 + (n_branch if use_gates else 0)]
    h_o, xp_o, lg_o = refs[-3:]
    if use_gates:
        sig = jax.nn.sigmoid(gate_ref[...])
        mixed = sum(_split_dot(sig, e_refs[b][...]) * o_refs[b][...] for b in range(n_branch))
    else:
        mixed = o_refs[0][...] * gate_ref[...]
    h = h_ref[...] + jnp.dot(mixed.astype(BF16), wo_ref[...], preferred_element_type=F32)
    h_o[...] = h
    xn = h * lax.rsqrt(jnp.mean(h * h, axis=-1, keepdims=True) + RMS_EPS) * ln_ref[...]
    xb = xn.astype(BF16)
    lg_o[...] = jnp.dot(xb, wr_ref[...], preferred_element_type=F32)
    u = lax.bitcast_convert_type(xb.astype(F32), jnp.uint32)
    half = u.shape[1] // 2
    xp_o[...] = (u[:, :half] >> 16) | (u[:, half:] & jnp.uint32(0xFFFF0000))


def mix_out(branches, gate, h, w_o, ln, w_router, gate_expand=None, gate_col=0):
    n, dm = h.shape
    tm = RW_TM
    nb = len(branches)
    row = lambda a: pl.BlockSpec((tm, a.shape[1]), lambda i: (i, 0))
    full = lambda a: pl.BlockSpec(a.shape, lambda i: (0,) * a.ndim)
    consts = [w_o.astype(BF16), ln[None, :], w_router.astype(BF16)] + list(gate_expand or [])
    gate_spec = row(gate) if gate_expand is None else pl.BlockSpec((tm, LANES), lambda i: (i, gate_col))
    return pl.pallas_call(
        functools.partial(_mix_out_kernel, n_branch=nb, use_gates=gate_expand is not None),
        grid=(n // tm,),
        in_specs=[row(a) for a in branches] + [gate_spec, row(h)] + [full(a) for a in consts],
        out_specs=[pl.BlockSpec((tm, dm), lambda i: (i, 0)), pl.BlockSpec((tm, dm // 2), lambda i: (i, 0)),
                   pl.BlockSpec((tm, LANES), lambda i: (i, 0))],
        out_shape=[jax.ShapeDtypeStruct((n, dm), F32), jax.ShapeDtypeStruct((n, dm // 2), jnp.uint32),
                   jax.ShapeDtypeStruct((n, LANES), F32)],
        compiler_params=_cparams(("arbitrary",)),
        name="mix_out",
    )(*branches, gate, h, *consts)


MOE_TM = 256
MOE_VMEM_LIMIT = 56 << 20


def _moe_kernel(te_ref, nt_ref, tok_ref, x_ref, c_ref, wgu_ref, wd_ref, o_ref, xt_scr):
    i = pl.program_id(0)

    @pl.when(i < nt_ref[0])
    def _():
        base = i * MOE_TM

        def gather_row(r, carry):
            xt_scr[pl.ds(r, 1), :] = x_ref[pl.ds(tok_ref[base + r], 1), :]
            return carry

        lax.fori_loop(0, MOE_TM, gather_row, 0, unroll=8)
        u = xt_scr[...]
        lo = lax.bitcast_convert_type(u << 16, F32).astype(BF16)
        hi = lax.bitcast_convert_type(u & jnp.uint32(0xFFFF0000), F32).astype(BF16)
        x = jnp.concatenate([lo, hi], axis=1)
        h = jnp.dot(x, wgu_ref[0], preferred_element_type=F32)
        g = h[:, :D_EXPERT]
        u2 = h[:, D_EXPERT:]
        hid = g * jax.nn.sigmoid(g) * u2 * c_ref[...]
        o_ref[...] = jnp.dot(hid.astype(BF16), wd_ref[0], preferred_element_type=F32)

    @pl.when(i >= nt_ref[0])
    def _():
        o_ref[...] = jnp.zeros_like(o_ref)


def moe_gmm(xp, row_tok, cs, tile_expert, n_tiles_used, wgu, wd):
    n, half = xp.shape
    dm = 2 * half
    p = row_tok.shape[0]
    nt = p // MOE_TM

    def row_map(i, te, ntu, tok):
        return (jnp.minimum(i, ntu[0] - 1), 0)

    grid_spec = pltpu.PrefetchScalarGridSpec(
        num_scalar_prefetch=3,
        grid=(nt,),
        in_specs=[pl.BlockSpec((n, half), lambda i, te, ntu, tok: (0, 0), pipeline_mode=pl.Buffered(1)),
                  pl.BlockSpec((MOE_TM, 1), row_map),
                  pl.BlockSpec((1, dm, 2 * D_EXPERT), lambda i, te, ntu, tok: (te[i], 0, 0)),
                  pl.BlockSpec((1, D_EXPERT, dm), lambda i, te, ntu, tok: (te[i], 0, 0))],
        out_specs=pl.BlockSpec((MOE_TM, dm), lambda i, te, ntu, tok: (i, 0)),
        scratch_shapes=[pltpu.VMEM((MOE_TM, half), jnp.uint32)],
    )
    return pl.pallas_call(
        _moe_kernel,
        grid_spec=grid_spec,
        out_shape=jax.ShapeDtypeStruct((p, dm), F32),
        compiler_params=pltpu.CompilerParams(dimension_semantics=("arbitrary",), vmem_limit_bytes=MOE_VMEM_LIMIT),
        name="moe_gmm",
    )(tile_expert, n_tiles_used, row_tok, xp, cs, wgu, wd)


def router_weight(w_rg, w_re):
    w_r = jnp.concatenate([w_rg, w_re], axis=1)
    return jnp.pad(w_r, ((0, 0), (0, LANES - w_r.shape[1])))


def hmoe(xp, logits, set_sizes, b_rg, b_re, w_gate, w_up, w_down):
    n = xp.shape[0]
    dm = 2 * xp.shape[1]
    lg = logits[:, :N_GROUPS] + b_rg
    le = (logits[:, N_GROUPS:N_GROUPS + N_EXPERTS] + b_re).reshape(n, N_GROUPS, EXPERTS_PER_GROUP)
    g_sel = jnp.argmax(lg, axis=-1)
    p_g = jnp.max(jax.nn.softmax(lg, axis=-1), axis=-1)
    le_g = jnp.take_along_axis(le, g_sel[:, None, None], axis=1)[:, 0]
    top_p, top_i = lax.top_k(jax.nn.softmax(le_g, axis=-1), 2)
    w_top = p_g[:, None] * top_p / jnp.sum(top_p, axis=-1, keepdims=True)
    eid = (g_sel[:, None] * EXPERTS_PER_GROUP + top_i).astype(jnp.int32).reshape(-1)
    cw = w_top.reshape(-1)

    n2 = 2 * n
    onehot = (eid[:, None] == jnp.arange(N_EXPERTS, dtype=jnp.int32)[None, :]).astype(jnp.int32)
    counts = jnp.sum(onehot, axis=0)
    rank = jnp.take_along_axis(jnp.cumsum(onehot, axis=0) - onehot, eid[:, None], axis=1)[:, 0]
    tiles_per = (counts + MOE_TM - 1) // MOE_TM
    tile_end = jnp.cumsum(tiles_per)
    tile_start = tile_end - tiles_per
    pos = tile_start[eid] * MOE_TM + rank
    p_rows = _round_up(n2 + N_EXPERTS * (MOE_TM - 1), MOE_TM)
    nt = p_rows // MOE_TM
    n_used = tile_end[-1].astype(jnp.int32)
    tile_ids = jnp.arange(nt, dtype=jnp.int32)
    tile_expert = jnp.sum((tile_end[None, :] <= jnp.minimum(tile_ids, n_used - 1)[:, None]).astype(jnp.int32), axis=1)

    tok = jnp.arange(n2, dtype=jnp.int32) // 2
    row_tok = jnp.zeros((p_rows,), jnp.int32).at[pos].set(tok)
    row_c = jnp.zeros((p_rows,), F32).at[pos].set(cw)
    wgu = jnp.concatenate([w_gate, w_up], axis=-1).reshape(N_EXPERTS, dm, 2 * D_EXPERT).astype(BF16)
    wd = w_down.reshape(N_EXPERTS, D_EXPERT, dm).astype(BF16)
    ys = moe_gmm(xp, row_tok, row_c[:, None], tile_expert, n_used.reshape(1), wgu, wd)
    pos2 = pos.reshape(n, 2)
    bounds = np.cumsum([0] + list(set_sizes))
    return [ys[pos2[a:b, 0]] + ys[pos2[a:b, 1]] for a, b in zip(bounds[:-1], bounds[1:])]


ATT_TQ = 256
ATT_TK = 512


def _cmp_sel_kernel(q_ref, kc_ref, vct_ref, ovt_ref, o_ref, sel_ref, *, n_cmp, n_sblk, tq):
    t0 = pl.program_id(2) * tq
    ncp = kc_ref.shape[2]
    n_io = lax.broadcasted_iota(jnp.int32, (ncp, tq), 0)
    t_io = lax.broadcasted_iota(jnp.int32, (ncp, tq), 1) + t0
    mask = (n_io * CMP_STRIDE + (CMP_BLOCK - 1) <= t_io) & (n_io < n_cmp)
    kc = kc_ref[0, 0]
    vct = vct_ref[0, 0]
    ovt = ovt_ref[...]
    imp = jnp.zeros((n_sblk, tq), F32)
    outs = []
    for r in range(GQA_R):
        qr = (q_ref[:, r * HEAD_DIM:(r + 1) * HEAD_DIM] * ATT_SCALE).astype(BF16)
        st = lax.dot_general(kc, qr, _NT, preferred_element_type=F32)
        st = jnp.where(mask, st, NEG_BIG)
        m = jnp.max(st, axis=0, keepdims=True)
        e = jnp.where(mask, jnp.exp(st - m), 0.0)
        p = (e / jnp.maximum(jnp.sum(e, axis=0, keepdims=True), 1e-30)).astype(BF16)
        outs.append(jnp.dot(vct, p, preferred_element_type=F32).T)
        imp = imp + jnp.dot(ovt, p, preferred_element_type=F32)
    o_ref[...] = jnp.concatenate(outs, axis=1)

    j_io = lax.broadcasted_iota(jnp.int32, (n_sblk, tq), 0)
    qblk = jnp.right_shift(lax.broadcasted_iota(jnp.int32, (n_sblk, tq), 1) + t0, SEL_BLOCK.bit_length() - 1)
    allowed = j_io <= qblk
    forced = (j_io == 0) | ((j_io > qblk - N_LOCAL_BLOCKS) & allowed)
    score = jnp.where(forced, FORCE_SCORE, imp)
    score = jnp.where(allowed, score, -jnp.inf)
    rank = jnp.zeros((n_sblk, tq), jnp.int32)
    for jp in range(n_sblk):
        row = score[jp:jp + 1, :]
        beats = (row > score) | ((row == score) & (j_io > jp))
        rank = rank + jnp.where(beats, 1, 0)
    sel = (rank < N_SEL) & allowed
    sel_ref[0, 0] = jnp.where(sel, 1.0, 0.0).astype(BF16)


def cmp_select(q, kc, vct, ovt, n_b, t, n_cmp, n_sblk, n_out):
    assert SEL_BLOCK & (SEL_BLOCK - 1) == 0
    tq = ATT_TQ
    nq = t // tq
    ncp = kc.shape[2]
    gw = GQA_R * HEAD_DIM
    return pl.pallas_call(
        functools.partial(_cmp_sel_kernel, n_cmp=n_cmp, n_sblk=n_sblk, tq=tq),
        grid=(n_b, KV_HEADS, nq),
        in_specs=[pl.BlockSpec((tq, gw), lambda b, g, qi: (b * nq + qi, g)),
                  pl.BlockSpec((1, 1, ncp, HEAD_DIM), lambda b, g, qi: (b, g, 0, 0)),
                  pl.BlockSpec((1, 1, HEAD_DIM, ncp), lambda b, g, qi: (b, g, 0, 0)),
                  pl.BlockSpec((n_sblk, ncp), lambda b, g, qi: (0, 0))],
        out_specs=[pl.BlockSpec((tq, gw), lambda b, g, qi: (b * nq + qi, g)),
                   pl.BlockSpec((1, 1, n_sblk, tq), lambda b, g, qi: (b, g, 0, qi))],
        out_shape=[jax.ShapeDtypeStruct((n_out, N_HEADS * HEAD_DIM), F32),
                   jax.ShapeDtypeStruct((n_b, KV_HEADS, n_sblk, t), BF16)],
        compiler_params=_cparams(("arbitrary", "arbitrary", "arbitrary")),
        name="attn_cmp_select",
    )(q, kc, vct, ovt)


def _flash_kernel(pq_ref, pk_ref, pfirst_ref, plast_ref, q_ref, kt_ref, vt_ref, *rest, window, use_sel, tq, tk):
    if use_sel:
        sel_ref, et_ref, o_ref, m_scr, l_scr, acc_scr = rest
    else:
        o_ref, m_scr, l_scr, acc_scr = rest
    p_id = pl.program_id(2)

    @pl.when(pfirst_ref[p_id] == 1)
    def _():
        m_scr[...] = jnp.full_like(m_scr, M_FLOOR)
        l_scr[...] = jnp.zeros_like(l_scr)
        acc_scr[...] = jnp.zeros_like(acc_scr)

    off = pq_ref[p_id] * tq - pk_ref[p_id] * tk
    diff = lax.broadcasted_iota(jnp.int32, (tk, tq), 0) - lax.broadcasted_iota(jnp.int32, (tk, tq), 1)
    mask = diff <= off
    if window is not None:
        mask = mask & (diff > off - window)
    if use_sel:
        selm = jnp.dot(et_ref[...], sel_ref[0, 0], preferred_element_type=F32)
        mask = mask & (selm > 0.5)
    bias = jnp.where(mask, 0.0, NEG_BIG)
    k = kt_ref[0, 0, 0].astype(BF16).T
    vt = vt_ref[0, 0, 0].astype(BF16)
    heads = range(GQA_R)
    qs = [(q_ref[:, r * HEAD_DIM:(r + 1) * HEAD_DIM] * ATT_SCALE).astype(BF16) for r in heads]
    sts = [lax.dot_general(k, qs[r], _NT, preferred_element_type=F32) + bias for r in heads]
    m_prev = [m_scr[r] for r in heads]
    m_new = [jnp.maximum(m_prev[r], jnp.max(sts[r], axis=0, keepdims=True)) for r in heads]
    ps = [jnp.exp(sts[r] - m_new[r]) for r in heads]
    alphas = [jnp.exp(m_prev[r] - m_new[r]) for r in heads]
    for r in heads:
        l_scr[r] = alphas[r] * l_scr[r] + jnp.sum(ps[r], axis=0, keepdims=True)
        acc_scr[r] = alphas[r] * acc_scr[r] + jnp.dot(vt, ps[r].astype(BF16), preferred_element_type=F32)
        m_scr[r] = m_new[r]

    @pl.when(plast_ref[p_id] == 1)
    def _():
        outs = [(acc_scr[r] / jnp.maximum(l_scr[r], 1e-30)).T for r in range(GQA_R)]
        o_ref[...] = jnp.concatenate(outs, axis=1)


def flash_attention(q, kv_t, k_stream, v_stream, n_b, t, window=None, sel=None):
    tq, tk = ATT_TQ, ATT_TK
    nq = t // tq
    first_tile = lambda qi: 0 if window is None else max(0, qi * tq - (window - 1)) // tk
    pairs = [(qi, kt) for qi in range(nq) for kt in range(first_tile(qi), (qi * tq + tq - 1) // tk + 1)]
    pq = np.array([p[0] for p in pairs], np.int32)
    pk = np.array([p[1] for p in pairs], np.int32)
    pfirst = np.array([1 if i == 0 or pairs[i - 1][0] != p[0] else 0 for i, p in enumerate(pairs)], np.int32)
    plast = np.array([1 if i == len(pairs) - 1 or pairs[i + 1][0] != p[0] else 0 for i, p in enumerate(pairs)], np.int32)
    gw = GQA_R * HEAD_DIM

    def tile_map(stream):
        return lambda b, g, p, pq_r, pk_r, f_r, l_r: (b, stream, g, 0, pk_r[p])

    in_specs = [pl.BlockSpec((tq, gw), lambda b, g, p, pq_r, pk_r, f_r, l_r: (b * nq + pq_r[p], g)),
                pl.BlockSpec((1, 1, 1, HEAD_DIM, tk), tile_map(k_stream)),
                pl.BlockSpec((1, 1, 1, HEAD_DIM, tk), tile_map(v_stream))]
    args = [q, kv_t, kv_t]
    if sel is not None:
        n_sblk = sel.shape[2]
        emat = (np.arange(t)[:, None] // SEL_BLOCK == np.arange(n_sblk)[None, :]).astype(np.float32)
        in_specs += [pl.BlockSpec((1, 1, n_sblk, tq), lambda b, g, p, pq_r, pk_r, f_r, l_r: (b, g, 0, pq_r[p])),
                     pl.BlockSpec((tk, n_sblk), lambda b, g, p, pq_r, pk_r, f_r, l_r: (pk_r[p], 0))]
        args += [sel, jnp.asarray(emat, BF16)]
    grid_spec = pltpu.PrefetchScalarGridSpec(
        num_scalar_prefetch=4,
        grid=(n_b, KV_HEADS, len(pairs)),
        in_specs=in_specs,
        out_specs=pl.BlockSpec((tq, gw), lambda b, g, p, pq_r, pk_r, f_r, l_r: (b * nq + pq_r[p], g)),
        scratch_shapes=[pltpu.VMEM((GQA_R, 1, tq), F32), pltpu.VMEM((GQA_R, 1, tq), F32),
                        pltpu.VMEM((GQA_R, HEAD_DIM, tq), F32)],
    )
    return pl.pallas_call(
        functools.partial(_flash_kernel, window=window, use_sel=sel is not None, tq=tq, tk=tk),
        grid_spec=grid_spec,
        out_shape=jax.ShapeDtypeStruct((n_b * t, N_HEADS * HEAD_DIM), F32),
        compiler_params=_cparams(("arbitrary", "arbitrary", "arbitrary")),
        name="attn_sel" if sel is not None else "attn_win",
    )(jnp.asarray(pq), jnp.asarray(pk), jnp.asarray(pfirst), jnp.asarray(plast), *args)


def _attn_pre_kernel(h_ref, ln_ref, wq_ref, wkv_ref, rc_ref, rm_ref, rp_ref, qg_o, qr_o, kvt_o):
    x = h_ref[...]
    xs = x * lax.rsqrt(jnp.mean(x * x, axis=-1, keepdims=True) + RMS_EPS)
    qg = jnp.dot((xs * ln_ref[0:1, :]).astype(BF16), wq_ref[...], preferred_element_type=F32)
    kv = jnp.dot((xs * ln_ref[1:2, :]).astype(BF16), wkv_ref[...], preferred_element_type=F32)
    qg_o[...] = qg

    def rope(v):
        reps = v.shape[1] // LANES
        tile = lambda ref: jnp.concatenate([ref[...]] * reps, axis=1)
        n = v.shape[1]
        return v * tile(rc_ref) + pltpu.roll(v, n - ROPE_DIMS // 2, 1) * tile(rm_ref) + pltpu.roll(v, ROPE_DIMS // 2, 1) * tile(rp_ref)

    n_q = N_HEADS * HEAD_DIM
    sw = KV_HEADS * HEAD_DIM
    qr_o[...] = rope(qg[:, :n_q])
    kv_r = jnp.concatenate([kv[:, :2 * sw], rope(kv[:, 2 * sw:3 * sw]), kv[:, 3 * sw:4 * sw],
                            rope(kv[:, 4 * sw:5 * sw]), kv[:, 5 * sw:]], axis=1)
    kvt_o[0] = kv_r.T


def attn_pre(h, n_b, t, ln_q, ln_kv, w_qg, w_kv):
    n, dm = h.shape
    tm = RW_TM
    tiles = t // tm
    half = ROPE_DIMS // 2
    inv = jnp.power(ROPE_THETA, -jnp.arange(half, dtype=F32) * (2.0 / ROPE_DIMS))
    ang = jnp.arange(t, dtype=F32)[:, None] * inv[None, :]
    cos, sin = jnp.cos(ang), jnp.sin(ang)
    pad = jnp.zeros((t, HEAD_DIM - ROPE_DIMS), F32)
    zero = jnp.zeros((t, half), F32)
    head = lambda parts: jnp.concatenate(parts, axis=1)
    two = lambda x: jnp.concatenate([x, x], axis=1)
    rc = two(head([cos, cos, pad + 1.0]))
    rm = two(head([-sin, zero, pad]))
    rp = two(head([zero, sin, pad]))
    nq_cols = w_qg.shape[1]
    nkv = w_kv.shape[1]
    tab = pl.BlockSpec((tm, LANES), lambda i: (i % tiles, 0))
    full = lambda a: pl.BlockSpec(a.shape, lambda i: (0,) * a.ndim)
    consts = [jnp.stack([ln_q, ln_kv]), w_qg.astype(BF16), w_kv.astype(BF16)]
    return pl.pallas_call(
        _attn_pre_kernel,
        grid=(n // tm,),
        in_specs=[pl.BlockSpec((tm, dm), lambda i: (i, 0))] + [full(a) for a in consts] + [tab, tab, tab],
        out_specs=[pl.BlockSpec((tm, nq_cols), lambda i: (i, 0)),
                   pl.BlockSpec((tm, N_HEADS * HEAD_DIM), lambda i: (i, 0)),
                   pl.BlockSpec((1, nkv, tm), lambda i: (i // tiles, 0, i % tiles))],
        out_shape=[jax.ShapeDtypeStruct((n, nq_cols), F32), jax.ShapeDtypeStruct((n, N_HEADS * HEAD_DIM), F32),
                   jax.ShapeDtypeStruct((n_b, nkv, t), F32)],
        compiler_params=_cparams(("arbitrary",)),
        name="attn_pre",
    )(h, *consts, rc, rm, rp)


def _cmp_kernel(q_ref, k_ref, v_ref, ov_ref, o_ref, imp_ref, *, n_cmp):
    q = q_ref[0] * ATT_SCALE
    s = lax.dot_general(q, k_ref[0], _NT, preferred_element_type=F32)
    mask = lax.broadcasted_iota(jnp.int32, s.shape, 1) < n_cmp
    s = jnp.where(mask, s, NEG_BIG)
    m = jnp.max(s, axis=-1, keepdims=True)
    e = jnp.where(mask, jnp.exp(s - m), 0.0)
    p = (e / jnp.maximum(jnp.sum(e, axis=-1, keepdims=True), 1e-30)).astype(BF16)
    o_ref[0] = jnp.dot(p, v_ref[0], preferred_element_type=F32)
    imp_ref[0] = jnp.dot(p, ov_ref[...], preferred_element_type=F32)


def cmp_attention_sample(q, kc, vc, ov, n_cmp):
    bg, rq, hd = q.shape
    nc = kc.shape[1]
    nb = ov.shape[1]
    return pl.pallas_call(
        functools.partial(_cmp_kernel, n_cmp=n_cmp),
        grid=(bg,),
        in_specs=[pl.BlockSpec((1, rq, hd), lambda b: (b, 0, 0)),
                  pl.BlockSpec((1, nc, hd), lambda b: (b, 0, 0)),
                  pl.BlockSpec((1, nc, hd), lambda b: (b, 0, 0)),
                  pl.BlockSpec((nc, nb), lambda b: (0, 0))],
        out_specs=[pl.BlockSpec((1, rq, hd), lambda b: (b, 0, 0)),
                   pl.BlockSpec((1, rq, nb), lambda b: (b, 0, 0))],
        out_shape=[jax.ShapeDtypeStruct((bg, rq, hd), F32),
                   jax.ShapeDtypeStruct((bg, rq, nb), F32)],
        compiler_params=_cparams(("arbitrary",)),
        name="attn_cmp_sample",
    )(q, kc, vc, ov)


PAGES_PER_STEP = 8


def _page_compress_kernel(pt_ref, *rest, n_pages, page, pb):
    c_refs = rest[:pb]
    wbd_ref, pe_ref, w2_ref, o_ref, a_scr = rest[pb:]
    p = pl.program_id(1)
    nsub = n_pages * page // CMP_STRIDE
    for i in range(pb):
        row0 = pl.multiple_of((p * pb + i) * page, page)
        for s in range(2):
            for gp in range(KV_HEADS // 2):
                x = c_refs[i][0, s, 2 * gp:2 * gp + 2].reshape(2 * HEAD_DIM, page)
                a_scr[s, gp, pl.ds(row0, page), :] = x.T

    @pl.when(p == n_pages // pb - 1)
    def _():
        hid = pe_ref.shape[-1]
        for s in range(2):
            for gp in range(KV_HEADS // 2):
                acc = jnp.zeros((nsub, 4 * hid), F32)
                for j in range(CMP_STRIDE):
                    lhs = a_scr[s, gp, pl.ds(j, nsub, stride=CMP_STRIDE), :].astype(BF16)
                    acc = acc + jnp.dot(lhs, wbd_ref[s, j], preferred_element_type=F32)
                for g2 in range(2):
                    first = acc[:, 2 * g2 * hid:(2 * g2 + 1) * hid]
                    second = acc[:, (2 * g2 + 1) * hid:(2 * g2 + 2) * hid]
                    pre = first + pltpu.roll(second, nsub - 1, 0) + pe_ref[s]
                    act = jax.nn.gelu(pre).astype(BF16)
                    o_ref[s, 0, 2 * gp + g2] = jnp.dot(act, w2_ref[s], preferred_element_type=F32).astype(BF16)


def page_compress(cache_t, page_table, cmp_pe, cmp_w1, cmp_b1, cmp_w2):
    n_b, n_pages = page_table.shape
    page = cache_t.shape[-1]
    nsub = n_pages * page // CMP_STRIDE
    hid = cmp_w1.shape[-1]
    pb = PAGES_PER_STEP
    assert n_pages % pb == 0
    w1r = cmp_w1.reshape(2, 2, CMP_STRIDE, HEAD_DIM, hid).transpose(0, 2, 3, 1, 4)
    w1r = w1r.reshape(2, CMP_STRIDE, HEAD_DIM, 2 * hid)
    zeros = jnp.zeros_like(w1r)
    wbd = jnp.concatenate([jnp.concatenate([w1r, zeros], axis=-1), jnp.concatenate([zeros, w1r], axis=-1)], axis=2)
    pe_term = (jnp.einsum('sjd,sjdc->sc', cmp_pe, cmp_w1.reshape(2, CMP_BLOCK, HEAD_DIM, hid)) + cmp_b1)[:, None, :]
    def page_map(i):
        return lambda b, p, pt: (pt[b * n_pages + p * pb + i], 0, 0, 0, 0)

    grid_spec = pltpu.PrefetchScalarGridSpec(
        num_scalar_prefetch=1,
        grid=(n_b, n_pages // pb),
        in_specs=[pl.BlockSpec((1, 2, KV_HEADS, HEAD_DIM, page), page_map(i)) for i in range(pb)] + [
            pl.BlockSpec((2, CMP_STRIDE, 2 * HEAD_DIM, 4 * hid), lambda b, p, pt: (0, 0, 0, 0)),
            pl.BlockSpec((2, 1, hid), lambda b, p, pt: (0, 0, 0)),
            pl.BlockSpec((2, hid, HEAD_DIM), lambda b, p, pt: (0, 0, 0))],
        out_specs=pl.BlockSpec((2, 1, KV_HEADS, nsub, HEAD_DIM), lambda b, p, pt: (0, b, 0, 0, 0)),
        scratch_shapes=[pltpu.VMEM((2, KV_HEADS // 2, n_pages * page, 2 * HEAD_DIM), F32)],
    )
    return pl.pallas_call(
        functools.partial(_page_compress_kernel, n_pages=n_pages, page=page, pb=pb),
        grid_spec=grid_spec,
        out_shape=jax.ShapeDtypeStruct((2, n_b, KV_HEADS, nsub, HEAD_DIM), BF16),
        compiler_params=_cparams(("arbitrary", "arbitrary")),
        name="page_compress",
    )(page_table.reshape(-1), *([cache_t] * pb), wbd.astype(BF16), pe_term, cmp_w2.astype(BF16))


def _decode_attn_kernel(phys_ref, blk_ref, flag_ref, q_ref, kn_ref, vn_ref, *rest, mode, past, steps, window, page):
    kt_refs = rest[:steps]
    vt_refs = rest[steps:2 * steps]
    o_ref = rest[2 * steps]
    idx = pl.program_id(0) * KV_HEADS + pl.program_id(1)
    rnd = lambda x: x.astype(BF16).astype(F32)
    q = q_ref[0, 0] * ATT_SCALE

    kn = rnd(kn_ref[0, 0])
    vn = rnd(vn_ref[0, 0])
    s_new = jnp.sum(q.astype(F32) * kn, axis=-1, keepdims=True)
    on = flag_ref[idx] == 1
    m = jnp.where(on, s_new, NEG_BIG)
    scs, masks = [], []
    for s in range(steps):
        kt = kt_refs[s][0, 0, 0].astype(BF16)
        tile = kt.shape[-1]
        sc = jnp.dot(q, kt, preferred_element_type=F32)
        lane = lax.broadcasted_iota(jnp.int32, sc.shape, 1)
        if mode == "sel":
            blk = blk_ref[idx * steps + s]
            per_page = page // SEL_BLOCK
            pos = (blk // per_page) * page + lane
            mask = (jnp.right_shift(pos, SEL_BLOCK.bit_length() - 1) == blk) & (pos <= past) & (blk >= 0)
        else:
            pos = past - tile + lane
            mask = (pos > past - window) & (pos <= past)
        sc = jnp.where(mask, sc, NEG_BIG)
        m = jnp.maximum(m, jnp.max(sc, axis=-1, keepdims=True))
        scs.append(sc)
        masks.append(mask)
    p_new = jnp.where(on, jnp.exp(s_new - m), 0.0)
    l = p_new
    acc = rnd(p_new) * vn
    for s in range(steps):
        p = jnp.where(masks[s], jnp.exp(scs[s] - m), 0.0)
        l = l + jnp.sum(p, axis=-1, keepdims=True)
        vt = vt_refs[s][0, 0, 0].astype(BF16)
        acc = acc + lax.dot_general(p.astype(BF16), vt, _NT, preferred_element_type=F32)
    o_ref[0, 0] = acc / jnp.maximum(l, 1e-30)


def decode_attention(q, k_new, v_new, kv_t, streams, phys, blk, flag, mode, past, steps, window=None, page=None):
    n_b, n_g, rows, hd = q.shape
    tile = kv_t.shape[-1]
    ks, vs = streams

    def kmap(st, s):
        if mode == "sel":
            return lambda b, g, ph, bl, fl: (ph[(b * n_g + g) * steps + s], st, g, 0, 0)
        return lambda b, g, ph, bl, fl: (b, st, g, 0, 0)

    tiles = ([pl.BlockSpec((1, 1, 1, hd, tile), kmap(ks, s)) for s in range(steps)]
             + [pl.BlockSpec((1, 1, 1, hd, tile), kmap(vs, s)) for s in range(steps)])
    grid_spec = pltpu.PrefetchScalarGridSpec(
        num_scalar_prefetch=3,
        grid=(n_b, n_g),
        in_specs=[pl.BlockSpec((1, 1, rows, hd), lambda b, g, ph, bl, fl: (b, g, 0, 0)),
                  pl.BlockSpec((1, 1, 1, hd), lambda b, g, ph, bl, fl: (b, g, 0, 0)),
                  pl.BlockSpec((1, 1, 1, hd), lambda b, g, ph, bl, fl: (b, g, 0, 0))] + tiles,
        out_specs=pl.BlockSpec((1, 1, rows, hd), lambda b, g, ph, bl, fl: (b, g, 0, 0)),
    )
    return pl.pallas_call(
        functools.partial(_decode_attn_kernel, mode=mode, past=past, steps=steps, window=window, page=page),
        grid_spec=grid_spec,
        out_shape=jax.ShapeDtypeStruct((n_b, n_g, rows, hd), F32),
        compiler_params=_cparams(("arbitrary", "arbitrary")),
        name="attn_decode_" + mode,
    )(phys, blk, flag, q, k_new, v_new, *([kv_t] * (2 * steps)))


def rmsnorm(x, g):
    y = x * lax.rsqrt(jnp.mean(x * x, axis=-1, keepdims=True) + RMS_EPS)
    return y * g


def rope(x, pos):
    half = ROPE_DIMS // 2
    inv = jnp.power(ROPE_THETA, -jnp.arange(half, dtype=F32) * (2.0 / ROPE_DIMS))
    ang = pos.astype(F32)[:, None] * inv[None, :]
    cos = jnp.cos(ang)[:, None, :]
    sin = jnp.sin(ang)[:, None, :]
    x1 = x[..., :half]
    x2 = x[..., half:ROPE_DIMS]
    return jnp.concatenate([x1 * cos - x2 * sin, x2 * cos + x1 * sin, x[..., ROPE_DIMS:]], axis=-1)


def compress(sub, pe, w1, b1, w2, n_b, nsub):
    half = CMP_STRIDE * HEAD_DIM
    w1cat = jnp.concatenate([w1[:half], w1[half:]], axis=1)
    hidden = w1.shape[1]
    part = mm(sub, w1cat).reshape(n_b, nsub, KV_HEADS, 2, hidden)
    pe_term = jnp.einsum('jd,jdc->c', pe, w1.reshape(CMP_BLOCK, HEAD_DIM, hidden)) + b1
    pre = part[:, :nsub - 1, :, 0] + part[:, 1:, :, 1] + pe_term
    act = jax.nn.gelu(pre)
    return mm(act.reshape(-1, hidden), w2).reshape(n_b, nsub - 1, KV_HEADS, HEAD_DIM)


def overlap_matrix(n_cmp, n_sblk, nc_pad, nb_pad):
    cs = np.arange(n_cmp) * CMP_STRIDE
    ce = cs + CMP_BLOCK - 1
    ss = np.arange(n_sblk) * SEL_BLOCK
    ov = ((cs[:, None] < ss[None, :] + SEL_BLOCK) & (ce[:, None] >= ss[None, :])).astype(np.float32)
    out = np.zeros((nc_pad, nb_pad), np.float32)
    out[:n_cmp, :n_sblk] = ov
    return out


def block_scores(imp, q_pos, n_sblk):
    j = jnp.arange(n_sblk)[None, :]
    qblk = (q_pos // SEL_BLOCK)[:, None]
    allowed = j <= qblk
    forced = (j == 0) | ((j > qblk - N_LOCAL_BLOCKS) & allowed)
    score = jnp.where(forced, FORCE_SCORE, imp)
    return jnp.where(allowed, score, -jnp.inf)


def nsa_prompt(qg, q_r, kv_t, n_b, t, cmp_pe, cmp_w1, cmp_b1, cmp_w2):
    nsub = t // CMP_STRIDE
    n_cmp = nsub - 1
    n_sblk = t // SEL_BLOCK

    def sub_blocks(s):
        x = kv_t[:, s].reshape(n_b, KV_HEADS, HEAD_DIM, nsub, CMP_STRIDE)
        return x.transpose(0, 3, 1, 4, 2).reshape(-1, CMP_STRIDE * HEAD_DIM)

    kc = compress(sub_blocks(0), cmp_pe[0], cmp_w1[0], cmp_b1[0], cmp_w2[0], n_b, nsub)
    vc = compress(sub_blocks(1), cmp_pe[1], cmp_w1[1], cmp_b1[1], cmp_w2[1], n_b, nsub)
    nc_pad = _round_up(n_cmp, LANES)
    kc = jnp.pad(kc, ((0, 0), (0, nc_pad - n_cmp), (0, 0), (0, 0)))
    vc = jnp.pad(vc, ((0, 0), (0, nc_pad - n_cmp), (0, 0), (0, 0)))
    ovt = jnp.asarray(overlap_matrix(n_cmp, n_sblk, nc_pad, n_sblk).T, BF16)
    o_cmp, sel = cmp_select(qg, kc.transpose(0, 2, 1, 3).astype(BF16), vc.transpose(0, 2, 3, 1).astype(BF16),
                            ovt, n_b, t, n_cmp, n_sblk, qg.shape[0])
    o_sel = flash_attention(q_r, kv_t, 2, 3, n_b, t, sel=sel)
    o_win = flash_attention(q_r, kv_t, 4, 5, n_b, t, window=WINDOW)
    return o_cmp, o_sel, o_win


def nsa_sample(qg, kv, cache_kv, cache_win, page_table, cmp_pe, cmp_w1, cmp_b1, cmp_w2):
    n_b, n_pages = page_table.shape
    n_phys, page = cache_kv.shape[:2]
    past = n_pages * page
    pos = jnp.full((1,), past, jnp.int32)
    q = qg[:, :N_HEADS * HEAD_DIM].reshape(n_b, 1, N_HEADS, HEAD_DIM)
    kv = kv.reshape(n_b, 1, N_KV_STREAMS, KV_HEADS, HEAD_DIM)
    k_sel = rope(kv[:, :, 2], pos)
    k_win = rope(kv[:, :, 4], pos)
    new_rows = jnp.stack([kv[:, :, 0], kv[:, :, 1], k_sel, kv[:, :, 3]], axis=2)
    new_win = jnp.stack([k_win, kv[:, :, 5]], axis=2)
    n_keep = min(WINDOW, cache_win.shape[1] + 1)
    win_state = jnp.concatenate([cache_win[:, cache_win.shape[1] + 1 - n_keep:], new_win], axis=1)

    cache_t = cache_kv.transpose(0, 2, 3, 4, 1)
    win_t = cache_win.transpose(0, 2, 3, 4, 1)

    l_tot = past + 1
    nsub = l_tot // CMP_STRIDE
    n_cmp = nsub - 1
    n_sblk = -(-l_tot // SEL_BLOCK)
    assert nsub * CMP_STRIDE == past and page % SEL_BLOCK == 0 and cache_win.shape[1] == WINDOW
    kvc = page_compress(cache_t, page_table, cmp_pe, cmp_w1, cmp_b1, cmp_w2)

    bg = n_b * KV_HEADS
    rq = 16
    qpad = lambda x: jnp.pad(x.reshape(n_b, KV_HEADS, GQA_R, HEAD_DIM), ((0, 0), (0, 0), (0, rq - GQA_R), (0, 0))).astype(BF16)

    nb_pad = _round_up(n_sblk, 8)
    ov = jnp.asarray(overlap_matrix(n_cmp, n_sblk, nsub, nb_pad), BF16)
    o_cmp, imp = cmp_attention_sample(qpad(q).reshape(bg, rq, HEAD_DIM), kvc[0].reshape(bg, nsub, HEAD_DIM),
                                      kvc[1].reshape(bg, nsub, HEAD_DIM), ov, n_cmp)
    imp = imp[:, :GQA_R, :n_sblk].sum(axis=1)
    score = block_scores(imp[:, None, :], pos, n_sblk)
    top_s, top_i = lax.top_k(score[:, 0], min(N_SEL, n_sblk))
    valid = jnp.isfinite(top_s)

    n_past_blk = past // SEL_BLOCK
    per_page = page // SEL_BLOCK
    n_k = top_i.shape[-1]
    in_past = valid & (top_i < n_past_blk)
    blk = jnp.where(in_past, top_i, -1).astype(jnp.int32)
    logical = jnp.clip(top_i, 0, n_past_blk - 1) // per_page
    phys = jnp.take_along_axis(page_table, logical.reshape(n_b, KV_HEADS * n_k), axis=1).reshape(bg, n_k)
    phys = jnp.where(in_past, phys, 0).astype(jnp.int32)
    new_flag = jnp.any(valid & (top_i >= n_past_blk), axis=-1).astype(jnp.int32)
    q_r = qpad(rope(q, pos))
    o_sel = decode_attention(q_r, new_rows[:, 0, 2][:, :, None, :], new_rows[:, 0, 3][:, :, None, :], cache_t, (2, 3),
                             phys.reshape(-1), blk.reshape(-1), new_flag, "sel", past, n_k, page=page)
    zeros = jnp.zeros((bg,), jnp.int32)
    o_win = decode_attention(q_r, new_win[:, 0, 0][:, :, None, :], new_win[:, 0, 1][:, :, None, :], win_t, (0, 1),
                             zeros, zeros, jnp.ones((bg,), jnp.int32), "win", past, 1, window=WINDOW)

    heads = lambda o: o.reshape(n_b, KV_HEADS, -1, HEAD_DIM)[:, :, :GQA_R].reshape(n_b, N_HEADS * HEAD_DIM)
    return (heads(o_cmp), heads(o_sel), heads(o_win)), new_rows, win_state


def rwkv_layer(h_p, h_t, n_p, t_p, n_s, state_wkv, state_shift, ln, mu, w_r, w_k, w_v, w0, w1, w2, a0, a1, a2, g1, g2,
               k_k, k_a, r_k, gn_w, gn_b):
    np_rows = n_p * t_p
    weights = (ln, mu, w_r, w_k, w_v, w0, w1, w2, a0, a1, a2, g1, g2)
    pre_p = rwkv_pre(h_p, jnp.zeros((RW_TM, D_MODEL), F32), np_rows, t_p, *weights)
    pre_t = rwkv_pre(h_t, jnp.pad(state_shift, ((0, RW_TM - n_s), (0, 0))), 0, RW_TM, *weights)

    def lanes(x, nb, t):
        return x.reshape(nb, t, RW_HEADS, RW_HEAD).transpose(1, 3, 0, 2).reshape(t, RW_HEAD, nb * RW_HEADS)

    def rows(o, nb, t):
        return o.reshape(t, RW_HEAD, nb, RW_HEADS).transpose(2, 0, 3, 1).reshape(nb * t, D_MODEL)

    head_vec = lambda w, nb: jnp.tile(w.reshape(RW_HEADS, RW_HEAD).T, (1, nb))
    vecs = lambda nb: [head_vec(w, nb) for w in (k_k, k_a, r_k, gn_w, gn_b)]
    s0_p = jnp.zeros((RW_HEAD, RW_HEAD, n_p * RW_HEADS), F32)
    o_p, st_p = rwkv_scan(*[lanes(x, n_p, t_p) for x in pre_p[:5]], *vecs(n_p), s0_p)
    s0_s = state_wkv.transpose(3, 2, 0, 1).reshape(RW_HEAD, RW_HEAD, n_s * RW_HEADS)
    o_s, st_s = rwkv_scan(*[lanes(x[:n_s], n_s, 1) for x in pre_t[:5]], *vecs(n_s), s0_s, dot_dtype=BF16)
    o_t = jnp.pad(rows(o_s, n_s, 1), ((0, RW_TM - n_s), (0, 0)))
    wkv_p = st_p.reshape(RW_HEAD, RW_HEAD, n_p, RW_HEADS).transpose(2, 3, 1, 0)
    wkv_s = st_s.reshape(RW_HEAD, RW_HEAD, n_s, RW_HEADS).transpose(2, 3, 1, 0)
    return (rows(o_p, n_p, t_p), pre_p[5], pre_p[6]), (o_t, pre_t[5], pre_t[6]), wkv_p, wkv_s


def kernel(x_prompt, x_sample, state_wkv, state_shift, cache_kv, cache_win, page_table, ln_mix, ln_ffn, ln_kv, ln_out, rw_mu, rw_wr, rw_wk, rw_wv, rw_wo, rw_w0, rw_w1, rw_w2, rw_a0, rw_a1, rw_a2, rw_g1, rw_g2, rw_kk, rw_ka, rw_rk, rw_lnw, rw_lnb, w_kv, cmp_pe, cmp_w1, cmp_b1, cmp_w2, nsa_wqg, nsa_wo, moe_wrg, moe_brg, moe_wre, moe_bre, moe_wgate, moe_wup, moe_wdown):
    n_p, t_p, dm = x_prompt.shape
    n_s = x_sample.shape[0]
    assert x_sample.shape[1] == 1
    np_rows = n_p * t_p
    h_p = x_prompt.reshape(np_rows, dm)
    h_t = jnp.pad(x_sample.reshape(n_s, dm), ((0, RW_TM - n_s), (0, 0)))
    sizes = (np_rows, RW_TM)

    def moe(fronts, layer):
        ys = hmoe(jnp.concatenate([f[1] for f in fronts]), jnp.concatenate([f[2] for f in fronts]), sizes,
                  moe_brg[layer], moe_bre[layer], moe_wgate[layer], moe_wup[layer], moe_wdown[layer])
        return [f[0] + y for f, y in zip(fronts, ys)]

    mix_p, mix_t, wkv_p, wkv_s = rwkv_layer(h_p, h_t, n_p, t_p, n_s, state_wkv[0], state_shift[0], ln_mix[0], rw_mu[0],
                                            rw_wr[0], rw_wk[0], rw_wv[0], rw_w0[0], rw_w1[0], rw_w2[0], rw_a0[0], rw_a1[0],
                                            rw_a2[0], rw_g1[0], rw_g2[0], rw_kk[0], rw_ka[0], rw_rk[0], rw_lnw[0], rw_lnb[0])
    p_shift = mix_p[2].reshape(n_p, t_p, dm)[:, -1][None]
    s_shift = mix_t[2][:n_s][None]
    w_router = router_weight(moe_wrg[0], moe_wre[0])
    h_p, h_t = moe([mix_out([o], g, hh, rw_wo[0], ln_ffn[0], w_router)
                    for (o, g, _), hh in ((mix_p, h_p), (mix_t, h_t))], 0)

    n_q = N_HEADS * HEAD_DIM
    gate_cols = _round_up(nsa_wqg.shape[-1] - n_q, LANES)
    w_qg = jnp.pad(nsa_wqg[0], ((0, 0), (0, n_q + gate_cols - nsa_wqg.shape[-1])))
    qg_p, qr_p, kvt = attn_pre(h_p, n_p, t_p, ln_mix[1], ln_kv, w_qg, w_kv)
    kvt = kvt.reshape(n_p, N_KV_STREAMS, KV_HEADS, HEAD_DIM, t_p)
    o_p = nsa_prompt(qg_p, qr_p, kvt, n_p, t_p, cmp_pe, cmp_w1, cmp_b1, cmp_w2)
    p_rows = kvt[:, :4].transpose(0, 4, 1, 2, 3)
    p_win = kvt[:, 4:, :, :, t_p - min(WINDOW, t_p):].transpose(0, 4, 1, 2, 3)
    qg_t = mm(rmsnorm(h_t, ln_mix[1]), w_qg)
    kv_t = mm(rmsnorm(h_t, ln_kv), w_kv)
    o_s, s_rows, s_win = nsa_sample(qg_t[:n_s, :nsa_wqg.shape[-1]], kv_t[:n_s], cache_kv, cache_win,
                                    page_table, cmp_pe, cmp_w1, cmp_b1, cmp_w2)
    o_t = [jnp.pad(x, ((0, RW_TM - n_s), (0, 0))) for x in o_s]
    expand = [jnp.asarray((np.arange(gate_cols)[:, None] == 3 * (np.arange(n_q)[None, :] // HEAD_DIM) + br)
                          .astype(np.float32), BF16) for br in range(3)]
    w_router = router_weight(moe_wrg[1], moe_wre[1])
    h_p, h_t = moe([mix_out(list(br), qg, hh, nsa_wo[0], ln_ffn[1], w_router, gate_expand=expand, gate_col=n_q // LANES)
                    for br, qg, hh in ((o_p, qg_p, h_p), (o_t, qg_t, h_t))], 1)

    y_prompt = rmsnorm(h_p, ln_out).reshape(n_p, t_p, dm)
    y_sample = rmsnorm(h_t[:n_s], ln_out).reshape(n_s, 1, dm)
    return (y_prompt, y_sample, wkv_p[None], p_shift, p_rows, p_win, wkv_s[None], s_shift, s_rows, s_win)
```

```python
import functools

import numpy as np
import jax
import jax.numpy as jnp
from jax import lax
from jax.experimental import pallas as pl
from jax.experimental.pallas import tpu as pltpu

F32 = jnp.float32
BF16 = jnp.bfloat16

D_MODEL = 1024
RMS_EPS = 1e-6
RW_HEAD = 64
RW_HEADS = D_MODEL // RW_HEAD
RW_GN_EPS = 64e-5
N_HEADS = 16
HEAD_DIM = 64
KV_HEADS = 4
GQA_R = N_HEADS // KV_HEADS
ROPE_DIMS = HEAD_DIM // 4
ROPE_THETA = 500000.0
N_KV_STREAMS = 6
CMP_BLOCK = 32
CMP_STRIDE = 16
SEL_BLOCK = 64
N_SEL = 16
N_LOCAL_BLOCKS = 2
FORCE_SCORE = 1.0e4
WINDOW = 512
N_GROUPS = 4
EXPERTS_PER_GROUP = 8
N_EXPERTS = N_GROUPS * EXPERTS_PER_GROUP
D_EXPERT = 256

LANES = 128
VMEM_LIMIT = 48 << 20
NEG_BIG = -1e30
M_FLOOR = -1e29
ATT_SCALE = HEAD_DIM ** -0.5
TOKEN_TILE = 512


def _round_up(x, m):
    return (x + m - 1) // m * m


def _cparams(sem):
    return pltpu.CompilerParams(dimension_semantics=sem, vmem_limit_bytes=VMEM_LIMIT)


_NT = (((1,), (1,)), ((), ()))


def _mm_kernel(x_ref, w_ref, o_ref):
    o_ref[...] = jnp.dot(x_ref[...].astype(BF16), w_ref[...], preferred_element_type=F32)


def mm(x, w, tm=TOKEN_TILE):
    m, k = x.shape
    n = w.shape[1]
    tm = min(tm, _round_up(m, 8))
    mp = _round_up(m, tm)
    if mp != m:
        x = jnp.pad(x, ((0, mp - m), (0, 0)))
    out = pl.pallas_call(
        _mm_kernel,
        grid=(mp // tm,),
        in_specs=[pl.BlockSpec((tm, k), lambda i: (i, 0)),
                  pl.BlockSpec((k, n), lambda i: (0, 0))],
        out_specs=pl.BlockSpec((tm, n), lambda i: (i, 0)),
        out_shape=jax.ShapeDtypeStruct((mp, n), F32),
        compiler_params=_cparams(("arbitrary",)),
        name="mm",
    )(x, w.astype(BF16))
    return out[:m] if mp != m else out


SCAN_TC = 32


def _scan_kernel(r_ref, d_ref, kin_ref, a_ref, v_ref, kkp_ref, kap_ref, rk_ref, gw_ref, gb_ref, s0_ref, o_ref, st_ref,
                 s_scr, n_ref, k_ref, b_ref, *, tc, dot_dtype):
    c = pl.program_id(1)
    rnd = lambda x: x.astype(dot_dtype).astype(F32)
    halves = [(h0, h0 + RW_HEAD // 2) for h0 in (0, RW_HEAD // 2)]

    @pl.when(c == 0)
    def _():
        s_scr[...] = s0_ref[...]

    def step(t, carry):
        vv = v_ref[t]
        k_t = kin_ref[t]
        a_t = a_ref[t]
        kk = k_t * kkp_ref[...]
        kk = kk / jnp.maximum(jnp.sqrt(jnp.sum(kk * kk, axis=0, keepdims=True)), 1e-12)
        k2 = k_t * (1.0 + (a_t - 1.0) * kap_ref[...])
        n_ref[...] = kk
        k_ref[...] = k2
        b_ref[...] = kk * a_t
        outs = []
        for lo, hi in halves:
            parts = [jnp.zeros((hi - lo, LANES), F32)] * 2
            for j in range(RW_HEAD):
                parts[j % 2] = parts[j % 2] + rnd(s_scr[j, lo:hi]) * rnd(n_ref[pl.ds(j, 1), :])
            sa = -(parts[0] + parts[1])
            vh = vv[lo:hi]
            out = jnp.zeros((hi - lo, LANES), F32)
            for j in range(RW_HEAD):
                sn = (s_scr[j, lo:hi] * d_ref[t, pl.ds(j, 1), :] + sa * b_ref[pl.ds(j, 1), :]
                      + vh * k_ref[pl.ds(j, 1), :])
                s_scr[j, lo:hi] = sn
                out = out + rnd(sn) * rnd(r_ref[t, pl.ds(j, 1), :])
            outs.append(out)
        out = jnp.concatenate(outs, axis=0)
        mean = jnp.mean(out, axis=0, keepdims=True)
        cen = out - mean
        var = jnp.mean(cen * cen, axis=0, keepdims=True)
        bonus = jnp.sum(r_ref[t] * k2 * rk_ref[...], axis=0, keepdims=True) * vv
        o_ref[t] = cen * lax.rsqrt(var + RW_GN_EPS) * gw_ref[...] + gb_ref[...] + bonus
        return carry

    lax.fori_loop(0, tc, step, 0)

    @pl.when(c == pl.num_programs(1) - 1)
    def _():
        st_ref[...] = s_scr[...]


def rwkv_scan(r, d, k, a, v, kkp, kap, rk, gw, gb, s0, dot_dtype=F32):
    t, _, l = r.shape
    tc = min(SCAN_TC, t)
    assert t % tc == 0 and l % LANES == 0
    seq = pl.BlockSpec((tc, RW_HEAD, LANES), lambda g, c: (c, 0, g))
    vec = pl.BlockSpec((RW_HEAD, LANES), lambda g, c: (0, g))
    st = pl.BlockSpec((RW_HEAD, RW_HEAD, LANES), lambda g, c: (0, 0, g))
    return pl.pallas_call(
        functools.partial(_scan_kernel, tc=tc, dot_dtype=dot_dtype),
        grid=(l // LANES, t // tc),
        in_specs=[seq] * 5 + [vec] * 5 + [st],
        out_specs=[seq, st],
        out_shape=[jax.ShapeDtypeStruct((t, RW_HEAD, l), F32),
                   jax.ShapeDtypeStruct((RW_HEAD, RW_HEAD, l), F32)],
        scratch_shapes=[pltpu.VMEM((RW_HEAD, RW_HEAD, LANES), F32)] + [pltpu.VMEM((RW_HEAD, LANES), F32)] * 3,
        compiler_params=_cparams(("arbitrary", "arbitrary")),
        name="rwkv_scan",
    )(r, d, k, a, v, kkp, kap, rk, gw, gb, s0)


RW_TM = 256


def _split_dot(x, w):
    hi = x.astype(BF16)
    lo = (x - hi.astype(F32)).astype(BF16)
    return jnp.dot(hi, w, preferred_element_type=F32) + jnp.dot(lo, w, preferred_element_type=F32)


def _rwkv_pre_kernel(x_ref, ps_ref, ln_ref, mu_ref, vec_ref, wr_ref, wk_ref, wv_ref, w1_ref, w2_ref, a1_ref, a2_ref,
                     g1_ref, g2_ref,
                     r_o, d_o, k_o, a_o, v_o, g_o, xn_o, carry_scr, *, tiles_per_seq, n_seq_tiles):
    i = pl.program_id(0)
    tm = x_ref.shape[0]

    @pl.when(i == 0)
    def _():
        carry_scr[...] = jnp.zeros_like(carry_scr)

    x = x_ref[...]
    xn = x * lax.rsqrt(jnp.mean(x * x, axis=-1, keepdims=True) + RMS_EPS) * ln_ref[...]
    xn_o[...] = xn
    first = jnp.where(i % tiles_per_seq == 0, jnp.zeros_like(carry_scr[...]), carry_scr[...])
    row0 = lax.broadcasted_iota(jnp.int32, xn.shape, 0) == 0
    prev = jnp.where(row0, first, pltpu.roll(xn, 1, 0))
    prev = jnp.where(i >= n_seq_tiles, ps_ref[...], prev)
    carry_scr[...] = xn[tm - 1:tm, :]
    xx = prev - xn
    mix = lambda c: (xn + xx * mu_ref[c:c + 1, :]).astype(BF16)
    dot = lambda a, w_ref: jnp.dot(a, w_ref[...], preferred_element_type=F32)
    w0, a0 = (vec_ref[c:c + 1, :] for c in range(2))
    r_o[...] = dot(mix(0), wr_ref)
    k_o[...] = dot(mix(2), wk_ref)
    v_o[...] = dot(mix(3), wv_ref)
    w_log = -jax.nn.softplus(-(w0 + dot(jnp.tanh(dot(mix(1), w1_ref)).astype(BF16), w2_ref))) - 0.5
    d_o[...] = jnp.exp(-jnp.exp(w_log))
    a_o[...] = jax.nn.sigmoid(a0 + dot(dot(mix(4), a1_ref).astype(BF16), a2_ref))
    g_o[...] = dot(jax.nn.sigmoid(dot(mix(5), g1_ref)).astype(BF16), g2_ref)


def rwkv_pre(h, prev_single, n_seq_rows, t, ln, mu, w_r, w_k, w_v, w0, w1, w2, a0, a1, a2, g1, g2):
    n, dm = h.shape
    tm = RW_TM
    assert n % tm == 0 and t % tm == 0 and n_seq_rows % tm == 0 and prev_single.shape == (tm, dm)
    row = pl.BlockSpec((tm, dm), lambda i: (i, 0))
    full = lambda a: pl.BlockSpec(a.shape, lambda i: (0,) * a.ndim)
    bf = lambda w: w.astype(BF16)
    consts = [prev_single, ln[None, :], mu, jnp.stack([w0, a0]), bf(w_r), bf(w_k), bf(w_v), bf(w1), bf(w2),
              bf(a1), bf(a2), bf(g1), bf(g2)]
    return pl.pallas_call(
        functools.partial(_rwkv_pre_kernel, tiles_per_seq=t // tm, n_seq_tiles=n_seq_rows // tm),
        grid=(n // tm,),
        in_specs=[row] + [full(a) for a in consts],
        out_specs=[row] * 7,
        out_shape=[jax.ShapeDtypeStruct((n, dm), F32)] * 7,
        scratch_shapes=[pltpu.VMEM((1, dm), F32)],
        compiler_params=_cparams(("arbitrary",)),
        name="rwkv_pre",
    )(h, *consts)


def _mix_out_kernel(*refs, n_branch, use_gates):
    o_refs = refs[:n_branch]
    gate_ref, h_ref, wo_ref, ln_ref, wr_ref = refs[n_branch:n_branch + 5]
    e_refs = refs[n_branch + 5:n_branch + 5 + (n_branch if use_gates else 0)]
    h_o, xp_o, lg_o = refs[-3:]
    if use_gates:
        sig = jax.nn.sigmoid(gate_ref[...])
        mixed = sum(_split_dot(sig, e_refs[b][...]) * o_refs[b][...] for b in range(n_branch))
    else:
        mixed = o_refs[0][...] * gate_ref[...]
    h = h_ref[...] + jnp.dot(mixed.astype(BF16), wo_ref[...], preferred_element_type=F32)
    h_o[...] = h
    xn = h * lax.rsqrt(jnp.mean(h * h, axis=-1, keepdims=True) + RMS_EPS) * ln_ref[...]
    xb = xn.astype(BF16)
    lg_o[...] = jnp.dot(xb, wr_ref[...], preferred_element_type=F32)
    u = lax.bitcast_convert_type(xb.astype(F32), jnp.uint32)
    half = u.shape[1] // 2
    xp_o[...] = (u[:, :half] >> 16) | (u[:, half:] & jnp.uint32(0xFFFF0000))


def mix_out(branches, gate, h, w_o, ln, w_router, gate_expand=None, gate_col=0):
    n, dm = h.shape
    tm = RW_TM
    nb = len(branches)
    row = lambda a: pl.BlockSpec((tm, a.shape[1]), lambda i: (i, 0))
    full = lambda a: pl.BlockSpec(a.shape, lambda i: (0,) * a.ndim)
    consts = [w_o.astype(BF16), ln[None, :], w_router.astype(BF16)] + list(gate_expand or [])
    gate_spec = row(gate) if gate_expand is None else pl.BlockSpec((tm, LANES), lambda i: (i, gate_col))
    return pl.pallas_call(
        functools.partial(_mix_out_kernel, n_branch=nb, use_gates=gate_expand is not None),
        grid=(n // tm,),
        in_specs=[row(a) for a in branches] + [gate_spec, row(h)] + [full(a) for a in consts],
        out_specs=[pl.BlockSpec((tm, dm), lambda i: (i, 0)), pl.BlockSpec((tm, dm // 2), lambda i: (i, 0)),
                   pl.BlockSpec((tm, LANES), lambda i: (i, 0))],
        out_shape=[jax.ShapeDtypeStruct((n, dm), F32), jax.ShapeDtypeStruct((n, dm // 2), jnp.uint32),
                   jax.ShapeDtypeStruct((n, LANES), F32)],
        compiler_params=_cparams(("arbitrary",)),
        name="mix_out",
    )(*branches, gate, h, *consts)


MOE_TM = 256
MOE_VMEM_LIMIT = 56 << 20


def _moe_kernel(te_ref, nt_ref, tok_ref, x_ref, c_ref, wgu_ref, wd_ref, o_ref, xt_scr):
    i = pl.program_id(0)

    @pl.when(i < nt_ref[0])
    def _():
        base = i * MOE_TM

        def gather_rows(g8, carry):
            r0 = pl.multiple_of(g8 * 8, 8)
            for u in range(8):
                xt_scr[pl.ds(r0 + u, 1), :] = x_ref[pl.ds(tok_ref[base + r0 + u], 1), :]
            return carry

        lax.fori_loop(0, MOE_TM // 8, gather_rows, 0)
        u = xt_scr[...]
        lo = lax.bitcast_convert_type(u << 16, F32).astype(BF16)
        hi = lax.bitcast_convert_type(u & jnp.uint32(0xFFFF0000), F32).astype(BF16)
        x = jnp.concatenate([lo, hi], axis=1)
        h = jnp.dot(x, wgu_ref[0], preferred_element_type=F32)
        g = h[:, :D_EXPERT]
        u2 = h[:, D_EXPERT:]
        hid = g * jax.nn.sigmoid(g) * u2 * c_ref[...]
        o_ref[...] = jnp.dot(hid.astype(BF16), wd_ref[0], preferred_element_type=F32)

    @pl.when(i >= nt_ref[0])
    def _():
        o_ref[...] = jnp.zeros_like(o_ref)


def moe_gmm(xp, row_tok, cs, tile_expert, n_tiles_used, wgu, wd):
    n, half = xp.shape
    dm = 2 * half
    p = row_tok.shape[0]
    nt = p // MOE_TM

    def row_map(i, te, ntu, tok):
        return (jnp.minimum(i, ntu[0] - 1), 0)

    grid_spec = pltpu.PrefetchScalarGridSpec(
        num_scalar_prefetch=3,
        grid=(nt,),
        in_specs=[pl.BlockSpec((n, half), lambda i, te, ntu, tok: (0, 0), pipeline_mode=pl.Buffered(1)),
                  pl.BlockSpec((MOE_TM, 1), row_map),
                  pl.BlockSpec((1, dm, 2 * D_EXPERT), lambda i, te, ntu, tok: (te[i], 0, 0)),
                  pl.BlockSpec((1, D_EXPERT, dm), lambda i, te, ntu, tok: (te[i], 0, 0))],
        out_specs=pl.BlockSpec((MOE_TM, dm), lambda i, te, ntu, tok: (i, 0)),
        scratch_shapes=[pltpu.VMEM((MOE_TM, half), jnp.uint32)],
    )
    return pl.pallas_call(
        _moe_kernel,
        grid_spec=grid_spec,
        out_shape=jax.ShapeDtypeStruct((p, dm), F32),
        compiler_params=pltpu.CompilerParams(dimension_semantics=("arbitrary",), vmem_limit_bytes=MOE_VMEM_LIMIT),
        name="moe_gmm",
    )(tile_expert, n_tiles_used, row_tok, xp, cs, wgu, wd)


def router_weight(w_rg, w_re):
    w_r = jnp.concatenate([w_rg, w_re], axis=1)
    return jnp.pad(w_r, ((0, 0), (0, LANES - w_r.shape[1])))


def hmoe(xp, logits, set_sizes, b_rg, b_re, w_gate, w_up, w_down):
    n = xp.shape[0]
    dm = 2 * xp.shape[1]
    lg = logits[:, :N_GROUPS] + b_rg
    le = (logits[:, N_GROUPS:N_GROUPS + N_EXPERTS] + b_re).reshape(n, N_GROUPS, EXPERTS_PER_GROUP)
    g_sel = jnp.argmax(lg, axis=-1)
    p_g = jnp.max(jax.nn.softmax(lg, axis=-1), axis=-1)
    le_g = jnp.take_along_axis(le, g_sel[:, None, None], axis=1)[:, 0]
    top_p, top_i = lax.top_k(jax.nn.softmax(le_g, axis=-1), 2)
    w_top = p_g[:, None] * top_p / jnp.sum(top_p, axis=-1, keepdims=True)
    eid = (g_sel[:, None] * EXPERTS_PER_GROUP + top_i).astype(jnp.int32).reshape(-1)
    cw = w_top.reshape(-1)

    n2 = 2 * n
    onehot = (eid[:, None] == jnp.arange(N_EXPERTS, dtype=jnp.int32)[None, :]).astype(jnp.int32)
    counts = jnp.sum(onehot, axis=0)
    rank = jnp.take_along_axis(jnp.cumsum(onehot, axis=0) - onehot, eid[:, None], axis=1)[:, 0]
    tiles_per = (counts + MOE_TM - 1) // MOE_TM
    tile_end = jnp.cumsum(tiles_per)
    tile_start = tile_end - tiles_per
    pos = tile_start[eid] * MOE_TM + rank
    p_rows = _round_up(n2 + N_EXPERTS * (MOE_TM - 1), MOE_TM)
    nt = p_rows // MOE_TM
    n_used = tile_end[-1].astype(jnp.int32)
    tile_ids = jnp.arange(nt, dtype=jnp.int32)
    tile_expert = jnp.sum((tile_end[None, :] <= jnp.minimum(tile_ids, n_used - 1)[:, None]).astype(jnp.int32), axis=1)

    assert n < 2 ** 24
    tok = (jnp.arange(n2, dtype=jnp.int32) // 2).astype(F32)
    rows = jnp.zeros((p_rows, 2), F32).at[pos].set(jnp.stack([tok, cw], axis=1))
    row_tok = rows[:, 0].astype(jnp.int32)
    wgu = jnp.concatenate([w_gate, w_up], axis=-1).reshape(N_EXPERTS, dm, 2 * D_EXPERT).astype(BF16)
    wd = w_down.reshape(N_EXPERTS, D_EXPERT, dm).astype(BF16)
    ys = moe_gmm(xp, row_tok, rows[:, 1:2], tile_expert, n_used.reshape(1), wgu, wd)
    pos2 = pos.reshape(n, 2)
    bounds = np.cumsum([0] + list(set_sizes))
    return [ys[pos2[a:b, 0]] + ys[pos2[a:b, 1]] for a, b in zip(bounds[:-1], bounds[1:])]


ATT_TQ = 256
ATT_TK = 512


def _cmp_sel_kernel(q_ref, kc_ref, vct_ref, ovt_ref, o_ref, sel_ref, *, n_cmp, n_sblk, tq):
    t0 = pl.program_id(2) * tq
    ncp = kc_ref.shape[2]
    n_io = lax.broadcasted_iota(jnp.int32, (ncp, tq), 0)
    t_io = lax.broadcasted_iota(jnp.int32, (ncp, tq), 1) + t0
    mask = (n_io * CMP_STRIDE + (CMP_BLOCK - 1) <= t_io) & (n_io < n_cmp)
    kc = kc_ref[0, 0]
    vct = vct_ref[0, 0]
    ovt = ovt_ref[...]
    imp = jnp.zeros((n_sblk, tq), F32)
    outs = []
    for r in range(GQA_R):
        qr = (q_ref[:, r * HEAD_DIM:(r + 1) * HEAD_DIM] * ATT_SCALE).astype(BF16)
        st = lax.dot_general(kc, qr, _NT, preferred_element_type=F32)
        st = jnp.where(mask, st, NEG_BIG)
        m = jnp.max(st, axis=0, keepdims=True)
        e = jnp.where(mask, jnp.exp(st - m), 0.0)
        p = (e / jnp.maximum(jnp.sum(e, axis=0, keepdims=True), 1e-30)).astype(BF16)
        outs.append(jnp.dot(vct, p, preferred_element_type=F32).T)
        imp = imp + jnp.dot(ovt, p, preferred_element_type=F32)
    o_ref[...] = jnp.concatenate(outs, axis=1)

    j_io = lax.broadcasted_iota(jnp.int32, (n_sblk, tq), 0)
    qblk = jnp.right_shift(lax.broadcasted_iota(jnp.int32, (n_sblk, tq), 1) + t0, SEL_BLOCK.bit_length() - 1)
    allowed = j_io <= qblk
    forced = (j_io == 0) | ((j_io > qblk - N_LOCAL_BLOCKS) & allowed)
    score = jnp.where(forced, FORCE_SCORE, imp)
    score = jnp.where(allowed, score, -jnp.inf)
    rank = jnp.zeros((n_sblk, tq), jnp.int32)
    for jp in range(n_sblk):
        row = score[jp:jp + 1, :]
        beats = (row > score) | ((row == score) & (j_io > jp))
        rank = rank + jnp.where(beats, 1, 0)
    sel = (rank < N_SEL) & allowed
    sel_ref[0, 0] = jnp.where(sel, 1.0, 0.0).astype(BF16)


def cmp_select(q, kc, vct, ovt, n_b, t, n_cmp, n_sblk, n_out):
    assert SEL_BLOCK & (SEL_BLOCK - 1) == 0
    tq = ATT_TQ
    nq = t // tq
    ncp = kc.shape[2]
    gw = GQA_R * HEAD_DIM
    return pl.pallas_call(
        functools.partial(_cmp_sel_kernel, n_cmp=n_cmp, n_sblk=n_sblk, tq=tq),
        grid=(n_b, KV_HEADS, nq),
        in_specs=[pl.BlockSpec((tq, gw), lambda b, g, qi: (b * nq + qi, g)),
                  pl.BlockSpec((1, 1, ncp, HEAD_DIM), lambda b, g, qi: (b, g, 0, 0)),
                  pl.BlockSpec((1, 1, HEAD_DIM, ncp), lambda b, g, qi: (b, g, 0, 0)),
                  pl.BlockSpec((n_sblk, ncp), lambda b, g, qi: (0, 0))],
        out_specs=[pl.BlockSpec((tq, gw), lambda b, g, qi: (b * nq + qi, g)),
                   pl.BlockSpec((1, 1, n_sblk, tq), lambda b, g, qi: (b, g, 0, qi))],
        out_shape=[jax.ShapeDtypeStruct((n_out, N_HEADS * HEAD_DIM), F32),
                   jax.ShapeDtypeStruct((n_b, KV_HEADS, n_sblk, t), BF16)],
        compiler_params=_cparams(("arbitrary", "arbitrary", "arbitrary")),
        name="attn_cmp_select",
    )(q, kc, vct, ovt)


def _flash_kernel(pq_ref, pk_ref, pfirst_ref, plast_ref, q_ref, kt_ref, vt_ref, *rest, window, use_sel, tq, tk):
    if use_sel:
        sel_ref, et_ref, o_ref, m_scr, l_scr, acc_scr = rest
    else:
        o_ref, m_scr, l_scr, acc_scr = rest
    p_id = pl.program_id(2)

    @pl.when(pfirst_ref[p_id] == 1)
    def _():
        m_scr[...] = jnp.full_like(m_scr, M_FLOOR)
        l_scr[...] = jnp.zeros_like(l_scr)
        acc_scr[...] = jnp.zeros_like(acc_scr)

    off = pq_ref[p_id] * tq - pk_ref[p_id] * tk
    diff = lax.broadcasted_iota(jnp.int32, (tk, tq), 0) - lax.broadcasted_iota(jnp.int32, (tk, tq), 1)
    mask = diff <= off
    if window is not None:
        mask = mask & (diff > off - window)
    if use_sel:
        selm = jnp.dot(et_ref[...], sel_ref[0, 0], preferred_element_type=F32)
        mask = mask & (selm > 0.5)
    bias = jnp.where(mask, 0.0, NEG_BIG)
    k = kt_ref[0, 0, 0].astype(BF16).T
    vt = vt_ref[0, 0, 0].astype(BF16)
    heads = range(GQA_R)
    qs = [(q_ref[:, r * HEAD_DIM:(r + 1) * HEAD_DIM] * ATT_SCALE).astype(BF16) for r in heads]
    sts = [lax.dot_general(k, qs[r], _NT, preferred_element_type=F32) + bias for r in heads]
    m_prev = [m_scr[r] for r in heads]
    m_new = [jnp.maximum(m_prev[r], jnp.max(sts[r], axis=0, keepdims=True)) for r in heads]
    ps = [jnp.exp(sts[r] - m_new[r]) for r in heads]
    alphas = [jnp.exp(m_prev[r] - m_new[r]) for r in heads]
    for r in heads:
        l_scr[r] = alphas[r] * l_scr[r] + jnp.sum(ps[r], axis=0, keepdims=True)
        acc_scr[r] = alphas[r] * acc_scr[r] + jnp.dot(vt, ps[r].astype(BF16), preferred_element_type=F32)
        m_scr[r] = m_new[r]

    @pl.when(plast_ref[p_id] == 1)
    def _():
        outs = [(acc_scr[r] / jnp.maximum(l_scr[r], 1e-30)).T for r in range(GQA_R)]
        o_ref[...] = jnp.concatenate(outs, axis=1)


def flash_attention(q, kv_t, k_stream, v_stream, n_b, t, window=None, sel=None):
    tq, tk = ATT_TQ, ATT_TK
    nq = t // tq
    first_tile = lambda qi: 0 if window is None else max(0, qi * tq - (window - 1)) // tk
    pairs = [(qi, kt) for qi in range(nq) for kt in range(first_tile(qi), (qi * tq + tq - 1) // tk + 1)]
    pq = np.array([p[0] for p in pairs], np.int32)
    pk = np.array([p[1] for p in pairs], np.int32)
    pfirst = np.array([1 if i == 0 or pairs[i - 1][0] != p[0] else 0 for i, p in enumerate(pairs)], np.int32)
    plast = np.array([1 if i == len(pairs) - 1 or pairs[i + 1][0] != p[0] else 0 for i, p in enumerate(pairs)], np.int32)
    gw = GQA_R * HEAD_DIM

    def tile_map(stream):
        return lambda b, g, p, pq_r, pk_r, f_r, l_r: (b, stream, g, 0, pk_r[p])

    in_specs = [pl.BlockSpec((tq, gw), lambda b, g, p, pq_r, pk_r, f_r, l_r: (b * nq + pq_r[p], g)),
                pl.BlockSpec((1, 1, 1, HEAD_DIM, tk), tile_map(k_stream)),
                pl.BlockSpec((1, 1, 1, HEAD_DIM, tk), tile_map(v_stream))]
    args = [q, kv_t, kv_t]
    if sel is not None:
        n_sblk = sel.shape[2]
        emat = (np.arange(t)[:, None] // SEL_BLOCK == np.arange(n_sblk)[None, :]).astype(np.float32)
        in_specs += [pl.BlockSpec((1, 1, n_sblk, tq), lambda b, g, p, pq_r, pk_r, f_r, l_r: (b, g, 0, pq_r[p])),
                     pl.BlockSpec((tk, n_sblk), lambda b, g, p, pq_r, pk_r, f_r, l_r: (pk_r[p], 0))]
        args += [sel, jnp.asarray(emat, BF16)]
    grid_spec = pltpu.PrefetchScalarGridSpec(
        num_scalar_prefetch=4,
        grid=(n_b, KV_HEADS, len(pairs)),
        in_specs=in_specs,
        out_specs=pl.BlockSpec((tq, gw), lambda b, g, p, pq_r, pk_r, f_r, l_r: (b * nq + pq_r[p], g)),
        scratch_shapes=[pltpu.VMEM((GQA_R, 1, tq), F32), pltpu.VMEM((GQA_R, 1, tq), F32),
                        pltpu.VMEM((GQA_R, HEAD_DIM, tq), F32)],
    )
    return pl.pallas_call(
        functools.partial(_flash_kernel, window=window, use_sel=sel is not None, tq=tq, tk=tk),
        grid_spec=grid_spec,
        out_shape=jax.ShapeDtypeStruct((n_b * t, N_HEADS * HEAD_DIM), F32),
        compiler_params=_cparams(("arbitrary", "arbitrary", "arbitrary")),
        name="attn_sel" if sel is not None else "attn_win",
    )(jnp.asarray(pq), jnp.asarray(pk), jnp.asarray(pfirst), jnp.asarray(plast), *args)


def _attn_pre_kernel(h_ref, ln_ref, wq_ref, wkv_ref, rc_ref, rm_ref, rp_ref, qg_o, qr_o, kvt_o):
    x = h_ref[...]
    xs = x * lax.rsqrt(jnp.mean(x * x, axis=-1, keepdims=True) + RMS_EPS)
    qg = jnp.dot((xs * ln_ref[0:1, :]).astype(BF16), wq_ref[...], preferred_element_type=F32)
    kv = jnp.dot((xs * ln_ref[1:2, :]).astype(BF16), wkv_ref[...], preferred_element_type=F32)
    qg_o[...] = qg

    def rope(v):
        reps = v.shape[1] // LANES
        tile = lambda ref: jnp.concatenate([ref[...]] * reps, axis=1)
        n = v.shape[1]
        return v * tile(rc_ref) + pltpu.roll(v, n - ROPE_DIMS // 2, 1) * tile(rm_ref) + pltpu.roll(v, ROPE_DIMS // 2, 1) * tile(rp_ref)

    n_q = N_HEADS * HEAD_DIM
    sw = KV_HEADS * HEAD_DIM
    qr_o[...] = rope(qg[:, :n_q])
    kv_r = jnp.concatenate([kv[:, :2 * sw], rope(kv[:, 2 * sw:3 * sw]), kv[:, 3 * sw:4 * sw],
                            rope(kv[:, 4 * sw:5 * sw]), kv[:, 5 * sw:]], axis=1)
    kvt_o[0] = kv_r.T


def attn_pre(h, n_b, t, ln_q, ln_kv, w_qg, w_kv):
    n, dm = h.shape
    tm = RW_TM
    tiles = t // tm
    half = ROPE_DIMS // 2
    inv = jnp.power(ROPE_THETA, -jnp.arange(half, dtype=F32) * (2.0 / ROPE_DIMS))
    ang = jnp.arange(t, dtype=F32)[:, None] * inv[None, :]
    cos, sin = jnp.cos(ang), jnp.sin(ang)
    pad = jnp.zeros((t, HEAD_DIM - ROPE_DIMS), F32)
    zero = jnp.zeros((t, half), F32)
    head = lambda parts: jnp.concatenate(parts, axis=1)
    two = lambda x: jnp.concatenate([x, x], axis=1)
    rc = two(head([cos, cos, pad + 1.0]))
    rm = two(head([-sin, zero, pad]))
    rp = two(head([zero, sin, pad]))
    nq_cols = w_qg.shape[1]
    nkv = w_kv.shape[1]
    tab = pl.BlockSpec((tm, LANES), lambda i: (i % tiles, 0))
    full = lambda a: pl.BlockSpec(a.shape, lambda i: (0,) * a.ndim)
    consts = [jnp.stack([ln_q, ln_kv]), w_qg.astype(BF16), w_kv.astype(BF16)]
    return pl.pallas_call(
        _attn_pre_kernel,
        grid=(n // tm,),
        in_specs=[pl.BlockSpec((tm, dm), lambda i: (i, 0))] + [full(a) for a in consts] + [tab, tab, tab],
        out_specs=[pl.BlockSpec((tm, nq_cols), lambda i: (i, 0)),
                   pl.BlockSpec((tm, N_HEADS * HEAD_DIM), lambda i: (i, 0)),
                   pl.BlockSpec((1, nkv, tm), lambda i: (i // tiles, 0, i % tiles))],
        out_shape=[jax.ShapeDtypeStruct((n, nq_cols), F32), jax.ShapeDtypeStruct((n, N_HEADS * HEAD_DIM), F32),
                   jax.ShapeDtypeStruct((n_b, nkv, t), F32)],
        compiler_params=_cparams(("arbitrary",)),
        name="attn_pre",
    )(h, *consts, rc, rm, rp)


def _cmp_kernel(q_ref, k_ref, v_ref, ov_ref, o_ref, imp_ref, *, n_cmp):
    q = q_ref[0] * ATT_SCALE
    s = lax.dot_general(q, k_ref[0], _NT, preferred_element_type=F32)
    mask = lax.broadcasted_iota(jnp.int32, s.shape, 1) < n_cmp
    s = jnp.where(mask, s, NEG_BIG)
    m = jnp.max(s, axis=-1, keepdims=True)
    e = jnp.where(mask, jnp.exp(s - m), 0.0)
    p = (e / jnp.maximum(jnp.sum(e, axis=-1, keepdims=True), 1e-30)).astype(BF16)
    o_ref[0] = jnp.dot(p, v_ref[0], preferred_element_type=F32)
    imp_ref[0] = jnp.dot(p, ov_ref[...], preferred_element_type=F32)


def cmp_attention_sample(q, kc, vc, ov, n_cmp):
    bg, rq, hd = q.shape
    nc = kc.shape[1]
    nb = ov.shape[1]
    return pl.pallas_call(
        functools.partial(_cmp_kernel, n_cmp=n_cmp),
        grid=(bg,),
        in_specs=[pl.BlockSpec((1, rq, hd), lambda b: (b, 0, 0)),
                  pl.BlockSpec((1, nc, hd), lambda b: (b, 0, 0)),
                  pl.BlockSpec((1, nc, hd), lambda b: (b, 0, 0)),
                  pl.BlockSpec((nc, nb), lambda b: (0, 0))],
        out_specs=[pl.BlockSpec((1, rq, hd), lambda b: (b, 0, 0)),
                   pl.BlockSpec((1, rq, nb), lambda b: (b, 0, 0))],
        out_shape=[jax.ShapeDtypeStruct((bg, rq, hd), F32),
                   jax.ShapeDtypeStruct((bg, rq, nb), F32)],
        compiler_params=_cparams(("arbitrary",)),
        name="attn_cmp_sample",
    )(q, kc, vc, ov)


PAGES_PER_STEP = 8


def _page_compress_kernel(pt_ref, *rest, n_pages, page, pb):
    c_refs = rest[:pb]
    wbd_ref, pe_ref, w2_ref, o_ref, a_scr = rest[pb:]
    p = pl.program_id(1)
    nsub = n_pages * page // CMP_STRIDE
    for i in range(pb):
        row0 = pl.multiple_of((p * pb + i) * page, page)
        for s in range(2):
            for gp in range(KV_HEADS // 2):
                x = c_refs[i][0, s, 2 * gp:2 * gp + 2].reshape(2 * HEAD_DIM, page)
                a_scr[s, gp, pl.ds(row0, page), :] = x.T

    @pl.when(p == n_pages // pb - 1)
    def _():
        hid = pe_ref.shape[-1]
        for s in range(2):
            for gp in range(KV_HEADS // 2):
                acc = jnp.zeros((nsub, 4 * hid), F32)
                for j in range(CMP_STRIDE):
                    lhs = a_scr[s, gp, pl.ds(j, nsub, stride=CMP_STRIDE), :].astype(BF16)
                    acc = acc + jnp.dot(lhs, wbd_ref[s, j], preferred_element_type=F32)
                for g2 in range(2):
                    first = acc[:, 2 * g2 * hid:(2 * g2 + 1) * hid]
                    second = acc[:, (2 * g2 + 1) * hid:(2 * g2 + 2) * hid]
                    pre = first + pltpu.roll(second, nsub - 1, 0) + pe_ref[s]
                    act = jax.nn.gelu(pre).astype(BF16)
                    o_ref[s, 0, 2 * gp + g2] = jnp.dot(act, w2_ref[s], preferred_element_type=F32).astype(BF16)


def page_compress(cache_t, page_table, cmp_pe, cmp_w1, cmp_b1, cmp_w2):
    n_b, n_pages = page_table.shape
    page = cache_t.shape[-1]
    nsub = n_pages * page // CMP_STRIDE
    hid = cmp_w1.shape[-1]
    pb = PAGES_PER_STEP
    assert n_pages % pb == 0
    w1r = cmp_w1.reshape(2, 2, CMP_STRIDE, HEAD_DIM, hid).transpose(0, 2, 3, 1, 4)
    w1r = w1r.reshape(2, CMP_STRIDE, HEAD_DIM, 2 * hid)
    zeros = jnp.zeros_like(w1r)
    wbd = jnp.concatenate([jnp.concatenate([w1r, zeros], axis=-1), jnp.concatenate([zeros, w1r], axis=-1)], axis=2)
    pe_term = (jnp.einsum('sjd,sjdc->sc', cmp_pe, cmp_w1.reshape(2, CMP_BLOCK, HEAD_DIM, hid)) + cmp_b1)[:, None, :]
    def page_map(i):
        return lambda b, p, pt: (pt[b * n_pages + p * pb + i], 0, 0, 0, 0)

    grid_spec = pltpu.PrefetchScalarGridSpec(
        num_scalar_prefetch=1,
        grid=(n_b, n_pages // pb),
        in_specs=[pl.BlockSpec((1, 2, KV_HEADS, HEAD_DIM, page), page_map(i)) for i in range(pb)] + [
            pl.BlockSpec((2, CMP_STRIDE, 2 * HEAD_DIM, 4 * hid), lambda b, p, pt: (0, 0, 0, 0)),
            pl.BlockSpec((2, 1, hid), lambda b, p, pt: (0, 0, 0)),
            pl.BlockSpec((2, hid, HEAD_DIM), lambda b, p, pt: (0, 0, 0))],
        out_specs=pl.BlockSpec((2, 1, KV_HEADS, nsub, HEAD_DIM), lambda b, p, pt: (0, b, 0, 0, 0)),
        scratch_shapes=[pltpu.VMEM((2, KV_HEADS // 2, n_pages * page, 2 * HEAD_DIM), F32)],
    )
    return pl.pallas_call(
        functools.partial(_page_compress_kernel, n_pages=n_pages, page=page, pb=pb),
        grid_spec=grid_spec,
        out_shape=jax.ShapeDtypeStruct((2, n_b, KV_HEADS, nsub, HEAD_DIM), BF16),
        compiler_params=_cparams(("arbitrary", "arbitrary")),
        name="page_compress",
    )(page_table.reshape(-1), *([cache_t] * pb), wbd.astype(BF16), pe_term, cmp_w2.astype(BF16))


def _decode_attn_kernel(phys_ref, blk_ref, flag_ref, q_ref, kn_ref, vn_ref, *rest, mode, past, steps, window, page):
    kt_refs = rest[:steps]
    vt_refs = rest[steps:2 * steps]
    o_ref = rest[2 * steps]
    idx = pl.program_id(0) * KV_HEADS + pl.program_id(1)
    rnd = lambda x: x.astype(BF16).astype(F32)
    q = q_ref[0, 0] * ATT_SCALE

    kn = rnd(kn_ref[0, 0])
    vn = rnd(vn_ref[0, 0])
    s_new = jnp.sum(q.astype(F32) * kn, axis=-1, keepdims=True)
    on = flag_ref[idx] == 1
    m = jnp.where(on, s_new, NEG_BIG)
    scs, masks = [], []
    for s in range(steps):
        kt = kt_refs[s][0, 0, 0].astype(BF16)
        tile = kt.shape[-1]
        sc = jnp.dot(q, kt, preferred_element_type=F32)
        lane = lax.broadcasted_iota(jnp.int32, sc.shape, 1)
        if mode == "sel":
            blk = blk_ref[idx * steps + s]
            per_page = page // SEL_BLOCK
            pos = (blk // per_page) * page + lane
            mask = (jnp.right_shift(pos, SEL_BLOCK.bit_length() - 1) == blk) & (pos <= past) & (blk >= 0)
        else:
            pos = past - tile + lane
            mask = (pos > past - window) & (pos <= past)
        sc = jnp.where(mask, sc, NEG_BIG)
        m = jnp.maximum(m, jnp.max(sc, axis=-1, keepdims=True))
        scs.append(sc)
        masks.append(mask)
    p_new = jnp.where(on, jnp.exp(s_new - m), 0.0)
    l = p_new
    acc = rnd(p_new) * vn
    for s in range(steps):
        p = jnp.where(masks[s], jnp.exp(scs[s] - m), 0.0)
        l = l + jnp.sum(p, axis=-1, keepdims=True)
        vt = vt_refs[s][0, 0, 0].astype(BF16)
        acc = acc + lax.dot_general(p.astype(BF16), vt, _NT, preferred_element_type=F32)
    o_ref[0, 0] = acc / jnp.maximum(l, 1e-30)


def decode_attention(q, k_new, v_new, kv_t, streams, phys, blk, flag, mode, past, steps, window=None, page=None):
    n_b, n_g, rows, hd = q.shape
    tile = kv_t.shape[-1]
    ks, vs = streams

    def kmap(st, s):
        if mode == "sel":
            return lambda b, g, ph, bl, fl: (ph[(b * n_g + g) * steps + s], st, g, 0, 0)
        return lambda b, g, ph, bl, fl: (b, st, g, 0, 0)

    tiles = ([pl.BlockSpec((1, 1, 1, hd, tile), kmap(ks, s)) for s in range(steps)]
             + [pl.BlockSpec((1, 1, 1, hd, tile), kmap(vs, s)) for s in range(steps)])
    grid_spec = pltpu.PrefetchScalarGridSpec(
        num_scalar_prefetch=3,
        grid=(n_b, n_g),
        in_specs=[pl.BlockSpec((1, 1, rows, hd), lambda b, g, ph, bl, fl: (b, g, 0, 0)),
                  pl.BlockSpec((1, 1, 1, hd), lambda b, g, ph, bl, fl: (b, g, 0, 0)),
                  pl.BlockSpec((1, 1, 1, hd), lambda b, g, ph, bl, fl: (b, g, 0, 0))] + tiles,
        out_specs=pl.BlockSpec((1, 1, rows, hd), lambda b, g, ph, bl, fl: (b, g, 0, 0)),
    )
    return pl.pallas_call(
        functools.partial(_decode_attn_kernel, mode=mode, past=past, steps=steps, window=window, page=page),
        grid_spec=grid_spec,
        out_shape=jax.ShapeDtypeStruct((n_b, n_g, rows, hd), F32),
        compiler_params=_cparams(("arbitrary", "arbitrary")),
        name="attn_decode_" + mode,
    )(phys, blk, flag, q, k_new, v_new, *([kv_t] * (2 * steps)))


def rmsnorm(x, g):
    y = x * lax.rsqrt(jnp.mean(x * x, axis=-1, keepdims=True) + RMS_EPS)
    return y * g


def rope(x, pos):
    half = ROPE_DIMS // 2
    inv = jnp.power(ROPE_THETA, -jnp.arange(half, dtype=F32) * (2.0 / ROPE_DIMS))
    ang = pos.astype(F32)[:, None] * inv[None, :]
    cos = jnp.cos(ang)[:, None, :]
    sin = jnp.sin(ang)[:, None, :]
    x1 = x[..., :half]
    x2 = x[..., half:ROPE_DIMS]
    return jnp.concatenate([x1 * cos - x2 * sin, x2 * cos + x1 * sin, x[..., ROPE_DIMS:]], axis=-1)


def compress(sub, pe, w1, b1, w2, n_b, nsub):
    half = CMP_STRIDE * HEAD_DIM
    w1cat = jnp.concatenate([w1[:half], w1[half:]], axis=1)
    hidden = w1.shape[1]
    part = mm(sub, w1cat).reshape(n_b, nsub, KV_HEADS, 2, hidden)
    pe_term = jnp.einsum('jd,jdc->c', pe, w1.reshape(CMP_BLOCK, HEAD_DIM, hidden)) + b1
    pre = part[:, :nsub - 1, :, 0] + part[:, 1:, :, 1] + pe_term
    act = jax.nn.gelu(pre)
    return mm(act.reshape(-1, hidden), w2).reshape(n_b, nsub - 1, KV_HEADS, HEAD_DIM)


def overlap_matrix(n_cmp, n_sblk, nc_pad, nb_pad):
    cs = np.arange(n_cmp) * CMP_STRIDE
    ce = cs + CMP_BLOCK - 1
    ss = np.arange(n_sblk) * SEL_BLOCK
    ov = ((cs[:, None] < ss[None, :] + SEL_BLOCK) & (ce[:, None] >= ss[None, :])).astype(np.float32)
    out = np.zeros((nc_pad, nb_pad), np.float32)
    out[:n_cmp, :n_sblk] = ov
    return out


def block_scores(imp, q_pos, n_sblk):
    j = jnp.arange(n_sblk)[None, :]
    qblk = (q_pos // SEL_BLOCK)[:, None]
    allowed = j <= qblk
    forced = (j == 0) | ((j > qblk - N_LOCAL_BLOCKS) & allowed)
    score = jnp.where(forced, FORCE_SCORE, imp)
    return jnp.where(allowed, score, -jnp.inf)


def nsa_prompt(qg, q_r, kv_t, n_b, t, cmp_pe, cmp_w1, cmp_b1, cmp_w2):
    nsub = t // CMP_STRIDE
    n_cmp = nsub - 1
    n_sblk = t // SEL_BLOCK

    def sub_blocks(s):
        x = kv_t[:, s].reshape(n_b, KV_HEADS, HEAD_DIM, nsub, CMP_STRIDE)
        return x.transpose(0, 3, 1, 4, 2).reshape(-1, CMP_STRIDE * HEAD_DIM)

    kc = compress(sub_blocks(0), cmp_pe[0], cmp_w1[0], cmp_b1[0], cmp_w2[0], n_b, nsub)
    vc = compress(sub_blocks(1), cmp_pe[1], cmp_w1[1], cmp_b1[1], cmp_w2[1], n_b, nsub)
    nc_pad = _round_up(n_cmp, LANES)
    kc = jnp.pad(kc, ((0, 0), (0, nc_pad - n_cmp), (0, 0), (0, 0)))
    vc = jnp.pad(vc, ((0, 0), (0, nc_pad - n_cmp), (0, 0), (0, 0)))
    ovt = jnp.asarray(overlap_matrix(n_cmp, n_sblk, nc_pad, n_sblk).T, BF16)
    o_cmp, sel = cmp_select(qg, kc.transpose(0, 2, 1, 3).astype(BF16), vc.transpose(0, 2, 3, 1).astype(BF16),
                            ovt, n_b, t, n_cmp, n_sblk, qg.shape[0])
    o_sel = flash_attention(q_r, kv_t, 2, 3, n_b, t, sel=sel)
    o_win = flash_attention(q_r, kv_t, 4, 5, n_b, t, window=WINDOW)
    return o_cmp, o_sel, o_win


def nsa_sample(qg, kv, cache_kv, cache_win, page_table, cmp_pe, cmp_w1, cmp_b1, cmp_w2):
    n_b, n_pages = page_table.shape
    n_phys, page = cache_kv.shape[:2]
    past = n_pages * page
    pos = jnp.full((1,), past, jnp.int32)
    q = qg[:, :N_HEADS * HEAD_DIM].reshape(n_b, 1, N_HEADS, HEAD_DIM)
    kv = kv.reshape(n_b, 1, N_KV_STREAMS, KV_HEADS, HEAD_DIM)
    k_sel = rope(kv[:, :, 2], pos)
    k_win = rope(kv[:, :, 4], pos)
    new_rows = jnp.stack([kv[:, :, 0], kv[:, :, 1], k_sel, kv[:, :, 3]], axis=2)
    new_win = jnp.stack([k_win, kv[:, :, 5]], axis=2)
    n_keep = min(WINDOW, cache_win.shape[1] + 1)
    win_state = jnp.concatenate([cache_win[:, cache_win.shape[1] + 1 - n_keep:], new_win], axis=1)

    cache_t = cache_kv.transpose(0, 2, 3, 4, 1)
    win_t = cache_win.transpose(0, 2, 3, 4, 1)

    l_tot = past + 1
    nsub = l_tot // CMP_STRIDE
    n_cmp = nsub - 1
    n_sblk = -(-l_tot // SEL_BLOCK)
    assert nsub * CMP_STRIDE == past and page % SEL_BLOCK == 0 and cache_win.shape[1] == WINDOW
    kvc = page_compress(cache_t, page_table, cmp_pe, cmp_w1, cmp_b1, cmp_w2)

    bg = n_b * KV_HEADS
    rq = 16
    qpad = lambda x: jnp.pad(x.reshape(n_b, KV_HEADS, GQA_R, HEAD_DIM), ((0, 0), (0, 0), (0, rq - GQA_R), (0, 0))).astype(BF16)

    nb_pad = _round_up(n_sblk, 8)
    ov = jnp.asarray(overlap_matrix(n_cmp, n_sblk, nsub, nb_pad), BF16)
    o_cmp, imp = cmp_attention_sample(qpad(q).reshape(bg, rq, HEAD_DIM), kvc[0].reshape(bg, nsub, HEAD_DIM),
                                      kvc[1].reshape(bg, nsub, HEAD_DIM), ov, n_cmp)
    imp = imp[:, :GQA_R, :n_sblk].sum(axis=1)
    score = block_scores(imp[:, None, :], pos, n_sblk)
    top_s, top_i = lax.top_k(score[:, 0], min(N_SEL, n_sblk))
    valid = jnp.isfinite(top_s)

    n_past_blk = past // SEL_BLOCK
    per_page = page // SEL_BLOCK
    n_k = top_i.shape[-1]
    in_past = valid & (top_i < n_past_blk)
    blk = jnp.where(in_past, top_i, -1).astype(jnp.int32)
    logical = jnp.clip(top_i, 0, n_past_blk - 1) // per_page
    phys = jnp.take_along_axis(page_table, logical.reshape(n_b, KV_HEADS * n_k), axis=1).reshape(bg, n_k)
    phys = jnp.where(in_past, phys, 0).astype(jnp.int32)
    new_flag = jnp.any(valid & (top_i >= n_past_blk), axis=-1).astype(jnp.int32)
    q_r = qpad(rope(q, pos))
    o_sel = decode_attention(q_r, new_rows[:, 0, 2][:, :, None, :], new_rows[:, 0, 3][:, :, None, :], cache_t, (2, 3),
                             phys.reshape(-1), blk.reshape(-1), new_flag, "sel", past, n_k, page=page)
    zeros = jnp.zeros((bg,), jnp.int32)
    o_win = decode_attention(q_r, new_win[:, 0, 0][:, :, None, :], new_win[:, 0, 1][:, :, None, :], win_t, (0, 1),
                             zeros, zeros, jnp.ones((bg,), jnp.int32), "win", past, 1, window=WINDOW)

    heads = lambda o: o.reshape(n_b, KV_HEADS, -1, HEAD_DIM)[:, :, :GQA_R].reshape(n_b, N_HEADS * HEAD_DIM)
    return (heads(o_cmp), heads(o_sel), heads(o_win)), new_rows, win_state


def rwkv_layer(h_p, h_t, n_p, t_p, n_s, state_wkv, state_shift, ln, mu, w_r, w_k, w_v, w0, w1, w2, a0, a1, a2, g1, g2,
               k_k, k_a, r_k, gn_w, gn_b):
    np_rows = n_p * t_p
    weights = (ln, mu, w_r, w_k, w_v, w0, w1, w2, a0, a1, a2, g1, g2)
    pre_p = rwkv_pre(h_p, jnp.zeros((RW_TM, D_MODEL), F32), np_rows, t_p, *weights)
    pre_t = rwkv_pre(h_t, jnp.pad(state_shift, ((0, RW_TM - n_s), (0, 0))), 0, RW_TM, *weights)

    def lanes(x, nb, t):
        return x.reshape(nb, t, RW_HEADS, RW_HEAD).transpose(1, 3, 0, 2).reshape(t, RW_HEAD, nb * RW_HEADS)

    def rows(o, nb, t):
        return o.reshape(t, RW_HEAD, nb, RW_HEADS).transpose(2, 0, 3, 1).reshape(nb * t, D_MODEL)

    head_vec = lambda w, nb: jnp.tile(w.reshape(RW_HEADS, RW_HEAD).T, (1, nb))
    vecs = lambda nb: [head_vec(w, nb) for w in (k_k, k_a, r_k, gn_w, gn_b)]
    s0_p = jnp.zeros((RW_HEAD, RW_HEAD, n_p * RW_HEADS), F32)
    o_p, st_p = rwkv_scan(*[lanes(x, n_p, t_p) for x in pre_p[:5]], *vecs(n_p), s0_p)
    s0_s = state_wkv.transpose(3, 2, 0, 1).reshape(RW_HEAD, RW_HEAD, n_s * RW_HEADS)
    o_s, st_s = rwkv_scan(*[lanes(x[:n_s], n_s, 1) for x in pre_t[:5]], *vecs(n_s), s0_s, dot_dtype=BF16)
    o_t = jnp.pad(rows(o_s, n_s, 1), ((0, RW_TM - n_s), (0, 0)))
    wkv_p = st_p.reshape(RW_HEAD, RW_HEAD, n_p, RW_HEADS).transpose(2, 3, 1, 0)
    wkv_s = st_s.reshape(RW_HEAD, RW_HEAD, n_s, RW_HEADS).transpose(2, 3, 1, 0)
    return (rows(o_p, n_p, t_p), pre_p[5], pre_p[6]), (o_t, pre_t[5], pre_t[6]), wkv_p, wkv_s


def kernel(x_prompt, x_sample, state_wkv, state_shift, cache_kv, cache_win, page_table, ln_mix, ln_ffn, ln_kv, ln_out, rw_mu, rw_wr, rw_wk, rw_wv, rw_wo, rw_w0, rw_w1, rw_w2, rw_a0, rw_a1, rw_a2, rw_g1, rw_g2, rw_kk, rw_ka, rw_rk, rw_lnw, rw_lnb, w_kv, cmp_pe, cmp_w1, cmp_b1, cmp_w2, nsa_wqg, nsa_wo, moe_wrg, moe_brg, moe_wre, moe_bre, moe_wgate, moe_wup, moe_wdown):
    n_p, t_p, dm = x_prompt.shape
    n_s = x_sample.shape[0]
    assert x_sample.shape[1] == 1
    np_rows = n_p * t_p
    h_p = x_prompt.reshape(np_rows, dm)
    h_t = jnp.pad(x_sample.reshape(n_s, dm), ((0, RW_TM - n_s), (0, 0)))
    sizes = (np_rows, RW_TM)

    def moe(fronts, layer):
        ys = hmoe(jnp.concatenate([f[1] for f in fronts]), jnp.concatenate([f[2] for f in fronts]), sizes,
                  moe_brg[layer], moe_bre[layer], moe_wgate[layer], moe_wup[layer], moe_wdown[layer])
        return [f[0] + y for f, y in zip(fronts, ys)]

    mix_p, mix_t, wkv_p, wkv_s = rwkv_layer(h_p, h_t, n_p, t_p, n_s, state_wkv[0], state_shift[0], ln_mix[0], rw_mu[0],
                                            rw_wr[0], rw_wk[0], rw_wv[0], rw_w0[0], rw_w1[0], rw_w2[0], rw_a0[0], rw_a1[0],
                                            rw_a2[0], rw_g1[0], rw_g2[0], rw_kk[0], rw_ka[0], rw_rk[0], rw_lnw[0], rw_lnb[0])
    p_shift = mix_p[2].reshape(n_p, t_p, dm)[:, -1][None]
    s_shift = mix_t[2][:n_s][None]
    w_router = router_weight(moe_wrg[0], moe_wre[0])
    h_p, h_t = moe([mix_out([o], g, hh, rw_wo[0], ln_ffn[0], w_router)
                    for (o, g, _), hh in ((mix_p, h_p), (mix_t, h_t))], 0)

    n_q = N_HEADS * HEAD_DIM
    gate_cols = _round_up(nsa_wqg.shape[-1] - n_q, LANES)
    w_qg = jnp.pad(nsa_wqg[0], ((0, 0), (0, n_q + gate_cols - nsa_wqg.shape[-1])))
    qg_p, qr_p, kvt = attn_pre(h_p, n_p, t_p, ln_mix[1], ln_kv, w_qg, w_kv)
    kvt = kvt.reshape(n_p, N_KV_STREAMS, KV_HEADS, HEAD_DIM, t_p)
    o_p = nsa_prompt(qg_p, qr_p, kvt, n_p, t_p, cmp_pe, cmp_w1, cmp_b1, cmp_w2)
    p_rows = kvt[:, :4].transpose(0, 4, 1, 2, 3)
    p_win = kvt[:, 4:, :, :, t_p - min(WINDOW, t_p):].transpose(0, 4, 1, 2, 3)
    qg_t = mm(rmsnorm(h_t, ln_mix[1]), w_qg)
    kv_t = mm(rmsnorm(h_t, ln_kv), w_kv)
    o_s, s_rows, s_win = nsa_sample(qg_t[:n_s, :nsa_wqg.shape[-1]], kv_t[:n_s], cache_kv, cache_win,
                                    page_table, cmp_pe, cmp_w1, cmp_b1, cmp_w2)
    o_t = [jnp.pad(x, ((0, RW_TM - n_s), (0, 0))) for x in o_s]
    expand = [jnp.asarray((np.arange(gate_cols)[:, None] == 3 * (np.arange(n_q)[None, :] // HEAD_DIM) + br)
                          .astype(np.float32), BF16) for br in range(3)]
    w_router = router_weight(moe_wrg[1], moe_wre[1])
    h_p, h_t = moe([mix_out(list(br), qg, hh, nsa_wo[0], ln_ffn[1], w_router, gate_expand=expand, gate_col=n_q // LANES)
                    for br, qg, hh in ((o_p, qg_p, h_p), (o_t, qg_t, h_t))], 1)

    y_prompt = rmsnorm(h_p, ln_out).reshape(n_p, t_p, dm)
    y_sample = rmsnorm(h_t[:n_s], ln_out).reshape(n_s, 1, dm)
    return (y_prompt, y_sample, wkv_p[None], p_shift, p_rows, p_win, wkv_s[None], s_shift, s_rows, s_win)
```

```python
import functools

import numpy as np
import jax
import jax.numpy as jnp
from jax import lax
from jax.experimental import pallas as pl
from jax.experimental.pallas import tpu as pltpu

F32 = jnp.float32
BF16 = jnp.bfloat16

D_MODEL = 1024
RMS_EPS = 1e-6
RW_HEAD = 64
RW_HEADS = D_MODEL // RW_HEAD
RW_GN_EPS = 64e-5
N_HEADS = 16
HEAD_DIM = 64
KV_HEADS = 4
GQA_R = N_HEADS // KV_HEADS
ROPE_DIMS = HEAD_DIM // 4
ROPE_THETA = 500000.0
N_KV_STREAMS = 6
CMP_BLOCK = 32
CMP_STRIDE = 16
SEL_BLOCK = 64
N_SEL = 16
N_LOCAL_BLOCKS = 2
FORCE_SCORE = 1.0e4
WINDOW = 512
N_GROUPS = 4
EXPERTS_PER_GROUP = 8
N_EXPERTS = N_GROUPS * EXPERTS_PER_GROUP
D_EXPERT = 256

LANES = 128
VMEM_LIMIT = 48 << 20
NEG_BIG = -1e30
M_FLOOR = -1e29
ATT_SCALE = HEAD_DIM ** -0.5
TOKEN_TILE = 512


def _round_up(x, m):
    return (x + m - 1) // m * m


def _cparams(sem):
    return pltpu.CompilerParams(dimension_semantics=sem, vmem_limit_bytes=VMEM_LIMIT)


_NT = (((1,), (1,)), ((), ()))


def _mm_kernel(x_ref, w_ref, o_ref):
    o_ref[...] = jnp.dot(x_ref[...].astype(BF16), w_ref[...], preferred_element_type=F32)


def mm(x, w, tm=TOKEN_TILE):
    m, k = x.shape
    n = w.shape[1]
    tm = min(tm, _round_up(m, 8))
    mp = _round_up(m, tm)
    if mp != m:
        x = jnp.pad(x, ((0, mp - m), (0, 0)))
    out = pl.pallas_call(
        _mm_kernel,
        grid=(mp // tm,),
        in_specs=[pl.BlockSpec((tm, k), lambda i: (i, 0)),
                  pl.BlockSpec((k, n), lambda i: (0, 0))],
        out_specs=pl.BlockSpec((tm, n), lambda i: (i, 0)),
        out_shape=jax.ShapeDtypeStruct((mp, n), F32),
        compiler_params=_cparams(("arbitrary",)),
        name="mm",
    )(x, w.astype(BF16))
    return out[:m] if mp != m else out


SCAN_TC = 32


def _scan_kernel(r_ref, d_ref, kin_ref, a_ref, v_ref, kkp_ref, kap_ref, rk_ref, gw_ref, gb_ref, s0_ref, o_ref, st_ref,
                 s_scr, n_ref, k_ref, b_ref, *, tc, dot_dtype):
    c = pl.program_id(1)
    rnd = lambda x: x.astype(dot_dtype).astype(F32)
    halves = [(h0, h0 + RW_HEAD // 2) for h0 in (0, RW_HEAD // 2)]

    @pl.when(c == 0)
    def _():
        s_scr[...] = s0_ref[...]

    def step(t, carry):
        vv = v_ref[t]
        k_t = kin_ref[t]
        a_t = a_ref[t]
        kk = k_t * kkp_ref[...]
        kk = kk / jnp.maximum(jnp.sqrt(jnp.sum(kk * kk, axis=0, keepdims=True)), 1e-12)
        k2 = k_t * (1.0 + (a_t - 1.0) * kap_ref[...])
        n_ref[...] = kk
        k_ref[...] = k2
        b_ref[...] = kk * a_t
        outs = []
        for lo, hi in halves:
            parts = [jnp.zeros((hi - lo, LANES), F32)] * 2
            for j in range(RW_HEAD):
                parts[j % 2] = parts[j % 2] + rnd(s_scr[j, lo:hi]) * rnd(n_ref[pl.ds(j, 1), :])
            sa = -(parts[0] + parts[1])
            vh = vv[lo:hi]
            out = jnp.zeros((hi - lo, LANES), F32)
            for j in range(RW_HEAD):
                sn = (s_scr[j, lo:hi] * d_ref[t, pl.ds(j, 1), :] + sa * b_ref[pl.ds(j, 1), :]
                      + vh * k_ref[pl.ds(j, 1), :])
                s_scr[j, lo:hi] = sn
                out = out + rnd(sn) * rnd(r_ref[t, pl.ds(j, 1), :])
            outs.append(out)
        out = jnp.concatenate(outs, axis=0)
        mean = jnp.mean(out, axis=0, keepdims=True)
        cen = out - mean
        var = jnp.mean(cen * cen, axis=0, keepdims=True)
        bonus = jnp.sum(r_ref[t] * k2 * rk_ref[...], axis=0, keepdims=True) * vv
        o_ref[t] = cen * lax.rsqrt(var + RW_GN_EPS) * gw_ref[...] + gb_ref[...] + bonus
        return carry

    lax.fori_loop(0, tc, step, 0)

    @pl.when(c == pl.num_programs(1) - 1)
    def _():
        st_ref[...] = s_scr[...]


def rwkv_scan(r, d, k, a, v, kkp, kap, rk, gw, gb, s0, dot_dtype=F32):
    t, _, l = r.shape
    tc = min(SCAN_TC, t)
    assert t % tc == 0 and l % LANES == 0
    seq = pl.BlockSpec((tc, RW_HEAD, LANES), lambda g, c: (c, 0, g))
    vec = pl.BlockSpec((RW_HEAD, LANES), lambda g, c: (0, g))
    st = pl.BlockSpec((RW_HEAD, RW_HEAD, LANES), lambda g, c: (0, 0, g))
    return pl.pallas_call(
        functools.partial(_scan_kernel, tc=tc, dot_dtype=dot_dtype),
        grid=(l // LANES, t // tc),
        in_specs=[seq] * 5 + [vec] * 5 + [st],
        out_specs=[seq, st],
        out_shape=[jax.ShapeDtypeStruct((t, RW_HEAD, l), F32),
                   jax.ShapeDtypeStruct((RW_HEAD, RW_HEAD, l), F32)],
        scratch_shapes=[pltpu.VMEM((RW_HEAD, RW_HEAD, LANES), F32)] + [pltpu.VMEM((RW_HEAD, LANES), F32)] * 3,
        compiler_params=_cparams(("arbitrary", "arbitrary")),
        name="rwkv_scan",
    )(r, d, k, a, v, kkp, kap, rk, gw, gb, s0)


TIME_BLK = LANES


def _time_major_kernel(x_ref, o_ref):
    o_ref[...] = x_ref[...].reshape(RW_HEAD * LANES, TIME_BLK).T.reshape(TIME_BLK, RW_HEAD, LANES)


def _time_minor_kernel(x_ref, o_ref):
    o_ref[...] = x_ref[...].reshape(TIME_BLK, RW_HEAD * LANES).T.reshape(RW_HEAD, LANES, TIME_BLK)


def time_major(x):
    _, l, t = x.shape
    return pl.pallas_call(
        _time_major_kernel,
        grid=(l // LANES, t // TIME_BLK),
        in_specs=[pl.BlockSpec((RW_HEAD, LANES, TIME_BLK), lambda g, c: (0, g, c))],
        out_specs=pl.BlockSpec((TIME_BLK, RW_HEAD, LANES), lambda g, c: (c, 0, g)),
        out_shape=jax.ShapeDtypeStruct((t, RW_HEAD, l), F32),
        compiler_params=_cparams(("arbitrary", "arbitrary")),
        name="time_major",
    )(x)


def time_minor(x):
    t, _, l = x.shape
    return pl.pallas_call(
        _time_minor_kernel,
        grid=(l // LANES, t // TIME_BLK),
        in_specs=[pl.BlockSpec((TIME_BLK, RW_HEAD, LANES), lambda g, c: (c, 0, g))],
        out_specs=pl.BlockSpec((RW_HEAD, LANES, TIME_BLK), lambda g, c: (0, g, c)),
        out_shape=jax.ShapeDtypeStruct((RW_HEAD, l, t), F32),
        compiler_params=_cparams(("arbitrary", "arbitrary")),
        name="time_minor",
    )(x)


RW_TM = 256


def _split_dot(x, w):
    hi = x.astype(BF16)
    lo = (x - hi.astype(F32)).astype(BF16)
    return jnp.dot(hi, w, preferred_element_type=F32) + jnp.dot(lo, w, preferred_element_type=F32)


def _rwkv_pre_kernel(x_ref, ps_ref, ln_ref, mu_ref, vec_ref, wr_ref, wk_ref, wv_ref, w1_ref, w2_ref, a1_ref, a2_ref,
                     g1_ref, g2_ref,
                     r_o, d_o, k_o, a_o, v_o, g_o, xn_o, carry_scr, *, tiles_per_seq, n_seq_tiles):
    i = pl.program_id(0)
    tm = x_ref.shape[0]

    @pl.when(i == 0)
    def _():
        carry_scr[...] = jnp.zeros_like(carry_scr)

    x = x_ref[...]
    xn = x * lax.rsqrt(jnp.mean(x * x, axis=-1, keepdims=True) + RMS_EPS) * ln_ref[...]
    xn_o[...] = xn
    first = jnp.where(i % tiles_per_seq == 0, jnp.zeros_like(carry_scr[...]), carry_scr[...])
    row0 = lax.broadcasted_iota(jnp.int32, xn.shape, 0) == 0
    prev = jnp.where(row0, first, pltpu.roll(xn, 1, 0))
    prev = jnp.where(i >= n_seq_tiles, ps_ref[...], prev)
    carry_scr[...] = xn[tm - 1:tm, :]
    xx = prev - xn
    mix = lambda c: (xn + xx * mu_ref[c:c + 1, :]).astype(BF16)
    dot = lambda a, w_ref: jnp.dot(a, w_ref[...], preferred_element_type=F32)
    w0, a0 = (vec_ref[c:c + 1, :] for c in range(2))
    r_o[...] = dot(mix(0), wr_ref)
    k_o[...] = dot(mix(2), wk_ref)
    v_o[...] = dot(mix(3), wv_ref)
    w_log = -jax.nn.softplus(-(w0 + dot(jnp.tanh(dot(mix(1), w1_ref)).astype(BF16), w2_ref))) - 0.5
    d_o[...] = jnp.exp(-jnp.exp(w_log))
    a_o[...] = jax.nn.sigmoid(a0 + dot(dot(mix(4), a1_ref).astype(BF16), a2_ref))
    g_o[...] = dot(jax.nn.sigmoid(dot(mix(5), g1_ref)).astype(BF16), g2_ref)


def rwkv_pre(h, prev_single, n_seq_rows, t, ln, mu, w_r, w_k, w_v, w0, w1, w2, a0, a1, a2, g1, g2):
    n, dm = h.shape
    tm = RW_TM
    assert n % tm == 0 and t % tm == 0 and n_seq_rows % tm == 0 and prev_single.shape == (tm, dm)
    row = pl.BlockSpec((tm, dm), lambda i: (i, 0))
    full = lambda a: pl.BlockSpec(a.shape, lambda i: (0,) * a.ndim)
    bf = lambda w: w.astype(BF16)
    consts = [prev_single, ln[None, :], mu, jnp.stack([w0, a0]), bf(w_r), bf(w_k), bf(w_v), bf(w1), bf(w2),
              bf(a1), bf(a2), bf(g1), bf(g2)]
    return pl.pallas_call(
        functools.partial(_rwkv_pre_kernel, tiles_per_seq=t // tm, n_seq_tiles=n_seq_rows // tm),
        grid=(n // tm,),
        in_specs=[row] + [full(a) for a in consts],
        out_specs=[row] * 7,
        out_shape=[jax.ShapeDtypeStruct((n, dm), F32)] * 7,
        scratch_shapes=[pltpu.VMEM((1, dm), F32)],
        compiler_params=_cparams(("arbitrary",)),
        name="rwkv_pre",
    )(h, *consts)


def _mix_out_kernel(*refs, n_branch, use_gates):
    o_refs = refs[:n_branch]
    gate_ref, h_ref, wo_ref, ln_ref, wr_ref = refs[n_branch:n_branch + 5]
    e_refs = refs[n_branch + 5:n_branch + 5 + (n_branch if use_gates else 0)]
    h_o, xp_o, lg_o = refs[-3:]
    if use_gates:
        sig = jax.nn.sigmoid(gate_ref[...])
        mixed = sum(_split_dot(sig, e_refs[b][...]) * o_refs[b][...] for b in range(n_branch))
    else:
        mixed = o_refs[0][...] * gate_ref[...]
    h = h_ref[...] + jnp.dot(mixed.astype(BF16), wo_ref[...], preferred_element_type=F32)
    h_o[...] = h
    xn = h * lax.rsqrt(jnp.mean(h * h, axis=-1, keepdims=True) + RMS_EPS) * ln_ref[...]
    xb = xn.astype(BF16)
    lg_o[...] = jnp.dot(xb, wr_ref[...], preferred_element_type=F32)
    u = lax.bitcast_convert_type(xb.astype(F32), jnp.uint32)
    half = u.shape[1] // 2
    xp_o[...] = (u[:, :half] >> 16) | (u[:, half:] & jnp.uint32(0xFFFF0000))


def mix_out(branches, gate, h, w_o, ln, w_router, gate_expand=None, gate_col=0):
    n, dm = h.shape
    tm = RW_TM
    nb = len(branches)
    row = lambda a: pl.BlockSpec((tm, a.shape[1]), lambda i: (i, 0))
    full = lambda a: pl.BlockSpec(a.shape, lambda i: (0,) * a.ndim)
    consts = [w_o.astype(BF16), ln[None, :], w_router.astype(BF16)] + list(gate_expand or [])
    gate_spec = row(gate) if gate_expand is None else pl.BlockSpec((tm, LANES), lambda i: (i, gate_col))
    return pl.pallas_call(
        functools.partial(_mix_out_kernel, n_branch=nb, use_gates=gate_expand is not None),
        grid=(n // tm,),
        in_specs=[row(a) for a in branches] + [gate_spec, row(h)] + [full(a) for a in consts],
        out_specs=[pl.BlockSpec((tm, dm), lambda i: (i, 0)), pl.BlockSpec((tm, dm // 2), lambda i: (i, 0)),
                   pl.BlockSpec((tm, LANES), lambda i: (i, 0))],
        out_shape=[jax.ShapeDtypeStruct((n, dm), F32), jax.ShapeDtypeStruct((n, dm // 2), jnp.uint32),
                   jax.ShapeDtypeStruct((n, LANES), F32)],
        compiler_params=_cparams(("arbitrary",)),
        name="mix_out",
    )(*branches, gate, h, *consts)


MOE_TM = 256
MOE_VMEM_LIMIT = 56 << 20


def _moe_kernel(te_ref, nt_ref, tok_ref, x_ref, c_ref, wgu_ref, wd_ref, o_ref, xt_scr):
    i = pl.program_id(0)

    @pl.when(i < nt_ref[0])
    def _():
        base = i * MOE_TM

        def gather_rows(g8, carry):
            r0 = pl.multiple_of(g8 * 8, 8)
            for u in range(8):
                xt_scr[pl.ds(r0 + u, 1), :] = x_ref[pl.ds(tok_ref[base + r0 + u], 1), :]
            return carry

        lax.fori_loop(0, MOE_TM // 8, gather_rows, 0)
        u = xt_scr[...]
        lo = lax.bitcast_convert_type(u << 16, F32).astype(BF16)
        hi = lax.bitcast_convert_type(u & jnp.uint32(0xFFFF0000), F32).astype(BF16)
        x = jnp.concatenate([lo, hi], axis=1)
        h = jnp.dot(x, wgu_ref[0], preferred_element_type=F32)
        g = h[:, :D_EXPERT]
        u2 = h[:, D_EXPERT:]
        hid = g * jax.nn.sigmoid(g) * u2 * c_ref[...]
        o_ref[...] = jnp.dot(hid.astype(BF16), wd_ref[0], preferred_element_type=F32)

    @pl.when(i >= nt_ref[0])
    def _():
        o_ref[...] = jnp.zeros_like(o_ref)


def moe_gmm(xp, row_tok, cs, tile_expert, n_tiles_used, wgu, wd):
    n, half = xp.shape
    dm = 2 * half
    p = row_tok.shape[0]
    nt = p // MOE_TM

    def row_map(i, te, ntu, tok):
        return (jnp.minimum(i, ntu[0] - 1), 0)

    grid_spec = pltpu.PrefetchScalarGridSpec(
        num_scalar_prefetch=3,
        grid=(nt,),
        in_specs=[pl.BlockSpec((n, half), lambda i, te, ntu, tok: (0, 0), pipeline_mode=pl.Buffered(1)),
                  pl.BlockSpec((MOE_TM, 1), row_map),
                  pl.BlockSpec((1, dm, 2 * D_EXPERT), lambda i, te, ntu, tok: (te[i], 0, 0)),
                  pl.BlockSpec((1, D_EXPERT, dm), lambda i, te, ntu, tok: (te[i], 0, 0))],
        out_specs=pl.BlockSpec((MOE_TM, dm), lambda i, te, ntu, tok: (i, 0)),
        scratch_shapes=[pltpu.VMEM((MOE_TM, half), jnp.uint32)],
    )
    return pl.pallas_call(
        _moe_kernel,
        grid_spec=grid_spec,
        out_shape=jax.ShapeDtypeStruct((p, dm), F32),
        compiler_params=pltpu.CompilerParams(dimension_semantics=("arbitrary",), vmem_limit_bytes=MOE_VMEM_LIMIT),
        name="moe_gmm",
    )(tile_expert, n_tiles_used, row_tok, xp, cs, wgu, wd)


def router_weight(w_rg, w_re):
    w_r = jnp.concatenate([w_rg, w_re], axis=1)
    return jnp.pad(w_r, ((0, 0), (0, LANES - w_r.shape[1])))


def hmoe(xp, logits, set_sizes, b_rg, b_re, w_gate, w_up, w_down):
    n = xp.shape[0]
    dm = 2 * xp.shape[1]
    lg = logits[:, :N_GROUPS] + b_rg
    le = (logits[:, N_GROUPS:N_GROUPS + N_EXPERTS] + b_re).reshape(n, N_GROUPS, EXPERTS_PER_GROUP)
    g_sel = jnp.argmax(lg, axis=-1)
    p_g = jnp.max(jax.nn.softmax(lg, axis=-1), axis=-1)
    le_g = jnp.take_along_axis(le, g_sel[:, None, None], axis=1)[:, 0]
    top_p, top_i = lax.top_k(jax.nn.softmax(le_g, axis=-1), 2)
    w_top = p_g[:, None] * top_p / jnp.sum(top_p, axis=-1, keepdims=True)
    eid = (g_sel[:, None] * EXPERTS_PER_GROUP + top_i).astype(jnp.int32).reshape(-1)
    cw = w_top.reshape(-1)

    n2 = 2 * n
    onehot = (eid[:, None] == jnp.arange(N_EXPERTS, dtype=jnp.int32)[None, :]).astype(jnp.int32)
    counts = jnp.sum(onehot, axis=0)
    rank = jnp.take_along_axis(jnp.cumsum(onehot, axis=0) - onehot, eid[:, None], axis=1)[:, 0]
    tiles_per = (counts + MOE_TM - 1) // MOE_TM
    tile_end = jnp.cumsum(tiles_per)
    tile_start = tile_end - tiles_per
    pos = tile_start[eid] * MOE_TM + rank
    p_rows = _round_up(n2 + N_EXPERTS * (MOE_TM - 1), MOE_TM)
    nt = p_rows // MOE_TM
    n_used = tile_end[-1].astype(jnp.int32)
    tile_ids = jnp.arange(nt, dtype=jnp.int32)
    tile_expert = jnp.sum((tile_end[None, :] <= jnp.minimum(tile_ids, n_used - 1)[:, None]).astype(jnp.int32), axis=1)

    assert n < 2 ** 24
    tok = (jnp.arange(n2, dtype=jnp.int32) // 2).astype(F32)
    rows = jnp.zeros((p_rows, 2), F32).at[pos].set(jnp.stack([tok, cw], axis=1))
    row_tok = rows[:, 0].astype(jnp.int32)
    wgu = jnp.concatenate([w_gate, w_up], axis=-1).reshape(N_EXPERTS, dm, 2 * D_EXPERT).astype(BF16)
    wd = w_down.reshape(N_EXPERTS, D_EXPERT, dm).astype(BF16)
    ys = moe_gmm(xp, row_tok, rows[:, 1:2], tile_expert, n_used.reshape(1), wgu, wd)
    pos2 = pos.reshape(n, 2)
    bounds = np.cumsum([0] + list(set_sizes))
    return [ys[pos2[a:b, 0]] + ys[pos2[a:b, 1]] for a, b in zip(bounds[:-1], bounds[1:])]


ATT_TQ = 256
ATT_TK = 512


def _cmp_sel_kernel(q_ref, kc_ref, vct_ref, ovt_ref, o_ref, sel_ref, *, n_cmp, n_sblk, tq):
    t0 = pl.program_id(2) * tq
    ncp = kc_ref.shape[2]
    n_io = lax.broadcasted_iota(jnp.int32, (ncp, tq), 0)
    t_io = lax.broadcasted_iota(jnp.int32, (ncp, tq), 1) + t0
    mask = (n_io * CMP_STRIDE + (CMP_BLOCK - 1) <= t_io) & (n_io < n_cmp)
    kc = kc_ref[0, 0]
    vct = vct_ref[0, 0]
    ovt = ovt_ref[...]
    imp = jnp.zeros((n_sblk, tq), F32)
    outs = []
    for r in range(GQA_R):
        qr = (q_ref[:, r * HEAD_DIM:(r + 1) * HEAD_DIM] * ATT_SCALE).astype(BF16)
        st = lax.dot_general(kc, qr, _NT, preferred_element_type=F32)
        st = jnp.where(mask, st, NEG_BIG)
        m = jnp.max(st, axis=0, keepdims=True)
        e = jnp.where(mask, jnp.exp(st - m), 0.0)
        p = (e / jnp.maximum(jnp.sum(e, axis=0, keepdims=True), 1e-30)).astype(BF16)
        outs.append(jnp.dot(vct, p, preferred_element_type=F32).T)
        imp = imp + jnp.dot(ovt, p, preferred_element_type=F32)
    o_ref[...] = jnp.concatenate(outs, axis=1)

    j_io = lax.broadcasted_iota(jnp.int32, (n_sblk, tq), 0)
    qblk = jnp.right_shift(lax.broadcasted_iota(jnp.int32, (n_sblk, tq), 1) + t0, SEL_BLOCK.bit_length() - 1)
    allowed = j_io <= qblk
    forced = (j_io == 0) | ((j_io > qblk - N_LOCAL_BLOCKS) & allowed)
    score = jnp.where(forced, FORCE_SCORE, imp)
    score = jnp.where(allowed, score, -jnp.inf)
    rank = jnp.zeros((n_sblk, tq), jnp.int32)
    for jp in range(n_sblk):
        row = score[jp:jp + 1, :]
        beats = (row > score) | ((row == score) & (j_io > jp))
        rank = rank + jnp.where(beats, 1, 0)
    sel = (rank < N_SEL) & allowed
    sel_ref[0, 0] = jnp.where(sel, 1.0, 0.0).astype(BF16)


def cmp_select(q, kc, vct, ovt, n_b, t, n_cmp, n_sblk, n_out):
    assert SEL_BLOCK & (SEL_BLOCK - 1) == 0
    tq = ATT_TQ
    nq = t // tq
    ncp = kc.shape[2]
    gw = GQA_R * HEAD_DIM
    return pl.pallas_call(
        functools.partial(_cmp_sel_kernel, n_cmp=n_cmp, n_sblk=n_sblk, tq=tq),
        grid=(n_b, KV_HEADS, nq),
        in_specs=[pl.BlockSpec((tq, gw), lambda b, g, qi: (b * nq + qi, g)),
                  pl.BlockSpec((1, 1, ncp, HEAD_DIM), lambda b, g, qi: (b, g, 0, 0)),
                  pl.BlockSpec((1, 1, HEAD_DIM, ncp), lambda b, g, qi: (b, g, 0, 0)),
                  pl.BlockSpec((n_sblk, ncp), lambda b, g, qi: (0, 0))],
        out_specs=[pl.BlockSpec((tq, gw), lambda b, g, qi: (b * nq + qi, g)),
                   pl.BlockSpec((1, 1, n_sblk, tq), lambda b, g, qi: (b, g, 0, qi))],
        out_shape=[jax.ShapeDtypeStruct((n_out, N_HEADS * HEAD_DIM), F32),
                   jax.ShapeDtypeStruct((n_b, KV_HEADS, n_sblk, t), BF16)],
        compiler_params=_cparams(("arbitrary", "arbitrary", "arbitrary")),
        name="attn_cmp_select",
    )(q, kc, vct, ovt)


def _flash_kernel(pq_ref, pk_ref, pfirst_ref, plast_ref, q_ref, kt_ref, vt_ref, *rest, window, use_sel, tq, tk):
    if use_sel:
        sel_ref, et_ref, o_ref, m_scr, l_scr, acc_scr = rest
    else:
        o_ref, m_scr, l_scr, acc_scr = rest
    p_id = pl.program_id(2)

    @pl.when(pfirst_ref[p_id] == 1)
    def _():
        m_scr[...] = jnp.full_like(m_scr, M_FLOOR)
        l_scr[...] = jnp.zeros_like(l_scr)
        acc_scr[...] = jnp.zeros_like(acc_scr)

    off = pq_ref[p_id] * tq - pk_ref[p_id] * tk
    diff = lax.broadcasted_iota(jnp.int32, (tk, tq), 0) - lax.broadcasted_iota(jnp.int32, (tk, tq), 1)
    mask = diff <= off
    if window is not None:
        mask = mask & (diff > off - window)
    if use_sel:
        selm = jnp.dot(et_ref[...], sel_ref[0, 0], preferred_element_type=F32)
        mask = mask & (selm > 0.5)
    bias = jnp.where(mask, 0.0, NEG_BIG)
    k = kt_ref[0, 0, 0].astype(BF16).T
    vt = vt_ref[0, 0, 0].astype(BF16)
    heads = range(GQA_R)
    qs = [(q_ref[:, r * HEAD_DIM:(r + 1) * HEAD_DIM] * ATT_SCALE).astype(BF16) for r in heads]
    sts = [lax.dot_general(k, qs[r], _NT, preferred_element_type=F32) + bias for r in heads]
    m_prev = [m_scr[r] for r in heads]
    m_new = [jnp.maximum(m_prev[r], jnp.max(sts[r], axis=0, keepdims=True)) for r in heads]
    ps = [jnp.exp(sts[r] - m_new[r]) for r in heads]
    alphas = [jnp.exp(m_prev[r] - m_new[r]) for r in heads]
    for r in heads:
        l_scr[r] = alphas[r] * l_scr[r] + jnp.sum(ps[r], axis=0, keepdims=True)
        acc_scr[r] = alphas[r] * acc_scr[r] + jnp.dot(vt, ps[r].astype(BF16), preferred_element_type=F32)
        m_scr[r] = m_new[r]

    @pl.when(plast_ref[p_id] == 1)
    def _():
        outs = [(acc_scr[r] / jnp.maximum(l_scr[r], 1e-30)).T for r in range(GQA_R)]
        o_ref[...] = jnp.concatenate(outs, axis=1)


def flash_attention(q, kv_t, k_stream, v_stream, n_b, t, window=None, sel=None):
    tq, tk = ATT_TQ, ATT_TK
    nq = t // tq
    first_tile = lambda qi: 0 if window is None else max(0, qi * tq - (window - 1)) // tk
    pairs = [(qi, kt) for qi in range(nq) for kt in range(first_tile(qi), (qi * tq + tq - 1) // tk + 1)]
    pq = np.array([p[0] for p in pairs], np.int32)
    pk = np.array([p[1] for p in pairs], np.int32)
    pfirst = np.array([1 if i == 0 or pairs[i - 1][0] != p[0] else 0 for i, p in enumerate(pairs)], np.int32)
    plast = np.array([1 if i == len(pairs) - 1 or pairs[i + 1][0] != p[0] else 0 for i, p in enumerate(pairs)], np.int32)
    gw = GQA_R * HEAD_DIM

    def tile_map(stream):
        return lambda b, g, p, pq_r, pk_r, f_r, l_r: (b, stream, g, 0, pk_r[p])

    in_specs = [pl.BlockSpec((tq, gw), lambda b, g, p, pq_r, pk_r, f_r, l_r: (b * nq + pq_r[p], g)),
                pl.BlockSpec((1, 1, 1, HEAD_DIM, tk), tile_map(k_stream)),
                pl.BlockSpec((1, 1, 1, HEAD_DIM, tk), tile_map(v_stream))]
    args = [q, kv_t, kv_t]
    if sel is not None:
        n_sblk = sel.shape[2]
        emat = (np.arange(t)[:, None] // SEL_BLOCK == np.arange(n_sblk)[None, :]).astype(np.float32)
        in_specs += [pl.BlockSpec((1, 1, n_sblk, tq), lambda b, g, p, pq_r, pk_r, f_r, l_r: (b, g, 0, pq_r[p])),
                     pl.BlockSpec((tk, n_sblk), lambda b, g, p, pq_r, pk_r, f_r, l_r: (pk_r[p], 0))]
        args += [sel, jnp.asarray(emat, BF16)]
    grid_spec = pltpu.PrefetchScalarGridSpec(
        num_scalar_prefetch=4,
        grid=(n_b, KV_HEADS, len(pairs)),
        in_specs=in_specs,
        out_specs=pl.BlockSpec((tq, gw), lambda b, g, p, pq_r, pk_r, f_r, l_r: (b * nq + pq_r[p], g)),
        scratch_shapes=[pltpu.VMEM((GQA_R, 1, tq), F32), pltpu.VMEM((GQA_R, 1, tq), F32),
                        pltpu.VMEM((GQA_R, HEAD_DIM, tq), F32)],
    )
    return pl.pallas_call(
        functools.partial(_flash_kernel, window=window, use_sel=sel is not None, tq=tq, tk=tk),
        grid_spec=grid_spec,
        out_shape=jax.ShapeDtypeStruct((n_b * t, N_HEADS * HEAD_DIM), F32),
        compiler_params=_cparams(("arbitrary", "arbitrary", "arbitrary")),
        name="attn_sel" if sel is not None else "attn_win",
    )(jnp.asarray(pq), jnp.asarray(pk), jnp.asarray(pfirst), jnp.asarray(plast), *args)


def _attn_pre_kernel(h_ref, ln_ref, wq_ref, wkv_ref, rc_ref, rm_ref, rp_ref, qg_o, qr_o, kvt_o):
    x = h_ref[...]
    xs = x * lax.rsqrt(jnp.mean(x * x, axis=-1, keepdims=True) + RMS_EPS)
    qg = jnp.dot((xs * ln_ref[0:1, :]).astype(BF16), wq_ref[...], preferred_element_type=F32)
    kv = jnp.dot((xs * ln_ref[1:2, :]).astype(BF16), wkv_ref[...], preferred_element_type=F32)
    qg_o[...] = qg

    def rope(v):
        reps = v.shape[1] // LANES
        tile = lambda ref: jnp.concatenate([ref[...]] * reps, axis=1)
        n = v.shape[1]
        return v * tile(rc_ref) + pltpu.roll(v, n - ROPE_DIMS // 2, 1) * tile(rm_ref) + pltpu.roll(v, ROPE_DIMS // 2, 1) * tile(rp_ref)

    n_q = N_HEADS * HEAD_DIM
    sw = KV_HEADS * HEAD_DIM
    qr_o[...] = rope(qg[:, :n_q])
    kv_r = jnp.concatenate([kv[:, :2 * sw], rope(kv[:, 2 * sw:3 * sw]), kv[:, 3 * sw:4 * sw],
                            rope(kv[:, 4 * sw:5 * sw]), kv[:, 5 * sw:]], axis=1)
    kvt_o[0] = kv_r.T


def attn_pre(h, n_b, t, ln_q, ln_kv, w_qg, w_kv):
    n, dm = h.shape
    tm = RW_TM
    tiles = t // tm
    half = ROPE_DIMS // 2
    inv = jnp.power(ROPE_THETA, -jnp.arange(half, dtype=F32) * (2.0 / ROPE_DIMS))
    ang = jnp.arange(t, dtype=F32)[:, None] * inv[None, :]
    cos, sin = jnp.cos(ang), jnp.sin(ang)
    pad = jnp.zeros((t, HEAD_DIM - ROPE_DIMS), F32)
    zero = jnp.zeros((t, half), F32)
    head = lambda parts: jnp.concatenate(parts, axis=1)
    two = lambda x: jnp.concatenate([x, x], axis=1)
    rc = two(head([cos, cos, pad + 1.0]))
    rm = two(head([-sin, zero, pad]))
    rp = two(head([zero, sin, pad]))
    nq_cols = w_qg.shape[1]
    nkv = w_kv.shape[1]
    tab = pl.BlockSpec((tm, LANES), lambda i: (i % tiles, 0))
    full = lambda a: pl.BlockSpec(a.shape, lambda i: (0,) * a.ndim)
    consts = [jnp.stack([ln_q, ln_kv]), w_qg.astype(BF16), w_kv.astype(BF16)]
    return pl.pallas_call(
        _attn_pre_kernel,
        grid=(n // tm,),
        in_specs=[pl.BlockSpec((tm, dm), lambda i: (i, 0))] + [full(a) for a in consts] + [tab, tab, tab],
        out_specs=[pl.BlockSpec((tm, nq_cols), lambda i: (i, 0)),
                   pl.BlockSpec((tm, N_HEADS * HEAD_DIM), lambda i: (i, 0)),
                   pl.BlockSpec((1, nkv, tm), lambda i: (i // tiles, 0, i % tiles))],
        out_shape=[jax.ShapeDtypeStruct((n, nq_cols), F32), jax.ShapeDtypeStruct((n, N_HEADS * HEAD_DIM), F32),
                   jax.ShapeDtypeStruct((n_b, nkv, t), F32)],
        compiler_params=_cparams(("arbitrary",)),
        name="attn_pre",
    )(h, *consts, rc, rm, rp)


def _cmp_kernel(q_ref, k_ref, v_ref, ov_ref, o_ref, imp_ref, *, n_cmp):
    q = q_ref[0] * ATT_SCALE
    s = lax.dot_general(q, k_ref[0], _NT, preferred_element_type=F32)
    mask = lax.broadcasted_iota(jnp.int32, s.shape, 1) < n_cmp
    s = jnp.where(mask, s, NEG_BIG)
    m = jnp.max(s, axis=-1, keepdims=True)
    e = jnp.where(mask, jnp.exp(s - m), 0.0)
    p = (e / jnp.maximum(jnp.sum(e, axis=-1, keepdims=True), 1e-30)).astype(BF16)
    o_ref[0] = jnp.dot(p, v_ref[0], preferred_element_type=F32)
    imp_ref[0] = jnp.dot(p, ov_ref[...], preferred_element_type=F32)


def cmp_attention_sample(q, kc, vc, ov, n_cmp):
    bg, rq, hd = q.shape
    nc = kc.shape[1]
    nb = ov.shape[1]
    return pl.pallas_call(
        functools.partial(_cmp_kernel, n_cmp=n_cmp),
        grid=(bg,),
        in_specs=[pl.BlockSpec((1, rq, hd), lambda b: (b, 0, 0)),
                  pl.BlockSpec((1, nc, hd), lambda b: (b, 0, 0)),
                  pl.BlockSpec((1, nc, hd), lambda b: (b, 0, 0)),
                  pl.BlockSpec((nc, nb), lambda b: (0, 0))],
        out_specs=[pl.BlockSpec((1, rq, hd), lambda b: (b, 0, 0)),
                   pl.BlockSpec((1, rq, nb), lambda b: (b, 0, 0))],
        out_shape=[jax.ShapeDtypeStruct((bg, rq, hd), F32),
                   jax.ShapeDtypeStruct((bg, rq, nb), F32)],
        compiler_params=_cparams(("arbitrary",)),
        name="attn_cmp_sample",
    )(q, kc, vc, ov)


PAGES_PER_STEP = 8


def _page_compress_kernel(pt_ref, *rest, n_pages, page, pb):
    c_refs = rest[:pb]
    wbd_ref, pe_ref, w2_ref, o_ref, a_scr = rest[pb:]
    p = pl.program_id(1)
    nsub = n_pages * page // CMP_STRIDE
    for i in range(pb):
        row0 = pl.multiple_of((p * pb + i) * page, page)
        for s in range(2):
            for gp in range(KV_HEADS // 2):
                x = c_refs[i][0, s, 2 * gp:2 * gp + 2].reshape(2 * HEAD_DIM, page)
                a_scr[s, gp, pl.ds(row0, page), :] = x.T

    @pl.when(p == n_pages // pb - 1)
    def _():
        hid = pe_ref.shape[-1]
        for s in range(2):
            for gp in range(KV_HEADS // 2):
                acc = jnp.zeros((nsub, 4 * hid), F32)
                for j in range(CMP_STRIDE):
                    lhs = a_scr[s, gp, pl.ds(j, nsub, stride=CMP_STRIDE), :].astype(BF16)
                    acc = acc + jnp.dot(lhs, wbd_ref[s, j], preferred_element_type=F32)
                for g2 in range(2):
                    first = acc[:, 2 * g2 * hid:(2 * g2 + 1) * hid]
                    second = acc[:, (2 * g2 + 1) * hid:(2 * g2 + 2) * hid]
                    pre = first + pltpu.roll(second, nsub - 1, 0) + pe_ref[s]
                    act = jax.nn.gelu(pre).astype(BF16)
                    o_ref[s, 0, 2 * gp + g2] = jnp.dot(act, w2_ref[s], preferred_element_type=F32).astype(BF16)


def page_compress(cache_t, page_table, cmp_pe, cmp_w1, cmp_b1, cmp_w2):
    n_b, n_pages = page_table.shape
    page = cache_t.shape[-1]
    nsub = n_pages * page // CMP_STRIDE
    hid = cmp_w1.shape[-1]
    pb = PAGES_PER_STEP
    assert n_pages % pb == 0
    w1r = cmp_w1.reshape(2, 2, CMP_STRIDE, HEAD_DIM, hid).transpose(0, 2, 3, 1, 4)
    w1r = w1r.reshape(2, CMP_STRIDE, HEAD_DIM, 2 * hid)
    zeros = jnp.zeros_like(w1r)
    wbd = jnp.concatenate([jnp.concatenate([w1r, zeros], axis=-1), jnp.concatenate([zeros, w1r], axis=-1)], axis=2)
    pe_term = (jnp.einsum('sjd,sjdc->sc', cmp_pe, cmp_w1.reshape(2, CMP_BLOCK, HEAD_DIM, hid)) + cmp_b1)[:, None, :]
    def page_map(i):
        return lambda b, p, pt: (pt[b * n_pages + p * pb + i], 0, 0, 0, 0)

    grid_spec = pltpu.PrefetchScalarGridSpec(
        num_scalar_prefetch=1,
        grid=(n_b, n_pages // pb),
        in_specs=[pl.BlockSpec((1, 2, KV_HEADS, HEAD_DIM, page), page_map(i)) for i in range(pb)] + [
            pl.BlockSpec((2, CMP_STRIDE, 2 * HEAD_DIM, 4 * hid), lambda b, p, pt: (0, 0, 0, 0)),
            pl.BlockSpec((2, 1, hid), lambda b, p, pt: (0, 0, 0)),
            pl.BlockSpec((2, hid, HEAD_DIM), lambda b, p, pt: (0, 0, 0))],
        out_specs=pl.BlockSpec((2, 1, KV_HEADS, nsub, HEAD_DIM), lambda b, p, pt: (0, b, 0, 0, 0)),
        scratch_shapes=[pltpu.VMEM((2, KV_HEADS // 2, n_pages * page, 2 * HEAD_DIM), F32)],
    )
    return pl.pallas_call(
        functools.partial(_page_compress_kernel, n_pages=n_pages, page=page, pb=pb),
        grid_spec=grid_spec,
        out_shape=jax.ShapeDtypeStruct((2, n_b, KV_HEADS, nsub, HEAD_DIM), BF16),
        compiler_params=_cparams(("arbitrary", "arbitrary")),
        name="page_compress",
    )(page_table.reshape(-1), *([cache_t] * pb), wbd.astype(BF16), pe_term, cmp_w2.astype(BF16))


def _decode_attn_kernel(phys_ref, blk_ref, flag_ref, q_ref, kn_ref, vn_ref, *rest, mode, past, steps, window, page):
    kt_refs = rest[:steps]
    vt_refs = rest[steps:2 * steps]
    o_ref = rest[2 * steps]
    idx = pl.program_id(0) * KV_HEADS + pl.program_id(1)
    rnd = lambda x: x.astype(BF16).astype(F32)
    q = q_ref[0, 0] * ATT_SCALE

    kn = rnd(kn_ref[0, 0])
    vn = rnd(vn_ref[0, 0])
    s_new = jnp.sum(q.astype(F32) * kn, axis=-1, keepdims=True)
    on = flag_ref[idx] == 1
    m = jnp.where(on, s_new, NEG_BIG)
    scs, masks = [], []
    for s in range(steps):
        kt = kt_refs[s][0, 0, 0].astype(BF16)
        tile = kt.shape[-1]
        sc = jnp.dot(q, kt, preferred_element_type=F32)
        lane = lax.broadcasted_iota(jnp.int32, sc.shape, 1)
        if mode == "sel":
            blk = blk_ref[idx * steps + s]
            per_page = page // SEL_BLOCK
            pos = (blk // per_page) * page + lane
            mask = (jnp.right_shift(pos, SEL_BLOCK.bit_length() - 1) == blk) & (pos <= past) & (blk >= 0)
        else:
            pos = past - tile + lane
            mask = (pos > past - window) & (pos <= past)
        sc = jnp.where(mask, sc, NEG_BIG)
        m = jnp.maximum(m, jnp.max(sc, axis=-1, keepdims=True))
        scs.append(sc)
        masks.append(mask)
    p_new = jnp.where(on, jnp.exp(s_new - m), 0.0)
    l = p_new
    acc = rnd(p_new) * vn
    for s in range(steps):
        p = jnp.where(masks[s], jnp.exp(scs[s] - m), 0.0)
        l = l + jnp.sum(p, axis=-1, keepdims=True)
        vt = vt_refs[s][0, 0, 0].astype(BF16)
        acc = acc + lax.dot_general(p.astype(BF16), vt, _NT, preferred_element_type=F32)
    o_ref[0, 0] = acc / jnp.maximum(l, 1e-30)


def decode_attention(q, k_new, v_new, kv_t, streams, phys, blk, flag, mode, past, steps, window=None, page=None):
    n_b, n_g, rows, hd = q.shape
    tile = kv_t.shape[-1]
    ks, vs = streams

    def kmap(st, s):
        if mode == "sel":
            return lambda b, g, ph, bl, fl: (ph[(b * n_g + g) * steps + s], st, g, 0, 0)
        return lambda b, g, ph, bl, fl: (b, st, g, 0, 0)

    tiles = ([pl.BlockSpec((1, 1, 1, hd, tile), kmap(ks, s)) for s in range(steps)]
             + [pl.BlockSpec((1, 1, 1, hd, tile), kmap(vs, s)) for s in range(steps)])
    grid_spec = pltpu.PrefetchScalarGridSpec(
        num_scalar_prefetch=3,
        grid=(n_b, n_g),
        in_specs=[pl.BlockSpec((1, 1, rows, hd), lambda b, g, ph, bl, fl: (b, g, 0, 0)),
                  pl.BlockSpec((1, 1, 1, hd), lambda b, g, ph, bl, fl: (b, g, 0, 0)),
                  pl.BlockSpec((1, 1, 1, hd), lambda b, g, ph, bl, fl: (b, g, 0, 0))] + tiles,
        out_specs=pl.BlockSpec((1, 1, rows, hd), lambda b, g, ph, bl, fl: (b, g, 0, 0)),
    )
    return pl.pallas_call(
        functools.partial(_decode_attn_kernel, mode=mode, past=past, steps=steps, window=window, page=page),
        grid_spec=grid_spec,
        out_shape=jax.ShapeDtypeStruct((n_b, n_g, rows, hd), F32),
        compiler_params=_cparams(("arbitrary", "arbitrary")),
        name="attn_decode_" + mode,
    )(phys, blk, flag, q, k_new, v_new, *([kv_t] * (2 * steps)))


def rmsnorm(x, g):
    y = x * lax.rsqrt(jnp.mean(x * x, axis=-1, keepdims=True) + RMS_EPS)
    return y * g


def rope(x, pos):
    half = ROPE_DIMS // 2
    inv = jnp.power(ROPE_THETA, -jnp.arange(half, dtype=F32) * (2.0 / ROPE_DIMS))
    ang = pos.astype(F32)[:, None] * inv[None, :]
    cos = jnp.cos(ang)[:, None, :]
    sin = jnp.sin(ang)[:, None, :]
    x1 = x[..., :half]
    x2 = x[..., half:ROPE_DIMS]
    return jnp.concatenate([x1 * cos - x2 * sin, x2 * cos + x1 * sin, x[..., ROPE_DIMS:]], axis=-1)


def compress(sub, pe, w1, b1, w2, n_b, nsub):
    half = CMP_STRIDE * HEAD_DIM
    w1cat = jnp.concatenate([w1[:half], w1[half:]], axis=1)
    hidden = w1.shape[1]
    part = mm(sub, w1cat).reshape(n_b, nsub, KV_HEADS, 2, hidden)
    pe_term = jnp.einsum('jd,jdc->c', pe, w1.reshape(CMP_BLOCK, HEAD_DIM, hidden)) + b1
    pre = part[:, :nsub - 1, :, 0] + part[:, 1:, :, 1] + pe_term
    act = jax.nn.gelu(pre)
    return mm(act.reshape(-1, hidden), w2).reshape(n_b, nsub - 1, KV_HEADS, HEAD_DIM)


def overlap_matrix(n_cmp, n_sblk, nc_pad, nb_pad):
    cs = np.arange(n_cmp) * CMP_STRIDE
    ce = cs + CMP_BLOCK - 1
    ss = np.arange(n_sblk) * SEL_BLOCK
    ov = ((cs[:, None] < ss[None, :] + SEL_BLOCK) & (ce[:, None] >= ss[None, :])).astype(np.float32)
    out = np.zeros((nc_pad, nb_pad), np.float32)
    out[:n_cmp, :n_sblk] = ov
    return out


def block_scores(imp, q_pos, n_sblk):
    j = jnp.arange(n_sblk)[None, :]
    qblk = (q_pos // SEL_BLOCK)[:, None]
    allowed = j <= qblk
    forced = (j == 0) | ((j > qblk - N_LOCAL_BLOCKS) & allowed)
    score = jnp.where(forced, FORCE_SCORE, imp)
    return jnp.where(allowed, score, -jnp.inf)


def nsa_prompt(qg, q_r, kv_t, n_b, t, cmp_pe, cmp_w1, cmp_b1, cmp_w2):
    nsub = t // CMP_STRIDE
    n_cmp = nsub - 1
    n_sblk = t // SEL_BLOCK

    def sub_blocks(s):
        x = kv_t[:, s].reshape(n_b, KV_HEADS, HEAD_DIM, nsub, CMP_STRIDE)
        return x.transpose(0, 3, 1, 4, 2).reshape(-1, CMP_STRIDE * HEAD_DIM)

    kc = compress(sub_blocks(0), cmp_pe[0], cmp_w1[0], cmp_b1[0], cmp_w2[0], n_b, nsub)
    vc = compress(sub_blocks(1), cmp_pe[1], cmp_w1[1], cmp_b1[1], cmp_w2[1], n_b, nsub)
    nc_pad = _round_up(n_cmp, LANES)
    kc = jnp.pad(kc, ((0, 0), (0, nc_pad - n_cmp), (0, 0), (0, 0)))
    vc = jnp.pad(vc, ((0, 0), (0, nc_pad - n_cmp), (0, 0), (0, 0)))
    ovt = jnp.asarray(overlap_matrix(n_cmp, n_sblk, nc_pad, n_sblk).T, BF16)
    o_cmp, sel = cmp_select(qg, kc.transpose(0, 2, 1, 3).astype(BF16), vc.transpose(0, 2, 3, 1).astype(BF16),
                            ovt, n_b, t, n_cmp, n_sblk, qg.shape[0])
    o_sel = flash_attention(q_r, kv_t, 2, 3, n_b, t, sel=sel)
    o_win = flash_attention(q_r, kv_t, 4, 5, n_b, t, window=WINDOW)
    return o_cmp, o_sel, o_win


def nsa_sample(qg, kv, cache_kv, cache_win, page_table, cmp_pe, cmp_w1, cmp_b1, cmp_w2):
    n_b, n_pages = page_table.shape
    n_phys, page = cache_kv.shape[:2]
    past = n_pages * page
    pos = jnp.full((1,), past, jnp.int32)
    q = qg[:, :N_HEADS * HEAD_DIM].reshape(n_b, 1, N_HEADS, HEAD_DIM)
    kv = kv.reshape(n_b, 1, N_KV_STREAMS, KV_HEADS, HEAD_DIM)
    k_sel = rope(kv[:, :, 2], pos)
    k_win = rope(kv[:, :, 4], pos)
    new_rows = jnp.stack([kv[:, :, 0], kv[:, :, 1], k_sel, kv[:, :, 3]], axis=2)
    new_win = jnp.stack([k_win, kv[:, :, 5]], axis=2)
    n_keep = min(WINDOW, cache_win.shape[1] + 1)
    win_state = jnp.concatenate([cache_win[:, cache_win.shape[1] + 1 - n_keep:], new_win], axis=1)

    cache_t = cache_kv.transpose(0, 2, 3, 4, 1)
    win_t = cache_win.transpose(0, 2, 3, 4, 1)

    l_tot = past + 1
    nsub = l_tot // CMP_STRIDE
    n_cmp = nsub - 1
    n_sblk = -(-l_tot // SEL_BLOCK)
    assert nsub * CMP_STRIDE == past and page % SEL_BLOCK == 0 and cache_win.shape[1] == WINDOW
    kvc = page_compress(cache_t, page_table, cmp_pe, cmp_w1, cmp_b1, cmp_w2)

    bg = n_b * KV_HEADS
    rq = 16
    qpad = lambda x: jnp.pad(x.reshape(n_b, KV_HEADS, GQA_R, HEAD_DIM), ((0, 0), (0, 0), (0, rq - GQA_R), (0, 0))).astype(BF16)

    nb_pad = _round_up(n_sblk, 8)
    ov = jnp.asarray(overlap_matrix(n_cmp, n_sblk, nsub, nb_pad), BF16)
    o_cmp, imp = cmp_attention_sample(qpad(q).reshape(bg, rq, HEAD_DIM), kvc[0].reshape(bg, nsub, HEAD_DIM),
                                      kvc[1].reshape(bg, nsub, HEAD_DIM), ov, n_cmp)
    imp = imp[:, :GQA_R, :n_sblk].sum(axis=1)
    score = block_scores(imp[:, None, :], pos, n_sblk)
    top_s, top_i = lax.top_k(score[:, 0], min(N_SEL, n_sblk))
    valid = jnp.isfinite(top_s)

    n_past_blk = past // SEL_BLOCK
    per_page = page // SEL_BLOCK
    n_k = top_i.shape[-1]
    in_past = valid & (top_i < n_past_blk)
    blk = jnp.where(in_past, top_i, -1).astype(jnp.int32)
    logical = jnp.clip(top_i, 0, n_past_blk - 1) // per_page
    phys = jnp.take_along_axis(page_table, logical.reshape(n_b, KV_HEADS * n_k), axis=1).reshape(bg, n_k)
    phys = jnp.where(in_past, phys, 0).astype(jnp.int32)
    new_flag = jnp.any(valid & (top_i >= n_past_blk), axis=-1).astype(jnp.int32)
    q_r = qpad(rope(q, pos))
    o_sel = decode_attention(q_r, new_rows[:, 0, 2][:, :, None, :], new_rows[:, 0, 3][:, :, None, :], cache_t, (2, 3),
                             phys.reshape(-1), blk.reshape(-1), new_flag, "sel", past, n_k, page=page)
    zeros = jnp.zeros((bg,), jnp.int32)
    o_win = decode_attention(q_r, new_win[:, 0, 0][:, :, None, :], new_win[:, 0, 1][:, :, None, :], win_t, (0, 1),
                             zeros, zeros, jnp.ones((bg,), jnp.int32), "win", past, 1, window=WINDOW)

    heads = lambda o: o.reshape(n_b, KV_HEADS, -1, HEAD_DIM)[:, :, :GQA_R].reshape(n_b, N_HEADS * HEAD_DIM)
    return (heads(o_cmp), heads(o_sel), heads(o_win)), new_rows, win_state


def rwkv_layer(h_p, h_t, n_p, t_p, n_s, state_wkv, state_shift, ln, mu, w_r, w_k, w_v, w0, w1, w2, a0, a1, a2, g1, g2,
               k_k, k_a, r_k, gn_w, gn_b):
    np_rows = n_p * t_p
    weights = (ln, mu, w_r, w_k, w_v, w0, w1, w2, a0, a1, a2, g1, g2)
    pre_p = rwkv_pre(h_p, jnp.zeros((RW_TM, D_MODEL), F32), np_rows, t_p, *weights)
    pre_t = rwkv_pre(h_t, jnp.pad(state_shift, ((0, RW_TM - n_s), (0, 0))), 0, RW_TM, *weights)

    def lanes(x, nb, t):
        x4 = x.reshape(nb, t, RW_HEADS, RW_HEAD)
        if t % TIME_BLK == 0 and (nb * RW_HEADS) % LANES == 0:
            return time_major(x4.transpose(3, 0, 2, 1).reshape(RW_HEAD, nb * RW_HEADS, t))
        return x4.transpose(1, 3, 0, 2).reshape(t, RW_HEAD, nb * RW_HEADS)

    def rows(o, nb, t):
        if t % TIME_BLK == 0 and (nb * RW_HEADS) % LANES == 0:
            return time_minor(o).reshape(RW_HEAD, nb, RW_HEADS, t).transpose(1, 3, 2, 0).reshape(nb * t, D_MODEL)
        return o.reshape(t, RW_HEAD, nb, RW_HEADS).transpose(2, 0, 3, 1).reshape(nb * t, D_MODEL)

    head_vec = lambda w, nb: jnp.tile(w.reshape(RW_HEADS, RW_HEAD).T, (1, nb))
    vecs = lambda nb: [head_vec(w, nb) for w in (k_k, k_a, r_k, gn_w, gn_b)]
    s0_p = jnp.zeros((RW_HEAD, RW_HEAD, n_p * RW_HEADS), F32)
    o_p, st_p = rwkv_scan(*[lanes(x, n_p, t_p) for x in pre_p[:5]], *vecs(n_p), s0_p)
    s0_s = state_wkv.transpose(3, 2, 0, 1).reshape(RW_HEAD, RW_HEAD, n_s * RW_HEADS)
    o_s, st_s = rwkv_scan(*[lanes(x[:n_s], n_s, 1) for x in pre_t[:5]], *vecs(n_s), s0_s, dot_dtype=BF16)
    o_t = jnp.pad(rows(o_s, n_s, 1), ((0, RW_TM - n_s), (0, 0)))
    wkv_p = st_p.reshape(RW_HEAD, RW_HEAD, n_p, RW_HEADS).transpose(2, 3, 1, 0)
    wkv_s = st_s.reshape(RW_HEAD, RW_HEAD, n_s, RW_HEADS).transpose(2, 3, 1, 0)
    return (rows(o_p, n_p, t_p), pre_p[5], pre_p[6]), (o_t, pre_t[5], pre_t[6]), wkv_p, wkv_s


def kernel(x_prompt, x_sample, state_wkv, state_shift, cache_kv, cache_win, page_table, ln_mix, ln_ffn, ln_kv, ln_out, rw_mu, rw_wr, rw_wk, rw_wv, rw_wo, rw_w0, rw_w1, rw_w2, rw_a0, rw_a1, rw_a2, rw_g1, rw_g2, rw_kk, rw_ka, rw_rk, rw_lnw, rw_lnb, w_kv, cmp_pe, cmp_w1, cmp_b1, cmp_w2, nsa_wqg, nsa_wo, moe_wrg, moe_brg, moe_wre, moe_bre, moe_wgate, moe_wup, moe_wdown):
    n_p, t_p, dm = x_prompt.shape
    n_s = x_sample.shape[0]
    assert x_sample.shape[1] == 1
    np_rows = n_p * t_p
    h_p = x_prompt.reshape(np_rows, dm)
    h_t = jnp.pad(x_sample.reshape(n_s, dm), ((0, RW_TM - n_s), (0, 0)))
    sizes = (np_rows, RW_TM)

    def moe(fronts, layer):
        ys = hmoe(jnp.concatenate([f[1] for f in fronts]), jnp.concatenate([f[2] for f in fronts]), sizes,
                  moe_brg[layer], moe_bre[layer], moe_wgate[layer], moe_wup[layer], moe_wdown[layer])
        return [f[0] + y for f, y in zip(fronts, ys)]

    mix_p, mix_t, wkv_p, wkv_s = rwkv_layer(h_p, h_t, n_p, t_p, n_s, state_wkv[0], state_shift[0], ln_mix[0], rw_mu[0],
                                            rw_wr[0], rw_wk[0], rw_wv[0], rw_w0[0], rw_w1[0], rw_w2[0], rw_a0[0], rw_a1[0],
                                            rw_a2[0], rw_g1[0], rw_g2[0], rw_kk[0], rw_ka[0], rw_rk[0], rw_lnw[0], rw_lnb[0])
    p_shift = mix_p[2].reshape(n_p, t_p, dm)[:, -1][None]
    s_shift = mix_t[2][:n_s][None]
    w_router = router_weight(moe_wrg[0], moe_wre[0])
    h_p, h_t = moe([mix_out([o], g, hh, rw_wo[0], ln_ffn[0], w_router)
                    for (o, g, _), hh in ((mix_p, h_p), (mix_t, h_t))], 0)

    n_q = N_HEADS * HEAD_DIM
    gate_cols = _round_up(nsa_wqg.shape[-1] - n_q, LANES)
    w_qg = jnp.pad(nsa_wqg[0], ((0, 0), (0, n_q + gate_cols - nsa_wqg.shape[-1])))
    qg_p, qr_p, kvt = attn_pre(h_p, n_p, t_p, ln_mix[1], ln_kv, w_qg, w_kv)
    kvt = kvt.reshape(n_p, N_KV_STREAMS, KV_HEADS, HEAD_DIM, t_p)
    o_p = nsa_prompt(qg_p, qr_p, kvt, n_p, t_p, cmp_pe, cmp_w1, cmp_b1, cmp_w2)
    p_rows = kvt[:, :4].transpose(0, 4, 1, 2, 3)
    p_win = kvt[:, 4:, :, :, t_p - min(WINDOW, t_p):].transpose(0, 4, 1, 2, 3)
    qg_t = mm(rmsnorm(h_t, ln_mix[1]), w_qg)
    kv_t = mm(rmsnorm(h_t, ln_kv), w_kv)
    o_s, s_rows, s_win = nsa_sample(qg_t[:n_s, :nsa_wqg.shape[-1]], kv_t[:n_s], cache_kv, cache_win,
                                    page_table, cmp_pe, cmp_w1, cmp_b1, cmp_w2)
    o_t = [jnp.pad(x, ((0, RW_TM - n_s), (0, 0))) for x in o_s]
    expand = [jnp.asarray((np.arange(gate_cols)[:, None] == 3 * (np.arange(n_q)[None, :] // HEAD_DIM) + br)
                          .astype(np.float32), BF16) for br in range(3)]
    w_router = router_weight(moe_wrg[1], moe_wre[1])
    h_p, h_t = moe([mix_out(list(br), qg, hh, nsa_wo[0], ln_ffn[1], w_router, gate_expand=expand, gate_col=n_q // LANES)
                    for br, qg, hh in ((o_p, qg_p, h_p), (o_t, qg_t, h_t))], 1)

    y_prompt = rmsnorm(h_p, ln_out).reshape(n_p, t_p, dm)
    y_sample = rmsnorm(h_t[:n_s], ln_out).reshape(n_s, 1, dm)
    return (y_prompt, y_sample, wkv_p[None], p_shift, p_rows, p_win, wkv_s[None], s_shift, s_rows, s_win)
```

```python
import functools

import numpy as np
import jax
import jax.numpy as jnp
from jax import lax
from jax.experimental import pallas as pl
from jax.experimental.pallas import tpu as pltpu

F32 = jnp.float32
BF16 = jnp.bfloat16

D_MODEL = 1024
RMS_EPS = 1e-6
RW_HEAD = 64
RW_HEADS = D_MODEL // RW_HEAD
RW_GN_EPS = 64e-5
N_HEADS = 16
HEAD_DIM = 64
KV_HEADS = 4
GQA_R = N_HEADS // KV_HEADS
ROPE_DIMS = HEAD_DIM // 4
ROPE_THETA = 500000.0
N_KV_STREAMS = 6
CMP_BLOCK = 32
CMP_STRIDE = 16
SEL_BLOCK = 64
N_SEL = 16
N_LOCAL_BLOCKS = 2
FORCE_SCORE = 1.0e4
WINDOW = 512
N_GROUPS = 4
EXPERTS_PER_GROUP = 8
N_EXPERTS = N_GROUPS * EXPERTS_PER_GROUP
D_EXPERT = 256

LANES = 128
VMEM_LIMIT = 48 << 20
NEG_BIG = -1e30
M_FLOOR = -1e29
ATT_SCALE = HEAD_DIM ** -0.5
TOKEN_TILE = 512


def _round_up(x, m):
    return (x + m - 1) // m * m


def _cparams(sem):
    return pltpu.CompilerParams(dimension_semantics=sem, vmem_limit_bytes=VMEM_LIMIT)


_NT = (((1,), (1,)), ((), ()))


def _mm_kernel(x_ref, w_ref, o_ref):
    o_ref[...] = jnp.dot(x_ref[...].astype(BF16), w_ref[...], preferred_element_type=F32)


def mm(x, w, tm=TOKEN_TILE):
    m, k = x.shape
    n = w.shape[1]
    tm = min(tm, _round_up(m, 8))
    mp = _round_up(m, tm)
    if mp != m:
        x = jnp.pad(x, ((0, mp - m), (0, 0)))
    out = pl.pallas_call(
        _mm_kernel,
        grid=(mp // tm,),
        in_specs=[pl.BlockSpec((tm, k), lambda i: (i, 0)),
                  pl.BlockSpec((k, n), lambda i: (0, 0))],
        out_specs=pl.BlockSpec((tm, n), lambda i: (i, 0)),
        out_shape=jax.ShapeDtypeStruct((mp, n), F32),
        compiler_params=_cparams(("arbitrary",)),
        name="mm",
    )(x, w.astype(BF16))
    return out[:m] if mp != m else out


SCAN_TC = 32


def _scan_kernel(r_ref, d_ref, kin_ref, a_ref, v_ref, kkp_ref, kap_ref, rk_ref, gw_ref, gb_ref, s0_ref, o_ref, st_ref,
                 s_scr, n_ref, k_ref, b_ref, *, tc, dot_dtype):
    c = pl.program_id(1)
    rnd = lambda x: x.astype(dot_dtype).astype(F32)
    halves = [(h0, h0 + RW_HEAD // 2) for h0 in (0, RW_HEAD // 2)]

    @pl.when(c == 0)
    def _():
        s_scr[...] = s0_ref[...]

    def step(t, carry):
        vv = v_ref[t]
        k_t = kin_ref[t]
        a_t = a_ref[t]
        kk = k_t * kkp_ref[...]
        kk = kk / jnp.maximum(jnp.sqrt(jnp.sum(kk * kk, axis=0, keepdims=True)), 1e-12)
        k2 = k_t * (1.0 + (a_t - 1.0) * kap_ref[...])
        n_ref[...] = kk
        k_ref[...] = k2
        b_ref[...] = kk * a_t
        outs = []
        for lo, hi in halves:
            parts = [jnp.zeros((hi - lo, LANES), F32)] * 2
            for j in range(RW_HEAD):
                parts[j % 2] = parts[j % 2] + rnd(s_scr[j, lo:hi]) * rnd(n_ref[pl.ds(j, 1), :])
            sa = -(parts[0] + parts[1])
            vh = vv[lo:hi]
            out = jnp.zeros((hi - lo, LANES), F32)
            for j in range(RW_HEAD):
                sn = (s_scr[j, lo:hi] * d_ref[t, pl.ds(j, 1), :] + sa * b_ref[pl.ds(j, 1), :]
                      + vh * k_ref[pl.ds(j, 1), :])
                s_scr[j, lo:hi] = sn
                out = out + rnd(sn) * rnd(r_ref[t, pl.ds(j, 1), :])
            outs.append(out)
        out = jnp.concatenate(outs, axis=0)
        mean = jnp.mean(out, axis=0, keepdims=True)
        cen = out - mean
        var = jnp.mean(cen * cen, axis=0, keepdims=True)
        bonus = jnp.sum(r_ref[t] * k2 * rk_ref[...], axis=0, keepdims=True) * vv
        o_ref[t] = cen * lax.rsqrt(var + RW_GN_EPS) * gw_ref[...] + gb_ref[...] + bonus
        return carry

    lax.fori_loop(0, tc, step, 0)

    @pl.when(c == pl.num_programs(1) - 1)
    def _():
        st_ref[...] = s_scr[...]


def rwkv_scan(r, d, k, a, v, kkp, kap, rk, gw, gb, s0, dot_dtype=F32):
    t, _, l = r.shape
    tc = min(SCAN_TC, t)
    assert t % tc == 0 and l % LANES == 0
    seq = pl.BlockSpec((tc, RW_HEAD, LANES), lambda g, c: (c, 0, g))
    vec = pl.BlockSpec((RW_HEAD, LANES), lambda g, c: (0, g))
    st = pl.BlockSpec((RW_HEAD, RW_HEAD, LANES), lambda g, c: (0, 0, g))
    return pl.pallas_call(
        functools.partial(_scan_kernel, tc=tc, dot_dtype=dot_dtype),
        grid=(l // LANES, t // tc),
        in_specs=[seq] * 5 + [vec] * 5 + [st],
        out_specs=[seq, st],
        out_shape=[jax.ShapeDtypeStruct((t, RW_HEAD, l), F32),
                   jax.ShapeDtypeStruct((RW_HEAD, RW_HEAD, l), F32)],
        scratch_shapes=[pltpu.VMEM((RW_HEAD, RW_HEAD, LANES), F32)] + [pltpu.VMEM((RW_HEAD, LANES), F32)] * 3,
        compiler_params=_cparams(("arbitrary", "arbitrary")),
        name="rwkv_scan",
    )(r, d, k, a, v, kkp, kap, rk, gw, gb, s0)


TIME_BLK = 2 * LANES


def _time_major_kernel(x_ref, o_ref):
    o_ref[...] = x_ref[...].reshape(RW_HEAD * LANES, TIME_BLK).T.reshape(TIME_BLK, RW_HEAD, LANES)


def _time_minor_kernel(x_ref, o_ref):
    o_ref[...] = x_ref[...].reshape(TIME_BLK, RW_HEAD * LANES).T.reshape(RW_HEAD, LANES, TIME_BLK)


def time_major(x):
    _, l, t = x.shape
    return pl.pallas_call(
        _time_major_kernel,
        grid=(l // LANES, t // TIME_BLK),
        in_specs=[pl.BlockSpec((RW_HEAD, LANES, TIME_BLK), lambda g, c: (0, g, c))],
        out_specs=pl.BlockSpec((TIME_BLK, RW_HEAD, LANES), lambda g, c: (c, 0, g)),
        out_shape=jax.ShapeDtypeStruct((t, RW_HEAD, l), F32),
        compiler_params=_cparams(("arbitrary", "arbitrary")),
        name="time_major",
    )(x)


def time_minor(x):
    t, _, l = x.shape
    return pl.pallas_call(
        _time_minor_kernel,
        grid=(l // LANES, t // TIME_BLK),
        in_specs=[pl.BlockSpec((TIME_BLK, RW_HEAD, LANES), lambda g, c: (c, 0, g))],
        out_specs=pl.BlockSpec((RW_HEAD, LANES, TIME_BLK), lambda g, c: (0, g, c)),
        out_shape=jax.ShapeDtypeStruct((RW_HEAD, l, t), F32),
        compiler_params=_cparams(("arbitrary", "arbitrary")),
        name="time_minor",
    )(x)


RW_TM = 256


def _split_dot(x, w):
    hi = x.astype(BF16)
    lo = (x - hi.astype(F32)).astype(BF16)
    return jnp.dot(hi, w, preferred_element_type=F32) + jnp.dot(lo, w, preferred_element_type=F32)


def _rwkv_pre_kernel(x_ref, ps_ref, ln_ref, mu_ref, vec_ref, wr_ref, wk_ref, wv_ref, w1_ref, w2_ref, a1_ref, a2_ref,
                     g1_ref, g2_ref,
                     r_o, d_o, k_o, a_o, v_o, g_o, xn_o, carry_scr, *, tiles_per_seq, n_seq_tiles):
    i = pl.program_id(0)
    tm = x_ref.shape[0]

    @pl.when(i == 0)
    def _():
        carry_scr[...] = jnp.zeros_like(carry_scr)

    x = x_ref[...]
    xn = x * lax.rsqrt(jnp.mean(x * x, axis=-1, keepdims=True) + RMS_EPS) * ln_ref[...]
    xn_o[...] = xn
    first = jnp.where(i % tiles_per_seq == 0, jnp.zeros_like(carry_scr[...]), carry_scr[...])
    row0 = lax.broadcasted_iota(jnp.int32, xn.shape, 0) == 0
    prev = jnp.where(row0, first, pltpu.roll(xn, 1, 0))
    prev = jnp.where(i >= n_seq_tiles, ps_ref[...], prev)
    carry_scr[...] = xn[tm - 1:tm, :]
    xx = prev - xn
    mix = lambda c: (xn + xx * mu_ref[c:c + 1, :]).astype(BF16)
    dot = lambda a, w_ref: jnp.dot(a, w_ref[...], preferred_element_type=F32)
    w0, a0 = (vec_ref[c:c + 1, :] for c in range(2))
    r_o[...] = dot(mix(0), wr_ref)
    k_o[...] = dot(mix(2), wk_ref)
    v_o[...] = dot(mix(3), wv_ref)
    w_log = -jax.nn.softplus(-(w0 + dot(jnp.tanh(dot(mix(1), w1_ref)).astype(BF16), w2_ref))) - 0.5
    d_o[...] = jnp.exp(-jnp.exp(w_log))
    a_o[...] = jax.nn.sigmoid(a0 + dot(dot(mix(4), a1_ref).astype(BF16), a2_ref))
    g_o[...] = dot(jax.nn.sigmoid(dot(mix(5), g1_ref)).astype(BF16), g2_ref)


def rwkv_pre(h, prev_single, n_seq_rows, t, ln, mu, w_r, w_k, w_v, w0, w1, w2, a0, a1, a2, g1, g2):
    n, dm = h.shape
    tm = RW_TM
    assert n % tm == 0 and t % tm == 0 and n_seq_rows % tm == 0 and prev_single.shape == (tm, dm)
    row = pl.BlockSpec((tm, dm), lambda i: (i, 0))
    full = lambda a: pl.BlockSpec(a.shape, lambda i: (0,) * a.ndim)
    bf = lambda w: w.astype(BF16)
    consts = [prev_single, ln[None, :], mu, jnp.stack([w0, a0]), bf(w_r), bf(w_k), bf(w_v), bf(w1), bf(w2),
              bf(a1), bf(a2), bf(g1), bf(g2)]
    return pl.pallas_call(
        functools.partial(_rwkv_pre_kernel, tiles_per_seq=t // tm, n_seq_tiles=n_seq_rows // tm),
        grid=(n // tm,),
        in_specs=[row] + [full(a) for a in consts],
        out_specs=[row] * 7,
        out_shape=[jax.ShapeDtypeStruct((n, dm), F32)] * 7,
        scratch_shapes=[pltpu.VMEM((1, dm), F32)],
        compiler_params=_cparams(("arbitrary",)),
        name="rwkv_pre",
    )(h, *consts)


def _mix_out_kernel(*refs, n_branch, use_gates):
    o_refs = refs[:n_branch]
    gate_ref, h_ref, wo_ref, ln_ref, wr_ref = refs[n_branch:n_branch + 5]
    e_refs = refs[n_branch + 5:n_branch + 5 + (n_branch if use_gates else 0)]
    h_o, xp_o, lg_o = refs[-3:]
    if use_gates:
        sig = jax.nn.sigmoid(gate_ref[...])
        mixed = sum(_split_dot(sig, e_refs[b][...]) * o_refs[b][...] for b in range(n_branch))
    else:
        mixed = o_refs[0][...] * gate_ref[...]
    h = h_ref[...] + jnp.dot(mixed.astype(BF16), wo_ref[...], preferred_element_type=F32)
    h_o[...] = h
    xn = h * lax.rsqrt(jnp.mean(h * h, axis=-1, keepdims=True) + RMS_EPS) * ln_ref[...]
    xb = xn.astype(BF16)
    lg_o[...] = jnp.dot(xb, wr_ref[...], preferred_element_type=F32)
    u = lax.bitcast_convert_type(xb.astype(F32), jnp.uint32)
    half = u.shape[1] // 2
    xp_o[...] = (u[:, :half] >> 16) | (u[:, half:] & jnp.uint32(0xFFFF0000))


def mix_out(branches, gate, h, w_o, ln, w_router, gate_expand=None, gate_col=0):
    n, dm = h.shape
    tm = RW_TM
    nb = len(branches)
    row = lambda a: pl.BlockSpec((tm, a.shape[1]), lambda i: (i, 0))
    full = lambda a: pl.BlockSpec(a.shape, lambda i: (0,) * a.ndim)
    consts = [w_o.astype(BF16), ln[None, :], w_router.astype(BF16)] + list(gate_expand or [])
    gate_spec = row(gate) if gate_expand is None else pl.BlockSpec((tm, LANES), lambda i: (i, gate_col))
    return pl.pallas_call(
        functools.partial(_mix_out_kernel, n_branch=nb, use_gates=gate_expand is not None),
        grid=(n // tm,),
        in_specs=[row(a) for a in branches] + [gate_spec, row(h)] + [full(a) for a in consts],
        out_specs=[pl.BlockSpec((tm, dm), lambda i: (i, 0)), pl.BlockSpec((tm, dm // 2), lambda i: (i, 0)),
                   pl.BlockSpec((tm, LANES), lambda i: (i, 0))],
        out_shape=[jax.ShapeDtypeStruct((n, dm), F32), jax.ShapeDtypeStruct((n, dm // 2), jnp.uint32),
                   jax.ShapeDtypeStruct((n, LANES), F32)],
        compiler_params=_cparams(("arbitrary",)),
        name="mix_out",
    )(*branches, gate, h, *consts)


MOE_TM = 256
MOE_VMEM_LIMIT = 56 << 20


def _moe_kernel(te_ref, nt_ref, tok_ref, x_ref, c_ref, wgu_ref, wd_ref, o_ref, xt_scr):
    i = pl.program_id(0)

    @pl.when(i < nt_ref[0])
    def _():
        base = i * MOE_TM

        def gather_rows(g8, carry):
            r0 = pl.multiple_of(g8 * 8, 8)
            for u in range(8):
                xt_scr[pl.ds(r0 + u, 1), :] = x_ref[pl.ds(tok_ref[base + r0 + u], 1), :]
            return carry

        lax.fori_loop(0, MOE_TM // 8, gather_rows, 0)
        u = xt_scr[...]
        lo = lax.bitcast_convert_type(u << 16, F32).astype(BF16)
        hi = lax.bitcast_convert_type(u & jnp.uint32(0xFFFF0000), F32).astype(BF16)
        x = jnp.concatenate([lo, hi], axis=1)
        h = jnp.dot(x, wgu_ref[0], preferred_element_type=F32)
        g = h[:, :D_EXPERT]
        u2 = h[:, D_EXPERT:]
        hid = g * jax.nn.sigmoid(g) * u2 * c_ref[...]
        o_ref[...] = jnp.dot(hid.astype(BF16), wd_ref[0], preferred_element_type=F32)

    @pl.when(i >= nt_ref[0])
    def _():
        o_ref[...] = jnp.zeros_like(o_ref)


def moe_gmm(xp, row_tok, cs, tile_expert, n_tiles_used, wgu, wd):
    n, half = xp.shape
    dm = 2 * half
    p = row_tok.shape[0]
    nt = p // MOE_TM

    def row_map(i, te, ntu, tok):
        return (jnp.minimum(i, ntu[0] - 1), 0)

    grid_spec = pltpu.PrefetchScalarGridSpec(
        num_scalar_prefetch=3,
        grid=(nt,),
        in_specs=[pl.BlockSpec((n, half), lambda i, te, ntu, tok: (0, 0), pipeline_mode=pl.Buffered(1)),
                  pl.BlockSpec((MOE_TM, 1), row_map),
                  pl.BlockSpec((1, dm, 2 * D_EXPERT), lambda i, te, ntu, tok: (te[i], 0, 0)),
                  pl.BlockSpec((1, D_EXPERT, dm), lambda i, te, ntu, tok: (te[i], 0, 0))],
        out_specs=pl.BlockSpec((MOE_TM, dm), lambda i, te, ntu, tok: (i, 0)),
        scratch_shapes=[pltpu.VMEM((MOE_TM, half), jnp.uint32)],
    )
    return pl.pallas_call(
        _moe_kernel,
        grid_spec=grid_spec,
        out_shape=jax.ShapeDtypeStruct((p, dm), F32),
        compiler_params=pltpu.CompilerParams(dimension_semantics=("arbitrary",), vmem_limit_bytes=MOE_VMEM_LIMIT),
        name="moe_gmm",
    )(tile_expert, n_tiles_used, row_tok, xp, cs, wgu, wd)


def router_weight(w_rg, w_re):
    w_r = jnp.concatenate([w_rg, w_re], axis=1)
    return jnp.pad(w_r, ((0, 0), (0, LANES - w_r.shape[1])))


def hmoe(xp, logits, set_sizes, b_rg, b_re, w_gate, w_up, w_down):
    n = xp.shape[0]
    dm = 2 * xp.shape[1]
    lg = logits[:, :N_GROUPS] + b_rg
    le = (logits[:, N_GROUPS:N_GROUPS + N_EXPERTS] + b_re).reshape(n, N_GROUPS, EXPERTS_PER_GROUP)
    g_sel = jnp.argmax(lg, axis=-1)
    p_g = jnp.max(jax.nn.softmax(lg, axis=-1), axis=-1)
    le_g = jnp.take_along_axis(le, g_sel[:, None, None], axis=1)[:, 0]
    top_p, top_i = lax.top_k(jax.nn.softmax(le_g, axis=-1), 2)
    w_top = p_g[:, None] * top_p / jnp.sum(top_p, axis=-1, keepdims=True)
    eid = (g_sel[:, None] * EXPERTS_PER_GROUP + top_i).astype(jnp.int32).reshape(-1)
    cw = w_top.reshape(-1)

    n2 = 2 * n
    onehot = (eid[:, None] == jnp.arange(N_EXPERTS, dtype=jnp.int32)[None, :]).astype(jnp.int32)
    counts = jnp.sum(onehot, axis=0)
    rank = jnp.take_along_axis(jnp.cumsum(onehot, axis=0) - onehot, eid[:, None], axis=1)[:, 0]
    tiles_per = (counts + MOE_TM - 1) // MOE_TM
    tile_end = jnp.cumsum(tiles_per)
    tile_start = tile_end - tiles_per
    pos = tile_start[eid] * MOE_TM + rank
    p_rows = _round_up(n2 + N_EXPERTS * (MOE_TM - 1), MOE_TM)
    nt = p_rows // MOE_TM
    n_used = tile_end[-1].astype(jnp.int32)
    tile_ids = jnp.arange(nt, dtype=jnp.int32)
    tile_expert = jnp.sum((tile_end[None, :] <= jnp.minimum(tile_ids, n_used - 1)[:, None]).astype(jnp.int32), axis=1)

    assert n < 2 ** 24
    tok = (jnp.arange(n2, dtype=jnp.int32) // 2).astype(F32)
    rows = jnp.zeros((p_rows, 2), F32).at[pos].set(jnp.stack([tok, cw], axis=1))
    row_tok = rows[:, 0].astype(jnp.int32)
    wgu = jnp.concatenate([w_gate, w_up], axis=-1).reshape(N_EXPERTS, dm, 2 * D_EXPERT).astype(BF16)
    wd = w_down.reshape(N_EXPERTS, D_EXPERT, dm).astype(BF16)
    ys = moe_gmm(xp, row_tok, rows[:, 1:2], tile_expert, n_used.reshape(1), wgu, wd)
    pos2 = pos.reshape(n, 2)
    bounds = np.cumsum([0] + list(set_sizes))
    return [ys[pos2[a:b, 0]] + ys[pos2[a:b, 1]] for a, b in zip(bounds[:-1], bounds[1:])]


ATT_TQ = 512
ATT_TK = 512


def _cmp_sel_kernel(q_ref, kc_ref, vct_ref, ovt_ref, o_ref, sel_ref, *, n_cmp, n_sblk, tq):
    t0 = pl.program_id(2) * tq
    ncp = kc_ref.shape[2]
    n_io = lax.broadcasted_iota(jnp.int32, (ncp, tq), 0)
    t_io = lax.broadcasted_iota(jnp.int32, (ncp, tq), 1) + t0
    mask = (n_io * CMP_STRIDE + (CMP_BLOCK - 1) <= t_io) & (n_io < n_cmp)
    kc = kc_ref[0, 0]
    vct = vct_ref[0, 0]
    ovt = ovt_ref[...]
    imp = jnp.zeros((n_sblk, tq), F32)
    outs = []
    for r in range(GQA_R):
        qr = (q_ref[:, r * HEAD_DIM:(r + 1) * HEAD_DIM] * ATT_SCALE).astype(BF16)
        st = lax.dot_general(kc, qr, _NT, preferred_element_type=F32)
        st = jnp.where(mask, st, NEG_BIG)
        m = jnp.max(st, axis=0, keepdims=True)
        e = jnp.where(mask, jnp.exp(st - m), 0.0)
        p = (e / jnp.maximum(jnp.sum(e, axis=0, keepdims=True), 1e-30)).astype(BF16)
        outs.append(jnp.dot(vct, p, preferred_element_type=F32).T)
        imp = imp + jnp.dot(ovt, p, preferred_element_type=F32)
    o_ref[...] = jnp.concatenate(outs, axis=1)

    j_io = lax.broadcasted_iota(jnp.int32, (n_sblk, tq), 0)
    qblk = jnp.right_shift(lax.broadcasted_iota(jnp.int32, (n_sblk, tq), 1) + t0, SEL_BLOCK.bit_length() - 1)
    allowed = j_io <= qblk
    forced = (j_io == 0) | ((j_io > qblk - N_LOCAL_BLOCKS) & allowed)
    score = jnp.where(forced, FORCE_SCORE, imp)
    score = jnp.where(allowed, score, -jnp.inf)
    rank = jnp.zeros((n_sblk, tq), jnp.int32)
    for jp in range(n_sblk):
        row = score[jp:jp + 1, :]
        beats = (row > score) | ((row == score) & (j_io > jp))
        rank = rank + jnp.where(beats, 1, 0)
    sel = (rank < N_SEL) & allowed
    sel_ref[0, 0] = jnp.where(sel, 1.0, 0.0).astype(BF16)


def cmp_select(q, kc, vct, ovt, n_b, t, n_cmp, n_sblk, n_out):
    assert SEL_BLOCK & (SEL_BLOCK - 1) == 0
    tq = ATT_TQ
    nq = t // tq
    ncp = kc.shape[2]
    gw = GQA_R * HEAD_DIM
    return pl.pallas_call(
        functools.partial(_cmp_sel_kernel, n_cmp=n_cmp, n_sblk=n_sblk, tq=tq),
        grid=(n_b, KV_HEADS, nq),
        in_specs=[pl.BlockSpec((tq, gw), lambda b, g, qi: (b * nq + qi, g)),
                  pl.BlockSpec((1, 1, ncp, HEAD_DIM), lambda b, g, qi: (b, g, 0, 0)),
                  pl.BlockSpec((1, 1, HEAD_DIM, ncp), lambda b, g, qi: (b, g, 0, 0)),
                  pl.BlockSpec((n_sblk, ncp), lambda b, g, qi: (0, 0))],
        out_specs=[pl.BlockSpec((tq, gw), lambda b, g, qi: (b * nq + qi, g)),
                   pl.BlockSpec((1, 1, n_sblk, tq), lambda b, g, qi: (b, g, 0, qi))],
        out_shape=[jax.ShapeDtypeStruct((n_out, N_HEADS * HEAD_DIM), F32),
                   jax.ShapeDtypeStruct((n_b, KV_HEADS, n_sblk, t), BF16)],
        compiler_params=_cparams(("arbitrary", "arbitrary", "arbitrary")),
        name="attn_cmp_select",
    )(q, kc, vct, ovt)


def _flash_kernel(pq_ref, pk_ref, pfirst_ref, plast_ref, q_ref, kt_ref, vt_ref, *rest, window, use_sel, tq, tk):
    if use_sel:
        sel_ref, et_ref, o_ref, m_scr, l_scr, acc_scr = rest
    else:
        o_ref, m_scr, l_scr, acc_scr = rest
    p_id = pl.program_id(2)

    @pl.when(pfirst_ref[p_id] == 1)
    def _():
        m_scr[...] = jnp.full_like(m_scr, M_FLOOR)
        l_scr[...] = jnp.zeros_like(l_scr)
        acc_scr[...] = jnp.zeros_like(acc_scr)

    off = pq_ref[p_id] * tq - pk_ref[p_id] * tk
    diff = lax.broadcasted_iota(jnp.int32, (tk, tq), 0) - lax.broadcasted_iota(jnp.int32, (tk, tq), 1)
    mask = diff <= off
    if window is not None:
        mask = mask & (diff > off - window)
    if use_sel:
        selm = jnp.dot(et_ref[...], sel_ref[0, 0], preferred_element_type=F32)
        mask = mask & (selm > 0.5)
    bias = jnp.where(mask, 0.0, NEG_BIG)
    k = kt_ref[0, 0, 0].astype(BF16).T
    vt = vt_ref[0, 0, 0].astype(BF16)
    heads = range(GQA_R)
    qs = [(q_ref[:, r * HEAD_DIM:(r + 1) * HEAD_DIM] * ATT_SCALE).astype(BF16) for r in heads]
    sts = [lax.dot_general(k, qs[r], _NT, preferred_element_type=F32) + bias for r in heads]
    m_prev = [m_scr[r] for r in heads]
    m_new = [jnp.maximum(m_prev[r], jnp.max(sts[r], axis=0, keepdims=True)) for r in heads]
    ps = [jnp.exp(sts[r] - m_new[r]) for r in heads]
    alphas = [jnp.exp(m_prev[r] - m_new[r]) for r in heads]
    for r in heads:
        l_scr[r] = alphas[r] * l_scr[r] + jnp.sum(ps[r], axis=0, keepdims=True)
        acc_scr[r] = alphas[r] * acc_scr[r] + jnp.dot(vt, ps[r].astype(BF16), preferred_element_type=F32)
        m_scr[r] = m_new[r]

    @pl.when(plast_ref[p_id] == 1)
    def _():
        outs = [(acc_scr[r] / jnp.maximum(l_scr[r], 1e-30)).T for r in range(GQA_R)]
        o_ref[...] = jnp.concatenate(outs, axis=1)


def flash_attention(q, kv_t, k_stream, v_stream, n_b, t, window=None, sel=None):
    tq, tk = ATT_TQ, ATT_TK
    nq = t // tq
    first_tile = lambda qi: 0 if window is None else max(0, qi * tq - (window - 1)) // tk
    pairs = [(qi, kt) for qi in range(nq) for kt in range(first_tile(qi), (qi * tq + tq - 1) // tk + 1)]
    pq = np.array([p[0] for p in pairs], np.int32)
    pk = np.array([p[1] for p in pairs], np.int32)
    pfirst = np.array([1 if i == 0 or pairs[i - 1][0] != p[0] else 0 for i, p in enumerate(pairs)], np.int32)
    plast = np.array([1 if i == len(pairs) - 1 or pairs[i + 1][0] != p[0] else 0 for i, p in enumerate(pairs)], np.int32)
    gw = GQA_R * HEAD_DIM

    def tile_map(stream):
        return lambda b, g, p, pq_r, pk_r, f_r, l_r: (b, stream, g, 0, pk_r[p])

    in_specs = [pl.BlockSpec((tq, gw), lambda b, g, p, pq_r, pk_r, f_r, l_r: (b * nq + pq_r[p], g)),
                pl.BlockSpec((1, 1, 1, HEAD_DIM, tk), tile_map(k_stream)),
                pl.BlockSpec((1, 1, 1, HEAD_DIM, tk), tile_map(v_stream))]
    args = [q, kv_t, kv_t]
    if sel is not None:
        n_sblk = sel.shape[2]
        emat = (np.arange(t)[:, None] // SEL_BLOCK == np.arange(n_sblk)[None, :]).astype(np.float32)
        in_specs += [pl.BlockSpec((1, 1, n_sblk, tq), lambda b, g, p, pq_r, pk_r, f_r, l_r: (b, g, 0, pq_r[p])),
                     pl.BlockSpec((tk, n_sblk), lambda b, g, p, pq_r, pk_r, f_r, l_r: (pk_r[p], 0))]
        args += [sel, jnp.asarray(emat, BF16)]
    grid_spec = pltpu.PrefetchScalarGridSpec(
        num_scalar_prefetch=4,
        grid=(n_b, KV_HEADS, len(pairs)),
        in_specs=in_specs,
        out_specs=pl.BlockSpec((tq, gw), lambda b, g, p, pq_r, pk_r, f_r, l_r: (b * nq + pq_r[p], g)),
        scratch_shapes=[pltpu.VMEM((GQA_R, 1, tq), F32), pltpu.VMEM((GQA_R, 1, tq), F32),
                        pltpu.VMEM((GQA_R, HEAD_DIM, tq), F32)],
    )
    return pl.pallas_call(
        functools.partial(_flash_kernel, window=window, use_sel=sel is not None, tq=tq, tk=tk),
        grid_spec=grid_spec,
        out_shape=jax.ShapeDtypeStruct((n_b * t, N_HEADS * HEAD_DIM), F32),
        compiler_params=_cparams(("arbitrary", "arbitrary", "arbitrary")),
        name="attn_sel" if sel is not None else "attn_win",
    )(jnp.asarray(pq), jnp.asarray(pk), jnp.asarray(pfirst), jnp.asarray(plast), *args)


def _attn_pre_kernel(h_ref, ln_ref, wq_ref, wkv_ref, rc_ref, rm_ref, rp_ref, qg_o, qr_o, kvt_o):
    x = h_ref[...]
    xs = x * lax.rsqrt(jnp.mean(x * x, axis=-1, keepdims=True) + RMS_EPS)
    qg = jnp.dot((xs * ln_ref[0:1, :]).astype(BF16), wq_ref[...], preferred_element_type=F32)
    kv = jnp.dot((xs * ln_ref[1:2, :]).astype(BF16), wkv_ref[...], preferred_element_type=F32)
    qg_o[...] = qg

    def rope(v):
        reps = v.shape[1] // LANES
        tile = lambda ref: jnp.concatenate([ref[...]] * reps, axis=1)
        n = v.shape[1]
        return v * tile(rc_ref) + pltpu.roll(v, n - ROPE_DIMS // 2, 1) * tile(rm_ref) + pltpu.roll(v, ROPE_DIMS // 2, 1) * tile(rp_ref)

    n_q = N_HEADS * HEAD_DIM
    sw = KV_HEADS * HEAD_DIM
    qr_o[...] = rope(qg[:, :n_q])
    kv_r = jnp.concatenate([kv[:, :2 * sw], rope(kv[:, 2 * sw:3 * sw]), kv[:, 3 * sw:4 * sw],
                            rope(kv[:, 4 * sw:5 * sw]), kv[:, 5 * sw:]], axis=1)
    kvt_o[0] = kv_r.T


def attn_pre(h, n_b, t, ln_q, ln_kv, w_qg, w_kv):
    n, dm = h.shape
    tm = RW_TM
    tiles = t // tm
    half = ROPE_DIMS // 2
    inv = jnp.power(ROPE_THETA, -jnp.arange(half, dtype=F32) * (2.0 / ROPE_DIMS))
    ang = jnp.arange(t, dtype=F32)[:, None] * inv[None, :]
    cos, sin = jnp.cos(ang), jnp.sin(ang)
    pad = jnp.zeros((t, HEAD_DIM - ROPE_DIMS), F32)
    zero = jnp.zeros((t, half), F32)
    head = lambda parts: jnp.concatenate(parts, axis=1)
    two = lambda x: jnp.concatenate([x, x], axis=1)
    rc = two(head([cos, cos, pad + 1.0]))
    rm = two(head([-sin, zero, pad]))
    rp = two(head([zero, sin, pad]))
    nq_cols = w_qg.shape[1]
    nkv = w_kv.shape[1]
    tab = pl.BlockSpec((tm, LANES), lambda i: (i % tiles, 0))
    full = lambda a: pl.BlockSpec(a.shape, lambda i: (0,) * a.ndim)
    consts = [jnp.stack([ln_q, ln_kv]), w_qg.astype(BF16), w_kv.astype(BF16)]
    return pl.pallas_call(
        _attn_pre_kernel,
        grid=(n // tm,),
        in_specs=[pl.BlockSpec((tm, dm), lambda i: (i, 0))] + [full(a) for a in consts] + [tab, tab, tab],
        out_specs=[pl.BlockSpec((tm, nq_cols), lambda i: (i, 0)),
                   pl.BlockSpec((tm, N_HEADS * HEAD_DIM), lambda i: (i, 0)),
                   pl.BlockSpec((1, nkv, tm), lambda i: (i // tiles, 0, i % tiles))],
        out_shape=[jax.ShapeDtypeStruct((n, nq_cols), F32), jax.ShapeDtypeStruct((n, N_HEADS * HEAD_DIM), F32),
                   jax.ShapeDtypeStruct((n_b, nkv, t), F32)],
        compiler_params=_cparams(("arbitrary",)),
        name="attn_pre",
    )(h, *consts, rc, rm, rp)


def _cmp_kernel(q_ref, k_ref, v_ref, ov_ref, o_ref, imp_ref, *, n_cmp):
    q = q_ref[0] * ATT_SCALE
    s = lax.dot_general(q, k_ref[0], _NT, preferred_element_type=F32)
    mask = lax.broadcasted_iota(jnp.int32, s.shape, 1) < n_cmp
    s = jnp.where(mask, s, NEG_BIG)
    m = jnp.max(s, axis=-1, keepdims=True)
    e = jnp.where(mask, jnp.exp(s - m), 0.0)
    p = (e / jnp.maximum(jnp.sum(e, axis=-1, keepdims=True), 1e-30)).astype(BF16)
    o_ref[0] = jnp.dot(p, v_ref[0], preferred_element_type=F32)
    imp_ref[0] = jnp.dot(p, ov_ref[...], preferred_element_type=F32)


def cmp_attention_sample(q, kc, vc, ov, n_cmp):
    bg, rq, hd = q.shape
    nc = kc.shape[1]
    nb = ov.shape[1]
    return pl.pallas_call(
        functools.partial(_cmp_kernel, n_cmp=n_cmp),
        grid=(bg,),
        in_specs=[pl.BlockSpec((1, rq, hd), lambda b: (b, 0, 0)),
                  pl.BlockSpec((1, nc, hd), lambda b: (b, 0, 0)),
                  pl.BlockSpec((1, nc, hd), lambda b: (b, 0, 0)),
                  pl.BlockSpec((nc, nb), lambda b: (0, 0))],
        out_specs=[pl.BlockSpec((1, rq, hd), lambda b: (b, 0, 0)),
                   pl.BlockSpec((1, rq, nb), lambda b: (b, 0, 0))],
        out_shape=[jax.ShapeDtypeStruct((bg, rq, hd), F32),
                   jax.ShapeDtypeStruct((bg, rq, nb), F32)],
        compiler_params=_cparams(("arbitrary",)),
        name="attn_cmp_sample",
    )(q, kc, vc, ov)


PAGES_PER_STEP = 16


def _page_compress_kernel(pt_ref, *rest, n_pages, page, pb):
    c_refs = rest[:pb]
    wbd_ref, pe_ref, w2_ref, o_ref, a_scr = rest[pb:]
    p = pl.program_id(1)
    nsub = n_pages * page // CMP_STRIDE
    for i in range(pb):
        row0 = pl.multiple_of((p * pb + i) * page, page)
        for s in range(2):
            for gp in range(KV_HEADS // 2):
                x = c_refs[i][0, s, 2 * gp:2 * gp + 2].reshape(2 * HEAD_DIM, page)
                a_scr[s, gp, pl.ds(row0, page), :] = x.T

    @pl.when(p == n_pages // pb - 1)
    def _():
        hid = pe_ref.shape[-1]
        for s in range(2):
            for gp in range(KV_HEADS // 2):
                acc = jnp.zeros((nsub, 4 * hid), F32)
                for j in range(CMP_STRIDE):
                    lhs = a_scr[s, gp, pl.ds(j, nsub, stride=CMP_STRIDE), :].astype(BF16)
                    acc = acc + jnp.dot(lhs, wbd_ref[s, j], preferred_element_type=F32)
                for g2 in range(2):
                    first = acc[:, 2 * g2 * hid:(2 * g2 + 1) * hid]
                    second = acc[:, (2 * g2 + 1) * hid:(2 * g2 + 2) * hid]
                    pre = first + pltpu.roll(second, nsub - 1, 0) + pe_ref[s]
                    act = jax.nn.gelu(pre).astype(BF16)
                    o_ref[s, 0, 2 * gp + g2] = jnp.dot(act, w2_ref[s], preferred_element_type=F32).astype(BF16)


def page_compress(cache_t, page_table, cmp_pe, cmp_w1, cmp_b1, cmp_w2):
    n_b, n_pages = page_table.shape
    page = cache_t.shape[-1]
    nsub = n_pages * page // CMP_STRIDE
    hid = cmp_w1.shape[-1]
    pb = PAGES_PER_STEP
    assert n_pages % pb == 0
    w1r = cmp_w1.reshape(2, 2, CMP_STRIDE, HEAD_DIM, hid).transpose(0, 2, 3, 1, 4)
    w1r = w1r.reshape(2, CMP_STRIDE, HEAD_DIM, 2 * hid)
    zeros = jnp.zeros_like(w1r)
    wbd = jnp.concatenate([jnp.concatenate([w1r, zeros], axis=-1), jnp.concatenate([zeros, w1r], axis=-1)], axis=2)
    pe_term = (jnp.einsum('sjd,sjdc->sc', cmp_pe, cmp_w1.reshape(2, CMP_BLOCK, HEAD_DIM, hid)) + cmp_b1)[:, None, :]
    def page_map(i):
        return lambda b, p, pt: (pt[b * n_pages + p * pb + i], 0, 0, 0, 0)

    grid_spec = pltpu.PrefetchScalarGridSpec(
        num_scalar_prefetch=1,
        grid=(n_b, n_pages // pb),
        in_specs=[pl.BlockSpec((1, 2, KV_HEADS, HEAD_DIM, page), page_map(i)) for i in range(pb)] + [
            pl.BlockSpec((2, CMP_STRIDE, 2 * HEAD_DIM, 4 * hid), lambda b, p, pt: (0, 0, 0, 0)),
            pl.BlockSpec((2, 1, hid), lambda b, p, pt: (0, 0, 0)),
            pl.BlockSpec((2, hid, HEAD_DIM), lambda b, p, pt: (0, 0, 0))],
        out_specs=pl.BlockSpec((2, 1, KV_HEADS, nsub, HEAD_DIM), lambda b, p, pt: (0, b, 0, 0, 0)),
        scratch_shapes=[pltpu.VMEM((2, KV_HEADS // 2, n_pages * page, 2 * HEAD_DIM), F32)],
    )
    return pl.pallas_call(
        functools.partial(_page_compress_kernel, n_pages=n_pages, page=page, pb=pb),
        grid_spec=grid_spec,
        out_shape=jax.ShapeDtypeStruct((2, n_b, KV_HEADS, nsub, HEAD_DIM), BF16),
        compiler_params=_cparams(("arbitrary", "arbitrary")),
        name="page_compress",
    )(page_table.reshape(-1), *([cache_t] * pb), wbd.astype(BF16), pe_term, cmp_w2.astype(BF16))


def _decode_attn_kernel(phys_ref, blk_ref, flag_ref, q_ref, kn_ref, vn_ref, *rest, mode, past, steps, window, page):
    kt_refs = rest[:steps]
    vt_refs = rest[steps:2 * steps]
    o_ref = rest[2 * steps]
    idx = pl.program_id(0) * KV_HEADS + pl.program_id(1)
    rnd = lambda x: x.astype(BF16).astype(F32)
    q = q_ref[0, 0] * ATT_SCALE

    kn = rnd(kn_ref[0, 0])
    vn = rnd(vn_ref[0, 0])
    s_new = jnp.sum(q.astype(F32) * kn, axis=-1, keepdims=True)
    on = flag_ref[idx] == 1
    m = jnp.where(on, s_new, NEG_BIG)
    scs, masks = [], []
    for s in range(steps):
        kt = kt_refs[s][0, 0, 0].astype(BF16)
        tile = kt.shape[-1]
        sc = jnp.dot(q, kt, preferred_element_type=F32)
        lane = lax.broadcasted_iota(jnp.int32, sc.shape, 1)
        if mode == "sel":
            blk = blk_ref[idx * steps + s]
            per_page = page // SEL_BLOCK
            pos = (blk // per_page) * page + lane
            mask = (jnp.right_shift(pos, SEL_BLOCK.bit_length() - 1) == blk) & (pos <= past) & (blk >= 0)
        else:
            pos = past - tile + lane
            mask = (pos > past - window) & (pos <= past)
        sc = jnp.where(mask, sc, NEG_BIG)
        m = jnp.maximum(m, jnp.max(sc, axis=-1, keepdims=True))
        scs.append(sc)
        masks.append(mask)
    p_new = jnp.where(on, jnp.exp(s_new - m), 0.0)
    l = p_new
    acc = rnd(p_new) * vn
    for s in range(steps):
        p = jnp.where(masks[s], jnp.exp(scs[s] - m), 0.0)
        l = l + jnp.sum(p, axis=-1, keepdims=True)
        vt = vt_refs[s][0, 0, 0].astype(BF16)
        acc = acc + lax.dot_general(p.astype(BF16), vt, _NT, preferred_element_type=F32)
    o_ref[0, 0] = acc / jnp.maximum(l, 1e-30)


def decode_attention(q, k_new, v_new, kv_t, streams, phys, blk, flag, mode, past, steps, window=None, page=None):
    n_b, n_g, rows, hd = q.shape
    tile = kv_t.shape[-1]
    ks, vs = streams

    def kmap(st, s):
        if mode == "sel":
            return lambda b, g, ph, bl, fl: (ph[(b * n_g + g) * steps + s], st, g, 0, 0)
        return lambda b, g, ph, bl, fl: (b, st, g, 0, 0)

    tiles = ([pl.BlockSpec((1, 1, 1, hd, tile), kmap(ks, s)) for s in range(steps)]
             + [pl.BlockSpec((1, 1, 1, hd, tile), kmap(vs, s)) for s in range(steps)])
    grid_spec = pltpu.PrefetchScalarGridSpec(
        num_scalar_prefetch=3,
        grid=(n_b, n_g),
        in_specs=[pl.BlockSpec((1, 1, rows, hd), lambda b, g, ph, bl, fl: (b, g, 0, 0)),
                  pl.BlockSpec((1, 1, 1, hd), lambda b, g, ph, bl, fl: (b, g, 0, 0)),
                  pl.BlockSpec((1, 1, 1, hd), lambda b, g, ph, bl, fl: (b, g, 0, 0))] + tiles,
        out_specs=pl.BlockSpec((1, 1, rows, hd), lambda b, g, ph, bl, fl: (b, g, 0, 0)),
    )
    return pl.pallas_call(
        functools.partial(_decode_attn_kernel, mode=mode, past=past, steps=steps, window=window, page=page),
        grid_spec=grid_spec,
        out_shape=jax.ShapeDtypeStruct((n_b, n_g, rows, hd), F32),
        compiler_params=_cparams(("arbitrary", "arbitrary")),
        name="attn_decode_" + mode,
    )(phys, blk, flag, q, k_new, v_new, *([kv_t] * (2 * steps)))


def rmsnorm(x, g):
    y = x * lax.rsqrt(jnp.mean(x * x, axis=-1, keepdims=True) + RMS_EPS)
    return y * g


def rope(x, pos):
    half = ROPE_DIMS // 2
    inv = jnp.power(ROPE_THETA, -jnp.arange(half, dtype=F32) * (2.0 / ROPE_DIMS))
    ang = pos.astype(F32)[:, None] * inv[None, :]
    cos = jnp.cos(ang)[:, None, :]
    sin = jnp.sin(ang)[:, None, :]
    x1 = x[..., :half]
    x2 = x[..., half:ROPE_DIMS]
    return jnp.concatenate([x1 * cos - x2 * sin, x2 * cos + x1 * sin, x[..., ROPE_DIMS:]], axis=-1)


def compress(sub, pe, w1, b1, w2, n_b, nsub):
    half = CMP_STRIDE * HEAD_DIM
    w1cat = jnp.concatenate([w1[:half], w1[half:]], axis=1)
    hidden = w1.shape[1]
    part = mm(sub, w1cat).reshape(n_b, nsub, KV_HEADS, 2, hidden)
    pe_term = jnp.einsum('jd,jdc->c', pe, w1.reshape(CMP_BLOCK, HEAD_DIM, hidden)) + b1
    pre = part[:, :nsub - 1, :, 0] + part[:, 1:, :, 1] + pe_term
    act = jax.nn.gelu(pre)
    return mm(act.reshape(-1, hidden), w2).reshape(n_b, nsub - 1, KV_HEADS, HEAD_DIM)


def overlap_matrix(n_cmp, n_sblk, nc_pad, nb_pad):
    cs = np.arange(n_cmp) * CMP_STRIDE
    ce = cs + CMP_BLOCK - 1
    ss = np.arange(n_sblk) * SEL_BLOCK
    ov = ((cs[:, None] < ss[None, :] + SEL_BLOCK) & (ce[:, None] >= ss[None, :])).astype(np.float32)
    out = np.zeros((nc_pad, nb_pad), np.float32)
    out[:n_cmp, :n_sblk] = ov
    return out


def block_scores(imp, q_pos, n_sblk):
    j = jnp.arange(n_sblk)[None, :]
    qblk = (q_pos // SEL_BLOCK)[:, None]
    allowed = j <= qblk
    forced = (j == 0) | ((j > qblk - N_LOCAL_BLOCKS) & allowed)
    score = jnp.where(forced, FORCE_SCORE, imp)
    return jnp.where(allowed, score, -jnp.inf)


def nsa_prompt(qg, q_r, kv_t, n_b, t, cmp_pe, cmp_w1, cmp_b1, cmp_w2):
    nsub = t // CMP_STRIDE
    n_cmp = nsub - 1
    n_sblk = t // SEL_BLOCK

    def sub_blocks(s):
        x = kv_t[:, s].reshape(n_b, KV_HEADS, HEAD_DIM, nsub, CMP_STRIDE)
        return x.transpose(0, 3, 1, 4, 2).reshape(-1, CMP_STRIDE * HEAD_DIM)

    kc = compress(sub_blocks(0), cmp_pe[0], cmp_w1[0], cmp_b1[0], cmp_w2[0], n_b, nsub)
    vc = compress(sub_blocks(1), cmp_pe[1], cmp_w1[1], cmp_b1[1], cmp_w2[1], n_b, nsub)
    nc_pad = _round_up(n_cmp, LANES)
    kc = jnp.pad(kc, ((0, 0), (0, nc_pad - n_cmp), (0, 0), (0, 0)))
    vc = jnp.pad(vc, ((0, 0), (0, nc_pad - n_cmp), (0, 0), (0, 0)))
    ovt = jnp.asarray(overlap_matrix(n_cmp, n_sblk, nc_pad, n_sblk).T, BF16)
    o_cmp, sel = cmp_select(qg, kc.transpose(0, 2, 1, 3).astype(BF16), vc.transpose(0, 2, 3, 1).astype(BF16),
                            ovt, n_b, t, n_cmp, n_sblk, qg.shape[0])
    o_sel = flash_attention(q_r, kv_t, 2, 3, n_b, t, sel=sel)
    o_win = flash_attention(q_r, kv_t, 4, 5, n_b, t, window=WINDOW)
    return o_cmp, o_sel, o_win


def nsa_sample(qg, kv, cache_kv, cache_win, page_table, cmp_pe, cmp_w1, cmp_b1, cmp_w2):
    n_b, n_pages = page_table.shape
    n_phys, page = cache_kv.shape[:2]
    past = n_pages * page
    pos = jnp.full((1,), past, jnp.int32)
    q = qg[:, :N_HEADS * HEAD_DIM].reshape(n_b, 1, N_HEADS, HEAD_DIM)
    kv = kv.reshape(n_b, 1, N_KV_STREAMS, KV_HEADS, HEAD_DIM)
    k_sel = rope(kv[:, :, 2], pos)
    k_win = rope(kv[:, :, 4], pos)
    new_rows = jnp.stack([kv[:, :, 0], kv[:, :, 1], k_sel, kv[:, :, 3]], axis=2)
    new_win = jnp.stack([k_win, kv[:, :, 5]], axis=2)
    n_keep = min(WINDOW, cache_win.shape[1] + 1)
    win_state = jnp.concatenate([cache_win[:, cache_win.shape[1] + 1 - n_keep:], new_win], axis=1)

    cache_t = cache_kv.transpose(0, 2, 3, 4, 1)
    win_t = cache_win.transpose(0, 2, 3, 4, 1)

    l_tot = past + 1
    nsub = l_tot // CMP_STRIDE
    n_cmp = nsub - 1
    n_sblk = -(-l_tot // SEL_BLOCK)
    assert nsub * CMP_STRIDE == past and page % SEL_BLOCK == 0 and cache_win.shape[1] == WINDOW
    kvc = page_compress(cache_t, page_table, cmp_pe, cmp_w1, cmp_b1, cmp_w2)

    bg = n_b * KV_HEADS
    rq = 16
    qpad = lambda x: jnp.pad(x.reshape(n_b, KV_HEADS, GQA_R, HEAD_DIM), ((0, 0), (0, 0), (0, rq - GQA_R), (0, 0))).astype(BF16)

    nb_pad = _round_up(n_sblk, 8)
    ov = jnp.asarray(overlap_matrix(n_cmp, n_sblk, nsub, nb_pad), BF16)
    o_cmp, imp = cmp_attention_sample(qpad(q).reshape(bg, rq, HEAD_DIM), kvc[0].reshape(bg, nsub, HEAD_DIM),
                                      kvc[1].reshape(bg, nsub, HEAD_DIM), ov, n_cmp)
    imp = imp[:, :GQA_R, :n_sblk].sum(axis=1)
    score = block_scores(imp[:, None, :], pos, n_sblk)
    top_s, top_i = lax.top_k(score[:, 0], min(N_SEL, n_sblk))
    valid = jnp.isfinite(top_s)

    n_past_blk = past // SEL_BLOCK
    per_page = page // SEL_BLOCK
    n_k = top_i.shape[-1]
    in_past = valid & (top_i < n_past_blk)
    blk = jnp.where(in_past, top_i, -1).astype(jnp.int32)
    logical = jnp.clip(top_i, 0, n_past_blk - 1) // per_page
    phys = jnp.take_along_axis(page_table, logical.reshape(n_b, KV_HEADS * n_k), axis=1).reshape(bg, n_k)
    phys = jnp.where(in_past, phys, 0).astype(jnp.int32)
    new_flag = jnp.any(valid & (top_i >= n_past_blk), axis=-1).astype(jnp.int32)
    q_r = qpad(rope(q, pos))
    o_sel = decode_attention(q_r, new_rows[:, 0, 2][:, :, None, :], new_rows[:, 0, 3][:, :, None, :], cache_t, (2, 3),
                             phys.reshape(-1), blk.reshape(-1), new_flag, "sel", past, n_k, page=page)
    zeros = jnp.zeros((bg,), jnp.int32)
    o_win = decode_attention(q_r, new_win[:, 0, 0][:, :, None, :], new_win[:, 0, 1][:, :, None, :], win_t, (0, 1),
                             zeros, zeros, jnp.ones((bg,), jnp.int32), "win", past, 1, window=WINDOW)

    heads = lambda o: o.reshape(n_b, KV_HEADS, -1, HEAD_DIM)[:, :, :GQA_R].reshape(n_b, N_HEADS * HEAD_DIM)
    return (heads(o_cmp), heads(o_sel), heads(o_win)), new_rows, win_state


def rwkv_layer(h_p, h_t, n_p, t_p, n_s, state_wkv, state_shift, ln, mu, w_r, w_k, w_v, w0, w1, w2, a0, a1, a2, g1, g2,
               k_k, k_a, r_k, gn_w, gn_b):
    np_rows = n_p * t_p
    weights = (ln, mu, w_r, w_k, w_v, w0, w1, w2, a0, a1, a2, g1, g2)
    pre_p = rwkv_pre(h_p, jnp.zeros((RW_TM, D_MODEL), F32), np_rows, t_p, *weights)
    pre_t = rwkv_pre(h_t, jnp.pad(state_shift, ((0, RW_TM - n_s), (0, 0))), 0, RW_TM, *weights)

    def lanes(x, nb, t):
        x4 = x.reshape(nb, t, RW_HEADS, RW_HEAD)
        if t % TIME_BLK == 0 and (nb * RW_HEADS) % LANES == 0:
            return time_major(x4.transpose(3, 0, 2, 1).reshape(RW_HEAD, nb * RW_HEADS, t))
        return x4.transpose(1, 3, 0, 2).reshape(t, RW_HEAD, nb * RW_HEADS)

    def rows(o, nb, t):
        if t % TIME_BLK == 0 and (nb * RW_HEADS) % LANES == 0:
            return time_minor(o).reshape(RW_HEAD, nb, RW_HEADS, t).transpose(1, 3, 2, 0).reshape(nb * t, D_MODEL)
        return o.reshape(t, RW_HEAD, nb, RW_HEADS).transpose(2, 0, 3, 1).reshape(nb * t, D_MODEL)

    head_vec = lambda w, nb: jnp.tile(w.reshape(RW_HEADS, RW_HEAD).T, (1, nb))
    vecs = lambda nb: [head_vec(w, nb) for w in (k_k, k_a, r_k, gn_w, gn_b)]
    s0_p = jnp.zeros((RW_HEAD, RW_HEAD, n_p * RW_HEADS), F32)
    o_p, st_p = rwkv_scan(*[lanes(x, n_p, t_p) for x in pre_p[:5]], *vecs(n_p), s0_p)
    s0_s = state_wkv.transpose(3, 2, 0, 1).reshape(RW_HEAD, RW_HEAD, n_s * RW_HEADS)
    o_s, st_s = rwkv_scan(*[lanes(x[:n_s], n_s, 1) for x in pre_t[:5]], *vecs(n_s), s0_s, dot_dtype=BF16)
    o_t = jnp.pad(rows(o_s, n_s, 1), ((0, RW_TM - n_s), (0, 0)))
    wkv_p = st_p.reshape(RW_HEAD, RW_HEAD, n_p, RW_HEADS).transpose(2, 3, 1, 0)
    wkv_s = st_s.reshape(RW_HEAD, RW_HEAD, n_s, RW_HEADS).transpose(2, 3, 1, 0)
    return (rows(o_p, n_p, t_p), pre_p[5], pre_p[6]), (o_t, pre_t[5], pre_t[6]), wkv_p, wkv_s


def kernel(x_prompt, x_sample, state_wkv, state_shift, cache_kv, cache_win, page_table, ln_mix, ln_ffn, ln_kv, ln_out, rw_mu, rw_wr, rw_wk, rw_wv, rw_wo, rw_w0, rw_w1, rw_w2, rw_a0, rw_a1, rw_a2, rw_g1, rw_g2, rw_kk, rw_ka, rw_rk, rw_lnw, rw_lnb, w_kv, cmp_pe, cmp_w1, cmp_b1, cmp_w2, nsa_wqg, nsa_wo, moe_wrg, moe_brg, moe_wre, moe_bre, moe_wgate, moe_wup, moe_wdown):
    n_p, t_p, dm = x_prompt.shape
    n_s = x_sample.shape[0]
    assert x_sample.shape[1] == 1
    np_rows = n_p * t_p
    h_p = x_prompt.reshape(np_rows, dm)
    h_t = jnp.pad(x_sample.reshape(n_s, dm), ((0, RW_TM - n_s), (0, 0)))
    sizes = (np_rows, RW_TM)

    def moe(fronts, layer):
        ys = hmoe(jnp.concatenate([f[1] for f in fronts]), jnp.concatenate([f[2] for f in fronts]), sizes,
                  moe_brg[layer], moe_bre[layer], moe_wgate[layer], moe_wup[layer], moe_wdown[layer])
        return [f[0] + y for f, y in zip(fronts, ys)]

    mix_p, mix_t, wkv_p, wkv_s = rwkv_layer(h_p, h_t, n_p, t_p, n_s, state_wkv[0], state_shift[0], ln_mix[0], rw_mu[0],
                                            rw_wr[0], rw_wk[0], rw_wv[0], rw_w0[0], rw_w1[0], rw_w2[0], rw_a0[0], rw_a1[0],
                                            rw_a2[0], rw_g1[0], rw_g2[0], rw_kk[0], rw_ka[0], rw_rk[0], rw_lnw[0], rw_lnb[0])
    p_shift = mix_p[2].reshape(n_p, t_p, dm)[:, -1][None]
    s_shift = mix_t[2][:n_s][None]
    w_router = router_weight(moe_wrg[0], moe_wre[0])
    h_p, h_t = moe([mix_out([o], g, hh, rw_wo[0], ln_ffn[0], w_router)
                    for (o, g, _), hh in ((mix_p, h_p), (mix_t, h_t))], 0)

    n_q = N_HEADS * HEAD_DIM
    gate_cols = _round_up(nsa_wqg.shape[-1] - n_q, LANES)
    w_qg = jnp.pad(nsa_wqg[0], ((0, 0), (0, n_q + gate_cols - nsa_wqg.shape[-1])))
    qg_p, qr_p, kvt = attn_pre(h_p, n_p, t_p, ln_mix[1], ln_kv, w_qg, w_kv)
    kvt = kvt.reshape(n_p, N_KV_STREAMS, KV_HEADS, HEAD_DIM, t_p)
    o_p = nsa_prompt(qg_p, qr_p, kvt, n_p, t_p, cmp_pe, cmp_w1, cmp_b1, cmp_w2)
    p_rows = kvt[:, :4].transpose(0, 4, 1, 2, 3)
    p_win = kvt[:, 4:, :, :, t_p - min(WINDOW, t_p):].transpose(0, 4, 1, 2, 3)
    qg_t = mm(rmsnorm(h_t, ln_mix[1]), w_qg)
    kv_t = mm(rmsnorm(h_t, ln_kv), w_kv)
    o_s, s_rows, s_win = nsa_sample(qg_t[:n_s, :nsa_wqg.shape[-1]], kv_t[:n_s], cache_kv, cache_win,
                                    page_table, cmp_pe, cmp_w1, cmp_b1, cmp_w2)
    o_t = [jnp.pad(x, ((0, RW_TM - n_s), (0, 0))) for x in o_s]
    expand = [jnp.asarray((np.arange(gate_cols)[:, None] == 3 * (np.arange(n_q)[None, :] // HEAD_DIM) + br)
                          .astype(np.float32), BF16) for br in range(3)]
    w_router = router_weight(moe_wrg[1], moe_wre[1])
    h_p, h_t = moe([mix_out(list(br), qg, hh, nsa_wo[0], ln_ffn[1], w_router, gate_expand=expand, gate_col=n_q // LANES)
                    for br, qg, hh in ((o_p, qg_p, h_p), (o_t, qg_t, h_t))], 1)

    y_prompt = rmsnorm(h_p, ln_out).reshape(n_p, t_p, dm)
    y_sample = rmsnorm(h_t[:n_s], ln_out).reshape(n_s, 1, dm)
    return (y_prompt, y_sample, wkv_p[None], p_shift, p_rows, p_win, wkv_s[None], s_shift, s_rows, s_win)
```

```python
import functools

import numpy as np
import jax
import jax.numpy as jnp
from jax import lax
from jax.experimental import pallas as pl
from jax.experimental.pallas import tpu as pltpu

F32 = jnp.float32
BF16 = jnp.bfloat16

D_MODEL = 1024
RMS_EPS = 1e-6
RW_HEAD = 64
RW_HEADS = D_MODEL // RW_HEAD
RW_GN_EPS = 64e-5
N_HEADS = 16
HEAD_DIM = 64
KV_HEADS = 4
GQA_R = N_HEADS // KV_HEADS
ROPE_DIMS = HEAD_DIM // 4
ROPE_THETA = 500000.0
N_KV_STREAMS = 6
CMP_BLOCK = 32
CMP_STRIDE = 16
SEL_BLOCK = 64
N_SEL = 16
N_LOCAL_BLOCKS = 2
FORCE_SCORE = 1.0e4
WINDOW = 512
N_GROUPS = 4
EXPERTS_PER_GROUP = 8
N_EXPERTS = N_GROUPS * EXPERTS_PER_GROUP
D_EXPERT = 256

LANES = 128
VMEM_LIMIT = 48 << 20
NEG_BIG = -1e30
M_FLOOR = -1e29
ATT_SCALE = HEAD_DIM ** -0.5
TOKEN_TILE = 512


def _round_up(x, m):
    return (x + m - 1) // m * m


def _cparams(sem):
    return pltpu.CompilerParams(dimension_semantics=sem, vmem_limit_bytes=VMEM_LIMIT)


_NT = (((1,), (1,)), ((), ()))


def _mm_kernel(x_ref, w_ref, o_ref):
    o_ref[...] = jnp.dot(x_ref[...].astype(BF16), w_ref[...], preferred_element_type=F32)


def mm(x, w, tm=TOKEN_TILE):
    m, k = x.shape
    n = w.shape[1]
    tm = min(tm, _round_up(m, 8))
    mp = _round_up(m, tm)
    if mp != m:
        x = jnp.pad(x, ((0, mp - m), (0, 0)))
    out = pl.pallas_call(
        _mm_kernel,
        grid=(mp // tm,),
        in_specs=[pl.BlockSpec((tm, k), lambda i: (i, 0)),
                  pl.BlockSpec((k, n), lambda i: (0, 0))],
        out_specs=pl.BlockSpec((tm, n), lambda i: (i, 0)),
        out_shape=jax.ShapeDtypeStruct((mp, n), F32),
        compiler_params=_cparams(("arbitrary",)),
        name="mm",
    )(x, w.astype(BF16))
    return out[:m] if mp != m else out


SCAN_TC = 32


def _scan_kernel(r_ref, d_ref, kin_ref, a_ref, v_ref, kkp_ref, kap_ref, rk_ref, gw_ref, gb_ref, s0_ref, o_ref, st_ref,
                 s_scr, n_ref, k_ref, b_ref, *, tc, dot_dtype):
    c = pl.program_id(1)
    rnd = lambda x: x.astype(dot_dtype).astype(F32)
    halves = [(h0, h0 + RW_HEAD // 2) for h0 in (0, RW_HEAD // 2)]

    @pl.when(c == 0)
    def _():
        s_scr[...] = s0_ref[...]

    def step(t, carry):
        vv = v_ref[t]
        k_t = kin_ref[t]
        a_t = a_ref[t]
        kk = k_t * kkp_ref[...]
        kk = kk / jnp.maximum(jnp.sqrt(jnp.sum(kk * kk, axis=0, keepdims=True)), 1e-12)
        k2 = k_t * (1.0 + (a_t - 1.0) * kap_ref[...])
        n_ref[...] = kk
        k_ref[...] = k2
        b_ref[...] = kk * a_t
        outs = []
        for lo, hi in halves:
            parts = [jnp.zeros((hi - lo, LANES), F32)] * 2
            for j in range(RW_HEAD):
                parts[j % 2] = parts[j % 2] + rnd(s_scr[j, lo:hi]) * rnd(n_ref[pl.ds(j, 1), :])
            sa = -(parts[0] + parts[1])
            vh = vv[lo:hi]
            out = jnp.zeros((hi - lo, LANES), F32)
            for j in range(RW_HEAD):
                sn = (s_scr[j, lo:hi] * d_ref[t, pl.ds(j, 1), :] + sa * b_ref[pl.ds(j, 1), :]
                      + vh * k_ref[pl.ds(j, 1), :])
                s_scr[j, lo:hi] = sn
                out = out + rnd(sn) * rnd(r_ref[t, pl.ds(j, 1), :])
            outs.append(out)
        out = jnp.concatenate(outs, axis=0)
        mean = jnp.mean(out, axis=0, keepdims=True)
        cen = out - mean
        var = jnp.mean(cen * cen, axis=0, keepdims=True)
        bonus = jnp.sum(r_ref[t] * k2 * rk_ref[...], axis=0, keepdims=True) * vv
        o_ref[t] = cen * lax.rsqrt(var + RW_GN_EPS) * gw_ref[...] + gb_ref[...] + bonus
        return carry

    lax.fori_loop(0, tc, step, 0)

    @pl.when(c == pl.num_programs(1) - 1)
    def _():
        st_ref[...] = s_scr[...]


def rwkv_scan(r, d, k, a, v, kkp, kap, rk, gw, gb, s0, dot_dtype=F32):
    t, _, l = r.shape
    tc = min(SCAN_TC, t)
    assert t % tc == 0 and l % LANES == 0
    seq = pl.BlockSpec((tc, RW_HEAD, LANES), lambda g, c: (c, 0, g))
    vec = pl.BlockSpec((RW_HEAD, LANES), lambda g, c: (0, g))
    st = pl.BlockSpec((RW_HEAD, RW_HEAD, LANES), lambda g, c: (0, 0, g))
    return pl.pallas_call(
        functools.partial(_scan_kernel, tc=tc, dot_dtype=dot_dtype),
        grid=(l // LANES, t // tc),
        in_specs=[seq] * 5 + [vec] * 5 + [st],
        out_specs=[seq, st],
        out_shape=[jax.ShapeDtypeStruct((t, RW_HEAD, l), F32),
                   jax.ShapeDtypeStruct((RW_HEAD, RW_HEAD, l), F32)],
        scratch_shapes=[pltpu.VMEM((RW_HEAD, RW_HEAD, LANES), F32)] + [pltpu.VMEM((RW_HEAD, LANES), F32)] * 3,
        compiler_params=_cparams(("arbitrary", "arbitrary")),
        name="rwkv_scan",
    )(r, d, k, a, v, kkp, kap, rk, gw, gb, s0)


TIME_BLK = 2 * LANES


def _time_major_kernel(x_ref, o_ref):
    o_ref[...] = x_ref[...].reshape(RW_HEAD * LANES, TIME_BLK).T.reshape(TIME_BLK, RW_HEAD, LANES)


def _time_minor_kernel(x_ref, o_ref):
    o_ref[...] = x_ref[...].reshape(TIME_BLK, RW_HEAD * LANES).T.reshape(RW_HEAD, LANES, TIME_BLK)


def time_major(x):
    _, l, t = x.shape
    return pl.pallas_call(
        _time_major_kernel,
        grid=(l // LANES, t // TIME_BLK),
        in_specs=[pl.BlockSpec((RW_HEAD, LANES, TIME_BLK), lambda g, c: (0, g, c))],
        out_specs=pl.BlockSpec((TIME_BLK, RW_HEAD, LANES), lambda g, c: (c, 0, g)),
        out_shape=jax.ShapeDtypeStruct((t, RW_HEAD, l), F32),
        compiler_params=_cparams(("arbitrary", "arbitrary")),
        name="time_major",
    )(x)


def time_minor(x):
    t, _, l = x.shape
    return pl.pallas_call(
        _time_minor_kernel,
        grid=(l // LANES, t // TIME_BLK),
        in_specs=[pl.BlockSpec((TIME_BLK, RW_HEAD, LANES), lambda g, c: (c, 0, g))],
        out_specs=pl.BlockSpec((RW_HEAD, LANES, TIME_BLK), lambda g, c: (0, g, c)),
        out_shape=jax.ShapeDtypeStruct((RW_HEAD, l, t), F32),
        compiler_params=_cparams(("arbitrary", "arbitrary")),
        name="time_minor",
    )(x)


RW_TM = 256


def _split_dot(x, w):
    hi = x.astype(BF16)
    lo = (x - hi.astype(F32)).astype(BF16)
    return jnp.dot(hi, w, preferred_element_type=F32) + jnp.dot(lo, w, preferred_element_type=F32)


def _rwkv_pre_kernel(x_ref, ps_ref, ln_ref, mu_ref, vec_ref, wr_ref, wk_ref, wv_ref, w1_ref, w2_ref, a1_ref, a2_ref,
                     g1_ref, g2_ref,
                     r_o, d_o, k_o, a_o, v_o, g_o, xn_o, carry_scr, *, tiles_per_seq, n_seq_tiles):
    i = pl.program_id(0)
    tm = x_ref.shape[0]

    @pl.when(i == 0)
    def _():
        carry_scr[...] = jnp.zeros_like(carry_scr)

    x = x_ref[...]
    xn = x * lax.rsqrt(jnp.mean(x * x, axis=-1, keepdims=True) + RMS_EPS) * ln_ref[...]
    xn_o[...] = xn
    first = jnp.where(i % tiles_per_seq == 0, jnp.zeros_like(carry_scr[...]), carry_scr[...])
    row0 = lax.broadcasted_iota(jnp.int32, xn.shape, 0) == 0
    prev = jnp.where(row0, first, pltpu.roll(xn, 1, 0))
    prev = jnp.where(i >= n_seq_tiles, ps_ref[...], prev)
    carry_scr[...] = xn[tm - 1:tm, :]
    xx = prev - xn
    mix = lambda c: (xn + xx * mu_ref[c:c + 1, :]).astype(BF16)
    dot = lambda a, w_ref: jnp.dot(a, w_ref[...], preferred_element_type=F32)
    w0, a0 = (vec_ref[c:c + 1, :] for c in range(2))
    r_o[...] = dot(mix(0), wr_ref)
    k_o[...] = dot(mix(2), wk_ref)
    v_o[...] = dot(mix(3), wv_ref)
    w_log = -jax.nn.softplus(-(w0 + dot(jnp.tanh(dot(mix(1), w1_ref)).astype(BF16), w2_ref))) - 0.5
    d_o[...] = jnp.exp(-jnp.exp(w_log))
    a_o[...] = jax.nn.sigmoid(a0 + dot(dot(mix(4), a1_ref).astype(BF16), a2_ref))
    g_o[...] = dot(jax.nn.sigmoid(dot(mix(5), g1_ref)).astype(BF16), g2_ref)


def rwkv_pre(h, prev_single, n_seq_rows, t, ln, mu, w_r, w_k, w_v, w0, w1, w2, a0, a1, a2, g1, g2):
    n, dm = h.shape
    tm = RW_TM
    assert n % tm == 0 and t % tm == 0 and n_seq_rows % tm == 0 and prev_single.shape == (tm, dm)
    row = pl.BlockSpec((tm, dm), lambda i: (i, 0))
    full = lambda a: pl.BlockSpec(a.shape, lambda i: (0,) * a.ndim)
    bf = lambda w: w.astype(BF16)
    consts = [prev_single, ln[None, :], mu, jnp.stack([w0, a0]), bf(w_r), bf(w_k), bf(w_v), bf(w1), bf(w2),
              bf(a1), bf(a2), bf(g1), bf(g2)]
    return pl.pallas_call(
        functools.partial(_rwkv_pre_kernel, tiles_per_seq=t // tm, n_seq_tiles=n_seq_rows // tm),
        grid=(n // tm,),
        in_specs=[row] + [full(a) for a in consts],
        out_specs=[row] * 7,
        out_shape=[jax.ShapeDtypeStruct((n, dm), F32)] * 7,
        scratch_shapes=[pltpu.VMEM((1, dm), F32)],
        compiler_params=_cparams(("arbitrary",)),
        name="rwkv_pre",
    )(h, *consts)


def _mix_out_kernel(*refs, n_branch, use_gates):
    o_refs = refs[:n_branch]
    gate_ref, h_ref, wo_ref, ln_ref, wr_ref = refs[n_branch:n_branch + 5]
    e_refs = refs[n_branch + 5:n_branch + 5 + (n_branch if use_gates else 0)]
    h_o, xp_o, lg_o = refs[-3:]
    if use_gates:
        sig = jax.nn.sigmoid(gate_ref[...])
        mixed = sum(_split_dot(sig, e_refs[b][...]) * o_refs[b][...] for b in range(n_branch))
    else:
        mixed = o_refs[0][...] * gate_ref[...]
    h = h_ref[...] + jnp.dot(mixed.astype(BF16), wo_ref[...], preferred_element_type=F32)
    h_o[...] = h
    xn = h * lax.rsqrt(jnp.mean(h * h, axis=-1, keepdims=True) + RMS_EPS) * ln_ref[...]
    xb = xn.astype(BF16)
    lg_o[...] = jnp.dot(xb, wr_ref[...], preferred_element_type=F32)
    u = lax.bitcast_convert_type(xb.astype(F32), jnp.uint32)
    half = u.shape[1] // 2
    xp_o[...] = (u[:, :half] >> 16) | (u[:, half:] & jnp.uint32(0xFFFF0000))


def mix_out(branches, gate, h, w_o, ln, w_router, gate_expand=None, gate_col=0):
    n, dm = h.shape
    tm = RW_TM
    nb = len(branches)
    row = lambda a: pl.BlockSpec((tm, a.shape[1]), lambda i: (i, 0))
    full = lambda a: pl.BlockSpec(a.shape, lambda i: (0,) * a.ndim)
    consts = [w_o.astype(BF16), ln[None, :], w_router.astype(BF16)] + list(gate_expand or [])
    gate_spec = row(gate) if gate_expand is None else pl.BlockSpec((tm, LANES), lambda i: (i, gate_col))
    return pl.pallas_call(
        functools.partial(_mix_out_kernel, n_branch=nb, use_gates=gate_expand is not None),
        grid=(n // tm,),
        in_specs=[row(a) for a in branches] + [gate_spec, row(h)] + [full(a) for a in consts],
        out_specs=[pl.BlockSpec((tm, dm), lambda i: (i, 0)), pl.BlockSpec((tm, dm // 2), lambda i: (i, 0)),
                   pl.BlockSpec((tm, LANES), lambda i: (i, 0))],
        out_shape=[jax.ShapeDtypeStruct((n, dm), F32), jax.ShapeDtypeStruct((n, dm // 2), jnp.uint32),
                   jax.ShapeDtypeStruct((n, LANES), F32)],
        compiler_params=_cparams(("arbitrary",)),
        name="mix_out",
    )(*branches, gate, h, *consts)


MOE_TM = 256
MOE_VMEM_LIMIT = 56 << 20


def _moe_kernel(te_ref, nt_ref, tok_ref, x_ref, c_ref, wgu_ref, wd_ref, o_ref, xt_scr):
    i = pl.program_id(0)

    @pl.when(i < nt_ref[0])
    def _():
        base = i * MOE_TM

        def gather_rows(g8, carry):
            r0 = pl.multiple_of(g8 * 8, 8)
            for u in range(8):
                xt_scr[pl.ds(r0 + u, 1), :] = x_ref[pl.ds(tok_ref[base + r0 + u], 1), :]
            return carry

        lax.fori_loop(0, MOE_TM // 8, gather_rows, 0)
        u = xt_scr[...]
        lo = lax.bitcast_convert_type(u << 16, F32).astype(BF16)
        hi = lax.bitcast_convert_type(u & jnp.uint32(0xFFFF0000), F32).astype(BF16)
        x = jnp.concatenate([lo, hi], axis=1)
        h = jnp.dot(x, wgu_ref[0], preferred_element_type=F32)
        g = h[:, :D_EXPERT]
        u2 = h[:, D_EXPERT:]
        hid = g * jax.nn.sigmoid(g) * u2 * c_ref[...]
        o_ref[...] = jnp.dot(hid.astype(BF16), wd_ref[0], preferred_element_type=F32)

    @pl.when(i >= nt_ref[0])
    def _():
        o_ref[...] = jnp.zeros_like(o_ref)


def moe_gmm(xp, row_tok, cs, tile_expert, n_tiles_used, wgu, wd):
    n, half = xp.shape
    dm = 2 * half
    p = row_tok.shape[0]
    nt = p // MOE_TM

    def row_map(i, te, ntu, tok):
        return (jnp.minimum(i, ntu[0] - 1), 0)

    grid_spec = pltpu.PrefetchScalarGridSpec(
        num_scalar_prefetch=3,
        grid=(nt,),
        in_specs=[pl.BlockSpec((n, half), lambda i, te, ntu, tok: (0, 0), pipeline_mode=pl.Buffered(1)),
                  pl.BlockSpec((MOE_TM, 1), row_map),
                  pl.BlockSpec((1, dm, 2 * D_EXPERT), lambda i, te, ntu, tok: (te[i], 0, 0)),
                  pl.BlockSpec((1, D_EXPERT, dm), lambda i, te, ntu, tok: (te[i], 0, 0))],
        out_specs=pl.BlockSpec((MOE_TM, dm), lambda i, te, ntu, tok: (i, 0)),
        scratch_shapes=[pltpu.VMEM((MOE_TM, half), jnp.uint32)],
    )
    return pl.pallas_call(
        _moe_kernel,
        grid_spec=grid_spec,
        out_shape=jax.ShapeDtypeStruct((p, dm), F32),
        compiler_params=pltpu.CompilerParams(dimension_semantics=("arbitrary",), vmem_limit_bytes=MOE_VMEM_LIMIT),
        name="moe_gmm",
    )(tile_expert, n_tiles_used, row_tok, xp, cs, wgu, wd)


def router_weight(w_rg, w_re):
    w_r = jnp.concatenate([w_rg, w_re], axis=1)
    return jnp.pad(w_r, ((0, 0), (0, LANES - w_r.shape[1])))


def hmoe(xp, logits, set_sizes, b_rg, b_re, w_gate, w_up, w_down):
    n = xp.shape[0]
    dm = 2 * xp.shape[1]
    lg = logits[:, :N_GROUPS] + b_rg
    le = (logits[:, N_GROUPS:N_GROUPS + N_EXPERTS] + b_re).reshape(n, N_GROUPS, EXPERTS_PER_GROUP)
    g_sel = jnp.argmax(lg, axis=-1)
    p_g = jnp.max(jax.nn.softmax(lg, axis=-1), axis=-1)
    le_g = jnp.take_along_axis(le, g_sel[:, None, None], axis=1)[:, 0]
    top_p, top_i = lax.top_k(jax.nn.softmax(le_g, axis=-1), 2)
    w_top = p_g[:, None] * top_p / jnp.sum(top_p, axis=-1, keepdims=True)
    eid = (g_sel[:, None] * EXPERTS_PER_GROUP + top_i).astype(jnp.int32).reshape(-1)
    cw = w_top.reshape(-1)

    n2 = 2 * n
    onehot = (eid[:, None] == jnp.arange(N_EXPERTS, dtype=jnp.int32)[None, :]).astype(jnp.int32)
    counts = jnp.sum(onehot, axis=0)
    rank = jnp.take_along_axis(jnp.cumsum(onehot, axis=0) - onehot, eid[:, None], axis=1)[:, 0]
    tiles_per = (counts + MOE_TM - 1) // MOE_TM
    tile_end = jnp.cumsum(tiles_per)
    tile_start = tile_end - tiles_per
    pos = tile_start[eid] * MOE_TM + rank
    p_rows = _round_up(n2 + N_EXPERTS * (MOE_TM - 1), MOE_TM)
    nt = p_rows // MOE_TM
    n_used = tile_end[-1].astype(jnp.int32)
    tile_ids = jnp.arange(nt, dtype=jnp.int32)
    tile_expert = jnp.sum((tile_end[None, :] <= jnp.minimum(tile_ids, n_used - 1)[:, None]).astype(jnp.int32), axis=1)

    assert n < 2 ** 24
    tok = (jnp.arange(n2, dtype=jnp.int32) // 2).astype(F32)
    rows = jnp.zeros((p_rows, 2), F32).at[pos].set(jnp.stack([tok, cw], axis=1))
    row_tok = rows[:, 0].astype(jnp.int32)
    wgu = jnp.concatenate([w_gate, w_up], axis=-1).reshape(N_EXPERTS, dm, 2 * D_EXPERT).astype(BF16)
    wd = w_down.reshape(N_EXPERTS, D_EXPERT, dm).astype(BF16)
    ys = moe_gmm(xp, row_tok, rows[:, 1:2], tile_expert, n_used.reshape(1), wgu, wd)
    pos2 = pos.reshape(n, 2)
    bounds = np.cumsum([0] + list(set_sizes))
    return [(ys[pos2[a:b, 0]], ys[pos2[a:b, 1]]) for a, b in zip(bounds[:-1], bounds[1:])]


ATT_TQ = 512
ATT_TK = 512


def _cmp_sel_kernel(q_ref, kc_ref, vct_ref, ovt_ref, o_ref, sel_ref, *, n_cmp, n_sblk, tq):
    t0 = pl.program_id(2) * tq
    ncp = kc_ref.shape[2]
    n_io = lax.broadcasted_iota(jnp.int32, (ncp, tq), 0)
    t_io = lax.broadcasted_iota(jnp.int32, (ncp, tq), 1) + t0
    mask = (n_io * CMP_STRIDE + (CMP_BLOCK - 1) <= t_io) & (n_io < n_cmp)
    kc = kc_ref[0, 0]
    vct = vct_ref[0, 0]
    ovt = ovt_ref[...]
    imp = jnp.zeros((n_sblk, tq), F32)
    outs = []
    for r in range(GQA_R):
        qr = (q_ref[:, r * HEAD_DIM:(r + 1) * HEAD_DIM] * ATT_SCALE).astype(BF16)
        st = lax.dot_general(kc, qr, _NT, preferred_element_type=F32)
        st = jnp.where(mask, st, NEG_BIG)
        m = jnp.max(st, axis=0, keepdims=True)
        e = jnp.where(mask, jnp.exp(st - m), 0.0)
        p = (e / jnp.maximum(jnp.sum(e, axis=0, keepdims=True), 1e-30)).astype(BF16)
        outs.append(jnp.dot(vct, p, preferred_element_type=F32).T)
        imp = imp + jnp.dot(ovt, p, preferred_element_type=F32)
    o_ref[...] = jnp.concatenate(outs, axis=1)

    j_io = lax.broadcasted_iota(jnp.int32, (n_sblk, tq), 0)
    qblk = jnp.right_shift(lax.broadcasted_iota(jnp.int32, (n_sblk, tq), 1) + t0, SEL_BLOCK.bit_length() - 1)
    allowed = j_io <= qblk
    forced = (j_io == 0) | ((j_io > qblk - N_LOCAL_BLOCKS) & allowed)
    score = jnp.where(forced, FORCE_SCORE, imp)
    score = jnp.where(allowed, score, -jnp.inf)
    rank = jnp.zeros((n_sblk, tq), jnp.int32)
    for jp in range(n_sblk):
        row = score[jp:jp + 1, :]
        beats = (row > score) | ((row == score) & (j_io > jp))
        rank = rank + jnp.where(beats, 1, 0)
    sel = (rank < N_SEL) & allowed
    sel_ref[0, 0] = jnp.where(sel, 1.0, 0.0).astype(BF16)


def cmp_select(q, kc, vct, ovt, n_b, t, n_cmp, n_sblk, n_out):
    assert SEL_BLOCK & (SEL_BLOCK - 1) == 0
    tq = ATT_TQ
    nq = t // tq
    ncp = kc.shape[2]
    gw = GQA_R * HEAD_DIM
    return pl.pallas_call(
        functools.partial(_cmp_sel_kernel, n_cmp=n_cmp, n_sblk=n_sblk, tq=tq),
        grid=(n_b, KV_HEADS, nq),
        in_specs=[pl.BlockSpec((tq, gw), lambda b, g, qi: (b * nq + qi, g)),
                  pl.BlockSpec((1, 1, ncp, HEAD_DIM), lambda b, g, qi: (b, g, 0, 0)),
                  pl.BlockSpec((1, 1, HEAD_DIM, ncp), lambda b, g, qi: (b, g, 0, 0)),
                  pl.BlockSpec((n_sblk, ncp), lambda b, g, qi: (0, 0))],
        out_specs=[pl.BlockSpec((tq, gw), lambda b, g, qi: (b * nq + qi, g)),
                   pl.BlockSpec((1, 1, n_sblk, tq), lambda b, g, qi: (b, g, 0, qi))],
        out_shape=[jax.ShapeDtypeStruct((n_out, N_HEADS * HEAD_DIM), F32),
                   jax.ShapeDtypeStruct((n_b, KV_HEADS, n_sblk, t), BF16)],
        compiler_params=_cparams(("arbitrary", "arbitrary", "arbitrary")),
        name="attn_cmp_select",
    )(q, kc, vct, ovt)


def _flash_kernel(pq_ref, pk_ref, pfirst_ref, plast_ref, q_ref, kt_ref, vt_ref, *rest, window, use_sel, tq, tk):
    if use_sel:
        sel_ref, et_ref, o_ref, m_scr, l_scr, acc_scr = rest
    else:
        o_ref, m_scr, l_scr, acc_scr = rest
    p_id = pl.program_id(2)

    @pl.when(pfirst_ref[p_id] == 1)
    def _():
        m_scr[...] = jnp.full_like(m_scr, M_FLOOR)
        l_scr[...] = jnp.zeros_like(l_scr)
        acc_scr[...] = jnp.zeros_like(acc_scr)

    off = pq_ref[p_id] * tq - pk_ref[p_id] * tk
    diff = lax.broadcasted_iota(jnp.int32, (tk, tq), 0) - lax.broadcasted_iota(jnp.int32, (tk, tq), 1)
    mask = diff <= off
    if window is not None:
        mask = mask & (diff > off - window)
    if use_sel:
        selm = jnp.dot(et_ref[...], sel_ref[0, 0], preferred_element_type=F32)
        mask = mask & (selm > 0.5)
    bias = jnp.where(mask, 0.0, NEG_BIG)
    k = kt_ref[0, 0, 0].astype(BF16).T
    vt = vt_ref[0, 0, 0].astype(BF16)
    heads = range(GQA_R)
    qs = [(q_ref[:, r * HEAD_DIM:(r + 1) * HEAD_DIM] * ATT_SCALE).astype(BF16) for r in heads]
    sts = [lax.dot_general(k, qs[r], _NT, preferred_element_type=F32) + bias for r in heads]
    m_prev = [m_scr[r] for r in heads]
    m_new = [jnp.maximum(m_prev[r], jnp.max(sts[r], axis=0, keepdims=True)) for r in heads]
    ps = [jnp.exp(sts[r] - m_new[r]) for r in heads]
    alphas = [jnp.exp(m_prev[r] - m_new[r]) for r in heads]
    for r in heads:
        l_scr[r] = alphas[r] * l_scr[r] + jnp.sum(ps[r], axis=0, keepdims=True)
        acc_scr[r] = alphas[r] * acc_scr[r] + jnp.dot(vt, ps[r].astype(BF16), preferred_element_type=F32)
        m_scr[r] = m_new[r]

    @pl.when(plast_ref[p_id] == 1)
    def _():
        outs = [(acc_scr[r] / jnp.maximum(l_scr[r], 1e-30)).T for r in range(GQA_R)]
        o_ref[...] = jnp.concatenate(outs, axis=1)


def flash_attention(q, kv_t, k_stream, v_stream, n_b, t, window=None, sel=None):
    tq, tk = ATT_TQ, ATT_TK
    nq = t // tq
    first_tile = lambda qi: 0 if window is None else max(0, qi * tq - (window - 1)) // tk
    pairs = [(qi, kt) for qi in range(nq) for kt in range(first_tile(qi), (qi * tq + tq - 1) // tk + 1)]
    pq = np.array([p[0] for p in pairs], np.int32)
    pk = np.array([p[1] for p in pairs], np.int32)
    pfirst = np.array([1 if i == 0 or pairs[i - 1][0] != p[0] else 0 for i, p in enumerate(pairs)], np.int32)
    plast = np.array([1 if i == len(pairs) - 1 or pairs[i + 1][0] != p[0] else 0 for i, p in enumerate(pairs)], np.int32)
    gw = GQA_R * HEAD_DIM

    def tile_map(stream):
        return lambda b, g, p, pq_r, pk_r, f_r, l_r: (b, stream, g, 0, pk_r[p])

    in_specs = [pl.BlockSpec((tq, gw), lambda b, g, p, pq_r, pk_r, f_r, l_r: (b * nq + pq_r[p], g)),
                pl.BlockSpec((1, 1, 1, HEAD_DIM, tk), tile_map(k_stream)),
                pl.BlockSpec((1, 1, 1, HEAD_DIM, tk), tile_map(v_stream))]
    args = [q, kv_t, kv_t]
    if sel is not None:
        n_sblk = sel.shape[2]
        emat = (np.arange(t)[:, None] // SEL_BLOCK == np.arange(n_sblk)[None, :]).astype(np.float32)
        in_specs += [pl.BlockSpec((1, 1, n_sblk, tq), lambda b, g, p, pq_r, pk_r, f_r, l_r: (b, g, 0, pq_r[p])),
                     pl.BlockSpec((tk, n_sblk), lambda b, g, p, pq_r, pk_r, f_r, l_r: (pk_r[p], 0))]
        args += [sel, jnp.asarray(emat, BF16)]
    grid_spec = pltpu.PrefetchScalarGridSpec(
        num_scalar_prefetch=4,
        grid=(n_b, KV_HEADS, len(pairs)),
        in_specs=in_specs,
        out_specs=pl.BlockSpec((tq, gw), lambda b, g, p, pq_r, pk_r, f_r, l_r: (b * nq + pq_r[p], g)),
        scratch_shapes=[pltpu.VMEM((GQA_R, 1, tq), F32), pltpu.VMEM((GQA_R, 1, tq), F32),
                        pltpu.VMEM((GQA_R, HEAD_DIM, tq), F32)],
    )
    return pl.pallas_call(
        functools.partial(_flash_kernel, window=window, use_sel=sel is not None, tq=tq, tk=tk),
        grid_spec=grid_spec,
        out_shape=jax.ShapeDtypeStruct((n_b * t, N_HEADS * HEAD_DIM), F32),
        compiler_params=_cparams(("arbitrary", "arbitrary", "arbitrary")),
        name="attn_sel" if sel is not None else "attn_win",
    )(jnp.asarray(pq), jnp.asarray(pk), jnp.asarray(pfirst), jnp.asarray(plast), *args)


def _attn_pre_kernel(h_ref, e0_ref, e1_ref, ln_ref, wq_ref, wkv_ref, rc_ref, rm_ref, rp_ref, h_o, qg_o, qr_o, kvt_o):
    x = h_ref[...] + e0_ref[...] + e1_ref[...]
    h_o[...] = x
    xs = x * lax.rsqrt(jnp.mean(x * x, axis=-1, keepdims=True) + RMS_EPS)
    qg = jnp.dot((xs * ln_ref[0:1, :]).astype(BF16), wq_ref[...], preferred_element_type=F32)
    kv = jnp.dot((xs * ln_ref[1:2, :]).astype(BF16), wkv_ref[...], preferred_element_type=F32)
    qg_o[...] = qg

    def rope(v):
        reps = v.shape[1] // LANES
        tile = lambda ref: jnp.concatenate([ref[...]] * reps, axis=1)
        n = v.shape[1]
        return v * tile(rc_ref) + pltpu.roll(v, n - ROPE_DIMS // 2, 1) * tile(rm_ref) + pltpu.roll(v, ROPE_DIMS // 2, 1) * tile(rp_ref)

    n_q = N_HEADS * HEAD_DIM
    sw = KV_HEADS * HEAD_DIM
    qr_o[...] = rope(qg[:, :n_q])
    kv_r = jnp.concatenate([kv[:, :2 * sw], rope(kv[:, 2 * sw:3 * sw]), kv[:, 3 * sw:4 * sw],
                            rope(kv[:, 4 * sw:5 * sw]), kv[:, 5 * sw:]], axis=1)
    kvt_o[0] = kv_r.T


def attn_pre(h, e0, e1, n_b, t, ln_q, ln_kv, w_qg, w_kv):
    n, dm = h.shape
    tm = RW_TM
    tiles = t // tm
    half = ROPE_DIMS // 2
    inv = jnp.power(ROPE_THETA, -jnp.arange(half, dtype=F32) * (2.0 / ROPE_DIMS))
    ang = jnp.arange(t, dtype=F32)[:, None] * inv[None, :]
    cos, sin = jnp.cos(ang), jnp.sin(ang)
    pad = jnp.zeros((t, HEAD_DIM - ROPE_DIMS), F32)
    zero = jnp.zeros((t, half), F32)
    head = lambda parts: jnp.concatenate(parts, axis=1)
    two = lambda x: jnp.concatenate([x, x], axis=1)
    rc = two(head([cos, cos, pad + 1.0]))
    rm = two(head([-sin, zero, pad]))
    rp = two(head([zero, sin, pad]))
    nq_cols = w_qg.shape[1]
    nkv = w_kv.shape[1]
    tab = pl.BlockSpec((tm, LANES), lambda i: (i % tiles, 0))
    full = lambda a: pl.BlockSpec(a.shape, lambda i: (0,) * a.ndim)
    consts = [jnp.stack([ln_q, ln_kv]), w_qg.astype(BF16), w_kv.astype(BF16)]
    return pl.pallas_call(
        _attn_pre_kernel,
        grid=(n // tm,),
        in_specs=[pl.BlockSpec((tm, dm), lambda i: (i, 0))] * 3 + [full(a) for a in consts] + [tab, tab, tab],
        out_specs=[pl.BlockSpec((tm, dm), lambda i: (i, 0)),
                   pl.BlockSpec((tm, nq_cols), lambda i: (i, 0)),
                   pl.BlockSpec((tm, N_HEADS * HEAD_DIM), lambda i: (i, 0)),
                   pl.BlockSpec((1, nkv, tm), lambda i: (i // tiles, 0, i % tiles))],
        out_shape=[jax.ShapeDtypeStruct((n, dm), F32), jax.ShapeDtypeStruct((n, nq_cols), F32),
                   jax.ShapeDtypeStruct((n, N_HEADS * HEAD_DIM), F32), jax.ShapeDtypeStruct((n_b, nkv, t), F32)],
        compiler_params=_cparams(("arbitrary",)),
        name="attn_pre",
    )(h, e0, e1, *consts, rc, rm, rp)


def _final_norm_kernel(h_ref, e0_ref, e1_ref, ln_ref, o_ref):
    x = h_ref[...] + e0_ref[...] + e1_ref[...]
    o_ref[...] = x * lax.rsqrt(jnp.mean(x * x, axis=-1, keepdims=True) + RMS_EPS) * ln_ref[...]


def final_norm(h, e0, e1, ln):
    n, dm = h.shape
    row = pl.BlockSpec((TOKEN_TILE, dm), lambda i: (i, 0))
    return pl.pallas_call(
        _final_norm_kernel,
        grid=(n // TOKEN_TILE,),
        in_specs=[row] * 3 + [pl.BlockSpec((1, dm), lambda i: (0, 0))],
        out_specs=row,
        out_shape=jax.ShapeDtypeStruct((n, dm), F32),
        compiler_params=_cparams(("arbitrary",)),
        name="final_norm",
    )(h, e0, e1, ln[None, :])


def _cmp_kernel(q_ref, k_ref, v_ref, ov_ref, o_ref, imp_ref, *, n_cmp):
    q = q_ref[0] * ATT_SCALE
    s = lax.dot_general(q, k_ref[0], _NT, preferred_element_type=F32)
    mask = lax.broadcasted_iota(jnp.int32, s.shape, 1) < n_cmp
    s = jnp.where(mask, s, NEG_BIG)
    m = jnp.max(s, axis=-1, keepdims=True)
    e = jnp.where(mask, jnp.exp(s - m), 0.0)
    p = (e / jnp.maximum(jnp.sum(e, axis=-1, keepdims=True), 1e-30)).astype(BF16)
    o_ref[0] = jnp.dot(p, v_ref[0], preferred_element_type=F32)
    imp_ref[0] = jnp.dot(p, ov_ref[...], preferred_element_type=F32)


def cmp_attention_sample(q, kc, vc, ov, n_cmp):
    bg, rq, hd = q.shape
    nc = kc.shape[1]
    nb = ov.shape[1]
    return pl.pallas_call(
        functools.partial(_cmp_kernel, n_cmp=n_cmp),
        grid=(bg,),
        in_specs=[pl.BlockSpec((1, rq, hd), lambda b: (b, 0, 0)),
                  pl.BlockSpec((1, nc, hd), lambda b: (b, 0, 0)),
                  pl.BlockSpec((1, nc, hd), lambda b: (b, 0, 0)),
                  pl.BlockSpec((nc, nb), lambda b: (0, 0))],
        out_specs=[pl.BlockSpec((1, rq, hd), lambda b: (b, 0, 0)),
                   pl.BlockSpec((1, rq, nb), lambda b: (b, 0, 0))],
        out_shape=[jax.ShapeDtypeStruct((bg, rq, hd), F32),
                   jax.ShapeDtypeStruct((bg, rq, nb), F32)],
        compiler_params=_cparams(("arbitrary",)),
        name="attn_cmp_sample",
    )(q, kc, vc, ov)


PAGES_PER_STEP = 16


def _page_compress_kernel(pt_ref, *rest, n_pages, page, pb):
    c_refs = rest[:pb]
    wbd_ref, pe_ref, w2_ref, o_ref, a_scr = rest[pb:]
    p = pl.program_id(1)
    nsub = n_pages * page // CMP_STRIDE
    for i in range(pb):
        row0 = pl.multiple_of((p * pb + i) * page, page)
        for s in range(2):
            for gp in range(KV_HEADS // 2):
                x = c_refs[i][0, s, 2 * gp:2 * gp + 2].reshape(2 * HEAD_DIM, page)
                a_scr[s, gp, pl.ds(row0, page), :] = x.T

    @pl.when(p == n_pages // pb - 1)
    def _():
        hid = pe_ref.shape[-1]
        for s in range(2):
            for gp in range(KV_HEADS // 2):
                acc = jnp.zeros((nsub, 4 * hid), F32)
                for j in range(CMP_STRIDE):
                    lhs = a_scr[s, gp, pl.ds(j, nsub, stride=CMP_STRIDE), :].astype(BF16)
                    acc = acc + jnp.dot(lhs, wbd_ref[s, j], preferred_element_type=F32)
                for g2 in range(2):
                    first = acc[:, 2 * g2 * hid:(2 * g2 + 1) * hid]
                    second = acc[:, (2 * g2 + 1) * hid:(2 * g2 + 2) * hid]
                    pre = first + pltpu.roll(second, nsub - 1, 0) + pe_ref[s]
                    act = jax.nn.gelu(pre).astype(BF16)
                    o_ref[s, 0, 2 * gp + g2] = jnp.dot(act, w2_ref[s], preferred_element_type=F32).astype(BF16)


def page_compress(cache_t, page_table, cmp_pe, cmp_w1, cmp_b1, cmp_w2):
    n_b, n_pages = page_table.shape
    page = cache_t.shape[-1]
    nsub = n_pages * page // CMP_STRIDE
    hid = cmp_w1.shape[-1]
    pb = PAGES_PER_STEP
    assert n_pages % pb == 0
    w1r = cmp_w1.reshape(2, 2, CMP_STRIDE, HEAD_DIM, hid).transpose(0, 2, 3, 1, 4)
    w1r = w1r.reshape(2, CMP_STRIDE, HEAD_DIM, 2 * hid)
    zeros = jnp.zeros_like(w1r)
    wbd = jnp.concatenate([jnp.concatenate([w1r, zeros], axis=-1), jnp.concatenate([zeros, w1r], axis=-1)], axis=2)
    pe_term = (jnp.einsum('sjd,sjdc->sc', cmp_pe, cmp_w1.reshape(2, CMP_BLOCK, HEAD_DIM, hid)) + cmp_b1)[:, None, :]
    def page_map(i):
        return lambda b, p, pt: (pt[b * n_pages + p * pb + i], 0, 0, 0, 0)

    grid_spec = pltpu.PrefetchScalarGridSpec(
        num_scalar_prefetch=1,
        grid=(n_b, n_pages // pb),
        in_specs=[pl.BlockSpec((1, 2, KV_HEADS, HEAD_DIM, page), page_map(i)) for i in range(pb)] + [
            pl.BlockSpec((2, CMP_STRIDE, 2 * HEAD_DIM, 4 * hid), lambda b, p, pt: (0, 0, 0, 0)),
            pl.BlockSpec((2, 1, hid), lambda b, p, pt: (0, 0, 0)),
            pl.BlockSpec((2, hid, HEAD_DIM), lambda b, p, pt: (0, 0, 0))],
        out_specs=pl.BlockSpec((2, 1, KV_HEADS, nsub, HEAD_DIM), lambda b, p, pt: (0, b, 0, 0, 0)),
        scratch_shapes=[pltpu.VMEM((2, KV_HEADS // 2, n_pages * page, 2 * HEAD_DIM), F32)],
    )
    return pl.pallas_call(
        functools.partial(_page_compress_kernel, n_pages=n_pages, page=page, pb=pb),
        grid_spec=grid_spec,
        out_shape=jax.ShapeDtypeStruct((2, n_b, KV_HEADS, nsub, HEAD_DIM), BF16),
        compiler_params=_cparams(("arbitrary", "arbitrary")),
        name="page_compress",
    )(page_table.reshape(-1), *([cache_t] * pb), wbd.astype(BF16), pe_term, cmp_w2.astype(BF16))


def _decode_attn_kernel(phys_ref, blk_ref, flag_ref, q_ref, kn_ref, vn_ref, *rest, mode, past, steps, window, page):
    kt_refs = rest[:steps]
    vt_refs = rest[steps:2 * steps]
    o_ref = rest[2 * steps]
    idx = pl.program_id(0) * KV_HEADS + pl.program_id(1)
    rnd = lambda x: x.astype(BF16).astype(F32)
    q = q_ref[0, 0] * ATT_SCALE

    kn = rnd(kn_ref[0, 0])
    vn = rnd(vn_ref[0, 0])
    s_new = jnp.sum(q.astype(F32) * kn, axis=-1, keepdims=True)
    on = flag_ref[idx] == 1
    m = jnp.where(on, s_new, NEG_BIG)
    scs, masks = [], []
    for s in range(steps):
        kt = kt_refs[s][0, 0, 0].astype(BF16)
        tile = kt.shape[-1]
        sc = jnp.dot(q, kt, preferred_element_type=F32)
        lane = lax.broadcasted_iota(jnp.int32, sc.shape, 1)
        if mode == "sel":
            blk = blk_ref[idx * steps + s]
            per_page = page // SEL_BLOCK
            pos = (blk // per_page) * page + lane
            mask = (jnp.right_shift(pos, SEL_BLOCK.bit_length() - 1) == blk) & (pos <= past) & (blk >= 0)
        else:
            pos = past - tile + lane
            mask = (pos > past - window) & (pos <= past)
        sc = jnp.where(mask, sc, NEG_BIG)
        m = jnp.maximum(m, jnp.max(sc, axis=-1, keepdims=True))
        scs.append(sc)
        masks.append(mask)
    p_new = jnp.where(on, jnp.exp(s_new - m), 0.0)
    l = p_new
    acc = rnd(p_new) * vn
    for s in range(steps):
        p = jnp.where(masks[s], jnp.exp(scs[s] - m), 0.0)
        l = l + jnp.sum(p, axis=-1, keepdims=True)
        vt = vt_refs[s][0, 0, 0].astype(BF16)
        acc = acc + lax.dot_general(p.astype(BF16), vt, _NT, preferred_element_type=F32)
    o_ref[0, 0] = acc / jnp.maximum(l, 1e-30)


def decode_attention(q, k_new, v_new, kv_t, streams, phys, blk, flag, mode, past, steps, window=None, page=None):
    n_b, n_g, rows, hd = q.shape
    tile = kv_t.shape[-1]
    ks, vs = streams

    def kmap(st, s):
        if mode == "sel":
            return lambda b, g, ph, bl, fl: (ph[(b * n_g + g) * steps + s], st, g, 0, 0)
        return lambda b, g, ph, bl, fl: (b, st, g, 0, 0)

    tiles = ([pl.BlockSpec((1, 1, 1, hd, tile), kmap(ks, s)) for s in range(steps)]
             + [pl.BlockSpec((1, 1, 1, hd, tile), kmap(vs, s)) for s in range(steps)])
    grid_spec = pltpu.PrefetchScalarGridSpec(
        num_scalar_prefetch=3,
        grid=(n_b, n_g),
        in_specs=[pl.BlockSpec((1, 1, rows, hd), lambda b, g, ph, bl, fl: (b, g, 0, 0)),
                  pl.BlockSpec((1, 1, 1, hd), lambda b, g, ph, bl, fl: (b, g, 0, 0)),
                  pl.BlockSpec((1, 1, 1, hd), lambda b, g, ph, bl, fl: (b, g, 0, 0))] + tiles,
        out_specs=pl.BlockSpec((1, 1, rows, hd), lambda b, g, ph, bl, fl: (b, g, 0, 0)),
    )
    return pl.pallas_call(
        functools.partial(_decode_attn_kernel, mode=mode, past=past, steps=steps, window=window, page=page),
        grid_spec=grid_spec,
        out_shape=jax.ShapeDtypeStruct((n_b, n_g, rows, hd), F32),
        compiler_params=_cparams(("arbitrary", "arbitrary")),
        name="attn_decode_" + mode,
    )(phys, blk, flag, q, k_new, v_new, *([kv_t] * (2 * steps)))


def rmsnorm(x, g):
    y = x * lax.rsqrt(jnp.mean(x * x, axis=-1, keepdims=True) + RMS_EPS)
    return y * g


def rope(x, pos):
    half = ROPE_DIMS // 2
    inv = jnp.power(ROPE_THETA, -jnp.arange(half, dtype=F32) * (2.0 / ROPE_DIMS))
    ang = pos.astype(F32)[:, None] * inv[None, :]
    cos = jnp.cos(ang)[:, None, :]
    sin = jnp.sin(ang)[:, None, :]
    x1 = x[..., :half]
    x2 = x[..., half:ROPE_DIMS]
    return jnp.concatenate([x1 * cos - x2 * sin, x2 * cos + x1 * sin, x[..., ROPE_DIMS:]], axis=-1)


def compress(sub, pe, w1, b1, w2, n_b, nsub):
    half = CMP_STRIDE * HEAD_DIM
    w1cat = jnp.concatenate([w1[:half], w1[half:]], axis=1)
    hidden = w1.shape[1]
    part = mm(sub, w1cat).reshape(n_b, nsub, KV_HEADS, 2, hidden)
    pe_term = jnp.einsum('jd,jdc->c', pe, w1.reshape(CMP_BLOCK, HEAD_DIM, hidden)) + b1
    pre = part[:, :nsub - 1, :, 0] + part[:, 1:, :, 1] + pe_term
    act = jax.nn.gelu(pre)
    return mm(act.reshape(-1, hidden), w2).reshape(n_b, nsub - 1, KV_HEADS, HEAD_DIM)


def overlap_matrix(n_cmp, n_sblk, nc_pad, nb_pad):
    cs = np.arange(n_cmp) * CMP_STRIDE
    ce = cs + CMP_BLOCK - 1
    ss = np.arange(n_sblk) * SEL_BLOCK
    ov = ((cs[:, None] < ss[None, :] + SEL_BLOCK) & (ce[:, None] >= ss[None, :])).astype(np.float32)
    out = np.zeros((nc_pad, nb_pad), np.float32)
    out[:n_cmp, :n_sblk] = ov
    return out


def block_scores(imp, q_pos, n_sblk):
    j = jnp.arange(n_sblk)[None, :]
    qblk = (q_pos // SEL_BLOCK)[:, None]
    allowed = j <= qblk
    forced = (j == 0) | ((j > qblk - N_LOCAL_BLOCKS) & allowed)
    score = jnp.where(forced, FORCE_SCORE, imp)
    return jnp.where(allowed, score, -jnp.inf)


def nsa_prompt(qg, q_r, kv_t, n_b, t, cmp_pe, cmp_w1, cmp_b1, cmp_w2):
    nsub = t // CMP_STRIDE
    n_cmp = nsub - 1
    n_sblk = t // SEL_BLOCK

    def sub_blocks(s):
        x = kv_t[:, s].reshape(n_b, KV_HEADS, HEAD_DIM, nsub, CMP_STRIDE)
        return x.transpose(0, 3, 1, 4, 2).reshape(-1, CMP_STRIDE * HEAD_DIM)

    kc = compress(sub_blocks(0), cmp_pe[0], cmp_w1[0], cmp_b1[0], cmp_w2[0], n_b, nsub)
    vc = compress(sub_blocks(1), cmp_pe[1], cmp_w1[1], cmp_b1[1], cmp_w2[1], n_b, nsub)
    nc_pad = _round_up(n_cmp, LANES)
    kc = jnp.pad(kc, ((0, 0), (0, nc_pad - n_cmp), (0, 0), (0, 0)))
    vc = jnp.pad(vc, ((0, 0), (0, nc_pad - n_cmp), (0, 0), (0, 0)))
    ovt = jnp.asarray(overlap_matrix(n_cmp, n_sblk, nc_pad, n_sblk).T, BF16)
    o_cmp, sel = cmp_select(qg, kc.transpose(0, 2, 1, 3).astype(BF16), vc.transpose(0, 2, 3, 1).astype(BF16),
                            ovt, n_b, t, n_cmp, n_sblk, qg.shape[0])
    o_sel = flash_attention(q_r, kv_t, 2, 3, n_b, t, sel=sel)
    o_win = flash_attention(q_r, kv_t, 4, 5, n_b, t, window=WINDOW)
    return o_cmp, o_sel, o_win


def nsa_sample(qg, kv, cache_kv, cache_win, page_table, cmp_pe, cmp_w1, cmp_b1, cmp_w2):
    n_b, n_pages = page_table.shape
    n_phys, page = cache_kv.shape[:2]
    past = n_pages * page
    pos = jnp.full((1,), past, jnp.int32)
    q = qg[:, :N_HEADS * HEAD_DIM].reshape(n_b, 1, N_HEADS, HEAD_DIM)
    kv = kv.reshape(n_b, 1, N_KV_STREAMS, KV_HEADS, HEAD_DIM)
    k_sel = rope(kv[:, :, 2], pos)
    k_win = rope(kv[:, :, 4], pos)
    new_rows = jnp.stack([kv[:, :, 0], kv[:, :, 1], k_sel, kv[:, :, 3]], axis=2)
    new_win = jnp.stack([k_win, kv[:, :, 5]], axis=2)
    n_keep = min(WINDOW, cache_win.shape[1] + 1)
    win_state = jnp.concatenate([cache_win[:, cache_win.shape[1] + 1 - n_keep:], new_win], axis=1)

    cache_t = cache_kv.transpose(0, 2, 3, 4, 1)
    win_t = cache_win.transpose(0, 2, 3, 4, 1)

    l_tot = past + 1
    nsub = l_tot // CMP_STRIDE
    n_cmp = nsub - 1
    n_sblk = -(-l_tot // SEL_BLOCK)
    assert nsub * CMP_STRIDE == past and page % SEL_BLOCK == 0 and cache_win.shape[1] == WINDOW
    kvc = page_compress(cache_t, page_table, cmp_pe, cmp_w1, cmp_b1, cmp_w2)

    bg = n_b * KV_HEADS
    rq = 16
    qpad = lambda x: jnp.pad(x.reshape(n_b, KV_HEADS, GQA_R, HEAD_DIM), ((0, 0), (0, 0), (0, rq - GQA_R), (0, 0))).astype(BF16)

    nb_pad = _round_up(n_sblk, 8)
    ov = jnp.asarray(overlap_matrix(n_cmp, n_sblk, nsub, nb_pad), BF16)
    o_cmp, imp = cmp_attention_sample(qpad(q).reshape(bg, rq, HEAD_DIM), kvc[0].reshape(bg, nsub, HEAD_DIM),
                                      kvc[1].reshape(bg, nsub, HEAD_DIM), ov, n_cmp)
    imp = imp[:, :GQA_R, :n_sblk].sum(axis=1)
    score = block_scores(imp[:, None, :], pos, n_sblk)
    top_s, top_i = lax.top_k(score[:, 0], min(N_SEL, n_sblk))
    valid = jnp.isfinite(top_s)

    n_past_blk = past // SEL_BLOCK
    per_page = page // SEL_BLOCK
    n_k = top_i.shape[-1]
    in_past = valid & (top_i < n_past_blk)
    blk = jnp.where(in_past, top_i, -1).astype(jnp.int32)
    logical = jnp.clip(top_i, 0, n_past_blk - 1) // per_page
    phys = jnp.take_along_axis(page_table, logical.reshape(n_b, KV_HEADS * n_k), axis=1).reshape(bg, n_k)
    phys = jnp.where(in_past, phys, 0).astype(jnp.int32)
    new_flag = jnp.any(valid & (top_i >= n_past_blk), axis=-1).astype(jnp.int32)
    q_r = qpad(rope(q, pos))
    o_sel = decode_attention(q_r, new_rows[:, 0, 2][:, :, None, :], new_rows[:, 0, 3][:, :, None, :], cache_t, (2, 3),
                             phys.reshape(-1), blk.reshape(-1), new_flag, "sel", past, n_k, page=page)
    zeros = jnp.zeros((bg,), jnp.int32)
    o_win = decode_attention(q_r, new_win[:, 0, 0][:, :, None, :], new_win[:, 0, 1][:, :, None, :], win_t, (0, 1),
                             zeros, zeros, jnp.ones((bg,), jnp.int32), "win", past, 1, window=WINDOW)

    heads = lambda o: o.reshape(n_b, KV_HEADS, -1, HEAD_DIM)[:, :, :GQA_R].reshape(n_b, N_HEADS * HEAD_DIM)
    return (heads(o_cmp), heads(o_sel), heads(o_win)), new_rows, win_state


def rwkv_layer(h_p, h_t, n_p, t_p, n_s, state_wkv, state_shift, ln, mu, w_r, w_k, w_v, w0, w1, w2, a0, a1, a2, g1, g2,
               k_k, k_a, r_k, gn_w, gn_b):
    np_rows = n_p * t_p
    weights = (ln, mu, w_r, w_k, w_v, w0, w1, w2, a0, a1, a2, g1, g2)
    pre_p = rwkv_pre(h_p, jnp.zeros((RW_TM, D_MODEL), F32), np_rows, t_p, *weights)
    pre_t = rwkv_pre(h_t, jnp.pad(state_shift, ((0, RW_TM - n_s), (0, 0))), 0, RW_TM, *weights)

    def lanes(x, nb, t):
        x4 = x.reshape(nb, t, RW_HEADS, RW_HEAD)
        if t % TIME_BLK == 0 and (nb * RW_HEADS) % LANES == 0:
            return time_major(x4.transpose(3, 0, 2, 1).reshape(RW_HEAD, nb * RW_HEADS, t))
        return x4.transpose(1, 3, 0, 2).reshape(t, RW_HEAD, nb * RW_HEADS)

    def rows(o, nb, t):
        if t % TIME_BLK == 0 and (nb * RW_HEADS) % LANES == 0:
            return time_minor(o).reshape(RW_HEAD, nb, RW_HEADS, t).transpose(1, 3, 2, 0).reshape(nb * t, D_MODEL)
        return o.reshape(t, RW_HEAD, nb, RW_HEADS).transpose(2, 0, 3, 1).reshape(nb * t, D_MODEL)

    head_vec = lambda w, nb: jnp.tile(w.reshape(RW_HEADS, RW_HEAD).T, (1, nb))
    vecs = lambda nb: [head_vec(w, nb) for w in (k_k, k_a, r_k, gn_w, gn_b)]
    s0_p = jnp.zeros((RW_HEAD, RW_HEAD, n_p * RW_HEADS), F32)
    o_p, st_p = rwkv_scan(*[lanes(x, n_p, t_p) for x in pre_p[:5]], *vecs(n_p), s0_p)
    s0_s = state_wkv.transpose(3, 2, 0, 1).reshape(RW_HEAD, RW_HEAD, n_s * RW_HEADS)
    o_s, st_s = rwkv_scan(*[lanes(x[:n_s], n_s, 1) for x in pre_t[:5]], *vecs(n_s), s0_s, dot_dtype=BF16)
    o_t = jnp.pad(rows(o_s, n_s, 1), ((0, RW_TM - n_s), (0, 0)))
    wkv_p = st_p.reshape(RW_HEAD, RW_HEAD, n_p, RW_HEADS).transpose(2, 3, 1, 0)
    wkv_s = st_s.reshape(RW_HEAD, RW_HEAD, n_s, RW_HEADS).transpose(2, 3, 1, 0)
    return (rows(o_p, n_p, t_p), pre_p[5], pre_p[6]), (o_t, pre_t[5], pre_t[6]), wkv_p, wkv_s


def kernel(x_prompt, x_sample, state_wkv, state_shift, cache_kv, cache_win, page_table, ln_mix, ln_ffn, ln_kv, ln_out, rw_mu, rw_wr, rw_wk, rw_wv, rw_wo, rw_w0, rw_w1, rw_w2, rw_a0, rw_a1, rw_a2, rw_g1, rw_g2, rw_kk, rw_ka, rw_rk, rw_lnw, rw_lnb, w_kv, cmp_pe, cmp_w1, cmp_b1, cmp_w2, nsa_wqg, nsa_wo, moe_wrg, moe_brg, moe_wre, moe_bre, moe_wgate, moe_wup, moe_wdown):
    n_p, t_p, dm = x_prompt.shape
    n_s = x_sample.shape[0]
    assert x_sample.shape[1] == 1
    np_rows = n_p * t_p
    h_p = x_prompt.reshape(np_rows, dm)
    h_t = jnp.pad(x_sample.reshape(n_s, dm), ((0, RW_TM - n_s), (0, 0)))
    sizes = (np_rows, RW_TM)

    def moe(fronts, layer):
        ys = hmoe(jnp.concatenate([f[1] for f in fronts]), jnp.concatenate([f[2] for f in fronts]), sizes,
                  moe_brg[layer], moe_bre[layer], moe_wgate[layer], moe_wup[layer], moe_wdown[layer])
        return [(f[0],) + tuple(y) for f, y in zip(fronts, ys)]

    mix_p, mix_t, wkv_p, wkv_s = rwkv_layer(h_p, h_t, n_p, t_p, n_s, state_wkv[0], state_shift[0], ln_mix[0], rw_mu[0],
                                            rw_wr[0], rw_wk[0], rw_wv[0], rw_w0[0], rw_w1[0], rw_w2[0], rw_a0[0], rw_a1[0],
                                            rw_a2[0], rw_g1[0], rw_g2[0], rw_kk[0], rw_ka[0], rw_rk[0], rw_lnw[0], rw_lnb[0])
    p_shift = mix_p[2].reshape(n_p, t_p, dm)[:, -1][None]
    s_shift = mix_t[2][:n_s][None]
    w_router = router_weight(moe_wrg[0], moe_wre[0])
    res_p, res_t = moe([mix_out([o], g, hh, rw_wo[0], ln_ffn[0], w_router)
                        for (o, g, _), hh in ((mix_p, h_p), (mix_t, h_t))], 0)
    h_t = sum(res_t)

    n_q = N_HEADS * HEAD_DIM
    gate_cols = _round_up(nsa_wqg.shape[-1] - n_q, LANES)
    w_qg = jnp.pad(nsa_wqg[0], ((0, 0), (0, n_q + gate_cols - nsa_wqg.shape[-1])))
    h_p, qg_p, qr_p, kvt = attn_pre(*res_p, n_p, t_p, ln_mix[1], ln_kv, w_qg, w_kv)
    kvt = kvt.reshape(n_p, N_KV_STREAMS, KV_HEADS, HEAD_DIM, t_p)
    o_p = nsa_prompt(qg_p, qr_p, kvt, n_p, t_p, cmp_pe, cmp_w1, cmp_b1, cmp_w2)
    p_rows = kvt[:, :4].transpose(0, 4, 1, 2, 3)
    p_win = kvt[:, 4:, :, :, t_p - min(WINDOW, t_p):].transpose(0, 4, 1, 2, 3)
    qg_t = mm(rmsnorm(h_t, ln_mix[1]), w_qg)
    kv_t = mm(rmsnorm(h_t, ln_kv), w_kv)
    o_s, s_rows, s_win = nsa_sample(qg_t[:n_s, :nsa_wqg.shape[-1]], kv_t[:n_s], cache_kv, cache_win,
                                    page_table, cmp_pe, cmp_w1, cmp_b1, cmp_w2)
    o_t = [jnp.pad(x, ((0, RW_TM - n_s), (0, 0))) for x in o_s]
    expand = [jnp.asarray((np.arange(gate_cols)[:, None] == 3 * (np.arange(n_q)[None, :] // HEAD_DIM) + br)
                          .astype(np.float32), BF16) for br in range(3)]
    w_router = router_weight(moe_wrg[1], moe_wre[1])
    res_p, res_t = moe([mix_out(list(br), qg, hh, nsa_wo[0], ln_ffn[1], w_router, gate_expand=expand, gate_col=n_q // LANES)
                        for br, qg, hh in ((o_p, qg_p, h_p), (o_t, qg_t, h_t))], 1)

    y_prompt = final_norm(*res_p, ln_out).reshape(n_p, t_p, dm)
    y_sample = rmsnorm(sum(res_t)[:n_s], ln_out).reshape(n_s, 1, dm)
    return (y_prompt, y_sample, wkv_p[None], p_shift, p_rows, p_win, wkv_s[None], s_shift, s_rows, s_win)
```

```python
import functools

import numpy as np
import jax
import jax.numpy as jnp
from jax import lax
from jax.experimental import pallas as pl
from jax.experimental.pallas import tpu as pltpu

F32 = jnp.float32
BF16 = jnp.bfloat16

D_MODEL = 1024
RMS_EPS = 1e-6
RW_HEAD = 64
RW_HEADS = D_MODEL // RW_HEAD
RW_GN_EPS = 64e-5
N_HEADS = 16
HEAD_DIM = 64
KV_HEADS = 4
GQA_R = N_HEADS // KV_HEADS
ROPE_DIMS = HEAD_DIM // 4
ROPE_THETA = 500000.0
N_KV_STREAMS = 6
CMP_BLOCK = 32
CMP_STRIDE = 16
SEL_BLOCK = 64
N_SEL = 16
N_LOCAL_BLOCKS = 2
FORCE_SCORE = 1.0e4
WINDOW = 512
N_GROUPS = 4
EXPERTS_PER_GROUP = 8
N_EXPERTS = N_GROUPS * EXPERTS_PER_GROUP
D_EXPERT = 256

LANES = 128
VMEM_LIMIT = 48 << 20
NEG_BIG = -1e30
M_FLOOR = -1e29
ATT_SCALE = HEAD_DIM ** -0.5
TOKEN_TILE = 512


def _round_up(x, m):
    return (x + m - 1) // m * m


def _cparams(sem):
    return pltpu.CompilerParams(dimension_semantics=sem, vmem_limit_bytes=VMEM_LIMIT)


_NT = (((1,), (1,)), ((), ()))


def _mm_kernel(x_ref, w_ref, o_ref):
    o_ref[...] = jnp.dot(x_ref[...].astype(BF16), w_ref[...], preferred_element_type=F32)


def mm(x, w, tm=TOKEN_TILE):
    m, k = x.shape
    n = w.shape[1]
    tm = min(tm, _round_up(m, 8))
    mp = _round_up(m, tm)
    if mp != m:
        x = jnp.pad(x, ((0, mp - m), (0, 0)))
    out = pl.pallas_call(
        _mm_kernel,
        grid=(mp // tm,),
        in_specs=[pl.BlockSpec((tm, k), lambda i: (i, 0)),
                  pl.BlockSpec((k, n), lambda i: (0, 0))],
        out_specs=pl.BlockSpec((tm, n), lambda i: (i, 0)),
        out_shape=jax.ShapeDtypeStruct((mp, n), F32),
        compiler_params=_cparams(("arbitrary",)),
        name="mm",
    )(x, w.astype(BF16))
    return out[:m] if mp != m else out


SCAN_TC = 64


def _scan_kernel(r_ref, d_ref, kin_ref, a_ref, v_ref, kkp_ref, kap_ref, rk_ref, gw_ref, gb_ref, s0_ref, o_ref, st_ref,
                 s_scr, n_ref, k_ref, b_ref, *, tc, dot_dtype):
    c = pl.program_id(1)
    rnd = lambda x: x.astype(dot_dtype).astype(F32)
    halves = [(h0, h0 + RW_HEAD // 2) for h0 in (0, RW_HEAD // 2)]

    @pl.when(c == 0)
    def _():
        s_scr[...] = s0_ref[...]

    def step(t, carry):
        vv = v_ref[t]
        k_t = kin_ref[t]
        a_t = a_ref[t]
        kk = k_t * kkp_ref[...]
        kk = kk / jnp.maximum(jnp.sqrt(jnp.sum(kk * kk, axis=0, keepdims=True)), 1e-12)
        k2 = k_t * (1.0 + (a_t - 1.0) * kap_ref[...])
        n_ref[...] = kk
        k_ref[...] = k2
        b_ref[...] = kk * a_t
        outs = []
        for lo, hi in halves:
            parts = [jnp.zeros((hi - lo, LANES), F32)] * 2
            for j in range(RW_HEAD):
                parts[j % 2] = parts[j % 2] + rnd(s_scr[j, lo:hi]) * rnd(n_ref[pl.ds(j, 1), :])
            sa = -(parts[0] + parts[1])
            vh = vv[lo:hi]
            out = jnp.zeros((hi - lo, LANES), F32)
            for j in range(RW_HEAD):
                sn = (s_scr[j, lo:hi] * d_ref[t, pl.ds(j, 1), :] + sa * b_ref[pl.ds(j, 1), :]
                      + vh * k_ref[pl.ds(j, 1), :])
                s_scr[j, lo:hi] = sn
                out = out + rnd(sn) * rnd(r_ref[t, pl.ds(j, 1), :])
            outs.append(out)
        out = jnp.concatenate(outs, axis=0)
        mean = jnp.mean(out, axis=0, keepdims=True)
        cen = out - mean
        var = jnp.mean(cen * cen, axis=0, keepdims=True)
        bonus = jnp.sum(r_ref[t] * k2 * rk_ref[...], axis=0, keepdims=True) * vv
        o_ref[t] = cen * lax.rsqrt(var + RW_GN_EPS) * gw_ref[...] + gb_ref[...] + bonus
        return carry

    lax.fori_loop(0, tc, step, 0)

    @pl.when(c == pl.num_programs(1) - 1)
    def _():
        st_ref[...] = s_scr[...]


def rwkv_scan(r, d, k, a, v, kkp, kap, rk, gw, gb, s0, dot_dtype=F32):
    t, _, l = r.shape
    tc = min(SCAN_TC, t)
    assert t % tc == 0 and l % LANES == 0
    seq = pl.BlockSpec((tc, RW_HEAD, LANES), lambda g, c: (c, 0, g))
    vec = pl.BlockSpec((RW_HEAD, LANES), lambda g, c: (0, g))
    st = pl.BlockSpec((RW_HEAD, RW_HEAD, LANES), lambda g, c: (0, 0, g))
    return pl.pallas_call(
        functools.partial(_scan_kernel, tc=tc, dot_dtype=dot_dtype),
        grid=(l // LANES, t // tc),
        in_specs=[seq] * 5 + [vec] * 5 + [st],
        out_specs=[seq, st],
        out_shape=[jax.ShapeDtypeStruct((t, RW_HEAD, l), F32),
                   jax.ShapeDtypeStruct((RW_HEAD, RW_HEAD, l), F32)],
        scratch_shapes=[pltpu.VMEM((RW_HEAD, RW_HEAD, LANES), F32)] + [pltpu.VMEM((RW_HEAD, LANES), F32)] * 3,
        compiler_params=_cparams(("arbitrary", "arbitrary")),
        name="rwkv_scan",
    )(r, d, k, a, v, kkp, kap, rk, gw, gb, s0)


TIME_BLK = 2 * LANES


def _time_major_kernel(x_ref, o_ref):
    o_ref[...] = x_ref[...].reshape(RW_HEAD * LANES, TIME_BLK).T.reshape(TIME_BLK, RW_HEAD, LANES)


def _time_minor_kernel(x_ref, o_ref):
    o_ref[...] = x_ref[...].reshape(TIME_BLK, RW_HEAD * LANES).T.reshape(RW_HEAD, LANES, TIME_BLK)


def time_major(x):
    _, l, t = x.shape
    return pl.pallas_call(
        _time_major_kernel,
        grid=(l // LANES, t // TIME_BLK),
        in_specs=[pl.BlockSpec((RW_HEAD, LANES, TIME_BLK), lambda g, c: (0, g, c))],
        out_specs=pl.BlockSpec((TIME_BLK, RW_HEAD, LANES), lambda g, c: (c, 0, g)),
        out_shape=jax.ShapeDtypeStruct((t, RW_HEAD, l), F32),
        compiler_params=_cparams(("arbitrary", "arbitrary")),
        name="time_major",
    )(x)


def time_minor(x):
    t, _, l = x.shape
    return pl.pallas_call(
        _time_minor_kernel,
        grid=(l // LANES, t // TIME_BLK),
        in_specs=[pl.BlockSpec((TIME_BLK, RW_HEAD, LANES), lambda g, c: (c, 0, g))],
        out_specs=pl.BlockSpec((RW_HEAD, LANES, TIME_BLK), lambda g, c: (0, g, c)),
        out_shape=jax.ShapeDtypeStruct((RW_HEAD, l, t), F32),
        compiler_params=_cparams(("arbitrary", "arbitrary")),
        name="time_minor",
    )(x)


RW_TM = 256


def _split_dot(x, w):
    hi = x.astype(BF16)
    lo = (x - hi.astype(F32)).astype(BF16)
    return jnp.dot(hi, w, preferred_element_type=F32) + jnp.dot(lo, w, preferred_element_type=F32)


def _rwkv_pre_kernel(x_ref, ps_ref, ln_ref, mu_ref, vec_ref, wr_ref, wk_ref, wv_ref, w1_ref, w2_ref, a1_ref, a2_ref,
                     g1_ref, g2_ref,
                     r_o, d_o, k_o, a_o, v_o, g_o, xn_o, carry_scr, *, tiles_per_seq, n_seq_tiles):
    i = pl.program_id(0)
    tm = x_ref.shape[0]

    @pl.when(i == 0)
    def _():
        carry_scr[...] = jnp.zeros_like(carry_scr)

    x = x_ref[...]
    xn = x * lax.rsqrt(jnp.mean(x * x, axis=-1, keepdims=True) + RMS_EPS) * ln_ref[...]
    xn_o[...] = xn
    first = jnp.where(i % tiles_per_seq == 0, jnp.zeros_like(carry_scr[...]), carry_scr[...])
    row0 = lax.broadcasted_iota(jnp.int32, xn.shape, 0) == 0
    prev = jnp.where(row0, first, pltpu.roll(xn, 1, 0))
    prev = jnp.where(i >= n_seq_tiles, ps_ref[...], prev)
    carry_scr[...] = xn[tm - 1:tm, :]
    xx = prev - xn
    mix = lambda c: (xn + xx * mu_ref[c:c + 1, :]).astype(BF16)
    dot = lambda a, w_ref: jnp.dot(a, w_ref[...], preferred_element_type=F32)
    w0, a0 = (vec_ref[c:c + 1, :] for c in range(2))
    r_o[...] = dot(mix(0), wr_ref)
    k_o[...] = dot(mix(2), wk_ref)
    v_o[...] = dot(mix(3), wv_ref)
    w_log = -jax.nn.softplus(-(w0 + dot(jnp.tanh(dot(mix(1), w1_ref)).astype(BF16), w2_ref))) - 0.5
    d_o[...] = jnp.exp(-jnp.exp(w_log))
    a_o[...] = jax.nn.sigmoid(a0 + dot(dot(mix(4), a1_ref).astype(BF16), a2_ref))
    g_o[...] = dot(jax.nn.sigmoid(dot(mix(5), g1_ref)).astype(BF16), g2_ref)


def rwkv_pre(h, prev_single, n_seq_rows, t, ln, mu, w_r, w_k, w_v, w0, w1, w2, a0, a1, a2, g1, g2):
    n, dm = h.shape
    tm = RW_TM
    assert n % tm == 0 and t % tm == 0 and n_seq_rows % tm == 0 and prev_single.shape == (tm, dm)
    row = pl.BlockSpec((tm, dm), lambda i: (i, 0))
    full = lambda a: pl.BlockSpec(a.shape, lambda i: (0,) * a.ndim)
    bf = lambda w: w.astype(BF16)
    consts = [prev_single, ln[None, :], mu, jnp.stack([w0, a0]), bf(w_r), bf(w_k), bf(w_v), bf(w1), bf(w2),
              bf(a1), bf(a2), bf(g1), bf(g2)]
    return pl.pallas_call(
        functools.partial(_rwkv_pre_kernel, tiles_per_seq=t // tm, n_seq_tiles=n_seq_rows // tm),
        grid=(n // tm,),
        in_specs=[row] + [full(a) for a in consts],
        out_specs=[row] * 7,
        out_shape=[jax.ShapeDtypeStruct((n, dm), F32)] * 7,
        scratch_shapes=[pltpu.VMEM((1, dm), F32)],
        compiler_params=_cparams(("arbitrary",)),
        name="rwkv_pre",
    )(h, *consts)


def _mix_out_kernel(*refs, n_branch, use_gates):
    o_refs = refs[:n_branch]
    gate_ref, h_ref, wo_ref, ln_ref, wr_ref = refs[n_branch:n_branch + 5]
    e_refs = refs[n_branch + 5:n_branch + 5 + (n_branch if use_gates else 0)]
    h_o, xp_o, lg_o = refs[-3:]
    if use_gates:
        sig = jax.nn.sigmoid(gate_ref[...])
        mixed = sum(_split_dot(sig, e_refs[b][...]) * o_refs[b][...] for b in range(n_branch))
    else:
        mixed = o_refs[0][...] * gate_ref[...]
    h = h_ref[...] + jnp.dot(mixed.astype(BF16), wo_ref[...], preferred_element_type=F32)
    h_o[...] = h
    xn = h * lax.rsqrt(jnp.mean(h * h, axis=-1, keepdims=True) + RMS_EPS) * ln_ref[...]
    xb = xn.astype(BF16)
    lg_o[...] = jnp.dot(xb, wr_ref[...], preferred_element_type=F32)
    u = lax.bitcast_convert_type(xb.astype(F32), jnp.uint32)
    half = u.shape[1] // 2
    xp_o[...] = (u[:, :half] >> 16) | (u[:, half:] & jnp.uint32(0xFFFF0000))


def mix_out(branches, gate, h, w_o, ln, w_router, gate_expand=None, gate_col=0):
    n, dm = h.shape
    tm = RW_TM
    nb = len(branches)
    row = lambda a: pl.BlockSpec((tm, a.shape[1]), lambda i: (i, 0))
    full = lambda a: pl.BlockSpec(a.shape, lambda i: (0,) * a.ndim)
    consts = [w_o.astype(BF16), ln[None, :], w_router.astype(BF16)] + list(gate_expand or [])
    gate_spec = row(gate) if gate_expand is None else pl.BlockSpec((tm, LANES), lambda i: (i, gate_col))
    return pl.pallas_call(
        functools.partial(_mix_out_kernel, n_branch=nb, use_gates=gate_expand is not None),
        grid=(n // tm,),
        in_specs=[row(a) for a in branches] + [gate_spec, row(h)] + [full(a) for a in consts],
        out_specs=[pl.BlockSpec((tm, dm), lambda i: (i, 0)), pl.BlockSpec((tm, dm // 2), lambda i: (i, 0)),
                   pl.BlockSpec((tm, LANES), lambda i: (i, 0))],
        out_shape=[jax.ShapeDtypeStruct((n, dm), F32), jax.ShapeDtypeStruct((n, dm // 2), jnp.uint32),
                   jax.ShapeDtypeStruct((n, LANES), F32)],
        compiler_params=_cparams(("arbitrary",)),
        name="mix_out",
    )(*branches, gate, h, *consts)


MOE_TM = 256
MOE_VMEM_LIMIT = 56 << 20


def _moe_kernel(te_ref, nt_ref, tok_ref, x_ref, c_ref, wgu_ref, wd_ref, o_ref, xt_scr):
    i = pl.program_id(0)

    @pl.when(i < nt_ref[0])
    def _():
        base = i * MOE_TM

        def gather_rows(g8, carry):
            r0 = pl.multiple_of(g8 * 8, 8)
            for u in range(8):
                xt_scr[pl.ds(r0 + u, 1), :] = x_ref[pl.ds(tok_ref[base + r0 + u], 1), :]
            return carry

        lax.fori_loop(0, MOE_TM // 8, gather_rows, 0)
        u = xt_scr[...]
        lo = lax.bitcast_convert_type(u << 16, F32).astype(BF16)
        hi = lax.bitcast_convert_type(u & jnp.uint32(0xFFFF0000), F32).astype(BF16)
        x = jnp.concatenate([lo, hi], axis=1)
        h = jnp.dot(x, wgu_ref[0], preferred_element_type=F32)
        g = h[:, :D_EXPERT]
        u2 = h[:, D_EXPERT:]
        hid = g * jax.nn.sigmoid(g) * u2 * c_ref[...]
        o_ref[...] = jnp.dot(hid.astype(BF16), wd_ref[0], preferred_element_type=F32)

    @pl.when(i >= nt_ref[0])
    def _():
        o_ref[...] = jnp.zeros_like(o_ref)


def moe_gmm(xp, row_tok, cs, tile_expert, n_tiles_used, wgu, wd):
    n, half = xp.shape
    dm = 2 * half
    p = row_tok.shape[0]
    nt = p // MOE_TM

    def row_map(i, te, ntu, tok):
        return (jnp.minimum(i, ntu[0] - 1), 0)

    grid_spec = pltpu.PrefetchScalarGridSpec(
        num_scalar_prefetch=3,
        grid=(nt,),
        in_specs=[pl.BlockSpec((n, half), lambda i, te, ntu, tok: (0, 0), pipeline_mode=pl.Buffered(1)),
                  pl.BlockSpec((MOE_TM, 1), row_map),
                  pl.BlockSpec((1, dm, 2 * D_EXPERT), lambda i, te, ntu, tok: (te[i], 0, 0)),
                  pl.BlockSpec((1, D_EXPERT, dm), lambda i, te, ntu, tok: (te[i], 0, 0))],
        out_specs=pl.BlockSpec((MOE_TM, dm), lambda i, te, ntu, tok: (i, 0)),
        scratch_shapes=[pltpu.VMEM((MOE_TM, half), jnp.uint32)],
    )
    return pl.pallas_call(
        _moe_kernel,
        grid_spec=grid_spec,
        out_shape=jax.ShapeDtypeStruct((p, dm), F32),
        compiler_params=pltpu.CompilerParams(dimension_semantics=("arbitrary",), vmem_limit_bytes=MOE_VMEM_LIMIT),
        name="moe_gmm",
    )(tile_expert, n_tiles_used, row_tok, xp, cs, wgu, wd)


def router_weight(w_rg, w_re):
    w_r = jnp.concatenate([w_rg, w_re], axis=1)
    return jnp.pad(w_r, ((0, 0), (0, LANES - w_r.shape[1])))


def hmoe(xp, logits, set_sizes, b_rg, b_re, w_gate, w_up, w_down):
    n = xp.shape[0]
    dm = 2 * xp.shape[1]
    lg = logits[:, :N_GROUPS] + b_rg
    le = (logits[:, N_GROUPS:N_GROUPS + N_EXPERTS] + b_re).reshape(n, N_GROUPS, EXPERTS_PER_GROUP)
    g_sel = jnp.argmax(lg, axis=-1)
    p_g = jnp.max(jax.nn.softmax(lg, axis=-1), axis=-1)
    le_g = jnp.take_along_axis(le, g_sel[:, None, None], axis=1)[:, 0]
    top_p, top_i = lax.top_k(jax.nn.softmax(le_g, axis=-1), 2)
    w_top = p_g[:, None] * top_p / jnp.sum(top_p, axis=-1, keepdims=True)
    eid = (g_sel[:, None] * EXPERTS_PER_GROUP + top_i).astype(jnp.int32).reshape(-1)
    cw = w_top.reshape(-1)

    n2 = 2 * n
    onehot = (eid[:, None] == jnp.arange(N_EXPERTS, dtype=jnp.int32)[None, :]).astype(jnp.int32)
    counts = jnp.sum(onehot, axis=0)
    rank = jnp.take_along_axis(jnp.cumsum(onehot, axis=0) - onehot, eid[:, None], axis=1)[:, 0]
    tiles_per = (counts + MOE_TM - 1) // MOE_TM
    tile_end = jnp.cumsum(tiles_per)
    tile_start = tile_end - tiles_per
    pos = tile_start[eid] * MOE_TM + rank
    p_rows = _round_up(n2 + N_EXPERTS * (MOE_TM - 1), MOE_TM)
    nt = p_rows // MOE_TM
    n_used = tile_end[-1].astype(jnp.int32)
    tile_ids = jnp.arange(nt, dtype=jnp.int32)
    tile_expert = jnp.sum((tile_end[None, :] <= jnp.minimum(tile_ids, n_used - 1)[:, None]).astype(jnp.int32), axis=1)

    assert n < 2 ** 24
    tok = (jnp.arange(n2, dtype=jnp.int32) // 2).astype(F32)
    rows = jnp.zeros((p_rows, 2), F32).at[pos].set(jnp.stack([tok, cw], axis=1))
    row_tok = rows[:, 0].astype(jnp.int32)
    wgu = jnp.concatenate([w_gate, w_up], axis=-1).reshape(N_EXPERTS, dm, 2 * D_EXPERT).astype(BF16)
    wd = w_down.reshape(N_EXPERTS, D_EXPERT, dm).astype(BF16)
    ys = moe_gmm(xp, row_tok, rows[:, 1:2], tile_expert, n_used.reshape(1), wgu, wd)
    pos2 = pos.reshape(n, 2)
    bounds = np.cumsum([0] + list(set_sizes))
    return [(ys[pos2[a:b, 0]], ys[pos2[a:b, 1]]) for a, b in zip(bounds[:-1], bounds[1:])]


ATT_TQ = 512
ATT_TK = 512


def _cmp_sel_kernel(q_ref, kc_ref, vct_ref, ovt_ref, o_ref, sel_ref, *, n_cmp, n_sblk, tq):
    t0 = pl.program_id(2) * tq
    ncp = kc_ref.shape[2]
    n_io = lax.broadcasted_iota(jnp.int32, (ncp, tq), 0)
    t_io = lax.broadcasted_iota(jnp.int32, (ncp, tq), 1) + t0
    mask = (n_io * CMP_STRIDE + (CMP_BLOCK - 1) <= t_io) & (n_io < n_cmp)
    kc = kc_ref[0, 0]
    vct = vct_ref[0, 0]
    ovt = ovt_ref[...]
    imp = jnp.zeros((n_sblk, tq), F32)
    outs = []
    for r in range(GQA_R):
        qr = (q_ref[:, r * HEAD_DIM:(r + 1) * HEAD_DIM] * ATT_SCALE).astype(BF16)
        st = lax.dot_general(kc, qr, _NT, preferred_element_type=F32)
        st = jnp.where(mask, st, NEG_BIG)
        m = jnp.max(st, axis=0, keepdims=True)
        e = jnp.where(mask, jnp.exp(st - m), 0.0)
        p = (e / jnp.maximum(jnp.sum(e, axis=0, keepdims=True), 1e-30)).astype(BF16)
        outs.append(jnp.dot(vct, p, preferred_element_type=F32).T)
        imp = imp + jnp.dot(ovt, p, preferred_element_type=F32)
    o_ref[...] = jnp.concatenate(outs, axis=1)

    j_io = lax.broadcasted_iota(jnp.int32, (n_sblk, tq), 0)
    qblk = jnp.right_shift(lax.broadcasted_iota(jnp.int32, (n_sblk, tq), 1) + t0, SEL_BLOCK.bit_length() - 1)
    allowed = j_io <= qblk
    forced = (j_io == 0) | ((j_io > qblk - N_LOCAL_BLOCKS) & allowed)
    score = jnp.where(forced, FORCE_SCORE, imp)
    score = jnp.where(allowed, score, -jnp.inf)
    rank = jnp.zeros((n_sblk, tq), jnp.int32)
    for jp in range(n_sblk):
        row = score[jp:jp + 1, :]
        beats = (row > score) | ((row == score) & (j_io > jp))
        rank = rank + jnp.where(beats, 1, 0)
    sel = (rank < N_SEL) & allowed
    sel_ref[0, 0] = jnp.where(sel, 1.0, 0.0).astype(BF16)


def cmp_select(q, kc, vct, ovt, n_b, t, n_cmp, n_sblk, n_out):
    assert SEL_BLOCK & (SEL_BLOCK - 1) == 0
    tq = ATT_TQ
    nq = t // tq
    ncp = kc.shape[2]
    gw = GQA_R * HEAD_DIM
    return pl.pallas_call(
        functools.partial(_cmp_sel_kernel, n_cmp=n_cmp, n_sblk=n_sblk, tq=tq),
        grid=(n_b, KV_HEADS, nq),
        in_specs=[pl.BlockSpec((tq, gw), lambda b, g, qi: (b * nq + qi, g)),
                  pl.BlockSpec((1, 1, ncp, HEAD_DIM), lambda b, g, qi: (b, g, 0, 0)),
                  pl.BlockSpec((1, 1, HEAD_DIM, ncp), lambda b, g, qi: (b, g, 0, 0)),
                  pl.BlockSpec((n_sblk, ncp), lambda b, g, qi: (0, 0))],
        out_specs=[pl.BlockSpec((tq, gw), lambda b, g, qi: (b * nq + qi, g)),
                   pl.BlockSpec((1, 1, n_sblk, tq), lambda b, g, qi: (b, g, 0, qi))],
        out_shape=[jax.ShapeDtypeStruct((n_out, N_HEADS * HEAD_DIM), F32),
                   jax.ShapeDtypeStruct((n_b, KV_HEADS, n_sblk, t), BF16)],
        compiler_params=_cparams(("arbitrary", "arbitrary", "arbitrary")),
        name="attn_cmp_select",
    )(q, kc, vct, ovt)


def _flash_kernel(pq_ref, pk_ref, pfirst_ref, plast_ref, q_ref, kt_ref, vt_ref, *rest, window, use_sel, tq, tk):
    if use_sel:
        sel_ref, et_ref, o_ref, m_scr, l_scr, acc_scr = rest
    else:
        o_ref, m_scr, l_scr, acc_scr = rest
    p_id = pl.program_id(2)

    @pl.when(pfirst_ref[p_id] == 1)
    def _():
        m_scr[...] = jnp.full_like(m_scr, M_FLOOR)
        l_scr[...] = jnp.zeros_like(l_scr)
        acc_scr[...] = jnp.zeros_like(acc_scr)

    off = pq_ref[p_id] * tq - pk_ref[p_id] * tk
    diff = lax.broadcasted_iota(jnp.int32, (tk, tq), 0) - lax.broadcasted_iota(jnp.int32, (tk, tq), 1)
    mask = diff <= off
    if window is not None:
        mask = mask & (diff > off - window)
    if use_sel:
        selm = jnp.dot(et_ref[...], sel_ref[0, 0], preferred_element_type=F32)
        mask = mask & (selm > 0.5)
    bias = jnp.where(mask, 0.0, NEG_BIG)
    k = kt_ref[0, 0, 0].astype(BF16).T
    vt = vt_ref[0, 0, 0].astype(BF16)
    heads = range(GQA_R)
    qs = [(q_ref[:, r * HEAD_DIM:(r + 1) * HEAD_DIM] * ATT_SCALE).astype(BF16) for r in heads]
    sts = [lax.dot_general(k, qs[r], _NT, preferred_element_type=F32) + bias for r in heads]
    m_prev = [m_scr[r] for r in heads]
    m_new = [jnp.maximum(m_prev[r], jnp.max(sts[r], axis=0, keepdims=True)) for r in heads]
    ps = [jnp.exp(sts[r] - m_new[r]) for r in heads]
    alphas = [jnp.exp(m_prev[r] - m_new[r]) for r in heads]
    for r in heads:
        l_scr[r] = alphas[r] * l_scr[r] + jnp.sum(ps[r], axis=0, keepdims=True)
        acc_scr[r] = alphas[r] * acc_scr[r] + jnp.dot(vt, ps[r].astype(BF16), preferred_element_type=F32)
        m_scr[r] = m_new[r]

    @pl.when(plast_ref[p_id] == 1)
    def _():
        outs = [(acc_scr[r] / jnp.maximum(l_scr[r], 1e-30)).T for r in range(GQA_R)]
        o_ref[...] = jnp.concatenate(outs, axis=1)


def flash_attention(q, kv_t, k_stream, v_stream, n_b, t, window=None, sel=None):
    tq, tk = ATT_TQ, ATT_TK
    nq = t // tq
    first_tile = lambda qi: 0 if window is None else max(0, qi * tq - (window - 1)) // tk
    pairs = [(qi, kt) for qi in range(nq) for kt in range(first_tile(qi), (qi * tq + tq - 1) // tk + 1)]
    pq = np.array([p[0] for p in pairs], np.int32)
    pk = np.array([p[1] for p in pairs], np.int32)
    pfirst = np.array([1 if i == 0 or pairs[i - 1][0] != p[0] else 0 for i, p in enumerate(pairs)], np.int32)
    plast = np.array([1 if i == len(pairs) - 1 or pairs[i + 1][0] != p[0] else 0 for i, p in enumerate(pairs)], np.int32)
    gw = GQA_R * HEAD_DIM

    def tile_map(stream):
        return lambda b, g, p, pq_r, pk_r, f_r, l_r: (b, stream, g, 0, pk_r[p])

    in_specs = [pl.BlockSpec((tq, gw), lambda b, g, p, pq_r, pk_r, f_r, l_r: (b * nq + pq_r[p], g)),
                pl.BlockSpec((1, 1, 1, HEAD_DIM, tk), tile_map(k_stream)),
                pl.BlockSpec((1, 1, 1, HEAD_DIM, tk), tile_map(v_stream))]
    args = [q, kv_t, kv_t]
    if sel is not None:
        n_sblk = sel.shape[2]
        emat = (np.arange(t)[:, None] // SEL_BLOCK == np.arange(n_sblk)[None, :]).astype(np.float32)
        in_specs += [pl.BlockSpec((1, 1, n_sblk, tq), lambda b, g, p, pq_r, pk_r, f_r, l_r: (b, g, 0, pq_r[p])),
                     pl.BlockSpec((tk, n_sblk), lambda b, g, p, pq_r, pk_r, f_r, l_r: (pk_r[p], 0))]
        args += [sel, jnp.asarray(emat, BF16)]
    grid_spec = pltpu.PrefetchScalarGridSpec(
        num_scalar_prefetch=4,
        grid=(n_b, KV_HEADS, len(pairs)),
        in_specs=in_specs,
        out_specs=pl.BlockSpec((tq, gw), lambda b, g, p, pq_r, pk_r, f_r, l_r: (b * nq + pq_r[p], g)),
        scratch_shapes=[pltpu.VMEM((GQA_R, 1, tq), F32), pltpu.VMEM((GQA_R, 1, tq), F32),
                        pltpu.VMEM((GQA_R, HEAD_DIM, tq), F32)],
    )
    return pl.pallas_call(
        functools.partial(_flash_kernel, window=window, use_sel=sel is not None, tq=tq, tk=tk),
        grid_spec=grid_spec,
        out_shape=jax.ShapeDtypeStruct((n_b * t, N_HEADS * HEAD_DIM), F32),
        compiler_params=_cparams(("arbitrary", "arbitrary", "arbitrary")),
        name="attn_sel" if sel is not None else "attn_win",
    )(jnp.asarray(pq), jnp.asarray(pk), jnp.asarray(pfirst), jnp.asarray(plast), *args)


def _attn_pre_kernel(h_ref, e0_ref, e1_ref, ln_ref, wq_ref, wkv_ref, rc_ref, rm_ref, rp_ref, h_o, qg_o, qr_o, kvt_o):
    x = h_ref[...] + e0_ref[...] + e1_ref[...]
    h_o[...] = x
    xs = x * lax.rsqrt(jnp.mean(x * x, axis=-1, keepdims=True) + RMS_EPS)
    qg = jnp.dot((xs * ln_ref[0:1, :]).astype(BF16), wq_ref[...], preferred_element_type=F32)
    kv = jnp.dot((xs * ln_ref[1:2, :]).astype(BF16), wkv_ref[...], preferred_element_type=F32)
    qg_o[...] = qg

    def rope(v):
        reps = v.shape[1] // LANES
        tile = lambda ref: jnp.concatenate([ref[...]] * reps, axis=1)
        n = v.shape[1]
        return v * tile(rc_ref) + pltpu.roll(v, n - ROPE_DIMS // 2, 1) * tile(rm_ref) + pltpu.roll(v, ROPE_DIMS // 2, 1) * tile(rp_ref)

    n_q = N_HEADS * HEAD_DIM
    sw = KV_HEADS * HEAD_DIM
    qr_o[...] = rope(qg[:, :n_q])
    kv_r = jnp.concatenate([kv[:, :2 * sw], rope(kv[:, 2 * sw:3 * sw]), kv[:, 3 * sw:4 * sw],
                            rope(kv[:, 4 * sw:5 * sw]), kv[:, 5 * sw:]], axis=1)
    kvt_o[0] = kv_r.T


def attn_pre(h, e0, e1, n_b, t, ln_q, ln_kv, w_qg, w_kv):
    n, dm = h.shape
    tm = RW_TM
    tiles = t // tm
    half = ROPE_DIMS // 2
    inv = jnp.power(ROPE_THETA, -jnp.arange(half, dtype=F32) * (2.0 / ROPE_DIMS))
    ang = jnp.arange(t, dtype=F32)[:, None] * inv[None, :]
    cos, sin = jnp.cos(ang), jnp.sin(ang)
    pad = jnp.zeros((t, HEAD_DIM - ROPE_DIMS), F32)
    zero = jnp.zeros((t, half), F32)
    head = lambda parts: jnp.concatenate(parts, axis=1)
    two = lambda x: jnp.concatenate([x, x], axis=1)
    rc = two(head([cos, cos, pad + 1.0]))
    rm = two(head([-sin, zero, pad]))
    rp = two(head([zero, sin, pad]))
    nq_cols = w_qg.shape[1]
    nkv = w_kv.shape[1]
    tab = pl.BlockSpec((tm, LANES), lambda i: (i % tiles, 0))
    full = lambda a: pl.BlockSpec(a.shape, lambda i: (0,) * a.ndim)
    consts = [jnp.stack([ln_q, ln_kv]), w_qg.astype(BF16), w_kv.astype(BF16)]
    return pl.pallas_call(
        _attn_pre_kernel,
        grid=(n // tm,),
        in_specs=[pl.BlockSpec((tm, dm), lambda i: (i, 0))] * 3 + [full(a) for a in consts] + [tab, tab, tab],
        out_specs=[pl.BlockSpec((tm, dm), lambda i: (i, 0)),
                   pl.BlockSpec((tm, nq_cols), lambda i: (i, 0)),
                   pl.BlockSpec((tm, N_HEADS * HEAD_DIM), lambda i: (i, 0)),
                   pl.BlockSpec((1, nkv, tm), lambda i: (i // tiles, 0, i % tiles))],
        out_shape=[jax.ShapeDtypeStruct((n, dm), F32), jax.ShapeDtypeStruct((n, nq_cols), F32),
                   jax.ShapeDtypeStruct((n, N_HEADS * HEAD_DIM), F32), jax.ShapeDtypeStruct((n_b, nkv, t), F32)],
        compiler_params=_cparams(("arbitrary",)),
        name="attn_pre",
    )(h, e0, e1, *consts, rc, rm, rp)


def _final_norm_kernel(h_ref, e0_ref, e1_ref, ln_ref, o_ref):
    x = h_ref[...] + e0_ref[...] + e1_ref[...]
    o_ref[...] = x * lax.rsqrt(jnp.mean(x * x, axis=-1, keepdims=True) + RMS_EPS) * ln_ref[...]


def final_norm(h, e0, e1, ln):
    n, dm = h.shape
    row = pl.BlockSpec((TOKEN_TILE, dm), lambda i: (i, 0))
    return pl.pallas_call(
        _final_norm_kernel,
        grid=(n // TOKEN_TILE,),
        in_specs=[row] * 3 + [pl.BlockSpec((1, dm), lambda i: (0, 0))],
        out_specs=row,
        out_shape=jax.ShapeDtypeStruct((n, dm), F32),
        compiler_params=_cparams(("arbitrary",)),
        name="final_norm",
    )(h, e0, e1, ln[None, :])


def _cmp_kernel(q_ref, k_ref, v_ref, ov_ref, o_ref, imp_ref, *, n_cmp):
    q = q_ref[0] * ATT_SCALE
    s = lax.dot_general(q, k_ref[0], _NT, preferred_element_type=F32)
    mask = lax.broadcasted_iota(jnp.int32, s.shape, 1) < n_cmp
    s = jnp.where(mask, s, NEG_BIG)
    m = jnp.max(s, axis=-1, keepdims=True)
    e = jnp.where(mask, jnp.exp(s - m), 0.0)
    p = (e / jnp.maximum(jnp.sum(e, axis=-1, keepdims=True), 1e-30)).astype(BF16)
    o_ref[0] = jnp.dot(p, v_ref[0], preferred_element_type=F32)
    imp_ref[0] = jnp.dot(p, ov_ref[...], preferred_element_type=F32)


def cmp_attention_sample(q, kc, vc, ov, n_cmp):
    bg, rq, hd = q.shape
    nc = kc.shape[1]
    nb = ov.shape[1]
    return pl.pallas_call(
        functools.partial(_cmp_kernel, n_cmp=n_cmp),
        grid=(bg,),
        in_specs=[pl.BlockSpec((1, rq, hd), lambda b: (b, 0, 0)),
                  pl.BlockSpec((1, nc, hd), lambda b: (b, 0, 0)),
                  pl.BlockSpec((1, nc, hd), lambda b: (b, 0, 0)),
                  pl.BlockSpec((nc, nb), lambda b: (0, 0))],
        out_specs=[pl.BlockSpec((1, rq, hd), lambda b: (b, 0, 0)),
                   pl.BlockSpec((1, rq, nb), lambda b: (b, 0, 0))],
        out_shape=[jax.ShapeDtypeStruct((bg, rq, hd), F32),
                   jax.ShapeDtypeStruct((bg, rq, nb), F32)],
        compiler_params=_cparams(("arbitrary",)),
        name="attn_cmp_sample",
    )(q, kc, vc, ov)


PAGES_PER_STEP = 32


def _page_compress_kernel(pt_ref, *rest, n_pages, page, pb):
    c_refs = rest[:pb]
    wbd_ref, pe_ref, w2_ref, o_ref, a_scr = rest[pb:]
    p = pl.program_id(1)
    nsub = n_pages * page // CMP_STRIDE
    for i in range(pb):
        row0 = pl.multiple_of((p * pb + i) * page, page)
        for s in range(2):
            for gp in range(KV_HEADS // 2):
                x = c_refs[i][0, s, 2 * gp:2 * gp + 2].reshape(2 * HEAD_DIM, page)
                a_scr[s, gp, pl.ds(row0, page), :] = x.T

    @pl.when(p == n_pages // pb - 1)
    def _():
        hid = pe_ref.shape[-1]
        for s in range(2):
            for gp in range(KV_HEADS // 2):
                acc = jnp.zeros((nsub, 4 * hid), F32)
                for j in range(CMP_STRIDE):
                    lhs = a_scr[s, gp, pl.ds(j, nsub, stride=CMP_STRIDE), :].astype(BF16)
                    acc = acc + jnp.dot(lhs, wbd_ref[s, j], preferred_element_type=F32)
                for g2 in range(2):
                    first = acc[:, 2 * g2 * hid:(2 * g2 + 1) * hid]
                    second = acc[:, (2 * g2 + 1) * hid:(2 * g2 + 2) * hid]
                    pre = first + pltpu.roll(second, nsub - 1, 0) + pe_ref[s]
                    act = jax.nn.gelu(pre).astype(BF16)
                    o_ref[s, 0, 2 * gp + g2] = jnp.dot(act, w2_ref[s], preferred_element_type=F32).astype(BF16)


def page_compress(cache_t, page_table, cmp_pe, cmp_w1, cmp_b1, cmp_w2):
    n_b, n_pages = page_table.shape
    page = cache_t.shape[-1]
    nsub = n_pages * page // CMP_STRIDE
    hid = cmp_w1.shape[-1]
    pb = PAGES_PER_STEP
    assert n_pages % pb == 0
    w1r = cmp_w1.reshape(2, 2, CMP_STRIDE, HEAD_DIM, hid).transpose(0, 2, 3, 1, 4)
    w1r = w1r.reshape(2, CMP_STRIDE, HEAD_DIM, 2 * hid)
    zeros = jnp.zeros_like(w1r)
    wbd = jnp.concatenate([jnp.concatenate([w1r, zeros], axis=-1), jnp.concatenate([zeros, w1r], axis=-1)], axis=2)
    pe_term = (jnp.einsum('sjd,sjdc->sc', cmp_pe, cmp_w1.reshape(2, CMP_BLOCK, HEAD_DIM, hid)) + cmp_b1)[:, None, :]
    def page_map(i):
        return lambda b, p, pt: (pt[b * n_pages + p * pb + i], 0, 0, 0, 0)

    grid_spec = pltpu.PrefetchScalarGridSpec(
        num_scalar_prefetch=1,
        grid=(n_b, n_pages // pb),
        in_specs=[pl.BlockSpec((1, 2, KV_HEADS, HEAD_DIM, page), page_map(i)) for i in range(pb)] + [
            pl.BlockSpec((2, CMP_STRIDE, 2 * HEAD_DIM, 4 * hid), lambda b, p, pt: (0, 0, 0, 0)),
            pl.BlockSpec((2, 1, hid), lambda b, p, pt: (0, 0, 0)),
            pl.BlockSpec((2, hid, HEAD_DIM), lambda b, p, pt: (0, 0, 0))],
        out_specs=pl.BlockSpec((2, 1, KV_HEADS, nsub, HEAD_DIM), lambda b, p, pt: (0, b, 0, 0, 0)),
        scratch_shapes=[pltpu.VMEM((2, KV_HEADS // 2, n_pages * page, 2 * HEAD_DIM), F32)],
    )
    return pl.pallas_call(
        functools.partial(_page_compress_kernel, n_pages=n_pages, page=page, pb=pb),
        grid_spec=grid_spec,
        out_shape=jax.ShapeDtypeStruct((2, n_b, KV_HEADS, nsub, HEAD_DIM), BF16),
        compiler_params=_cparams(("arbitrary", "arbitrary")),
        name="page_compress",
    )(page_table.reshape(-1), *([cache_t] * pb), wbd.astype(BF16), pe_term, cmp_w2.astype(BF16))


def _decode_attn_kernel(phys_ref, blk_ref, flag_ref, q_ref, kn_ref, vn_ref, *rest, mode, past, steps, window, page):
    kt_refs = rest[:steps]
    vt_refs = rest[steps:2 * steps]
    o_ref = rest[2 * steps]
    idx = pl.program_id(0) * KV_HEADS + pl.program_id(1)
    rnd = lambda x: x.astype(BF16).astype(F32)
    q = q_ref[0, 0] * ATT_SCALE

    kn = rnd(kn_ref[0, 0])
    vn = rnd(vn_ref[0, 0])
    s_new = jnp.sum(q.astype(F32) * kn, axis=-1, keepdims=True)
    on = flag_ref[idx] == 1
    m = jnp.where(on, s_new, NEG_BIG)
    scs, masks = [], []
    for s in range(steps):
        kt = kt_refs[s][0, 0, 0].astype(BF16)
        tile = kt.shape[-1]
        sc = jnp.dot(q, kt, preferred_element_type=F32)
        lane = lax.broadcasted_iota(jnp.int32, sc.shape, 1)
        if mode == "sel":
            blk = blk_ref[idx * steps + s]
            per_page = page // SEL_BLOCK
            pos = (blk // per_page) * page + lane
            mask = (jnp.right_shift(pos, SEL_BLOCK.bit_length() - 1) == blk) & (pos <= past) & (blk >= 0)
        else:
            pos = past - tile + lane
            mask = (pos > past - window) & (pos <= past)
        sc = jnp.where(mask, sc, NEG_BIG)
        m = jnp.maximum(m, jnp.max(sc, axis=-1, keepdims=True))
        scs.append(sc)
        masks.append(mask)
    p_new = jnp.where(on, jnp.exp(s_new - m), 0.0)
    l = p_new
    acc = rnd(p_new) * vn
    for s in range(steps):
        p = jnp.where(masks[s], jnp.exp(scs[s] - m), 0.0)
        l = l + jnp.sum(p, axis=-1, keepdims=True)
        vt = vt_refs[s][0, 0, 0].astype(BF16)
        acc = acc + lax.dot_general(p.astype(BF16), vt, _NT, preferred_element_type=F32)
    o_ref[0, 0] = acc / jnp.maximum(l, 1e-30)


def decode_attention(q, k_new, v_new, kv_t, streams, phys, blk, flag, mode, past, steps, window=None, page=None):
    n_b, n_g, rows, hd = q.shape
    tile = kv_t.shape[-1]
    ks, vs = streams

    def kmap(st, s):
        if mode == "sel":
            return lambda b, g, ph, bl, fl: (ph[(b * n_g + g) * steps + s], st, g, 0, 0)
        return lambda b, g, ph, bl, fl: (b, st, g, 0, 0)

    tiles = ([pl.BlockSpec((1, 1, 1, hd, tile), kmap(ks, s)) for s in range(steps)]
             + [pl.BlockSpec((1, 1, 1, hd, tile), kmap(vs, s)) for s in range(steps)])
    grid_spec = pltpu.PrefetchScalarGridSpec(
        num_scalar_prefetch=3,
        grid=(n_b, n_g),
        in_specs=[pl.BlockSpec((1, 1, rows, hd), lambda b, g, ph, bl, fl: (b, g, 0, 0)),
                  pl.BlockSpec((1, 1, 1, hd), lambda b, g, ph, bl, fl: (b, g, 0, 0)),
                  pl.BlockSpec((1, 1, 1, hd), lambda b, g, ph, bl, fl: (b, g, 0, 0))] + tiles,
        out_specs=pl.BlockSpec((1, 1, rows, hd), lambda b, g, ph, bl, fl: (b, g, 0, 0)),
    )
    return pl.pallas_call(
        functools.partial(_decode_attn_kernel, mode=mode, past=past, steps=steps, window=window, page=page),
        grid_spec=grid_spec,
        out_shape=jax.ShapeDtypeStruct((n_b, n_g, rows, hd), F32),
        compiler_params=_cparams(("arbitrary", "arbitrary")),
        name="attn_decode_" + mode,
    )(phys, blk, flag, q, k_new, v_new, *([kv_t] * (2 * steps)))


def rmsnorm(x, g):
    y = x * lax.rsqrt(jnp.mean(x * x, axis=-1, keepdims=True) + RMS_EPS)
    return y * g


def rope(x, pos):
    half = ROPE_DIMS // 2
    inv = jnp.power(ROPE_THETA, -jnp.arange(half, dtype=F32) * (2.0 / ROPE_DIMS))
    ang = pos.astype(F32)[:, None] * inv[None, :]
    cos = jnp.cos(ang)[:, None, :]
    sin = jnp.sin(ang)[:, None, :]
    x1 = x[..., :half]
    x2 = x[..., half:ROPE_DIMS]
    return jnp.concatenate([x1 * cos - x2 * sin, x2 * cos + x1 * sin, x[..., ROPE_DIMS:]], axis=-1)


def compress(sub, pe, w1, b1, w2, n_b, nsub):
    half = CMP_STRIDE * HEAD_DIM
    w1cat = jnp.concatenate([w1[:half], w1[half:]], axis=1)
    hidden = w1.shape[1]
    part = mm(sub, w1cat).reshape(n_b, nsub, KV_HEADS, 2, hidden)
    pe_term = jnp.einsum('jd,jdc->c', pe, w1.reshape(CMP_BLOCK, HEAD_DIM, hidden)) + b1
    pre = part[:, :nsub - 1, :, 0] + part[:, 1:, :, 1] + pe_term
    act = jax.nn.gelu(pre)
    return mm(act.reshape(-1, hidden), w2).reshape(n_b, nsub - 1, KV_HEADS, HEAD_DIM)


def overlap_matrix(n_cmp, n_sblk, nc_pad, nb_pad):
    cs = np.arange(n_cmp) * CMP_STRIDE
    ce = cs + CMP_BLOCK - 1
    ss = np.arange(n_sblk) * SEL_BLOCK
    ov = ((cs[:, None] < ss[None, :] + SEL_BLOCK) & (ce[:, None] >= ss[None, :])).astype(np.float32)
    out = np.zeros((nc_pad, nb_pad), np.float32)
    out[:n_cmp, :n_sblk] = ov
    return out


def block_scores(imp, q_pos, n_sblk):
    j = jnp.arange(n_sblk)[None, :]
    qblk = (q_pos // SEL_BLOCK)[:, None]
    allowed = j <= qblk
    forced = (j == 0) | ((j > qblk - N_LOCAL_BLOCKS) & allowed)
    score = jnp.where(forced, FORCE_SCORE, imp)
    return jnp.where(allowed, score, -jnp.inf)


def nsa_prompt(qg, q_r, kv_t, n_b, t, cmp_pe, cmp_w1, cmp_b1, cmp_w2):
    nsub = t // CMP_STRIDE
    n_cmp = nsub - 1
    n_sblk = t // SEL_BLOCK

    def sub_blocks(s):
        x = kv_t[:, s].reshape(n_b, KV_HEADS, HEAD_DIM, nsub, CMP_STRIDE)
        return x.transpose(0, 3, 1, 4, 2).reshape(-1, CMP_STRIDE * HEAD_DIM)

    kc = compress(sub_blocks(0), cmp_pe[0], cmp_w1[0], cmp_b1[0], cmp_w2[0], n_b, nsub)
    vc = compress(sub_blocks(1), cmp_pe[1], cmp_w1[1], cmp_b1[1], cmp_w2[1], n_b, nsub)
    nc_pad = _round_up(n_cmp, LANES)
    kc = jnp.pad(kc, ((0, 0), (0, nc_pad - n_cmp), (0, 0), (0, 0)))
    vc = jnp.pad(vc, ((0, 0), (0, nc_pad - n_cmp), (0, 0), (0, 0)))
    ovt = jnp.asarray(overlap_matrix(n_cmp, n_sblk, nc_pad, n_sblk).T, BF16)
    o_cmp, sel = cmp_select(qg, kc.transpose(0, 2, 1, 3).astype(BF16), vc.transpose(0, 2, 3, 1).astype(BF16),
                            ovt, n_b, t, n_cmp, n_sblk, qg.shape[0])
    o_sel = flash_attention(q_r, kv_t, 2, 3, n_b, t, sel=sel)
    o_win = flash_attention(q_r, kv_t, 4, 5, n_b, t, window=WINDOW)
    return o_cmp, o_sel, o_win


def nsa_sample(qg, kv, cache_kv, cache_win, page_table, cmp_pe, cmp_w1, cmp_b1, cmp_w2):
    n_b, n_pages = page_table.shape
    n_phys, page = cache_kv.shape[:2]
    past = n_pages * page
    pos = jnp.full((1,), past, jnp.int32)
    q = qg[:, :N_HEADS * HEAD_DIM].reshape(n_b, 1, N_HEADS, HEAD_DIM)
    kv = kv.reshape(n_b, 1, N_KV_STREAMS, KV_HEADS, HEAD_DIM)
    k_sel = rope(kv[:, :, 2], pos)
    k_win = rope(kv[:, :, 4], pos)
    new_rows = jnp.stack([kv[:, :, 0], kv[:, :, 1], k_sel, kv[:, :, 3]], axis=2)
    new_win = jnp.stack([k_win, kv[:, :, 5]], axis=2)
    n_keep = min(WINDOW, cache_win.shape[1] + 1)
    win_state = jnp.concatenate([cache_win[:, cache_win.shape[1] + 1 - n_keep:], new_win], axis=1)

    cache_t = cache_kv.transpose(0, 2, 3, 4, 1)
    win_t = cache_win.transpose(0, 2, 3, 4, 1)

    l_tot = past + 1
    nsub = l_tot // CMP_STRIDE
    n_cmp = nsub - 1
    n_sblk = -(-l_tot // SEL_BLOCK)
    assert nsub * CMP_STRIDE == past and page % SEL_BLOCK == 0 and cache_win.shape[1] == WINDOW
    kvc = page_compress(cache_t, page_table, cmp_pe, cmp_w1, cmp_b1, cmp_w2)

    bg = n_b * KV_HEADS
    rq = 16
    qpad = lambda x: jnp.pad(x.reshape(n_b, KV_HEADS, GQA_R, HEAD_DIM), ((0, 0), (0, 0), (0, rq - GQA_R), (0, 0))).astype(BF16)

    nb_pad = _round_up(n_sblk, 8)
    ov = jnp.asarray(overlap_matrix(n_cmp, n_sblk, nsub, nb_pad), BF16)
    o_cmp, imp = cmp_attention_sample(qpad(q).reshape(bg, rq, HEAD_DIM), kvc[0].reshape(bg, nsub, HEAD_DIM),
                                      kvc[1].reshape(bg, nsub, HEAD_DIM), ov, n_cmp)
    imp = imp[:, :GQA_R, :n_sblk].sum(axis=1)
    score = block_scores(imp[:, None, :], pos, n_sblk)
    top_s, top_i = lax.top_k(score[:, 0], min(N_SEL, n_sblk))
    valid = jnp.isfinite(top_s)

    n_past_blk = past // SEL_BLOCK
    per_page = page // SEL_BLOCK
    n_k = top_i.shape[-1]
    in_past = valid & (top_i < n_past_blk)
    blk = jnp.where(in_past, top_i, -1).astype(jnp.int32)
    logical = jnp.clip(top_i, 0, n_past_blk - 1) // per_page
    phys = jnp.take_along_axis(page_table, logical.reshape(n_b, KV_HEADS * n_k), axis=1).reshape(bg, n_k)
    phys = jnp.where(in_past, phys, 0).astype(jnp.int32)
    new_flag = jnp.any(valid & (top_i >= n_past_blk), axis=-1).astype(jnp.int32)
    q_r = qpad(rope(q, pos))
    o_sel = decode_attention(q_r, new_rows[:, 0, 2][:, :, None, :], new_rows[:, 0, 3][:, :, None, :], cache_t, (2, 3),
                             phys.reshape(-1), blk.reshape(-1), new_flag, "sel", past, n_k, page=page)
    zeros = jnp.zeros((bg,), jnp.int32)
    o_win = decode_attention(q_r, new_win[:, 0, 0][:, :, None, :], new_win[:, 0, 1][:, :, None, :], win_t, (0, 1),
                             zeros, zeros, jnp.ones((bg,), jnp.int32), "win", past, 1, window=WINDOW)

    heads = lambda o: o.reshape(n_b, KV_HEADS, -1, HEAD_DIM)[:, :, :GQA_R].reshape(n_b, N_HEADS * HEAD_DIM)
    return (heads(o_cmp), heads(o_sel), heads(o_win)), new_rows, win_state


def rwkv_layer(h_p, h_t, n_p, t_p, n_s, state_wkv, state_shift, ln, mu, w_r, w_k, w_v, w0, w1, w2, a0, a1, a2, g1, g2,
               k_k, k_a, r_k, gn_w, gn_b):
    np_rows = n_p * t_p
    weights = (ln, mu, w_r, w_k, w_v, w0, w1, w2, a0, a1, a2, g1, g2)
    pre_p = rwkv_pre(h_p, jnp.zeros((RW_TM, D_MODEL), F32), np_rows, t_p, *weights)
    pre_t = rwkv_pre(h_t, jnp.pad(state_shift, ((0, RW_TM - n_s), (0, 0))), 0, RW_TM, *weights)

    def lanes(x, nb, t):
        x4 = x.reshape(nb, t, RW_HEADS, RW_HEAD)
        if t % TIME_BLK == 0 and (nb * RW_HEADS) % LANES == 0:
            return time_major(x4.transpose(3, 0, 2, 1).reshape(RW_HEAD, nb * RW_HEADS, t))
        return x4.transpose(1, 3, 0, 2).reshape(t, RW_HEAD, nb * RW_HEADS)

    def rows(o, nb, t):
        if t % TIME_BLK == 0 and (nb * RW_HEADS) % LANES == 0:
            return time_minor(o).reshape(RW_HEAD, nb, RW_HEADS, t).transpose(1, 3, 2, 0).reshape(nb * t, D_MODEL)
        return o.reshape(t, RW_HEAD, nb, RW_HEADS).transpose(2, 0, 3, 1).reshape(nb * t, D_MODEL)

    head_vec = lambda w, nb: jnp.tile(w.reshape(RW_HEADS, RW_HEAD).T, (1, nb))
    vecs = lambda nb: [head_vec(w, nb) for w in (k_k, k_a, r_k, gn_w, gn_b)]
    s0_p = jnp.zeros((RW_HEAD, RW_HEAD, n_p * RW_HEADS), F32)
    o_p, st_p = rwkv_scan(*[lanes(x, n_p, t_p) for x in pre_p[:5]], *vecs(n_p), s0_p)
    s0_s = state_wkv.transpose(3, 2, 0, 1).reshape(RW_HEAD, RW_HEAD, n_s * RW_HEADS)
    o_s, st_s = rwkv_scan(*[lanes(x[:n_s], n_s, 1) for x in pre_t[:5]], *vecs(n_s), s0_s, dot_dtype=BF16)
    o_t = jnp.pad(rows(o_s, n_s, 1), ((0, RW_TM - n_s), (0, 0)))
    wkv_p = st_p.reshape(RW_HEAD, RW_HEAD, n_p, RW_HEADS).transpose(2, 3, 1, 0)
    wkv_s = st_s.reshape(RW_HEAD, RW_HEAD, n_s, RW_HEADS).transpose(2, 3, 1, 0)
    return (rows(o_p, n_p, t_p), pre_p[5], pre_p[6]), (o_t, pre_t[5], pre_t[6]), wkv_p, wkv_s


def kernel(x_prompt, x_sample, state_wkv, state_shift, cache_kv, cache_win, page_table, ln_mix, ln_ffn, ln_kv, ln_out, rw_mu, rw_wr, rw_wk, rw_wv, rw_wo, rw_w0, rw_w1, rw_w2, rw_a0, rw_a1, rw_a2, rw_g1, rw_g2, rw_kk, rw_ka, rw_rk, rw_lnw, rw_lnb, w_kv, cmp_pe, cmp_w1, cmp_b1, cmp_w2, nsa_wqg, nsa_wo, moe_wrg, moe_brg, moe_wre, moe_bre, moe_wgate, moe_wup, moe_wdown):
    n_p, t_p, dm = x_prompt.shape
    n_s = x_sample.shape[0]
    assert x_sample.shape[1] == 1
    np_rows = n_p * t_p
    h_p = x_prompt.reshape(np_rows, dm)
    h_t = jnp.pad(x_sample.reshape(n_s, dm), ((0, RW_TM - n_s), (0, 0)))
    sizes = (np_rows, RW_TM)

    def moe(fronts, layer):
        ys = hmoe(jnp.concatenate([f[1] for f in fronts]), jnp.concatenate([f[2] for f in fronts]), sizes,
                  moe_brg[layer], moe_bre[layer], moe_wgate[layer], moe_wup[layer], moe_wdown[layer])
        return [(f[0],) + tuple(y) for f, y in zip(fronts, ys)]

    mix_p, mix_t, wkv_p, wkv_s = rwkv_layer(h_p, h_t, n_p, t_p, n_s, state_wkv[0], state_shift[0], ln_mix[0], rw_mu[0],
                                            rw_wr[0], rw_wk[0], rw_wv[0], rw_w0[0], rw_w1[0], rw_w2[0], rw_a0[0], rw_a1[0],
                                            rw_a2[0], rw_g1[0], rw_g2[0], rw_kk[0], rw_ka[0], rw_rk[0], rw_lnw[0], rw_lnb[0])
    p_shift = mix_p[2].reshape(n_p, t_p, dm)[:, -1][None]
    s_shift = mix_t[2][:n_s][None]
    w_router = router_weight(moe_wrg[0], moe_wre[0])
    res_p, res_t = moe([mix_out([o], g, hh, rw_wo[0], ln_ffn[0], w_router)
                        for (o, g, _), hh in ((mix_p, h_p), (mix_t, h_t))], 0)
    h_t = sum(res_t)

    n_q = N_HEADS * HEAD_DIM
    gate_cols = _round_up(nsa_wqg.shape[-1] - n_q, LANES)
    w_qg = jnp.pad(nsa_wqg[0], ((0, 0), (0, n_q + gate_cols - nsa_wqg.shape[-1])))
    h_p, qg_p, qr_p, kvt = attn_pre(*res_p, n_p, t_p, ln_mix[1], ln_kv, w_qg, w_kv)
    kvt = kvt.reshape(n_p, N_KV_STREAMS, KV_HEADS, HEAD_DIM, t_p)
    o_p = nsa_prompt(qg_p, qr_p, kvt, n_p, t_p, cmp_pe, cmp_w1, cmp_b1, cmp_w2)
    p_rows = kvt[:, :4].transpose(0, 4, 1, 2, 3)
    p_win = kvt[:, 4:, :, :, t_p - min(WINDOW, t_p):].transpose(0, 4, 1, 2, 3)
    qg_t = mm(rmsnorm(h_t, ln_mix[1]), w_qg)
    kv_t = mm(rmsnorm(h_t, ln_kv), w_kv)
    o_s, s_rows, s_win = nsa_sample(qg_t[:n_s, :nsa_wqg.shape[-1]], kv_t[:n_s], cache_kv, cache_win,
                                    page_table, cmp_pe, cmp_w1, cmp_b1, cmp_w2)
    o_t = [jnp.pad(x, ((0, RW_TM - n_s), (0, 0))) for x in o_s]
    expand = [jnp.asarray((np.arange(gate_cols)[:, None] == 3 * (np.arange(n_q)[None, :] // HEAD_DIM) + br)
                          .astype(np.float32), BF16) for br in range(3)]
    w_router = router_weight(moe_wrg[1], moe_wre[1])
    res_p, res_t = moe([mix_out(list(br), qg, hh, nsa_wo[0], ln_ffn[1], w_router, gate_expand=expand, gate_col=n_q // LANES)
                        for br, qg, hh in ((o_p, qg_p, h_p), (o_t, qg_t, h_t))], 1)

    y_prompt = final_norm(*res_p, ln_out).reshape(n_p, t_p, dm)
    y_sample = rmsnorm(sum(res_t)[:n_s], ln_out).reshape(n_s, 1, dm)
    return (y_prompt, y_sample, wkv_p[None], p_shift, p_rows, p_win, wkv_s[None], s_shift, s_rows, s_win)
```

```python
import functools

import numpy as np
import jax
import jax.numpy as jnp
from jax import lax
from jax.experimental import pallas as pl
from jax.experimental.pallas import tpu as pltpu

F32 = jnp.float32
BF16 = jnp.bfloat16

D_MODEL = 1024
RMS_EPS = 1e-6
RW_HEAD = 64
RW_HEADS = D_MODEL // RW_HEAD
RW_GN_EPS = 64e-5
N_HEADS = 16
HEAD_DIM = 64
KV_HEADS = 4
GQA_R = N_HEADS // KV_HEADS
ROPE_DIMS = HEAD_DIM // 4
ROPE_THETA = 500000.0
N_KV_STREAMS = 6
CMP_BLOCK = 32
CMP_STRIDE = 16
SEL_BLOCK = 64
N_SEL = 16
N_LOCAL_BLOCKS = 2
FORCE_SCORE = 1.0e4
WINDOW = 512
N_GROUPS = 4
EXPERTS_PER_GROUP = 8
N_EXPERTS = N_GROUPS * EXPERTS_PER_GROUP
D_EXPERT = 256

LANES = 128
VMEM_LIMIT = 48 << 20
NEG_BIG = -1e30
M_FLOOR = -1e29
ATT_SCALE = HEAD_DIM ** -0.5
TOKEN_TILE = 512


def _round_up(x, m):
    return (x + m - 1) // m * m


def _cparams(sem):
    return pltpu.CompilerParams(dimension_semantics=sem, vmem_limit_bytes=VMEM_LIMIT)


_NT = (((1,), (1,)), ((), ()))


def _mm_kernel(x_ref, w_ref, o_ref):
    o_ref[...] = jnp.dot(x_ref[...].astype(BF16), w_ref[...], preferred_element_type=F32)


def mm(x, w, tm=TOKEN_TILE):
    m, k = x.shape
    n = w.shape[1]
    tm = min(tm, _round_up(m, 8))
    mp = _round_up(m, tm)
    if mp != m:
        x = jnp.pad(x, ((0, mp - m), (0, 0)))
    out = pl.pallas_call(
        _mm_kernel,
        grid=(mp // tm,),
        in_specs=[pl.BlockSpec((tm, k), lambda i: (i, 0)),
                  pl.BlockSpec((k, n), lambda i: (0, 0))],
        out_specs=pl.BlockSpec((tm, n), lambda i: (i, 0)),
        out_shape=jax.ShapeDtypeStruct((mp, n), F32),
        compiler_params=_cparams(("arbitrary",)),
        name="mm",
    )(x, w.astype(BF16))
    return out[:m] if mp != m else out


SCAN_TC = 64


def _scan_kernel(r_ref, d_ref, kin_ref, a_ref, v_ref, kkp_ref, kap_ref, rk_ref, gw_ref, gb_ref, s0_ref, o_ref, st_ref,
                 s_scr, n_ref, k_ref, b_ref, *, tc, dot_dtype):
    c = pl.program_id(1)
    rnd = lambda x: x.astype(dot_dtype).astype(F32)
    halves = [(h0, h0 + RW_HEAD // 2) for h0 in (0, RW_HEAD // 2)]

    @pl.when(c == 0)
    def _():
        s_scr[...] = s0_ref[...]

    def step(t, carry):
        vv = v_ref[t]
        k_t = kin_ref[t]
        a_t = a_ref[t]
        kk = k_t * kkp_ref[...]
        kk = kk / jnp.maximum(jnp.sqrt(jnp.sum(kk * kk, axis=0, keepdims=True)), 1e-12)
        k2 = k_t * (1.0 + (a_t - 1.0) * kap_ref[...])
        n_ref[...] = kk
        k_ref[...] = k2
        b_ref[...] = kk * a_t
        outs = []
        for lo, hi in halves:
            parts = [jnp.zeros((hi - lo, LANES), F32)] * 2
            for j in range(RW_HEAD):
                parts[j % 2] = parts[j % 2] + rnd(s_scr[j, lo:hi]) * rnd(n_ref[pl.ds(j, 1), :])
            sa = -(parts[0] + parts[1])
            vh = vv[lo:hi]
            out = jnp.zeros((hi - lo, LANES), F32)
            for j in range(RW_HEAD):
                sn = (s_scr[j, lo:hi] * d_ref[t, pl.ds(j, 1), :] + sa * b_ref[pl.ds(j, 1), :]
                      + vh * k_ref[pl.ds(j, 1), :])
                s_scr[j, lo:hi] = sn
                out = out + rnd(sn) * rnd(r_ref[t, pl.ds(j, 1), :])
            outs.append(out)
        out = jnp.concatenate(outs, axis=0)
        mean = jnp.mean(out, axis=0, keepdims=True)
        cen = out - mean
        var = jnp.mean(cen * cen, axis=0, keepdims=True)
        bonus = jnp.sum(r_ref[t] * k2 * rk_ref[...], axis=0, keepdims=True) * vv
        o_ref[t] = cen * lax.rsqrt(var + RW_GN_EPS) * gw_ref[...] + gb_ref[...] + bonus
        return carry

    lax.fori_loop(0, tc, step, 0)

    @pl.when(c == pl.num_programs(1) - 1)
    def _():
        st_ref[...] = s_scr[...]


def rwkv_scan(r, d, k, a, v, kkp, kap, rk, gw, gb, s0, dot_dtype=F32):
    t, _, l = r.shape
    tc = min(SCAN_TC, t)
    assert t % tc == 0 and l % LANES == 0
    seq = pl.BlockSpec((tc, RW_HEAD, LANES), lambda g, c: (c, 0, g))
    vec = pl.BlockSpec((RW_HEAD, LANES), lambda g, c: (0, g))
    st = pl.BlockSpec((RW_HEAD, RW_HEAD, LANES), lambda g, c: (0, 0, g))
    return pl.pallas_call(
        functools.partial(_scan_kernel, tc=tc, dot_dtype=dot_dtype),
        grid=(l // LANES, t // tc),
        in_specs=[seq] * 5 + [vec] * 5 + [st],
        out_specs=[seq, st],
        out_shape=[jax.ShapeDtypeStruct((t, RW_HEAD, l), F32),
                   jax.ShapeDtypeStruct((RW_HEAD, RW_HEAD, l), F32)],
        scratch_shapes=[pltpu.VMEM((RW_HEAD, RW_HEAD, LANES), F32)] + [pltpu.VMEM((RW_HEAD, LANES), F32)] * 3,
        compiler_params=_cparams(("arbitrary", "arbitrary")),
        name="rwkv_scan",
    )(r, d, k, a, v, kkp, kap, rk, gw, gb, s0)


TIME_BLK = 2 * LANES


def _time_major_kernel(x_ref, o_ref):
    o_ref[...] = x_ref[...].reshape(RW_HEAD * LANES, TIME_BLK).T.reshape(TIME_BLK, RW_HEAD, LANES)


def _time_minor_kernel(x_ref, o_ref):
    o_ref[...] = x_ref[...].reshape(TIME_BLK, RW_HEAD * LANES).T.reshape(RW_HEAD, LANES, TIME_BLK)


def time_major(x):
    _, l, t = x.shape
    return pl.pallas_call(
        _time_major_kernel,
        grid=(l // LANES, t // TIME_BLK),
        in_specs=[pl.BlockSpec((RW_HEAD, LANES, TIME_BLK), lambda g, c: (0, g, c))],
        out_specs=pl.BlockSpec((TIME_BLK, RW_HEAD, LANES), lambda g, c: (c, 0, g)),
        out_shape=jax.ShapeDtypeStruct((t, RW_HEAD, l), F32),
        compiler_params=_cparams(("arbitrary", "arbitrary")),
        name="time_major",
    )(x)


def time_minor(x):
    t, _, l = x.shape
    return pl.pallas_call(
        _time_minor_kernel,
        grid=(l // LANES, t // TIME_BLK),
        in_specs=[pl.BlockSpec((TIME_BLK, RW_HEAD, LANES), lambda g, c: (c, 0, g))],
        out_specs=pl.BlockSpec((RW_HEAD, LANES, TIME_BLK), lambda g, c: (0, g, c)),
        out_shape=jax.ShapeDtypeStruct((RW_HEAD, l, t), F32),
        compiler_params=_cparams(("arbitrary", "arbitrary")),
        name="time_minor",
    )(x)


RW_TM = 256


def _split_dot(x, w):
    hi = x.astype(BF16)
    lo = (x - hi.astype(F32)).astype(BF16)
    return jnp.dot(hi, w, preferred_element_type=F32) + jnp.dot(lo, w, preferred_element_type=F32)


def _rwkv_pre_kernel(x_ref, ps_ref, ln_ref, mu_ref, vec_ref, wr_ref, wk_ref, wv_ref, w1_ref, w2_ref, a1_ref, a2_ref,
                     g1_ref, g2_ref,
                     r_o, d_o, k_o, a_o, v_o, g_o, xn_o, carry_scr, *, tiles_per_seq, n_seq_tiles):
    i = pl.program_id(0)
    tm = x_ref.shape[0]

    @pl.when(i == 0)
    def _():
        carry_scr[...] = jnp.zeros_like(carry_scr)

    x = x_ref[...]
    xn = x * lax.rsqrt(jnp.mean(x * x, axis=-1, keepdims=True) + RMS_EPS) * ln_ref[...]
    xn_o[...] = xn
    first = jnp.where(i % tiles_per_seq == 0, jnp.zeros_like(carry_scr[...]), carry_scr[...])
    row0 = lax.broadcasted_iota(jnp.int32, xn.shape, 0) == 0
    prev = jnp.where(row0, first, pltpu.roll(xn, 1, 0))
    prev = jnp.where(i >= n_seq_tiles, ps_ref[...], prev)
    carry_scr[...] = xn[tm - 1:tm, :]
    xx = prev - xn
    mix = lambda c: (xn + xx * mu_ref[c:c + 1, :]).astype(BF16)
    dot = lambda a, w_ref: jnp.dot(a, w_ref[...], preferred_element_type=F32)
    w0, a0 = (vec_ref[c:c + 1, :] for c in range(2))
    r_o[...] = dot(mix(0), wr_ref)
    k_o[...] = dot(mix(2), wk_ref)
    v_o[...] = dot(mix(3), wv_ref)
    w_log = -jax.nn.softplus(-(w0 + dot(jnp.tanh(dot(mix(1), w1_ref)).astype(BF16), w2_ref))) - 0.5
    d_o[...] = jnp.exp(-jnp.exp(w_log))
    a_o[...] = jax.nn.sigmoid(a0 + dot(dot(mix(4), a1_ref).astype(BF16), a2_ref))
    g_o[...] = dot(jax.nn.sigmoid(dot(mix(5), g1_ref)).astype(BF16), g2_ref)


def rwkv_pre(h, prev_single, n_seq_rows, t, ln, mu, w_r, w_k, w_v, w0, w1, w2, a0, a1, a2, g1, g2):
    n, dm = h.shape
    tm = RW_TM
    assert n % tm == 0 and t % tm == 0 and n_seq_rows % tm == 0 and prev_single.shape == (tm, dm)
    row = pl.BlockSpec((tm, dm), lambda i: (i, 0))
    full = lambda a: pl.BlockSpec(a.shape, lambda i: (0,) * a.ndim)
    bf = lambda w: w.astype(BF16)
    consts = [prev_single, ln[None, :], mu, jnp.stack([w0, a0]), bf(w_r), bf(w_k), bf(w_v), bf(w1), bf(w2),
              bf(a1), bf(a2), bf(g1), bf(g2)]
    return pl.pallas_call(
        functools.partial(_rwkv_pre_kernel, tiles_per_seq=t // tm, n_seq_tiles=n_seq_rows // tm),
        grid=(n // tm,),
        in_specs=[row] + [full(a) for a in consts],
        out_specs=[row] * 7,
        out_shape=[jax.ShapeDtypeStruct((n, dm), F32)] * 7,
        scratch_shapes=[pltpu.VMEM((1, dm), F32)],
        compiler_params=_cparams(("arbitrary",)),
        name="rwkv_pre",
    )(h, *consts)


def _mix_out_kernel(*refs, n_branch, use_gates):
    o_refs = refs[:n_branch]
    gate_ref, h_ref, wo_ref, ln_ref, wr_ref = refs[n_branch:n_branch + 5]
    e_refs = refs[n_branch + 5:n_branch + 5 + (n_branch if use_gates else 0)]
    h_o, xp_o, lg_o = refs[-3:]
    if use_gates:
        sig = jax.nn.sigmoid(gate_ref[...])
        mixed = sum(_split_dot(sig, e_refs[b][...]) * o_refs[b][...] for b in range(n_branch))
    else:
        mixed = o_refs[0][...] * gate_ref[...]
    h = h_ref[...] + jnp.dot(mixed.astype(BF16), wo_ref[...], preferred_element_type=F32)
    h_o[...] = h
    xn = h * lax.rsqrt(jnp.mean(h * h, axis=-1, keepdims=True) + RMS_EPS) * ln_ref[...]
    xb = xn.astype(BF16)
    lg_o[...] = jnp.dot(xb, wr_ref[...], preferred_element_type=F32)
    u = lax.bitcast_convert_type(xb.astype(F32), jnp.uint32)
    half = u.shape[1] // 2
    xp_o[...] = (u[:, :half] >> 16) | (u[:, half:] & jnp.uint32(0xFFFF0000))


def mix_out(branches, gate, h, w_o, ln, w_router, gate_expand=None, gate_col=0):
    n, dm = h.shape
    tm = min(TOKEN_TILE, n)
    nb = len(branches)
    row = lambda a: pl.BlockSpec((tm, a.shape[1]), lambda i: (i, 0))
    full = lambda a: pl.BlockSpec(a.shape, lambda i: (0,) * a.ndim)
    consts = [w_o.astype(BF16), ln[None, :], w_router.astype(BF16)] + list(gate_expand or [])
    gate_spec = row(gate) if gate_expand is None else pl.BlockSpec((tm, LANES), lambda i: (i, gate_col))
    return pl.pallas_call(
        functools.partial(_mix_out_kernel, n_branch=nb, use_gates=gate_expand is not None),
        grid=(n // tm,),
        in_specs=[row(a) for a in branches] + [gate_spec, row(h)] + [full(a) for a in consts],
        out_specs=[pl.BlockSpec((tm, dm), lambda i: (i, 0)), pl.BlockSpec((tm, dm // 2), lambda i: (i, 0)),
                   pl.BlockSpec((tm, LANES), lambda i: (i, 0))],
        out_shape=[jax.ShapeDtypeStruct((n, dm), F32), jax.ShapeDtypeStruct((n, dm // 2), jnp.uint32),
                   jax.ShapeDtypeStruct((n, LANES), F32)],
        compiler_params=_cparams(("arbitrary",)),
        name="mix_out",
    )(*branches, gate, h, *consts)


MOE_TM = 256
MOE_VMEM_LIMIT = 56 << 20


def _moe_kernel(te_ref, nt_ref, tok_ref, x_ref, c_ref, wgu_ref, wd_ref, o_ref, xt_scr):
    i = pl.program_id(0)

    @pl.when(i < nt_ref[0])
    def _():
        base = i * MOE_TM

        def gather_rows(g8, carry):
            r0 = pl.multiple_of(g8 * 8, 8)
            for u in range(8):
                xt_scr[pl.ds(r0 + u, 1), :] = x_ref[pl.ds(tok_ref[base + r0 + u], 1), :]
            return carry

        lax.fori_loop(0, MOE_TM // 8, gather_rows, 0)
        u = xt_scr[...]
        lo = lax.bitcast_convert_type(u << 16, F32).astype(BF16)
        hi = lax.bitcast_convert_type(u & jnp.uint32(0xFFFF0000), F32).astype(BF16)
        x = jnp.concatenate([lo, hi], axis=1)
        h = jnp.dot(x, wgu_ref[0], preferred_element_type=F32)
        g = h[:, :D_EXPERT]
        u2 = h[:, D_EXPERT:]
        hid = g * jax.nn.sigmoid(g) * u2 * c_ref[...]
        o_ref[...] = jnp.dot(hid.astype(BF16), wd_ref[0], preferred_element_type=F32)

    @pl.when(i >= nt_ref[0])
    def _():
        o_ref[...] = jnp.zeros_like(o_ref)


def moe_gmm(xp, row_tok, cs, tile_expert, n_tiles_used, wgu, wd):
    n, half = xp.shape
    dm = 2 * half
    p = row_tok.shape[0]
    nt = p // MOE_TM

    def row_map(i, te, ntu, tok):
        return (jnp.minimum(i, ntu[0] - 1), 0)

    grid_spec = pltpu.PrefetchScalarGridSpec(
        num_scalar_prefetch=3,
        grid=(nt,),
        in_specs=[pl.BlockSpec((n, half), lambda i, te, ntu, tok: (0, 0), pipeline_mode=pl.Buffered(1)),
                  pl.BlockSpec((MOE_TM, 1), row_map),
                  pl.BlockSpec((1, dm, 2 * D_EXPERT), lambda i, te, ntu, tok: (te[i], 0, 0)),
                  pl.BlockSpec((1, D_EXPERT, dm), lambda i, te, ntu, tok: (te[i], 0, 0))],
        out_specs=pl.BlockSpec((MOE_TM, dm), lambda i, te, ntu, tok: (i, 0)),
        scratch_shapes=[pltpu.VMEM((MOE_TM, half), jnp.uint32)],
    )
    return pl.pallas_call(
        _moe_kernel,
        grid_spec=grid_spec,
        out_shape=jax.ShapeDtypeStruct((p, dm), F32),
        compiler_params=pltpu.CompilerParams(dimension_semantics=("arbitrary",), vmem_limit_bytes=MOE_VMEM_LIMIT),
        name="moe_gmm",
    )(tile_expert, n_tiles_used, row_tok, xp, cs, wgu, wd)


def router_weight(w_rg, w_re):
    w_r = jnp.concatenate([w_rg, w_re], axis=1)
    return jnp.pad(w_r, ((0, 0), (0, LANES - w_r.shape[1])))


def hmoe(xp, logits, set_sizes, b_rg, b_re, w_gate, w_up, w_down):
    n = xp.shape[0]
    dm = 2 * xp.shape[1]
    lg = logits[:, :N_GROUPS] + b_rg
    le = (logits[:, N_GROUPS:N_GROUPS + N_EXPERTS] + b_re).reshape(n, N_GROUPS, EXPERTS_PER_GROUP)
    g_sel = jnp.argmax(lg, axis=-1)
    p_g = jnp.max(jax.nn.softmax(lg, axis=-1), axis=-1)
    le_g = jnp.take_along_axis(le, g_sel[:, None, None], axis=1)[:, 0]
    top_p, top_i = lax.top_k(jax.nn.softmax(le_g, axis=-1), 2)
    w_top = p_g[:, None] * top_p / jnp.sum(top_p, axis=-1, keepdims=True)
    eid = (g_sel[:, None] * EXPERTS_PER_GROUP + top_i).astype(jnp.int32).reshape(-1)
    cw = w_top.reshape(-1)

    n2 = 2 * n
    onehot = (eid[:, None] == jnp.arange(N_EXPERTS, dtype=jnp.int32)[None, :]).astype(jnp.int32)
    counts = jnp.sum(onehot, axis=0)
    rank = jnp.take_along_axis(jnp.cumsum(onehot, axis=0) - onehot, eid[:, None], axis=1)[:, 0]
    tiles_per = (counts + MOE_TM - 1) // MOE_TM
    tile_end = jnp.cumsum(tiles_per)
    tile_start = tile_end - tiles_per
    pos = tile_start[eid] * MOE_TM + rank
    p_rows = _round_up(n2 + N_EXPERTS * (MOE_TM - 1), MOE_TM)
    nt = p_rows // MOE_TM
    n_used = tile_end[-1].astype(jnp.int32)
    tile_ids = jnp.arange(nt, dtype=jnp.int32)
    tile_expert = jnp.sum((tile_end[None, :] <= jnp.minimum(tile_ids, n_used - 1)[:, None]).astype(jnp.int32), axis=1)

    assert n < 2 ** 24
    tok = (jnp.arange(n2, dtype=jnp.int32) // 2).astype(F32)
    rows = jnp.zeros((p_rows, 2), F32).at[pos].set(jnp.stack([tok, cw], axis=1))
    row_tok = rows[:, 0].astype(jnp.int32)
    wgu = jnp.concatenate([w_gate, w_up], axis=-1).reshape(N_EXPERTS, dm, 2 * D_EXPERT).astype(BF16)
    wd = w_down.reshape(N_EXPERTS, D_EXPERT, dm).astype(BF16)
    ys = moe_gmm(xp, row_tok, rows[:, 1:2], tile_expert, n_used.reshape(1), wgu, wd)
    pos2 = pos.reshape(n, 2)
    bounds = np.cumsum([0] + list(set_sizes))
    return [(ys[pos2[a:b, 0]], ys[pos2[a:b, 1]]) for a, b in zip(bounds[:-1], bounds[1:])]


ATT_TQ = 512
ATT_TK = 512


def _cmp_sel_kernel(q_ref, kc_ref, vct_ref, ovt_ref, o_ref, sel_ref, *, n_cmp, n_sblk, tq):
    t0 = pl.program_id(2) * tq
    ncp = kc_ref.shape[2]
    n_io = lax.broadcasted_iota(jnp.int32, (ncp, tq), 0)
    t_io = lax.broadcasted_iota(jnp.int32, (ncp, tq), 1) + t0
    mask = (n_io * CMP_STRIDE + (CMP_BLOCK - 1) <= t_io) & (n_io < n_cmp)
    kc = kc_ref[0, 0]
    vct = vct_ref[0, 0]
    ovt = ovt_ref[...]
    imp = jnp.zeros((n_sblk, tq), F32)
    outs = []
    for r in range(GQA_R):
        qr = (q_ref[:, r * HEAD_DIM:(r + 1) * HEAD_DIM] * ATT_SCALE).astype(BF16)
        st = lax.dot_general(kc, qr, _NT, preferred_element_type=F32)
        st = jnp.where(mask, st, NEG_BIG)
        m = jnp.max(st, axis=0, keepdims=True)
        e = jnp.where(mask, jnp.exp(st - m), 0.0)
        p = (e / jnp.maximum(jnp.sum(e, axis=0, keepdims=True), 1e-30)).astype(BF16)
        outs.append(jnp.dot(vct, p, preferred_element_type=F32).T)
        imp = imp + jnp.dot(ovt, p, preferred_element_type=F32)
    o_ref[...] = jnp.concatenate(outs, axis=1)

    j_io = lax.broadcasted_iota(jnp.int32, (n_sblk, tq), 0)
    qblk = jnp.right_shift(lax.broadcasted_iota(jnp.int32, (n_sblk, tq), 1) + t0, SEL_BLOCK.bit_length() - 1)
    allowed = j_io <= qblk
    forced = (j_io == 0) | ((j_io > qblk - N_LOCAL_BLOCKS) & allowed)
    score = jnp.where(forced, FORCE_SCORE, imp)
    score = jnp.where(allowed, score, -jnp.inf)
    rank = jnp.zeros((n_sblk, tq), jnp.int32)
    for jp in range(n_sblk):
        row = score[jp:jp + 1, :]
        beats = (row > score) | ((row == score) & (j_io > jp))
        rank = rank + jnp.where(beats, 1, 0)
    sel = (rank < N_SEL) & allowed
    sel_ref[0, 0] = jnp.where(sel, 1.0, 0.0).astype(BF16)


def cmp_select(q, kc, vct, ovt, n_b, t, n_cmp, n_sblk, n_out):
    assert SEL_BLOCK & (SEL_BLOCK - 1) == 0
    tq = ATT_TQ
    nq = t // tq
    ncp = kc.shape[2]
    gw = GQA_R * HEAD_DIM
    return pl.pallas_call(
        functools.partial(_cmp_sel_kernel, n_cmp=n_cmp, n_sblk=n_sblk, tq=tq),
        grid=(n_b, KV_HEADS, nq),
        in_specs=[pl.BlockSpec((tq, gw), lambda b, g, qi: (b * nq + qi, g)),
                  pl.BlockSpec((1, 1, ncp, HEAD_DIM), lambda b, g, qi: (b, g, 0, 0)),
                  pl.BlockSpec((1, 1, HEAD_DIM, ncp), lambda b, g, qi: (b, g, 0, 0)),
                  pl.BlockSpec((n_sblk, ncp), lambda b, g, qi: (0, 0))],
        out_specs=[pl.BlockSpec((tq, gw), lambda b, g, qi: (b * nq + qi, g)),
                   pl.BlockSpec((1, 1, n_sblk, tq), lambda b, g, qi: (b, g, 0, qi))],
        out_shape=[jax.ShapeDtypeStruct((n_out, N_HEADS * HEAD_DIM), F32),
                   jax.ShapeDtypeStruct((n_b, KV_HEADS, n_sblk, t), BF16)],
        compiler_params=_cparams(("arbitrary", "arbitrary", "arbitrary")),
        name="attn_cmp_select",
    )(q, kc, vct, ovt)


def _flash_kernel(pq_ref, pk_ref, pfirst_ref, plast_ref, q_ref, kt_ref, vt_ref, *rest, window, use_sel, tq, tk):
    if use_sel:
        sel_ref, et_ref, o_ref, m_scr, l_scr, acc_scr = rest
    else:
        o_ref, m_scr, l_scr, acc_scr = rest
    p_id = pl.program_id(2)

    @pl.when(pfirst_ref[p_id] == 1)
    def _():
        m_scr[...] = jnp.full_like(m_scr, M_FLOOR)
        l_scr[...] = jnp.zeros_like(l_scr)
        acc_scr[...] = jnp.zeros_like(acc_scr)

    off = pq_ref[p_id] * tq - pk_ref[p_id] * tk
    diff = lax.broadcasted_iota(jnp.int32, (tk, tq), 0) - lax.broadcasted_iota(jnp.int32, (tk, tq), 1)
    mask = diff <= off
    if window is not None:
        mask = mask & (diff > off - window)
    if use_sel:
        selm = jnp.dot(et_ref[...], sel_ref[0, 0], preferred_element_type=F32)
        mask = mask & (selm > 0.5)
    bias = jnp.where(mask, 0.0, NEG_BIG)
    k = kt_ref[0, 0, 0].astype(BF16).T
    vt = vt_ref[0, 0, 0].astype(BF16)
    heads = range(GQA_R)
    qs = [(q_ref[:, r * HEAD_DIM:(r + 1) * HEAD_DIM] * ATT_SCALE).astype(BF16) for r in heads]
    sts = [lax.dot_general(k, qs[r], _NT, preferred_element_type=F32) + bias for r in heads]
    m_prev = [m_scr[r] for r in heads]
    m_new = [jnp.maximum(m_prev[r], jnp.max(sts[r], axis=0, keepdims=True)) for r in heads]
    ps = [jnp.exp(sts[r] - m_new[r]) for r in heads]
    alphas = [jnp.exp(m_prev[r] - m_new[r]) for r in heads]
    for r in heads:
        l_scr[r] = alphas[r] * l_scr[r] + jnp.sum(ps[r], axis=0, keepdims=True)
        acc_scr[r] = alphas[r] * acc_scr[r] + jnp.dot(vt, ps[r].astype(BF16), preferred_element_type=F32)
        m_scr[r] = m_new[r]

    @pl.when(plast_ref[p_id] == 1)
    def _():
        outs = [(acc_scr[r] / jnp.maximum(l_scr[r], 1e-30)).T for r in range(GQA_R)]
        o_ref[...] = jnp.concatenate(outs, axis=1)


def flash_attention(q, kv_t, k_stream, v_stream, n_b, t, window=None, sel=None):
    tq, tk = ATT_TQ, ATT_TK
    nq = t // tq
    first_tile = lambda qi: 0 if window is None else max(0, qi * tq - (window - 1)) // tk
    pairs = [(qi, kt) for qi in range(nq) for kt in range(first_tile(qi), (qi * tq + tq - 1) // tk + 1)]
    pq = np.array([p[0] for p in pairs], np.int32)
    pk = np.array([p[1] for p in pairs], np.int32)
    pfirst = np.array([1 if i == 0 or pairs[i - 1][0] != p[0] else 0 for i, p in enumerate(pairs)], np.int32)
    plast = np.array([1 if i == len(pairs) - 1 or pairs[i + 1][0] != p[0] else 0 for i, p in enumerate(pairs)], np.int32)
    gw = GQA_R * HEAD_DIM

    def tile_map(stream):
        return lambda b, g, p, pq_r, pk_r, f_r, l_r: (b, stream, g, 0, pk_r[p])

    in_specs = [pl.BlockSpec((tq, gw), lambda b, g, p, pq_r, pk_r, f_r, l_r: (b * nq + pq_r[p], g)),
                pl.BlockSpec((1, 1, 1, HEAD_DIM, tk), tile_map(k_stream)),
                pl.BlockSpec((1, 1, 1, HEAD_DIM, tk), tile_map(v_stream))]
    args = [q, kv_t, kv_t]
    if sel is not None:
        n_sblk = sel.shape[2]
        emat = (np.arange(t)[:, None] // SEL_BLOCK == np.arange(n_sblk)[None, :]).astype(np.float32)
        in_specs += [pl.BlockSpec((1, 1, n_sblk, tq), lambda b, g, p, pq_r, pk_r, f_r, l_r: (b, g, 0, pq_r[p])),
                     pl.BlockSpec((tk, n_sblk), lambda b, g, p, pq_r, pk_r, f_r, l_r: (pk_r[p], 0))]
        args += [sel, jnp.asarray(emat, BF16)]
    grid_spec = pltpu.PrefetchScalarGridSpec(
        num_scalar_prefetch=4,
        grid=(n_b, KV_HEADS, len(pairs)),
        in_specs=in_specs,
        out_specs=pl.BlockSpec((tq, gw), lambda b, g, p, pq_r, pk_r, f_r, l_r: (b * nq + pq_r[p], g)),
        scratch_shapes=[pltpu.VMEM((GQA_R, 1, tq), F32), pltpu.VMEM((GQA_R, 1, tq), F32),
                        pltpu.VMEM((GQA_R, HEAD_DIM, tq), F32)],
    )
    return pl.pallas_call(
        functools.partial(_flash_kernel, window=window, use_sel=sel is not None, tq=tq, tk=tk),
        grid_spec=grid_spec,
        out_shape=jax.ShapeDtypeStruct((n_b * t, N_HEADS * HEAD_DIM), F32),
        compiler_params=_cparams(("arbitrary", "arbitrary", "arbitrary")),
        name="attn_sel" if sel is not None else "attn_win",
    )(jnp.asarray(pq), jnp.asarray(pk), jnp.asarray(pfirst), jnp.asarray(plast), *args)


def _attn_pre_kernel(h_ref, e0_ref, e1_ref, ln_ref, wq_ref, wkv_ref, rc_ref, rm_ref, rp_ref, h_o, qg_o, qr_o, kvt_o):
    x = h_ref[...] + e0_ref[...] + e1_ref[...]
    h_o[...] = x
    xs = x * lax.rsqrt(jnp.mean(x * x, axis=-1, keepdims=True) + RMS_EPS)
    qg = jnp.dot((xs * ln_ref[0:1, :]).astype(BF16), wq_ref[...], preferred_element_type=F32)
    kv = jnp.dot((xs * ln_ref[1:2, :]).astype(BF16), wkv_ref[...], preferred_element_type=F32)
    qg_o[...] = qg

    def rope(v):
        reps = v.shape[1] // LANES
        tile = lambda ref: jnp.concatenate([ref[...]] * reps, axis=1)
        n = v.shape[1]
        return v * tile(rc_ref) + pltpu.roll(v, n - ROPE_DIMS // 2, 1) * tile(rm_ref) + pltpu.roll(v, ROPE_DIMS // 2, 1) * tile(rp_ref)

    n_q = N_HEADS * HEAD_DIM
    sw = KV_HEADS * HEAD_DIM
    qr_o[...] = rope(qg[:, :n_q])
    kv_r = jnp.concatenate([kv[:, :2 * sw], rope(kv[:, 2 * sw:3 * sw]), kv[:, 3 * sw:4 * sw],
                            rope(kv[:, 4 * sw:5 * sw]), kv[:, 5 * sw:]], axis=1)
    kvt_o[0] = kv_r.T


def attn_pre(h, e0, e1, n_b, t, ln_q, ln_kv, w_qg, w_kv):
    n, dm = h.shape
    tm = RW_TM
    tiles = t // tm
    half = ROPE_DIMS // 2
    inv = jnp.power(ROPE_THETA, -jnp.arange(half, dtype=F32) * (2.0 / ROPE_DIMS))
    ang = jnp.arange(t, dtype=F32)[:, None] * inv[None, :]
    cos, sin = jnp.cos(ang), jnp.sin(ang)
    pad = jnp.zeros((t, HEAD_DIM - ROPE_DIMS), F32)
    zero = jnp.zeros((t, half), F32)
    head = lambda parts: jnp.concatenate(parts, axis=1)
    two = lambda x: jnp.concatenate([x, x], axis=1)
    rc = two(head([cos, cos, pad + 1.0]))
    rm = two(head([-sin, zero, pad]))
    rp = two(head([zero, sin, pad]))
    nq_cols = w_qg.shape[1]
    nkv = w_kv.shape[1]
    tab = pl.BlockSpec((tm, LANES), lambda i: (i % tiles, 0))
    full = lambda a: pl.BlockSpec(a.shape, lambda i: (0,) * a.ndim)
    consts = [jnp.stack([ln_q, ln_kv]), w_qg.astype(BF16), w_kv.astype(BF16)]
    return pl.pallas_call(
        _attn_pre_kernel,
        grid=(n // tm,),
        in_specs=[pl.BlockSpec((tm, dm), lambda i: (i, 0))] * 3 + [full(a) for a in consts] + [tab, tab, tab],
        out_specs=[pl.BlockSpec((tm, dm), lambda i: (i, 0)),
                   pl.BlockSpec((tm, nq_cols), lambda i: (i, 0)),
                   pl.BlockSpec((tm, N_HEADS * HEAD_DIM), lambda i: (i, 0)),
                   pl.BlockSpec((1, nkv, tm), lambda i: (i // tiles, 0, i % tiles))],
        out_shape=[jax.ShapeDtypeStruct((n, dm), F32), jax.ShapeDtypeStruct((n, nq_cols), F32),
                   jax.ShapeDtypeStruct((n, N_HEADS * HEAD_DIM), F32), jax.ShapeDtypeStruct((n_b, nkv, t), F32)],
        compiler_params=_cparams(("arbitrary",)),
        name="attn_pre",
    )(h, e0, e1, *consts, rc, rm, rp)


def _final_norm_kernel(h_ref, e0_ref, e1_ref, ln_ref, o_ref):
    x = h_ref[...] + e0_ref[...] + e1_ref[...]
    o_ref[...] = x * lax.rsqrt(jnp.mean(x * x, axis=-1, keepdims=True) + RMS_EPS) * ln_ref[...]


def final_norm(h, e0, e1, ln):
    n, dm = h.shape
    row = pl.BlockSpec((TOKEN_TILE, dm), lambda i: (i, 0))
    return pl.pallas_call(
        _final_norm_kernel,
        grid=(n // TOKEN_TILE,),
        in_specs=[row] * 3 + [pl.BlockSpec((1, dm), lambda i: (0, 0))],
        out_specs=row,
        out_shape=jax.ShapeDtypeStruct((n, dm), F32),
        compiler_params=_cparams(("arbitrary",)),
        name="final_norm",
    )(h, e0, e1, ln[None, :])


def _cmp_kernel(q_ref, k_ref, v_ref, ov_ref, o_ref, imp_ref, *, n_cmp):
    q = q_ref[0] * ATT_SCALE
    s = lax.dot_general(q, k_ref[0], _NT, preferred_element_type=F32)
    mask = lax.broadcasted_iota(jnp.int32, s.shape, 1) < n_cmp
    s = jnp.where(mask, s, NEG_BIG)
    m = jnp.max(s, axis=-1, keepdims=True)
    e = jnp.where(mask, jnp.exp(s - m), 0.0)
    p = (e / jnp.maximum(jnp.sum(e, axis=-1, keepdims=True), 1e-30)).astype(BF16)
    o_ref[0] = jnp.dot(p, v_ref[0], preferred_element_type=F32)
    imp_ref[0] = jnp.dot(p, ov_ref[...], preferred_element_type=F32)


def cmp_attention_sample(q, kc, vc, ov, n_cmp):
    bg, rq, hd = q.shape
    nc = kc.shape[1]
    nb = ov.shape[1]
    return pl.pallas_call(
        functools.partial(_cmp_kernel, n_cmp=n_cmp),
        grid=(bg,),
        in_specs=[pl.BlockSpec((1, rq, hd), lambda b: (b, 0, 0)),
                  pl.BlockSpec((1, nc, hd), lambda b: (b, 0, 0)),
                  pl.BlockSpec((1, nc, hd), lambda b: (b, 0, 0)),
                  pl.BlockSpec((nc, nb), lambda b: (0, 0))],
        out_specs=[pl.BlockSpec((1, rq, hd), lambda b: (b, 0, 0)),
                   pl.BlockSpec((1, rq, nb), lambda b: (b, 0, 0))],
        out_shape=[jax.ShapeDtypeStruct((bg, rq, hd), F32),
                   jax.ShapeDtypeStruct((bg, rq, nb), F32)],
        compiler_params=_cparams(("arbitrary",)),
        name="attn_cmp_sample",
    )(q, kc, vc, ov)


PAGES_PER_STEP = 32


def _page_compress_kernel(pt_ref, *rest, n_pages, page, pb):
    c_refs = rest[:pb]
    wbd_ref, pe_ref, w2_ref, o_ref, a_scr = rest[pb:]
    p = pl.program_id(1)
    nsub = n_pages * page // CMP_STRIDE
    for i in range(pb):
        row0 = pl.multiple_of((p * pb + i) * page, page)
        for s in range(2):
            for gp in range(KV_HEADS // 2):
                x = c_refs[i][0, s, 2 * gp:2 * gp + 2].reshape(2 * HEAD_DIM, page)
                a_scr[s, gp, pl.ds(row0, page), :] = x.T

    @pl.when(p == n_pages // pb - 1)
    def _():
        hid = pe_ref.shape[-1]
        for s in range(2):
            for gp in range(KV_HEADS // 2):
                acc = jnp.zeros((nsub, 4 * hid), F32)
                for j in range(CMP_STRIDE):
                    lhs = a_scr[s, gp, pl.ds(j, nsub, stride=CMP_STRIDE), :].astype(BF16)
                    acc = acc + jnp.dot(lhs, wbd_ref[s, j], preferred_element_type=F32)
                for g2 in range(2):
                    first = acc[:, 2 * g2 * hid:(2 * g2 + 1) * hid]
                    second = acc[:, (2 * g2 + 1) * hid:(2 * g2 + 2) * hid]
                    pre = first + pltpu.roll(second, nsub - 1, 0) + pe_ref[s]
                    act = jax.nn.gelu(pre).astype(BF16)
                    o_ref[s, 0, 2 * gp + g2] = jnp.dot(act, w2_ref[s], preferred_element_type=F32).astype(BF16)


def page_compress(cache_t, page_table, cmp_pe, cmp_w1, cmp_b1, cmp_w2):
    n_b, n_pages = page_table.shape
    page = cache_t.shape[-1]
    nsub = n_pages * page // CMP_STRIDE
    hid = cmp_w1.shape[-1]
    pb = PAGES_PER_STEP
    assert n_pages % pb == 0
    w1r = cmp_w1.reshape(2, 2, CMP_STRIDE, HEAD_DIM, hid).transpose(0, 2, 3, 1, 4)
    w1r = w1r.reshape(2, CMP_STRIDE, HEAD_DIM, 2 * hid)
    zeros = jnp.zeros_like(w1r)
    wbd = jnp.concatenate([jnp.concatenate([w1r, zeros], axis=-1), jnp.concatenate([zeros, w1r], axis=-1)], axis=2)
    pe_term = (jnp.einsum('sjd,sjdc->sc', cmp_pe, cmp_w1.reshape(2, CMP_BLOCK, HEAD_DIM, hid)) + cmp_b1)[:, None, :]
    def page_map(i):
        return lambda b, p, pt: (pt[b * n_pages + p * pb + i], 0, 0, 0, 0)

    grid_spec = pltpu.PrefetchScalarGridSpec(
        num_scalar_prefetch=1,
        grid=(n_b, n_pages // pb),
        in_specs=[pl.BlockSpec((1, 2, KV_HEADS, HEAD_DIM, page), page_map(i)) for i in range(pb)] + [
            pl.BlockSpec((2, CMP_STRIDE, 2 * HEAD_DIM, 4 * hid), lambda b, p, pt: (0, 0, 0, 0)),
            pl.BlockSpec((2, 1, hid), lambda b, p, pt: (0, 0, 0)),
            pl.BlockSpec((2, hid, HEAD_DIM), lambda b, p, pt: (0, 0, 0))],
        out_specs=pl.BlockSpec((2, 1, KV_HEADS, nsub, HEAD_DIM), lambda b, p, pt: (0, b, 0, 0, 0)),
        scratch_shapes=[pltpu.VMEM((2, KV_HEADS // 2, n_pages * page, 2 * HEAD_DIM), F32)],
    )
    return pl.pallas_call(
        functools.partial(_page_compress_kernel, n_pages=n_pages, page=page, pb=pb),
        grid_spec=grid_spec,
        out_shape=jax.ShapeDtypeStruct((2, n_b, KV_HEADS, nsub, HEAD_DIM), BF16),
        compiler_params=_cparams(("arbitrary", "arbitrary")),
        name="page_compress",
    )(page_table.reshape(-1), *([cache_t] * pb), wbd.astype(BF16), pe_term, cmp_w2.astype(BF16))


def _decode_attn_kernel(phys_ref, blk_ref, flag_ref, q_ref, kn_ref, vn_ref, *rest, mode, past, steps, window, page):
    kt_refs = rest[:steps]
    vt_refs = rest[steps:2 * steps]
    o_ref = rest[2 * steps]
    idx = pl.program_id(0) * KV_HEADS + pl.program_id(1)
    rnd = lambda x: x.astype(BF16).astype(F32)
    q = q_ref[0, 0] * ATT_SCALE

    kn = rnd(kn_ref[0, 0])
    vn = rnd(vn_ref[0, 0])
    s_new = jnp.sum(q.astype(F32) * kn, axis=-1, keepdims=True)
    on = flag_ref[idx] == 1
    m = jnp.where(on, s_new, NEG_BIG)
    scs, masks = [], []
    for s in range(steps):
        kt = kt_refs[s][0, 0, 0].astype(BF16)
        tile = kt.shape[-1]
        sc = jnp.dot(q, kt, preferred_element_type=F32)
        lane = lax.broadcasted_iota(jnp.int32, sc.shape, 1)
        if mode == "sel":
            blk = blk_ref[idx * steps + s]
            per_page = page // SEL_BLOCK
            pos = (blk // per_page) * page + lane
            mask = (jnp.right_shift(pos, SEL_BLOCK.bit_length() - 1) == blk) & (pos <= past) & (blk >= 0)
        else:
            pos = past - tile + lane
            mask = (pos > past - window) & (pos <= past)
        sc = jnp.where(mask, sc, NEG_BIG)
        m = jnp.maximum(m, jnp.max(sc, axis=-1, keepdims=True))
        scs.append(sc)
        masks.append(mask)
    p_new = jnp.where(on, jnp.exp(s_new - m), 0.0)
    l = p_new
    acc = rnd(p_new) * vn
    for s in range(steps):
        p = jnp.where(masks[s], jnp.exp(scs[s] - m), 0.0)
        l = l + jnp.sum(p, axis=-1, keepdims=True)
        vt = vt_refs[s][0, 0, 0].astype(BF16)
        acc = acc + lax.dot_general(p.astype(BF16), vt, _NT, preferred_element_type=F32)
    o_ref[0, 0] = acc / jnp.maximum(l, 1e-30)


def decode_attention(q, k_new, v_new, kv_t, streams, phys, blk, flag, mode, past, steps, window=None, page=None):
    n_b, n_g, rows, hd = q.shape
    tile = kv_t.shape[-1]
    ks, vs = streams

    def kmap(st, s):
        if mode == "sel":
            return lambda b, g, ph, bl, fl: (ph[(b * n_g + g) * steps + s], st, g, 0, 0)
        return lambda b, g, ph, bl, fl: (b, st, g, 0, 0)

    tiles = ([pl.BlockSpec((1, 1, 1, hd, tile), kmap(ks, s)) for s in range(steps)]
             + [pl.BlockSpec((1, 1, 1, hd, tile), kmap(vs, s)) for s in range(steps)])
    grid_spec = pltpu.PrefetchScalarGridSpec(
        num_scalar_prefetch=3,
        grid=(n_b, n_g),
        in_specs=[pl.BlockSpec((1, 1, rows, hd), lambda b, g, ph, bl, fl: (b, g, 0, 0)),
                  pl.BlockSpec((1, 1, 1, hd), lambda b, g, ph, bl, fl: (b, g, 0, 0)),
                  pl.BlockSpec((1, 1, 1, hd), lambda b, g, ph, bl, fl: (b, g, 0, 0))] + tiles,
        out_specs=pl.BlockSpec((1, 1, rows, hd), lambda b, g, ph, bl, fl: (b, g, 0, 0)),
    )
    return pl.pallas_call(
        functools.partial(_decode_attn_kernel, mode=mode, past=past, steps=steps, window=window, page=page),
        grid_spec=grid_spec,
        out_shape=jax.ShapeDtypeStruct((n_b, n_g, rows, hd), F32),
        compiler_params=_cparams(("arbitrary", "arbitrary")),
        name="attn_decode_" + mode,
    )(phys, blk, flag, q, k_new, v_new, *([kv_t] * (2 * steps)))


def rmsnorm(x, g):
    y = x * lax.rsqrt(jnp.mean(x * x, axis=-1, keepdims=True) + RMS_EPS)
    return y * g


def rope(x, pos):
    half = ROPE_DIMS // 2
    inv = jnp.power(ROPE_THETA, -jnp.arange(half, dtype=F32) * (2.0 / ROPE_DIMS))
    ang = pos.astype(F32)[:, None] * inv[None, :]
    cos = jnp.cos(ang)[:, None, :]
    sin = jnp.sin(ang)[:, None, :]
    x1 = x[..., :half]
    x2 = x[..., half:ROPE_DIMS]
    return jnp.concatenate([x1 * cos - x2 * sin, x2 * cos + x1 * sin, x[..., ROPE_DIMS:]], axis=-1)


def compress(sub, pe, w1, b1, w2, n_b, nsub):
    half = CMP_STRIDE * HEAD_DIM
    w1cat = jnp.concatenate([w1[:half], w1[half:]], axis=1)
    hidden = w1.shape[1]
    part = mm(sub, w1cat).reshape(n_b, nsub, KV_HEADS, 2, hidden)
    pe_term = jnp.einsum('jd,jdc->c', pe, w1.reshape(CMP_BLOCK, HEAD_DIM, hidden)) + b1
    pre = part[:, :nsub - 1, :, 0] + part[:, 1:, :, 1] + pe_term
    act = jax.nn.gelu(pre)
    return mm(act.reshape(-1, hidden), w2).reshape(n_b, nsub - 1, KV_HEADS, HEAD_DIM)


def overlap_matrix(n_cmp, n_sblk, nc_pad, nb_pad):
    cs = np.arange(n_cmp) * CMP_STRIDE
    ce = cs + CMP_BLOCK - 1
    ss = np.arange(n_sblk) * SEL_BLOCK
    ov = ((cs[:, None] < ss[None, :] + SEL_BLOCK) & (ce[:, None] >= ss[None, :])).astype(np.float32)
    out = np.zeros((nc_pad, nb_pad), np.float32)
    out[:n_cmp, :n_sblk] = ov
    return out


def block_scores(imp, q_pos, n_sblk):
    j = jnp.arange(n_sblk)[None, :]
    qblk = (q_pos // SEL_BLOCK)[:, None]
    allowed = j <= qblk
    forced = (j == 0) | ((j > qblk - N_LOCAL_BLOCKS) & allowed)
    score = jnp.where(forced, FORCE_SCORE, imp)
    return jnp.where(allowed, score, -jnp.inf)


def nsa_prompt(qg, q_r, kv_t, n_b, t, cmp_pe, cmp_w1, cmp_b1, cmp_w2):
    nsub = t // CMP_STRIDE
    n_cmp = nsub - 1
    n_sblk = t // SEL_BLOCK

    def sub_blocks(s):
        x = kv_t[:, s].reshape(n_b, KV_HEADS, HEAD_DIM, nsub, CMP_STRIDE)
        return x.transpose(0, 3, 1, 4, 2).reshape(-1, CMP_STRIDE * HEAD_DIM)

    kc = compress(sub_blocks(0), cmp_pe[0], cmp_w1[0], cmp_b1[0], cmp_w2[0], n_b, nsub)
    vc = compress(sub_blocks(1), cmp_pe[1], cmp_w1[1], cmp_b1[1], cmp_w2[1], n_b, nsub)
    nc_pad = _round_up(n_cmp, LANES)
    kc = jnp.pad(kc, ((0, 0), (0, nc_pad - n_cmp), (0, 0), (0, 0)))
    vc = jnp.pad(vc, ((0, 0), (0, nc_pad - n_cmp), (0, 0), (0, 0)))
    ovt = jnp.asarray(overlap_matrix(n_cmp, n_sblk, nc_pad, n_sblk).T, BF16)
    o_cmp, sel = cmp_select(qg, kc.transpose(0, 2, 1, 3).astype(BF16), vc.transpose(0, 2, 3, 1).astype(BF16),
                            ovt, n_b, t, n_cmp, n_sblk, qg.shape[0])
    o_sel = flash_attention(q_r, kv_t, 2, 3, n_b, t, sel=sel)
    o_win = flash_attention(q_r, kv_t, 4, 5, n_b, t, window=WINDOW)
    return o_cmp, o_sel, o_win


def nsa_sample(qg, kv, cache_kv, cache_win, page_table, cmp_pe, cmp_w1, cmp_b1, cmp_w2):
    n_b, n_pages = page_table.shape
    n_phys, page = cache_kv.shape[:2]
    past = n_pages * page
    pos = jnp.full((1,), past, jnp.int32)
    q = qg[:, :N_HEADS * HEAD_DIM].reshape(n_b, 1, N_HEADS, HEAD_DIM)
    kv = kv.reshape(n_b, 1, N_KV_STREAMS, KV_HEADS, HEAD_DIM)
    k_sel = rope(kv[:, :, 2], pos)
    k_win = rope(kv[:, :, 4], pos)
    new_rows = jnp.stack([kv[:, :, 0], kv[:, :, 1], k_sel, kv[:, :, 3]], axis=2)
    new_win = jnp.stack([k_win, kv[:, :, 5]], axis=2)
    n_keep = min(WINDOW, cache_win.shape[1] + 1)
    win_state = jnp.concatenate([cache_win[:, cache_win.shape[1] + 1 - n_keep:], new_win], axis=1)

    cache_t = cache_kv.transpose(0, 2, 3, 4, 1)
    win_t = cache_win.transpose(0, 2, 3, 4, 1)

    l_tot = past + 1
    nsub = l_tot // CMP_STRIDE
    n_cmp = nsub - 1
    n_sblk = -(-l_tot // SEL_BLOCK)
    assert nsub * CMP_STRIDE == past and page % SEL_BLOCK == 0 and cache_win.shape[1] == WINDOW
    kvc = page_compress(cache_t, page_table, cmp_pe, cmp_w1, cmp_b1, cmp_w2)

    bg = n_b * KV_HEADS
    rq = 16
    qpad = lambda x: jnp.pad(x.reshape(n_b, KV_HEADS, GQA_R, HEAD_DIM), ((0, 0), (0, 0), (0, rq - GQA_R), (0, 0))).astype(BF16)

    nb_pad = _round_up(n_sblk, 8)
    ov = jnp.asarray(overlap_matrix(n_cmp, n_sblk, nsub, nb_pad), BF16)
    o_cmp, imp = cmp_attention_sample(qpad(q).reshape(bg, rq, HEAD_DIM), kvc[0].reshape(bg, nsub, HEAD_DIM),
                                      kvc[1].reshape(bg, nsub, HEAD_DIM), ov, n_cmp)
    imp = imp[:, :GQA_R, :n_sblk].sum(axis=1)
    score = block_scores(imp[:, None, :], pos, n_sblk)
    top_s, top_i = lax.top_k(score[:, 0], min(N_SEL, n_sblk))
    valid = jnp.isfinite(top_s)

    n_past_blk = past // SEL_BLOCK
    per_page = page // SEL_BLOCK
    n_k = top_i.shape[-1]
    in_past = valid & (top_i < n_past_blk)
    blk = jnp.where(in_past, top_i, -1).astype(jnp.int32)
    logical = jnp.clip(top_i, 0, n_past_blk - 1) // per_page
    phys = jnp.take_along_axis(page_table, logical.reshape(n_b, KV_HEADS * n_k), axis=1).reshape(bg, n_k)
    phys = jnp.where(in_past, phys, 0).astype(jnp.int32)
    new_flag = jnp.any(valid & (top_i >= n_past_blk), axis=-1).astype(jnp.int32)
    q_r = qpad(rope(q, pos))
    o_sel = decode_attention(q_r, new_rows[:, 0, 2][:, :, None, :], new_rows[:, 0, 3][:, :, None, :], cache_t, (2, 3),
                             phys.reshape(-1), blk.reshape(-1), new_flag, "sel", past, n_k, page=page)
    zeros = jnp.zeros((bg,), jnp.int32)
    o_win = decode_attention(q_r, new_win[:, 0, 0][:, :, None, :], new_win[:, 0, 1][:, :, None, :], win_t, (0, 1),
                             zeros, zeros, jnp.ones((bg,), jnp.int32), "win", past, 1, window=WINDOW)

    heads = lambda o: o.reshape(n_b, KV_HEADS, -1, HEAD_DIM)[:, :, :GQA_R].reshape(n_b, N_HEADS * HEAD_DIM)
    return (heads(o_cmp), heads(o_sel), heads(o_win)), new_rows, win_state


def rwkv_layer(h_p, h_t, n_p, t_p, n_s, state_wkv, state_shift, ln, mu, w_r, w_k, w_v, w0, w1, w2, a0, a1, a2, g1, g2,
               k_k, k_a, r_k, gn_w, gn_b):
    np_rows = n_p * t_p
    weights = (ln, mu, w_r, w_k, w_v, w0, w1, w2, a0, a1, a2, g1, g2)
    pre_p = rwkv_pre(h_p, jnp.zeros((RW_TM, D_MODEL), F32), np_rows, t_p, *weights)
    pre_t = rwkv_pre(h_t, jnp.pad(state_shift, ((0, RW_TM - n_s), (0, 0))), 0, RW_TM, *weights)

    def lanes(x, nb, t):
        x4 = x.reshape(nb, t, RW_HEADS, RW_HEAD)
        if t % TIME_BLK == 0 and (nb * RW_HEADS) % LANES == 0:
            return time_major(x4.transpose(3, 0, 2, 1).reshape(RW_HEAD, nb * RW_HEADS, t))
        return x4.transpose(1, 3, 0, 2).reshape(t, RW_HEAD, nb * RW_HEADS)

    def rows(o, nb, t):
        if t % TIME_BLK == 0 and (nb * RW_HEADS) % LANES == 0:
            return time_minor(o).reshape(RW_HEAD, nb, RW_HEADS, t).transpose(1, 3, 2, 0).reshape(nb * t, D_MODEL)
        return o.reshape(t, RW_HEAD, nb, RW_HEADS).transpose(2, 0, 3, 1).reshape(nb * t, D_MODEL)

    head_vec = lambda w, nb: jnp.tile(w.reshape(RW_HEADS, RW_HEAD).T, (1, nb))
    vecs = lambda nb: [head_vec(w, nb) for w in (k_k, k_a, r_k, gn_w, gn_b)]
    s0_p = jnp.zeros((RW_HEAD, RW_HEAD, n_p * RW_HEADS), F32)
    o_p, st_p = rwkv_scan(*[lanes(x, n_p, t_p) for x in pre_p[:5]], *vecs(n_p), s0_p)
    s0_s = state_wkv.transpose(3, 2, 0, 1).reshape(RW_HEAD, RW_HEAD, n_s * RW_HEADS)
    o_s, st_s = rwkv_scan(*[lanes(x[:n_s], n_s, 1) for x in pre_t[:5]], *vecs(n_s), s0_s, dot_dtype=BF16)
    o_t = jnp.pad(rows(o_s, n_s, 1), ((0, RW_TM - n_s), (0, 0)))
    wkv_p = st_p.reshape(RW_HEAD, RW_HEAD, n_p, RW_HEADS).transpose(2, 3, 1, 0)
    wkv_s = st_s.reshape(RW_HEAD, RW_HEAD, n_s, RW_HEADS).transpose(2, 3, 1, 0)
    return (rows(o_p, n_p, t_p), pre_p[5], pre_p[6]), (o_t, pre_t[5], pre_t[6]), wkv_p, wkv_s


def kernel(x_prompt, x_sample, state_wkv, state_shift, cache_kv, cache_win, page_table, ln_mix, ln_ffn, ln_kv, ln_out, rw_mu, rw_wr, rw_wk, rw_wv, rw_wo, rw_w0, rw_w1, rw_w2, rw_a0, rw_a1, rw_a2, rw_g1, rw_g2, rw_kk, rw_ka, rw_rk, rw_lnw, rw_lnb, w_kv, cmp_pe, cmp_w1, cmp_b1, cmp_w2, nsa_wqg, nsa_wo, moe_wrg, moe_brg, moe_wre, moe_bre, moe_wgate, moe_wup, moe_wdown):
    n_p, t_p, dm = x_prompt.shape
    n_s = x_sample.shape[0]
    assert x_sample.shape[1] == 1
    np_rows = n_p * t_p
    h_p = x_prompt.reshape(np_rows, dm)
    h_t = jnp.pad(x_sample.reshape(n_s, dm), ((0, RW_TM - n_s), (0, 0)))
    sizes = (np_rows, RW_TM)

    def moe(fronts, layer):
        ys = hmoe(jnp.concatenate([f[1] for f in fronts]), jnp.concatenate([f[2] for f in fronts]), sizes,
                  moe_brg[layer], moe_bre[layer], moe_wgate[layer], moe_wup[layer], moe_wdown[layer])
        return [(f[0],) + tuple(y) for f, y in zip(fronts, ys)]

    mix_p, mix_t, wkv_p, wkv_s = rwkv_layer(h_p, h_t, n_p, t_p, n_s, state_wkv[0], state_shift[0], ln_mix[0], rw_mu[0],
                                            rw_wr[0], rw_wk[0], rw_wv[0], rw_w0[0], rw_w1[0], rw_w2[0], rw_a0[0], rw_a1[0],
                                            rw_a2[0], rw_g1[0], rw_g2[0], rw_kk[0], rw_ka[0], rw_rk[0], rw_lnw[0], rw_lnb[0])
    p_shift = mix_p[2].reshape(n_p, t_p, dm)[:, -1][None]
    s_shift = mix_t[2][:n_s][None]
    w_router = router_weight(moe_wrg[0], moe_wre[0])
    res_p, res_t = moe([mix_out([o], g, hh, rw_wo[0], ln_ffn[0], w_router)
                        for (o, g, _), hh in ((mix_p, h_p), (mix_t, h_t))], 0)
    h_t = sum(res_t)

    n_q = N_HEADS * HEAD_DIM
    gate_cols = _round_up(nsa_wqg.shape[-1] - n_q, LANES)
    w_qg = jnp.pad(nsa_wqg[0], ((0, 0), (0, n_q + gate_cols - nsa_wqg.shape[-1])))
    h_p, qg_p, qr_p, kvt = attn_pre(*res_p, n_p, t_p, ln_mix[1], ln_kv, w_qg, w_kv)
    kvt = kvt.reshape(n_p, N_KV_STREAMS, KV_HEADS, HEAD_DIM, t_p)
    o_p = nsa_prompt(qg_p, qr_p, kvt, n_p, t_p, cmp_pe, cmp_w1, cmp_b1, cmp_w2)
    p_rows = kvt[:, :4].transpose(0, 4, 1, 2, 3)
    p_win = kvt[:, 4:, :, :, t_p - min(WINDOW, t_p):].transpose(0, 4, 1, 2, 3)
    qg_t = mm(rmsnorm(h_t, ln_mix[1]), w_qg)
    kv_t = mm(rmsnorm(h_t, ln_kv), w_kv)
    o_s, s_rows, s_win = nsa_sample(qg_t[:n_s, :nsa_wqg.shape[-1]], kv_t[:n_s], cache_kv, cache_win,
                                    page_table, cmp_pe, cmp_w1, cmp_b1, cmp_w2)
    o_t = [jnp.pad(x, ((0, RW_TM - n_s), (0, 0))) for x in o_s]
    expand = [jnp.asarray((np.arange(gate_cols)[:, None] == 3 * (np.arange(n_q)[None, :] // HEAD_DIM) + br)
                          .astype(np.float32), BF16) for br in range(3)]
    w_router = router_weight(moe_wrg[1], moe_wre[1])
    res_p, res_t = moe([mix_out(list(br), qg, hh, nsa_wo[0], ln_ffn[1], w_router, gate_expand=expand, gate_col=n_q // LANES)
                        for br, qg, hh in ((o_p, qg_p, h_p), (o_t, qg_t, h_t))], 1)

    y_prompt = final_norm(*res_p, ln_out).reshape(n_p, t_p, dm)
    y_sample = rmsnorm(sum(res_t)[:n_s], ln_out).reshape(n_s, 1, dm)
    return (y_prompt, y_sample, wkv_p[None], p_shift, p_rows, p_win, wkv_s[None], s_shift, s_rows, s_win)
```
